```python
import math
import jax, jax.numpy as jnp
from jax import lax
import numpy as np

D_MODEL = 1024
BATCH = 16
SEQ = 2048
DEPTH = 2

CHUNK = 64
EPS = 1e-6
NEG_INF = -1e30
RET_HEADS = 4
RET_DK = 128
RET_DV = 128
RET_W = RET_HEADS * RET_DV
ROPE_BASE = 10000.0
CONV_W = 512
CONV_K = 31
ATT_HEADS = 8
ATT_DH = 64
ATT_W = ATT_HEADS * ATT_DH
ATT_LEFT_CHUNKS = 8
ATT_BAND = (ATT_LEFT_CHUNKS + 1) * CHUNK
MAX_REL = 256
N_BRANCH = 3
IN_COLS = 4 * RET_W + 2 * CONV_W + 3 * ATT_W + N_BRANCH * D_MODEL
N_GROUPS = 4
EXPERTS_PER_GROUP = 8
TOP_K_INNER = 2
EXPERT_HIDDEN = 256
N_MOD = 6

kernel_name = 'hybrid_retention_conv_chunkattn_hmoe'


def _rmsnorm(x, g):
    xf = x.astype(jnp.float32)
    y = xf * lax.rsqrt(jnp.mean(xf * xf, axis=-1, keepdims=True) + EPS)
    return (y * g.astype(jnp.float32)).astype(x.dtype)


def _layernorm_f32(x, g, b):
    xf = x.astype(jnp.float32)
    mu = jnp.mean(xf, axis=-1, keepdims=True)
    var = jnp.mean(jnp.square(xf - mu), axis=-1, keepdims=True)
    return (xf - mu) * lax.rsqrt(var + EPS) * g + b


def _rope(x, pos):
    half = x.shape[-1] // 2
    inv = ROPE_BASE ** (-jnp.arange(half, dtype=jnp.float32) / half)
    ang = pos.astype(jnp.float32)[..., None] * inv
    cos = jnp.cos(ang)[:, :, None, :]
    sin = jnp.sin(ang)[:, :, None, :]
    x1, x2 = x[..., :half], x[..., half:]
    return jnp.concatenate([x1 * cos - x2 * sin, x1 * sin + x2 * cos], axis=-1)


def _retention(q, k, v, g, positions, gn_gain):
    B, S, _ = q.shape
    N = S // CHUNK
    f32 = jnp.float32
    q = _rope(q.astype(f32).reshape(B, S, RET_HEADS, RET_DK), positions)
    k = _rope(k.astype(f32).reshape(B, S, RET_HEADS, RET_DK), positions) * (RET_DK ** -0.5)
    v = v.astype(f32).reshape(B, S, RET_HEADS, RET_DV)

    def chunks(t):
        return t.reshape(B, N, CHUNK, RET_HEADS, -1).transpose(0, 3, 1, 2, 4)

    q, k, v = chunks(q), chunks(k), chunks(v)
    log_gamma = jnp.log1p(-jnp.exp2(-5.0 - jnp.arange(RET_HEADS, dtype=f32)))
    pos = jnp.arange(CHUNK, dtype=f32)
    diff = pos[:, None] - pos[None, :]
    intra_decay = jnp.where(diff >= 0.0,
                            jnp.exp(log_gamma[:, None, None] * jnp.maximum(diff, 0.0)), 0.0)
    scores = jnp.einsum('bhncd,bhnmd->bhncm', q, k) * intra_decay[None, :, None]
    intra = jnp.einsum('bhncm,bhnme->bhnce', scores, v)
    key_decay = jnp.exp(log_gamma[:, None] * (CHUNK - 1.0 - pos)[None, :])
    kv = jnp.einsum('bhnmd,hm,bhnme->nbhde', k, key_decay, v)
    chunk_decay = jnp.exp(log_gamma * CHUNK)[None, :, None, None]

    def step(state, kv_n):
        return chunk_decay * state + kv_n, state

    _, prev = lax.scan(step, jnp.zeros((B, RET_HEADS, RET_DK, RET_DV), f32), kv)
    query_decay = jnp.exp(log_gamma[:, None] * (pos + 1.0)[None, :])
    cross = jnp.einsum('bhncd,nbhde->bhnce', q, prev) * query_decay[None, :, None, :, None]
    o = (intra + cross).transpose(0, 2, 3, 1, 4).reshape(B, S, RET_HEADS, RET_DV)
    mu = jnp.mean(o, axis=-1, keepdims=True)
    var = jnp.mean(jnp.square(o - mu), axis=-1, keepdims=True)
    o = (o - mu) * lax.rsqrt(var + EPS) * gn_gain.astype(f32).reshape(RET_HEADS, RET_DV)
    o = o.reshape(B, S, RET_W) * jax.nn.silu(g.astype(f32))
    return o.astype(g.dtype)


def _conv_module(u, w_dw, b_dw, ln_g, ln_b):
    a, b = jnp.split(u, 2, axis=-1)
    z = (a * jax.nn.sigmoid(b)).astype(w_dw.dtype)
    z = jnp.pad(z, ((0, 0), (CONV_K - 1, 0), (0, 0)))
    z = lax.conv_general_dilated(z, w_dw, (1,), 'VALID',
                                 dimension_numbers=('NWC', 'WIO', 'NWC'),
                                 feature_group_count=CONV_W) + b_dw
    z = _layernorm_f32(z, ln_g.astype(jnp.float32), ln_b.astype(jnp.float32))
    return jax.nn.silu(z).astype(u.dtype)


def _chunk_attention(q, k, v, q_gain, k_gain, rel_table):
    B, S, _ = q.shape
    N = S // CHUNK
    f32 = jnp.float32

    def heads(t):
        return t.reshape(B, S, ATT_HEADS, ATT_DH).transpose(0, 2, 1, 3)

    qh = heads(_rmsnorm(q.reshape(B, S, ATT_HEADS, ATT_DH), q_gain).reshape(B, S, ATT_W))
    kh = heads(_rmsnorm(k.reshape(B, S, ATT_HEADS, ATT_DH), k_gain).reshape(B, S, ATT_W))
    vh = heads(v)
    pad = ((0, 0), (0, 0), (ATT_LEFT_CHUNKS * CHUNK, 0), (0, 0))
    kp = jnp.pad(kh, pad)
    vp = jnp.pad(vh, pad)
    rel = ATT_LEFT_CHUNKS * CHUNK + jnp.arange(CHUNK)[:, None] - jnp.arange(ATT_BAND)[None, :]
    bias = rel_table.astype(f32)[:, jnp.clip(rel, -MAX_REL, MAX_REL) + MAX_REL]
    scale = ATT_DH ** -0.5
    band_idx = jnp.arange(ATT_BAND)

    def one_chunk(i):
        qi = lax.dynamic_slice_in_dim(qh, i * CHUNK, CHUNK, axis=2)
        ki = lax.dynamic_slice_in_dim(kp, i * CHUNK, ATT_BAND, axis=2)
        vi = lax.dynamic_slice_in_dim(vp, i * CHUNK, ATT_BAND, axis=2)
        s = jnp.einsum('bhqd,bhkd->bhqk', qi, ki).astype(f32) * scale + bias[None]
        valid = band_idx >= (ATT_LEFT_CHUNKS - i) * CHUNK
        s = jnp.where(valid[None, None, None, :], s, NEG_INF)
        p = jax.nn.softmax(s, axis=-1).astype(vi.dtype)
        return jnp.einsum('bhqk,bhkd->bhqd', p, vi)

    o = lax.map(one_chunk, jnp.arange(N))
    return o.transpose(1, 0, 3, 2, 4).reshape(B, S, ATT_W)


def _hier_moe(h, w_group, b_group, w_inner, b_inner, w_up, w_down):
    B, S, D = h.shape
    T = B * S
    f32 = jnp.float32
    t = h.reshape(T, D)
    gl = (t @ w_group + b_group).astype(f32)
    gsel = jnp.argmax(gl, axis=-1)
    p_group = jnp.take_along_axis(jax.nn.softmax(gl, axis=-1), gsel[:, None], axis=1)[:, 0]
    il = (t @ w_inner + b_inner).astype(f32).reshape(T, N_GROUPS, EXPERTS_PER_GROUP)
    chosen = jnp.take_along_axis(il, gsel[:, None, None], axis=1)[:, 0]
    topv, topi = lax.top_k(chosen, TOP_K_INNER)
    topw = jax.nn.softmax(topv, axis=-1) * p_group[:, None]
    w_e = jnp.sum(jax.nn.one_hot(topi, EXPERTS_PER_GROUP, dtype=f32) * topw[..., None], axis=1)
    w_ge = jax.nn.one_hot(gsel, N_GROUPS, dtype=f32)[:, :, None] * w_e[:, None, :]
    y = jnp.zeros((T, D), f32)
    for gi in range(N_GROUPS):
        hid = jnp.einsum('td,edf->tef', t, w_up[gi])
        a, b = jnp.split(hid, 2, axis=-1)
        act = jax.nn.silu(a.astype(f32)) * b.astype(f32) * w_ge[:, gi, :, None]
        y = y + jnp.einsum('tef,efd->td', act.astype(t.dtype), w_down[gi]).astype(f32)
    return y.reshape(B, S, D)


def _layer(x, c, positions, w_ada, b_ada, g_mix, g_ffn, w_in, b_gate, ret_gn,
           conv_w, conv_b, conv_ln_g, conv_ln_b, att_q_gain, att_k_gain, att_rel_bias,
           w_ret_out, w_conv_out, w_att_out, w_out,
           w_group, b_group, w_inner, b_inner, w_up, w_down):
    B, S, D = x.shape
    mod = jax.nn.silu(c) @ w_ada + b_ada
    shift1, scale1, gate1, shift2, scale2, gate2 = jnp.split(mod, N_MOD, axis=-1)

    h = _rmsnorm(x, g_mix) * (1.0 + scale1[:, None, :]) + shift1[:, None, :]
    proj = h @ w_in
    cuts = [RET_W, 2 * RET_W, 3 * RET_W, 4 * RET_W,
            4 * RET_W + 2 * CONV_W,
            4 * RET_W + 2 * CONV_W + ATT_W,
            4 * RET_W + 2 * CONV_W + 2 * ATT_W,
            4 * RET_W + 2 * CONV_W + 3 * ATT_W]
    rq, rk, rv, rg, cu, aq, ak, av, gl = jnp.split(proj, cuts, axis=-1)
    y_ret = _retention(rq, rk, rv, rg, positions, ret_gn) @ w_ret_out
    y_conv = _conv_module(cu, conv_w, conv_b, conv_ln_g, conv_ln_b) @ w_conv_out
    y_att = _chunk_attention(aq, ak, av, att_q_gain, att_k_gain, att_rel_bias) @ w_att_out
    gates = jax.nn.sigmoid((gl + b_gate).reshape(B, S, N_BRANCH, D))
    merged = gates[:, :, 0] * y_ret + gates[:, :, 1] * y_conv + gates[:, :, 2] * y_att
    x = x + (gate1[:, None, :] * (merged @ w_out)).astype(x.dtype)

    h2 = _rmsnorm(x, g_ffn) * (1.0 + scale2[:, None, :]) + shift2[:, None, :]
    y_ffn = _hier_moe(h2, w_group, b_group, w_inner, b_inner, w_up, w_down)
    x = x + (gate2[:, None, :] * y_ffn).astype(x.dtype)
    return x


def setup_inputs(seed: int = 0) -> dict:
    key = jax.random.key(seed)
    ks = jax.random.split(key, 32)
    f32 = jnp.float32
    L, D = DEPTH, D_MODEL
    E, G, F = EXPERTS_PER_GROUP, N_GROUPS, EXPERT_HIDDEN

    def nrm(k, shape, scale):
        return jax.random.normal(k, shape, f32) * scale

    x = jax.random.normal(ks[0], (BATCH, SEQ, D), f32)
    c = jax.random.normal(ks[1], (BATCH, D), f32)
    offset = jax.random.randint(ks[2], (BATCH, 1), 0, 4096, dtype=jnp.int32)
    positions = (offset + jnp.arange(SEQ, dtype=jnp.int32)[None, :]).astype(jnp.int32)
    return {
        'x': x,
        'c': c,
        'positions': positions,
        'w_ada': nrm(ks[3], (L, D, N_MOD * D), 0.5 * D ** -0.5),
        'b_ada': nrm(ks[4], (L, N_MOD * D), 0.02),
        'g_mix': 1.0 + nrm(ks[5], (L, D), 0.02),
        'g_ffn': 1.0 + nrm(ks[6], (L, D), 0.02),
        'w_in': nrm(ks[7], (L, D, IN_COLS), D ** -0.5),
        'b_gate': nrm(ks[8], (L, N_BRANCH * D), 0.02),
        'ret_gn': 1.0 + nrm(ks[9], (L, RET_W), 0.02),
        'conv_w': nrm(ks[10], (L, CONV_K, 1, CONV_W), CONV_K ** -0.5),
        'conv_b': nrm(ks[11], (L, CONV_W), 0.02),
        'conv_ln_g': 1.0 + nrm(ks[12], (L, CONV_W), 0.02),
        'conv_ln_b': nrm(ks[13], (L, CONV_W), 0.02),
        'att_q_gain': 1.0 + nrm(ks[14], (L, ATT_DH), 0.02),
        'att_k_gain': 1.0 + nrm(ks[15], (L, ATT_DH), 0.02),
        'att_rel_bias': nrm(ks[16], (L, ATT_HEADS, 2 * MAX_REL + 1), 0.1),
        'w_ret_out': nrm(ks[17], (L, RET_W, D), RET_W ** -0.5),
        'w_conv_out': nrm(ks[18], (L, CONV_W, D), CONV_W ** -0.5),
        'w_att_out': nrm(ks[19], (L, ATT_W, D), ATT_W ** -0.5),
        'w_out': nrm(ks[20], (L, D, D), D ** -0.5),
        'w_group': nrm(ks[21], (L, D, G), D ** -0.5),
        'b_group': nrm(ks[22], (L, G), 0.01),
        'w_inner': nrm(ks[23], (L, D, G * E), D ** -0.5),
        'b_inner': nrm(ks[24], (L, G * E), 0.01),
        'w_up': nrm(ks[25], (L, G, E, D, 2 * F), D ** -0.5),
        'w_down': nrm(ks[26], (L, G, E, F, D), F ** -0.5),
    }


def reference(x, c, positions, w_ada, b_ada, g_mix, g_ffn, w_in, b_gate, ret_gn,
              conv_w, conv_b, conv_ln_g, conv_ln_b, att_q_gain, att_k_gain, att_rel_bias,
              w_ret_out, w_conv_out, w_att_out, w_out,
              w_group, b_group, w_inner, b_inner, w_up, w_down):
    for l in range(DEPTH):
        x = _layer(x, c, positions, w_ada[l], b_ada[l], g_mix[l], g_ffn[l], w_in[l], b_gate[l],
                   ret_gn[l], conv_w[l], conv_b[l], conv_ln_g[l], conv_ln_b[l],
                   att_q_gain[l], att_k_gain[l], att_rel_bias[l],
                   w_ret_out[l], w_conv_out[l], w_att_out[l], w_out[l],
                   w_group[l], b_group[l], w_inner[l], b_inner[l], w_up[l], w_down[l])
    return x
```

```python
import functools
import math

import jax
import jax.numpy as jnp
from jax import lax
from jax.experimental import pallas as pl
from jax.experimental.pallas import tpu as pltpu

F32 = jnp.float32
BF16 = jnp.bfloat16

D_MODEL = 1024
CHUNK = 64
EPS = 1e-6
NEG_INF = -1e30
RET_HEADS = 4
RET_DK = 128
RET_W = 512
ROPE_BASE = 10000.0
CONV_W = 512
CONV_K = 31
ATT_HEADS = 8
ATT_DH = 64
ATT_W = 512
ATT_LEFT_CHUNKS = 8
MAX_REL = 256
N_BRANCH = 3
IN_COLS = 4 * RET_W + 2 * CONV_W + 3 * ATT_W + N_BRANCH * D_MODEL
N_GROUPS = 4
EXPERTS_PER_GROUP = 8
N_EXPERTS = N_GROUPS * EXPERTS_PER_GROUP
EXPERT_HIDDEN = 256
N_MOD = 6

LANES = 128
V7X_VMEM_LIMIT_BYTES = 56 * 1024 * 1024

COL_GATE = 0
COL_RQ, COL_RK, COL_RV, COL_RG = 6, 7, 8, 9
COL_CA, COL_CB = 10, 11
COL_AQ, COL_AK, COL_AV = 12, 13, 14

RET_BLOCK = 256
ATT_QBLOCK = 256
ATT_BAND = ATT_QBLOCK + ATT_LEFT_CHUNKS * CHUNK
ATT_PAD = ATT_LEFT_CHUNKS * CHUNK
CONV_TILE = 512
CONV_HIST = 32
CONV_ROWS = 64
MOE_TILE = 256


def _cparams(sem):
    return pltpu.CompilerParams(dimension_semantics=sem, vmem_limit_bytes=V7X_VMEM_LIMIT_BYTES)


def _adaln_kernel(c_ref, w_ref, b_ref, o_ref):
    c = c_ref[...]
    cs = c * jax.nn.sigmoid(c)
    o_ref[0] = jnp.dot(cs, w_ref[0], preferred_element_type=F32,
                       precision=lax.Precision.HIGHEST) + b_ref[0]


def _adaln(c, w_ada, b_ada):
    L, D, N = w_ada.shape
    B = c.shape[0]
    tn = 1536
    return pl.pallas_call(
        _adaln_kernel,
        grid=(L, N // tn),
        in_specs=[pl.BlockSpec((B, D), lambda l, j: (0, 0)),
                  pl.BlockSpec((1, D, tn), lambda l, j: (l, 0, j)),
                  pl.BlockSpec((1, 1, tn), lambda l, j: (l, 0, j))],
        out_specs=pl.BlockSpec((1, B, tn), lambda l, j: (l, 0, j)),
        out_shape=jax.ShapeDtypeStruct((L, B, N), F32),
        compiler_params=_cparams(("arbitrary", "arbitrary")),
        name="adaln",
    )(c, w_ada, b_ada.reshape(L, 1, N))


def _rope_kernel(pos_ref, inv_ref, ph_ref, o_ref):
    o_ref[...] = jnp.cos(pos_ref[...] * inv_ref[...] - ph_ref[...])


def _rope_table(positions):
    T = positions.size
    half = RET_DK // 2
    inv = ROPE_BASE ** (-jnp.arange(half, dtype=F32) / half)
    inv2 = jnp.concatenate([inv, inv]).reshape(1, LANES)
    phase = jnp.concatenate([jnp.zeros((half,), F32), jnp.full((half,), math.pi / 2, F32)]).reshape(1, LANES)
    posb = jnp.broadcast_to(positions.reshape(T, 1).astype(F32), (T, LANES))
    tm = 2048
    return pl.pallas_call(
        _rope_kernel,
        grid=(T // tm,),
        in_specs=[pl.BlockSpec((tm, LANES), lambda i: (i, 0)),
                  pl.BlockSpec((1, LANES), lambda i: (0, 0)),
                  pl.BlockSpec((1, LANES), lambda i: (0, 0))],
        out_specs=pl.BlockSpec((tm, LANES), lambda i: (i, 0)),
        out_shape=jax.ShapeDtypeStruct((T, LANES), F32),
        compiler_params=_cparams(("arbitrary",)),
        name="rope_table",
    )(posb, inv2, phase)


def _inproj_kernel(x_ref, mod_ref, g_ref, w_ref, o_ref, h_ref):
    @pl.when(pl.program_id(1) == 0)
    def _():
        x = x_ref[...]
        y = x * lax.rsqrt(jnp.mean(x * x, axis=-1, keepdims=True) + EPS) * g_ref[...]
        h = y * (1.0 + mod_ref[0, 1:2, :]) + mod_ref[0, 0:1, :]
        h_ref[...] = h.astype(BF16)

    o_ref[...] = jnp.dot(h_ref[...], w_ref[...], preferred_element_type=F32).astype(BF16)


def _inproj(x2, mod, g_mix, w_in_b, S):
    T, D = x2.shape
    N = w_in_b.shape[1]
    tm, tn = 1024, 1536
    return pl.pallas_call(
        _inproj_kernel,
        grid=(T // tm, N // tn),
        in_specs=[pl.BlockSpec((tm, D), lambda i, j: (i, 0)),
                  pl.BlockSpec((1, N_MOD, D), lambda i, j: (i * tm // S, 0, 0)),
                  pl.BlockSpec((1, D), lambda i, j: (0, 0)),
                  pl.BlockSpec((D, tn), lambda i, j: (0, j))],
        out_specs=pl.BlockSpec((tm, tn), lambda i, j: (i, j)),
        out_shape=jax.ShapeDtypeStruct((T, N), BF16),
        scratch_shapes=[pltpu.VMEM((tm, D), BF16)],
        compiler_params=_cparams(("arbitrary", "arbitrary")),
        name="inproj",
    )(x2, mod, g_mix.reshape(1, D), w_in_b)


def _retention_kernel(q_ref, k_ref, v_ref, g_ref, cs_ref, gn_ref, o_ref, state_ref):
    C = RET_BLOCK

    @pl.when(pl.program_id(1) == 0)
    def _():
        state_ref[...] = jnp.zeros_like(state_ref)

    cs = cs_ref[...]
    csr = pltpu.roll(cs, RET_DK // 2, 1)
    first = lax.broadcasted_iota(jnp.int32, (C, RET_DK), 1) < RET_DK // 2
    cos2 = jnp.where(first, cs, csr)
    sin2 = jnp.where(first, -csr, cs)
    diff = (lax.broadcasted_iota(jnp.int32, (C, C), 0) - lax.broadcasted_iota(jnp.int32, (C, C), 1)).astype(F32)
    row = lax.broadcasted_iota(jnp.int32, (C, 1), 0).astype(F32)
    for h in range(RET_HEADS):
        lg = math.log1p(-(2.0 ** (-5.0 - h)))
        sl = slice(h * RET_DK, (h + 1) * RET_DK)
        qh = q_ref[:, sl].astype(F32)
        kh = k_ref[:, sl].astype(F32)
        vh = v_ref[:, sl]
        qr = (qh * cos2 + pltpu.roll(qh, RET_DK // 2, 1) * sin2).astype(BF16)
        kr = (kh * cos2 + pltpu.roll(kh, RET_DK // 2, 1) * sin2) * (RET_DK ** -0.5)
        decay = jnp.where(diff >= 0.0, jnp.exp(lg * jnp.maximum(diff, 0.0)), 0.0)
        s = lax.dot_general(qr, kr.astype(BF16), (((1,), (1,)), ((), ())),
                            preferred_element_type=F32) * decay
        o = jnp.dot(s.astype(BF16), vh, preferred_element_type=F32)
        st = state_ref[h]
        o = o + jnp.dot(qr, st.astype(BF16), preferred_element_type=F32) * jnp.exp(lg * (row + 1.0))
        kd = (kr * jnp.exp(lg * (C - 1.0 - row))).astype(BF16)
        kv = lax.dot_general(kd, vh, (((0,), (0,)), ((), ())), preferred_element_type=F32)
        state_ref[h] = math.exp(lg * C) * st + kv
        mu = jnp.mean(o, axis=-1, keepdims=True)
        oc = o - mu
        var = jnp.mean(oc * oc, axis=-1, keepdims=True)
        on = oc * lax.rsqrt(var + EPS) * gn_ref[:, sl]
        gh = g_ref[:, sl].astype(F32)
        o_ref[:, sl] = (on * (gh * jax.nn.sigmoid(gh))).astype(BF16)


def _retention(proj, cs_tab, ret_gn, B, S):
    T = B * S
    C = RET_BLOCK
    nblk = S // C

    def col(c):
        return pl.BlockSpec((C, RET_W), lambda b, i, c=c: (b * nblk + i, c))

    return pl.pallas_call(
        _retention_kernel,
        grid=(B, nblk),
        in_specs=[col(COL_RQ), col(COL_RK), col(COL_RV), col(COL_RG),
                  pl.BlockSpec((C, LANES), lambda b, i: (b * nblk + i, 0)),
                  pl.BlockSpec((1, RET_W), lambda b, i: (0, 0))],
        out_specs=pl.BlockSpec((C, RET_W), lambda b, i: (b * nblk + i, 0)),
        out_shape=jax.ShapeDtypeStruct((T, RET_W), BF16),
        scratch_shapes=[pltpu.VMEM((RET_HEADS, RET_DK, RET_DK), F32)],
        compiler_params=_cparams(("arbitrary", "arbitrary")),
        name="retention",
    )(proj, proj, proj, proj, cs_tab, ret_gn.reshape(1, RET_W))


def _conv_kernel(a_ref, b_ref, w_ref, bias_ref, lg_ref, lb_ref, o_ref, z_ref):
    tm = CONV_TILE
    H = CONV_HIST

    @pl.when(pl.program_id(1) == 0)
    def _():
        z_ref[0:H, :] = jnp.zeros((H, CONV_W), F32)

    a = a_ref[...].astype(F32)
    b = b_ref[...].astype(F32)
    z_ref[H:H + tm, :] = a * jax.nn.sigmoid(b)
    base = H - (CONV_K - 1)
    for r0 in range(0, tm, CONV_ROWS):
        acc = jnp.zeros((CONV_ROWS, CONV_W), F32)
        for j in range(CONV_K):
            acc = acc + z_ref[r0 + base + j:r0 + base + j + CONV_ROWS, :] * w_ref[j:j + 1, :]
        acc = acc + bias_ref[...]
        mu = jnp.mean(acc, axis=-1, keepdims=True)
        ac = acc - mu
        var = jnp.mean(ac * ac, axis=-1, keepdims=True)
        y = ac * lax.rsqrt(var + EPS) * lg_ref[...] + lb_ref[...]
        o_ref[r0:r0 + CONV_ROWS, :] = (y * jax.nn.sigmoid(y)).astype(BF16)
    z_ref[0:H, :] = z_ref[tm:tm + H, :]


def _conv(proj, conv_w, conv_b, ln_g, ln_b, B, S):
    T = B * S
    tm = CONV_TILE
    nblk = S // tm
    vec = pl.BlockSpec((1, CONV_W), lambda b, i: (0, 0))
    return pl.pallas_call(
        _conv_kernel,
        grid=(B, nblk),
        in_specs=[pl.BlockSpec((tm, CONV_W), lambda b, i: (b * nblk + i, COL_CA)),
                  pl.BlockSpec((tm, CONV_W), lambda b, i: (b * nblk + i, COL_CB)),
                  pl.BlockSpec((CONV_K, CONV_W), lambda b, i: (0, 0)),
                  vec, vec, vec],
        out_specs=pl.BlockSpec((tm, CONV_W), lambda b, i: (b * nblk + i, 0)),
        out_shape=jax.ShapeDtypeStruct((T, CONV_W), BF16),
        scratch_shapes=[pltpu.VMEM((CONV_HIST + tm, CONV_W), F32)],
        compiler_params=_cparams(("arbitrary", "arbitrary")),
        name="conv",
    )(proj, proj, conv_w.reshape(CONV_K, CONV_W), conv_b.reshape(1, CONV_W),
      ln_g.reshape(1, CONV_W), ln_b.reshape(1, CONV_W))


def _attn_kernel(q_ref, k_ref, v_ref, bm_ref, qg_ref, kg_ref, o_ref, kn_ref, vs_ref, *, S):
    tq = ATT_QBLOCK
    qi = pl.program_id(1)

    @pl.when(qi == 0)
    def _():
        rows = 512
        for h in range(ATT_HEADS):
            sl = slice(h * ATT_DH, (h + 1) * ATT_DH)
            kn_ref[h, 0:ATT_PAD, :] = jnp.zeros((ATT_PAD, ATT_DH), BF16)
            vs_ref[h, 0:ATT_PAD, :] = jnp.zeros((ATT_PAD, ATT_DH), BF16)
            for r0 in range(0, S, rows):
                kh = k_ref[r0:r0 + rows, sl].astype(F32)
                kn = kh * lax.rsqrt(jnp.mean(kh * kh, axis=-1, keepdims=True) + EPS) * kg_ref[...]
                kn_ref[h, ATT_PAD + r0:ATT_PAD + r0 + rows, :] = kn.astype(BF16)
                vs_ref[h, ATT_PAD + r0:ATT_PAD + r0 + rows, :] = v_ref[r0:r0 + rows, sl]

    start = pl.multiple_of(qi * tq, tq)
    key_pos = lax.broadcasted_iota(jnp.int32, (tq, ATT_BAND), 1) + start
    valid = key_pos >= ATT_PAD
    outs = []
    for h in range(ATT_HEADS):
        sl = slice(h * ATT_DH, (h + 1) * ATT_DH)
        qh = q_ref[:, sl].astype(F32)
        qn = qh * lax.rsqrt(jnp.mean(qh * qh, axis=-1, keepdims=True) + EPS) * qg_ref[...] * (ATT_DH ** -0.5)
        kb = kn_ref[h, pl.ds(start, ATT_BAND), :]
        vb = vs_ref[h, pl.ds(start, ATT_BAND), :]
        s = lax.dot_general(qn.astype(BF16), kb, (((1,), (1,)), ((), ())), preferred_element_type=F32)
        s = jnp.where(valid, s + bm_ref[h], NEG_INF)
        m = jnp.max(s, axis=-1, keepdims=True)
        p = jnp.exp(s - m)
        l = jnp.sum(p, axis=-1, keepdims=True)
        o = jnp.dot(p.astype(BF16), vb, preferred_element_type=F32)
        outs.append(o / l)
    o_ref[...] = jnp.concatenate(outs, axis=-1).astype(BF16)


def _attn_bias_mask(rel_table):
    r = jnp.arange(ATT_QBLOCK)[:, None]
    u = jnp.arange(ATT_BAND)[None, :]
    off = u - (r // CHUNK) * CHUNK
    inband = (off >= 0) & (off < (ATT_LEFT_CHUNKS + 1) * CHUNK)
    rel = r + ATT_PAD - u
    bias = rel_table.astype(F32)[:, jnp.clip(rel, -MAX_REL, MAX_REL) + MAX_REL]
    return jnp.where(inband[None], bias, NEG_INF)


def _attention(proj, bias_mask, q_gain, k_gain, B, S):
    T = B * S
    tq = ATT_QBLOCK
    nblk = S // tq
    vec = pl.BlockSpec((1, ATT_DH), lambda b, i: (0, 0))
    return pl.pallas_call(
        functools.partial(_attn_kernel, S=S),
        grid=(B, nblk),
        in_specs=[pl.BlockSpec((tq, ATT_W), lambda b, i: (b * nblk + i, COL_AQ)),
                  pl.BlockSpec((S, ATT_W), lambda b, i: (b, COL_AK)),
                  pl.BlockSpec((S, ATT_W), lambda b, i: (b, COL_AV)),
                  pl.BlockSpec((ATT_HEADS, tq, ATT_BAND), lambda b, i: (0, 0, 0)),
                  vec, vec],
        out_specs=pl.BlockSpec((tq, ATT_W), lambda b, i: (b * nblk + i, 0)),
        out_shape=jax.ShapeDtypeStruct((T, ATT_W), BF16),
        scratch_shapes=[pltpu.VMEM((ATT_HEADS, ATT_PAD + S, ATT_DH), BF16),
                        pltpu.VMEM((ATT_HEADS, ATT_PAD + S, ATT_DH), BF16)],
        compiler_params=_cparams(("arbitrary", "arbitrary")),
        name="chunk_attention",
    )(proj, proj, proj, bias_mask, q_gain.reshape(1, ATT_DH), k_gain.reshape(1, ATT_DH))


def _merge_kernel(x_ref, ret_ref, conv_ref, att_ref, g0_ref, g1_ref, g2_ref, bg_ref, mod_ref,
                  wr_ref, wc_ref, wa_ref, wo_ref, gf_ref, wrt_ref, brt_ref,
                  xo_ref, h2_ref, route_ref):
    D = D_MODEL

    def gate(g_ref, k):
        return jax.nn.sigmoid(g_ref[...].astype(F32) + bg_ref[:, k * D:(k + 1) * D])

    merged = gate(g0_ref, 0) * jnp.dot(ret_ref[...], wr_ref[...], preferred_element_type=F32)
    merged = merged + gate(g1_ref, 1) * jnp.dot(conv_ref[...], wc_ref[...], preferred_element_type=F32)
    merged = merged + gate(g2_ref, 2) * jnp.dot(att_ref[...], wa_ref[...], preferred_element_type=F32)
    y = jnp.dot(merged.astype(BF16), wo_ref[...], preferred_element_type=F32)
    x = x_ref[...] + mod_ref[0, 2:3, :] * y
    xo_ref[...] = x

    h2 = x * lax.rsqrt(jnp.mean(x * x, axis=-1, keepdims=True) + EPS) * gf_ref[...]
    h2 = h2 * (1.0 + mod_ref[0, 4:5, :]) + mod_ref[0, 3:4, :]
    h2_ref[...] = h2
    logits = jnp.dot(h2, wrt_ref[...], preferred_element_type=F32,
                     precision=lax.Precision.HIGHEST) + brt_ref[...]
    tm = logits.shape[0]
    lane = lax.broadcasted_iota(jnp.int32, (tm, LANES), 1)
    big = jnp.int32(LANES)
    is_g = (lane >= N_EXPERTS) & (lane < N_EXPERTS + N_GROUPS)
    gl = jnp.where(is_g, logits, -jnp.inf)
    gmax = jnp.max(gl, axis=-1, keepdims=True)
    gsel = jnp.min(jnp.where(gl == gmax, lane - N_EXPERTS, big), axis=-1, keepdims=True)
    p_group = 1.0 / jnp.sum(jnp.where(is_g, jnp.exp(gl - gmax), 0.0), axis=-1, keepdims=True)
    in_grp = (lane >= gsel * EXPERTS_PER_GROUP) & (lane < (gsel + 1) * EXPERTS_PER_GROUP)
    ch = jnp.where(in_grp, logits, -jnp.inf)
    v1 = jnp.max(ch, axis=-1, keepdims=True)
    i1 = jnp.min(jnp.where(ch == v1, lane, big), axis=-1, keepdims=True)
    ch2 = jnp.where(lane == i1, -jnp.inf, ch)
    v2 = jnp.max(ch2, axis=-1, keepdims=True)
    i2 = jnp.min(jnp.where(ch2 == v2, lane, big), axis=-1, keepdims=True)
    e = jnp.exp(v2 - v1)
    w1 = p_group / (1.0 + e)
    w2 = p_group * e / (1.0 + e)
    route = jnp.where(lane == 0, i1.astype(F32),
                      jnp.where(lane == 1, i2.astype(F32),
                                jnp.where(lane == 2, w1, jnp.where(lane == 3, w2, 0.0))))
    route_ref[...] = route


def _merge(x2, proj, ret, conv, att, b_gate, mod, wr, wc, wa, wo, g_ffn, w_rt, b_rt, S):
    T, D = x2.shape
    tm = 512
    row = lambda i: (i, 0)
    const = lambda i: (0, 0)
    br = pl.BlockSpec((tm, RET_W), row)
    return pl.pallas_call(
        _merge_kernel,
        grid=(T // tm,),
        in_specs=[pl.BlockSpec((tm, D), row), br, br, br,
                  pl.BlockSpec((tm, D), lambda i: (i, COL_GATE)),
                  pl.BlockSpec((tm, D), lambda i: (i, COL_GATE + 1)),
                  pl.BlockSpec((tm, D), lambda i: (i, COL_GATE + 2)),
                  pl.BlockSpec((1, N_BRANCH * D), const),
                  pl.BlockSpec((1, N_MOD, D), lambda i: (i * tm // S, 0, 0)),
                  pl.BlockSpec((RET_W, D), const), pl.BlockSpec((CONV_W, D), const),
                  pl.BlockSpec((ATT_W, D), const), pl.BlockSpec((D, D), const),
                  pl.BlockSpec((1, D), const),
                  pl.BlockSpec((D, LANES), const), pl.BlockSpec((1, LANES), const)],
        out_specs=[pl.BlockSpec((tm, D), row), pl.BlockSpec((tm, D), row),
                   pl.BlockSpec((tm, LANES), row)],
        out_shape=[jax.ShapeDtypeStruct((T, D), F32), jax.ShapeDtypeStruct((T, D), F32),
                   jax.ShapeDtypeStruct((T, LANES), F32)],
        compiler_params=_cparams(("arbitrary",)),
        name="merge_router",
    )(x2, ret, conv, att, proj, proj, proj, b_gate.reshape(1, N_BRANCH * D), mod,
      wr, wc, wa, wo, g_ffn.reshape(1, D), w_rt, b_rt)


def _dispatch_kernel(dest_ref, h_hbm, xs_in_hbm, xs_hbm, sem, *, rows):
    del xs_in_hbm
    base = pl.program_id(0) * rows

    def issue(j, carry):
        src = h_hbm.at[pl.ds(base + j, 1), :]
        pltpu.make_async_copy(src, xs_hbm.at[pl.ds(dest_ref[2 * j], 1), :], sem).start()
        pltpu.make_async_copy(src, xs_hbm.at[pl.ds(dest_ref[2 * j + 1], 1), :], sem).start()
        return carry

    lax.fori_loop(0, rows, issue, 0)
    for _ in range(2):
        pltpu.make_async_copy(h_hbm.at[pl.ds(0, rows), :], xs_hbm.at[pl.ds(0, rows), :], sem).wait()


def _dispatch(h2, dest, n_pad):
    T, D = h2.shape
    rows = 2048
    zeros = jnp.zeros((n_pad, D), h2.dtype)
    return pl.pallas_call(
        functools.partial(_dispatch_kernel, rows=rows),
        grid=(T // rows,),
        in_specs=[pl.BlockSpec((2 * rows,), lambda i: (i,), memory_space=pltpu.SMEM),
                  pl.BlockSpec(memory_space=pl.ANY),
                  pl.BlockSpec(memory_space=pl.ANY)],
        out_specs=pl.BlockSpec(memory_space=pl.ANY),
        out_shape=jax.ShapeDtypeStruct((n_pad, D), h2.dtype),
        scratch_shapes=[pltpu.SemaphoreType.DMA(())],
        input_output_aliases={2: 0},
        compiler_params=_cparams(("arbitrary",)),
        name="moe_dispatch",
    )(dest, h2, zeros)


def _ffn_kernel(te_ref, nv_ref, xs_ref, wu_ref, wd_ref, o_ref):
    i = pl.program_id(0)

    @pl.when(i < nv_ref[0])
    def _():
        h = xs_ref[...].astype(BF16)
        hid = jnp.dot(h, wu_ref[0], preferred_element_type=F32)
        a = hid[:, :EXPERT_HIDDEN]
        b = hid[:, EXPERT_HIDDEN:]
        act = a * jax.nn.sigmoid(a) * b
        o_ref[...] = jnp.dot(act.astype(BF16), wd_ref[0], preferred_element_type=F32)

    @pl.when(i >= nv_ref[0])
    def _():
        o_ref[...] = jnp.zeros_like(o_ref)


def _ffn(xs, tile_expert, n_valid, w_up_b, w_down_b):
    n_pad, D = xs.shape
    tr = MOE_TILE
    F2 = w_up_b.shape[-1]
    grid_spec = pltpu.PrefetchScalarGridSpec(
        num_scalar_prefetch=2,
        grid=(n_pad // tr,),
        in_specs=[pl.BlockSpec((tr, D), lambda i, te, nv: (i, 0)),
                  pl.BlockSpec((1, D, F2), lambda i, te, nv: (te[i], 0, 0)),
                  pl.BlockSpec((1, EXPERT_HIDDEN, D), lambda i, te, nv: (te[i], 0, 0))],
        out_specs=pl.BlockSpec((tr, D), lambda i, te, nv: (i, 0)),
    )
    return pl.pallas_call(
        _ffn_kernel,
        grid_spec=grid_spec,
        out_shape=jax.ShapeDtypeStruct((n_pad, D), F32),
        compiler_params=_cparams(("arbitrary",)),
        name="moe_experts",
    )(tile_expert, n_valid, xs, w_up_b, w_down_b)


def _combine_kernel(dest_ref, x_ref, route_ref, mod_ref, ye_hbm, o_ref, buf, sem, *, rows):
    def issue(j, carry):
        for k in range(2):
            pltpu.make_async_copy(ye_hbm.at[pl.ds(dest_ref[2 * j + k], 1), :],
                                  buf.at[k, pl.ds(j, 1), :], sem.at[k]).start()
        return carry

    lax.fori_loop(0, rows, issue, 0)
    for k in range(2):
        pltpu.make_async_copy(ye_hbm.at[pl.ds(0, rows), :], buf.at[k], sem.at[k]).wait()
    r = route_ref[...]
    y = r[:, 2:3] * buf[0] + r[:, 3:4] * buf[1]
    o_ref[...] = x_ref[...] + mod_ref[0, 5:6, :] * y


def _combine(x2, route, mod, ye, dest, S):
    T, D = x2.shape
    rows = 256
    return pl.pallas_call(
        functools.partial(_combine_kernel, rows=rows),
        grid=(T // rows,),
        in_specs=[pl.BlockSpec((2 * rows,), lambda i: (i,), memory_space=pltpu.SMEM),
                  pl.BlockSpec((rows, D), lambda i: (i, 0)),
                  pl.BlockSpec((rows, LANES), lambda i: (i, 0)),
                  pl.BlockSpec((1, N_MOD, D), lambda i: (i * rows // S, 0, 0)),
                  pl.BlockSpec(memory_space=pl.ANY)],
        out_specs=pl.BlockSpec((rows, D), lambda i: (i, 0)),
        out_shape=jax.ShapeDtypeStruct((T, D), F32),
        scratch_shapes=[pltpu.VMEM((2, rows, D), F32), pltpu.SemaphoreType.DMA((2,))],
        compiler_params=_cparams(("arbitrary",)),
        name="moe_combine",
    )(dest, x2, route, mod, ye)


def _routing_tables(route, n_tiles):
    tr = MOE_TILE
    ef = route[:, :2].astype(jnp.int32).reshape(-1)
    onehot = (ef[:, None] == jnp.arange(N_EXPERTS, dtype=jnp.int32)[None, :]).astype(jnp.int32)
    csum = jnp.cumsum(onehot, axis=0)
    rank = jnp.take_along_axis(csum, ef[:, None], axis=1)[:, 0] - 1
    counts = csum[-1]
    padded = ((counts + tr - 1) // tr) * tr
    ends = jnp.cumsum(padded)
    dest = (ends - padded)[ef] + rank
    tile_start = jnp.arange(n_tiles, dtype=jnp.int32) * tr
    tile_expert = jnp.minimum(jnp.searchsorted(ends, tile_start, side="right"), N_EXPERTS - 1).astype(jnp.int32)
    n_valid = (ends[-1] // tr).astype(jnp.int32).reshape(1)
    return dest.astype(jnp.int32), tile_expert, n_valid


def _reorder_in_cols(w_in):
    cuts = 4 * RET_W + 2 * CONV_W + 3 * ATT_W
    return jnp.concatenate([w_in[..., cuts:], w_in[..., :cuts]], axis=-1)


def kernel(x, c, positions, w_ada, b_ada, g_mix, g_ffn, w_in, b_gate, ret_gn, conv_w, conv_b, conv_ln_g, conv_ln_b, att_q_gain, att_k_gain, att_rel_bias, w_ret_out, w_conv_out, w_att_out, w_out, w_group, b_group, w_inner, b_inner, w_up, w_down):
    B, S, D = x.shape
    L = w_ada.shape[0]
    T = B * S
    assert D == D_MODEL and S % CONV_TILE == 0 and T % 2048 == 0
    n_pad = 2 * T + N_EXPERTS * MOE_TILE
    n_tiles = n_pad // MOE_TILE

    mod_all = _adaln(c, w_ada, b_ada).reshape(L, B, N_MOD, D)
    cs_tab = _rope_table(positions)
    x2 = x.reshape(T, D)
    for l in range(L):
        mod = mod_all[l]
        proj = _inproj(x2, mod, g_mix[l], _reorder_in_cols(w_in[l]).astype(BF16), S)
        ret = _retention(proj, cs_tab, ret_gn[l], B, S)
        conv = _conv(proj, conv_w[l], conv_b[l], conv_ln_g[l], conv_ln_b[l], B, S)
        att = _attention(proj, _attn_bias_mask(att_rel_bias[l]), att_q_gain[l], att_k_gain[l], B, S)
        w_rt = jnp.zeros((D, LANES), F32)
        w_rt = w_rt.at[:, :N_EXPERTS].set(w_inner[l]).at[:, N_EXPERTS:N_EXPERTS + N_GROUPS].set(w_group[l])
        b_rt = jnp.zeros((1, LANES), F32)
        b_rt = b_rt.at[0, :N_EXPERTS].set(b_inner[l]).at[0, N_EXPERTS:N_EXPERTS + N_GROUPS].set(b_group[l])
        x2, h2, route = _merge(x2, proj, ret, conv, att, b_gate[l], mod,
                               w_ret_out[l].astype(BF16), w_conv_out[l].astype(BF16),
                               w_att_out[l].astype(BF16), w_out[l].astype(BF16),
                               g_ffn[l], w_rt, b_rt, S)
        dest, tile_expert, n_valid = _routing_tables(route, n_tiles)
        xs = _dispatch(h2, dest, n_pad)
        ye = _ffn(xs, tile_expert, n_valid,
                  w_up[l].reshape(N_EXPERTS, D, 2 * EXPERT_HIDDEN).astype(BF16),
                  w_down[l].reshape(N_EXPERTS, EXPERT_HIDDEN, D).astype(BF16))
        x2 = _combine(x2, route, mod, ye, dest, S)
    return x2.reshape(B, S, D)
```

```python
import functools
import math

import jax
import jax.numpy as jnp
from jax import lax
from jax.experimental import pallas as pl
from jax.experimental.pallas import tpu as pltpu

F32 = jnp.float32
BF16 = jnp.bfloat16

D_MODEL = 1024
CHUNK = 64
EPS = 1e-6
NEG_INF = -1e30
RET_HEADS = 4
RET_DK = 128
RET_W = 512
ROPE_BASE = 10000.0
CONV_W = 512
CONV_K = 31
ATT_HEADS = 8
ATT_DH = 64
ATT_W = 512
ATT_LEFT_CHUNKS = 8
MAX_REL = 256
N_BRANCH = 3
IN_COLS = 4 * RET_W + 2 * CONV_W + 3 * ATT_W + N_BRANCH * D_MODEL
N_GROUPS = 4
EXPERTS_PER_GROUP = 8
N_EXPERTS = N_GROUPS * EXPERTS_PER_GROUP
EXPERT_HIDDEN = 256
N_MOD = 6

LANES = 128
V7X_VMEM_LIMIT_BYTES = 56 * 1024 * 1024

COL_GATE = 0
COL_RQ, COL_RK, COL_RV, COL_RG = 6, 7, 8, 9
COL_CA, COL_CB = 10, 11
COL_AQ, COL_AK, COL_AV = 12, 13, 14

RET_BLOCK = 256
ATT_QBLOCK = 256
ATT_BAND = ATT_QBLOCK + ATT_LEFT_CHUNKS * CHUNK
ATT_PAD = ATT_LEFT_CHUNKS * CHUNK
CONV_TILE = 512
CONV_HIST = 32
CONV_ROWS = 64
MOE_TILE = 256


def _cparams(sem):
    return pltpu.CompilerParams(dimension_semantics=sem, vmem_limit_bytes=V7X_VMEM_LIMIT_BYTES)


def _adaln_kernel(c_ref, w_ref, b_ref, o_ref):
    c = c_ref[...]
    cs = c * jax.nn.sigmoid(c)
    o_ref[0] = jnp.dot(cs, w_ref[0], preferred_element_type=F32,
                       precision=lax.Precision.HIGHEST) + b_ref[0]


def _adaln(c, w_ada, b_ada):
    L, D, N = w_ada.shape
    B = c.shape[0]
    tn = 1536
    return pl.pallas_call(
        _adaln_kernel,
        grid=(L, N // tn),
        in_specs=[pl.BlockSpec((B, D), lambda l, j: (0, 0)),
                  pl.BlockSpec((1, D, tn), lambda l, j: (l, 0, j)),
                  pl.BlockSpec((1, 1, tn), lambda l, j: (l, 0, j))],
        out_specs=pl.BlockSpec((1, B, tn), lambda l, j: (l, 0, j)),
        out_shape=jax.ShapeDtypeStruct((L, B, N), F32),
        compiler_params=_cparams(("arbitrary", "arbitrary")),
        name="adaln",
    )(c, w_ada, b_ada.reshape(L, 1, N))


def _rope_kernel(pos_ref, inv_ref, ph_ref, o_ref):
    o_ref[...] = jnp.cos(pos_ref[...] * inv_ref[...] - ph_ref[...])


def _rope_table(positions):
    T = positions.size
    half = RET_DK // 2
    inv = ROPE_BASE ** (-jnp.arange(half, dtype=F32) / half)
    inv2 = jnp.concatenate([inv, inv]).reshape(1, LANES)
    phase = jnp.concatenate([jnp.zeros((half,), F32), jnp.full((half,), math.pi / 2, F32)]).reshape(1, LANES)
    posb = jnp.broadcast_to(positions.reshape(T, 1).astype(F32), (T, LANES))
    tm = 2048
    return pl.pallas_call(
        _rope_kernel,
        grid=(T // tm,),
        in_specs=[pl.BlockSpec((tm, LANES), lambda i: (i, 0)),
                  pl.BlockSpec((1, LANES), lambda i: (0, 0)),
                  pl.BlockSpec((1, LANES), lambda i: (0, 0))],
        out_specs=pl.BlockSpec((tm, LANES), lambda i: (i, 0)),
        out_shape=jax.ShapeDtypeStruct((T, LANES), F32),
        compiler_params=_cparams(("arbitrary",)),
        name="rope_table",
    )(posb, inv2, phase)


def _inproj_kernel(x_ref, mod_ref, g_ref, w_ref, o_ref, h_ref):
    @pl.when(pl.program_id(1) == 0)
    def _():
        x = x_ref[...]
        y = x * lax.rsqrt(jnp.mean(x * x, axis=-1, keepdims=True) + EPS) * g_ref[...]
        h = y * (1.0 + mod_ref[0, 1:2, :]) + mod_ref[0, 0:1, :]
        h_ref[...] = h.astype(BF16)

    o_ref[...] = jnp.dot(h_ref[...], w_ref[...], preferred_element_type=F32).astype(BF16)


def _inproj(x2, mod, g_mix, w_in_b, S):
    T, D = x2.shape
    N = w_in_b.shape[1]
    tm, tn = 1024, 1536
    return pl.pallas_call(
        _inproj_kernel,
        grid=(T // tm, N // tn),
        in_specs=[pl.BlockSpec((tm, D), lambda i, j: (i, 0)),
                  pl.BlockSpec((1, N_MOD, D), lambda i, j: (i * tm // S, 0, 0)),
                  pl.BlockSpec((1, D), lambda i, j: (0, 0)),
                  pl.BlockSpec((D, tn), lambda i, j: (0, j))],
        out_specs=pl.BlockSpec((tm, tn), lambda i, j: (i, j)),
        out_shape=jax.ShapeDtypeStruct((T, N), BF16),
        scratch_shapes=[pltpu.VMEM((tm, D), BF16)],
        compiler_params=_cparams(("arbitrary", "arbitrary")),
        name="inproj",
    )(x2, mod, g_mix.reshape(1, D), w_in_b)


def _retention_kernel(q_ref, k_ref, v_ref, g_ref, cs_ref, gn_ref, o_ref, state_ref):
    C = RET_BLOCK

    @pl.when(pl.program_id(1) == 0)
    def _():
        state_ref[...] = jnp.zeros_like(state_ref)

    cs = cs_ref[...]
    csr = pltpu.roll(cs, RET_DK // 2, 1)
    first = lax.broadcasted_iota(jnp.int32, (C, RET_DK), 1) < RET_DK // 2
    cos2 = jnp.where(first, cs, csr)
    sin2 = jnp.where(first, -csr, cs)
    diff = (lax.broadcasted_iota(jnp.int32, (C, C), 0) - lax.broadcasted_iota(jnp.int32, (C, C), 1)).astype(F32)
    row = lax.broadcasted_iota(jnp.int32, (C, 1), 0).astype(F32)
    for h in range(RET_HEADS):
        lg = math.log1p(-(2.0 ** (-5.0 - h)))
        sl = slice(h * RET_DK, (h + 1) * RET_DK)
        qh = q_ref[:, sl].astype(F32)
        kh = k_ref[:, sl].astype(F32)
        vh = v_ref[:, sl]
        qr = (qh * cos2 + pltpu.roll(qh, RET_DK // 2, 1) * sin2).astype(BF16)
        kr = (kh * cos2 + pltpu.roll(kh, RET_DK // 2, 1) * sin2) * (RET_DK ** -0.5)
        decay = jnp.where(diff >= 0.0, jnp.exp(lg * jnp.maximum(diff, 0.0)), 0.0)
        s = lax.dot_general(qr, kr.astype(BF16), (((1,), (1,)), ((), ())),
                            preferred_element_type=F32) * decay
        o = jnp.dot(s.astype(BF16), vh, preferred_element_type=F32)
        st = state_ref[h]
        o = o + jnp.dot(qr, st.astype(BF16), preferred_element_type=F32) * jnp.exp(lg * (row + 1.0))
        kd = (kr * jnp.exp(lg * (C - 1.0 - row))).astype(BF16)
        kv = lax.dot_general(kd, vh, (((0,), (0,)), ((), ())), preferred_element_type=F32)
        state_ref[h] = math.exp(lg * C) * st + kv
        mu = jnp.mean(o, axis=-1, keepdims=True)
        oc = o - mu
        var = jnp.mean(oc * oc, axis=-1, keepdims=True)
        on = oc * lax.rsqrt(var + EPS) * gn_ref[:, sl]
        gh = g_ref[:, sl].astype(F32)
        o_ref[:, sl] = (on * (gh * jax.nn.sigmoid(gh))).astype(BF16)


def _retention(proj, cs_tab, ret_gn, B, S):
    T = B * S
    C = RET_BLOCK
    nblk = S // C

    def col(c):
        return pl.BlockSpec((C, RET_W), lambda b, i, c=c: (b * nblk + i, c))

    return pl.pallas_call(
        _retention_kernel,
        grid=(B, nblk),
        in_specs=[col(COL_RQ), col(COL_RK), col(COL_RV), col(COL_RG),
                  pl.BlockSpec((C, LANES), lambda b, i: (b * nblk + i, 0)),
                  pl.BlockSpec((1, RET_W), lambda b, i: (0, 0))],
        out_specs=pl.BlockSpec((C, RET_W), lambda b, i: (b * nblk + i, 0)),
        out_shape=jax.ShapeDtypeStruct((T, RET_W), BF16),
        scratch_shapes=[pltpu.VMEM((RET_HEADS, RET_DK, RET_DK), F32)],
        compiler_params=_cparams(("arbitrary", "arbitrary")),
        name="retention",
    )(proj, proj, proj, proj, cs_tab, ret_gn.reshape(1, RET_W))


def _conv_kernel(a_ref, b_ref, w_ref, bias_ref, lg_ref, lb_ref, o_ref, z_ref):
    tm = CONV_TILE
    H = CONV_HIST

    @pl.when(pl.program_id(1) == 0)
    def _():
        z_ref[0:H, :] = jnp.zeros((H, CONV_W), F32)

    a = a_ref[...].astype(F32)
    b = b_ref[...].astype(F32)
    z_ref[H:H + tm, :] = a * jax.nn.sigmoid(b)
    base = H - (CONV_K - 1)
    for r0 in range(0, tm, CONV_ROWS):
        acc = jnp.zeros((CONV_ROWS, CONV_W), F32)
        for j in range(CONV_K):
            acc = acc + z_ref[r0 + base + j:r0 + base + j + CONV_ROWS, :] * w_ref[j:j + 1, :]
        acc = acc + bias_ref[...]
        mu = jnp.mean(acc, axis=-1, keepdims=True)
        ac = acc - mu
        var = jnp.mean(ac * ac, axis=-1, keepdims=True)
        y = ac * lax.rsqrt(var + EPS) * lg_ref[...] + lb_ref[...]
        o_ref[r0:r0 + CONV_ROWS, :] = (y * jax.nn.sigmoid(y)).astype(BF16)
    z_ref[0:H, :] = z_ref[tm:tm + H, :]


def _conv(proj, conv_w, conv_b, ln_g, ln_b, B, S):
    T = B * S
    tm = CONV_TILE
    nblk = S // tm
    vec = pl.BlockSpec((1, CONV_W), lambda b, i: (0, 0))
    return pl.pallas_call(
        _conv_kernel,
        grid=(B, nblk),
        in_specs=[pl.BlockSpec((tm, CONV_W), lambda b, i: (b * nblk + i, COL_CA)),
                  pl.BlockSpec((tm, CONV_W), lambda b, i: (b * nblk + i, COL_CB)),
                  pl.BlockSpec((CONV_K, CONV_W), lambda b, i: (0, 0)),
                  vec, vec, vec],
        out_specs=pl.BlockSpec((tm, CONV_W), lambda b, i: (b * nblk + i, 0)),
        out_shape=jax.ShapeDtypeStruct((T, CONV_W), BF16),
        scratch_shapes=[pltpu.VMEM((CONV_HIST + tm, CONV_W), F32)],
        compiler_params=_cparams(("arbitrary", "arbitrary")),
        name="conv",
    )(proj, proj, conv_w.reshape(CONV_K, CONV_W), conv_b.reshape(1, CONV_W),
      ln_g.reshape(1, CONV_W), ln_b.reshape(1, CONV_W))


def _attn_kernel(q_ref, k_ref, v_ref, bm_ref, qg_ref, kg_ref, o_ref, kn_ref, vs_ref, *, S):
    tq = ATT_QBLOCK
    qi = pl.program_id(1)

    @pl.when(qi == 0)
    def _():
        rows = 512
        for h in range(ATT_HEADS):
            sl = slice(h * ATT_DH, (h + 1) * ATT_DH)
            kn_ref[h, 0:ATT_PAD, :] = jnp.zeros((ATT_PAD, ATT_DH), BF16)
            vs_ref[h, 0:ATT_PAD, :] = jnp.zeros((ATT_PAD, ATT_DH), BF16)
            for r0 in range(0, S, rows):
                kh = k_ref[r0:r0 + rows, sl].astype(F32)
                kn = kh * lax.rsqrt(jnp.mean(kh * kh, axis=-1, keepdims=True) + EPS) * kg_ref[...]
                kn_ref[h, ATT_PAD + r0:ATT_PAD + r0 + rows, :] = kn.astype(BF16)
                vs_ref[h, ATT_PAD + r0:ATT_PAD + r0 + rows, :] = v_ref[r0:r0 + rows, sl]

    start = pl.multiple_of(qi * tq, tq)
    key_pos = lax.broadcasted_iota(jnp.int32, (tq, ATT_BAND), 1) + start
    valid = key_pos >= ATT_PAD
    outs = []
    for h in range(ATT_HEADS):
        sl = slice(h * ATT_DH, (h + 1) * ATT_DH)
        qh = q_ref[:, sl].astype(F32)
        qn = qh * lax.rsqrt(jnp.mean(qh * qh, axis=-1, keepdims=True) + EPS) * qg_ref[...] * (ATT_DH ** -0.5)
        kb = kn_ref[h, pl.ds(start, ATT_BAND), :]
        vb = vs_ref[h, pl.ds(start, ATT_BAND), :]
        s = lax.dot_general(qn.astype(BF16), kb, (((1,), (1,)), ((), ())), preferred_element_type=F32)
        s = jnp.where(valid, s + bm_ref[h], NEG_INF)
        m = jnp.max(s, axis=-1, keepdims=True)
        p = jnp.exp(s - m)
        l = jnp.sum(p, axis=-1, keepdims=True)
        o = jnp.dot(p.astype(BF16), vb, preferred_element_type=F32)
        outs.append(o / l)
    o_ref[...] = jnp.concatenate(outs, axis=-1).astype(BF16)


def _attn_bias_mask(rel_table):
    r = jnp.arange(ATT_QBLOCK)[:, None]
    u = jnp.arange(ATT_BAND)[None, :]
    off = u - (r // CHUNK) * CHUNK
    inband = (off >= 0) & (off < (ATT_LEFT_CHUNKS + 1) * CHUNK)
    rel = r + ATT_PAD - u
    bias = rel_table.astype(F32)[:, jnp.clip(rel, -MAX_REL, MAX_REL) + MAX_REL]
    return jnp.where(inband[None], bias, NEG_INF)


def _attention(proj, bias_mask, q_gain, k_gain, B, S):
    T = B * S
    tq = ATT_QBLOCK
    nblk = S // tq
    vec = pl.BlockSpec((1, ATT_DH), lambda b, i: (0, 0))
    return pl.pallas_call(
        functools.partial(_attn_kernel, S=S),
        grid=(B, nblk),
        in_specs=[pl.BlockSpec((tq, ATT_W), lambda b, i: (b * nblk + i, COL_AQ)),
                  pl.BlockSpec((S, ATT_W), lambda b, i: (b, COL_AK)),
                  pl.BlockSpec((S, ATT_W), lambda b, i: (b, COL_AV)),
                  pl.BlockSpec((ATT_HEADS, tq, ATT_BAND), lambda b, i: (0, 0, 0)),
                  vec, vec],
        out_specs=pl.BlockSpec((tq, ATT_W), lambda b, i: (b * nblk + i, 0)),
        out_shape=jax.ShapeDtypeStruct((T, ATT_W), BF16),
        scratch_shapes=[pltpu.VMEM((ATT_HEADS, ATT_PAD + S, ATT_DH), BF16),
                        pltpu.VMEM((ATT_HEADS, ATT_PAD + S, ATT_DH), BF16)],
        compiler_params=_cparams(("arbitrary", "arbitrary")),
        name="chunk_attention",
    )(proj, proj, proj, bias_mask, q_gain.reshape(1, ATT_DH), k_gain.reshape(1, ATT_DH))


def _merge_kernel(x_ref, ret_ref, conv_ref, att_ref, g0_ref, g1_ref, g2_ref, bg_ref, mod_ref,
                  wr_ref, wc_ref, wa_ref, wo_ref, gf_ref, wrt_ref, brt_ref,
                  xo_ref, h2_ref, route_ref, cnt_out_ref, tri_ref, cnt_ref):
    D = D_MODEL

    @pl.when(pl.program_id(0) == 0)
    def _():
        n = tri_ref.shape[0]
        tri_ref[...] = (lax.broadcasted_iota(jnp.int32, (n, n), 1)
                        < lax.broadcasted_iota(jnp.int32, (n, n), 0)).astype(BF16)
        cnt_ref[...] = jnp.zeros_like(cnt_ref)

    def gate(g_ref, k):
        return jax.nn.sigmoid(g_ref[...].astype(F32) + bg_ref[:, k * D:(k + 1) * D])

    merged = gate(g0_ref, 0) * jnp.dot(ret_ref[...], wr_ref[...], preferred_element_type=F32)
    merged = merged + gate(g1_ref, 1) * jnp.dot(conv_ref[...], wc_ref[...], preferred_element_type=F32)
    merged = merged + gate(g2_ref, 2) * jnp.dot(att_ref[...], wa_ref[...], preferred_element_type=F32)
    y = jnp.dot(merged.astype(BF16), wo_ref[...], preferred_element_type=F32)
    x = x_ref[...] + mod_ref[0, 2:3, :] * y
    xo_ref[...] = x

    h2 = x * lax.rsqrt(jnp.mean(x * x, axis=-1, keepdims=True) + EPS) * gf_ref[...]
    h2 = h2 * (1.0 + mod_ref[0, 4:5, :]) + mod_ref[0, 3:4, :]
    h2_ref[...] = h2
    logits = jnp.dot(h2, wrt_ref[...], preferred_element_type=F32,
                     precision=lax.Precision.HIGHEST) + brt_ref[...]
    tm = logits.shape[0]
    lane = lax.broadcasted_iota(jnp.int32, (tm, LANES), 1)
    big = jnp.int32(LANES)
    is_g = (lane >= N_EXPERTS) & (lane < N_EXPERTS + N_GROUPS)
    gl = jnp.where(is_g, logits, -jnp.inf)
    gmax = jnp.max(gl, axis=-1, keepdims=True)
    gsel = jnp.min(jnp.where(gl == gmax, lane - N_EXPERTS, big), axis=-1, keepdims=True)
    p_group = 1.0 / jnp.sum(jnp.where(is_g, jnp.exp(gl - gmax), 0.0), axis=-1, keepdims=True)
    in_grp = (lane >= gsel * EXPERTS_PER_GROUP) & (lane < (gsel + 1) * EXPERTS_PER_GROUP)
    ch = jnp.where(in_grp, logits, -jnp.inf)
    v1 = jnp.max(ch, axis=-1, keepdims=True)
    i1 = jnp.min(jnp.where(ch == v1, lane, big), axis=-1, keepdims=True)
    ch2 = jnp.where(lane == i1, -jnp.inf, ch)
    v2 = jnp.max(ch2, axis=-1, keepdims=True)
    i2 = jnp.min(jnp.where(ch2 == v2, lane, big), axis=-1, keepdims=True)
    e = jnp.exp(v2 - v1)
    w1 = p_group / (1.0 + e)
    w2 = p_group * e / (1.0 + e)
    sel = ((lane == i1) | (lane == i2)).astype(BF16)
    before = jnp.dot(tri_ref[...], sel, preferred_element_type=F32) + cnt_ref[...]
    rank1 = jnp.sum(jnp.where(lane == i1, before, 0.0), axis=-1, keepdims=True)
    rank2 = jnp.sum(jnp.where(lane == i2, before, 0.0), axis=-1, keepdims=True)
    cnt_ref[...] += jnp.sum(sel.astype(F32), axis=0, keepdims=True)
    cnt_out_ref[...] = cnt_ref[...]
    vals = (i1.astype(F32), i2.astype(F32), w1, w2, rank1, rank2)
    route = jnp.zeros((tm, LANES), F32)
    for k, v in enumerate(vals):
        route = jnp.where(lane == k, v, route)
    route_ref[...] = route


def _merge(x2, proj, ret, conv, att, b_gate, mod, wr, wc, wa, wo, g_ffn, w_rt, b_rt, S):
    T, D = x2.shape
    tm = 512
    row = lambda i: (i, 0)
    const = lambda i: (0, 0)
    br = pl.BlockSpec((tm, RET_W), row)
    return pl.pallas_call(
        _merge_kernel,
        grid=(T // tm,),
        in_specs=[pl.BlockSpec((tm, D), row), br, br, br,
                  pl.BlockSpec((tm, D), lambda i: (i, COL_GATE)),
                  pl.BlockSpec((tm, D), lambda i: (i, COL_GATE + 1)),
                  pl.BlockSpec((tm, D), lambda i: (i, COL_GATE + 2)),
                  pl.BlockSpec((1, N_BRANCH * D), const),
                  pl.BlockSpec((1, N_MOD, D), lambda i: (i * tm // S, 0, 0)),
                  pl.BlockSpec((RET_W, D), const), pl.BlockSpec((CONV_W, D), const),
                  pl.BlockSpec((ATT_W, D), const), pl.BlockSpec((D, D), const),
                  pl.BlockSpec((1, D), const),
                  pl.BlockSpec((D, LANES), const), pl.BlockSpec((1, LANES), const)],
        out_specs=[pl.BlockSpec((tm, D), row), pl.BlockSpec((tm, D), row),
                   pl.BlockSpec((tm, LANES), row), pl.BlockSpec((1, LANES), const)],
        out_shape=[jax.ShapeDtypeStruct((T, D), F32), jax.ShapeDtypeStruct((T, D), F32),
                   jax.ShapeDtypeStruct((T, LANES), F32), jax.ShapeDtypeStruct((1, LANES), F32)],
        scratch_shapes=[pltpu.VMEM((tm, tm), BF16), pltpu.VMEM((1, LANES), F32)],
        compiler_params=_cparams(("arbitrary",)),
        name="merge_router",
    )(x2, ret, conv, att, proj, proj, proj, b_gate.reshape(1, N_BRANCH * D), mod,
      wr, wc, wa, wo, g_ffn.reshape(1, D), w_rt, b_rt)


def _dispatch_kernel(dest_ref, h_ref, xs_in_hbm, xs_hbm, sem, *, rows):
    del xs_in_hbm

    def issue(j, carry):
        for k in range(2):
            pltpu.make_async_copy(h_ref.at[pl.ds(j, 1), :],
                                  xs_hbm.at[pl.ds(dest_ref[2 * j + k], 1), :], sem).start()
        return carry

    lax.fori_loop(0, rows, issue, 0, unroll=8)
    for _ in range(2):
        pltpu.make_async_copy(h_ref, xs_hbm.at[pl.ds(0, rows), :], sem).wait()


def _dispatch(h2, dest, n_pad):
    T, D = h2.shape
    rows = 512
    zeros = jnp.zeros((n_pad, D), h2.dtype)
    return pl.pallas_call(
        functools.partial(_dispatch_kernel, rows=rows),
        grid=(T // rows,),
        in_specs=[pl.BlockSpec((2 * rows,), lambda i: (i,), memory_space=pltpu.SMEM),
                  pl.BlockSpec((rows, D), lambda i: (i, 0)),
                  pl.BlockSpec(memory_space=pl.ANY)],
        out_specs=pl.BlockSpec(memory_space=pl.ANY),
        out_shape=jax.ShapeDtypeStruct((n_pad, D), h2.dtype),
        scratch_shapes=[pltpu.SemaphoreType.DMA(())],
        input_output_aliases={2: 0},
        compiler_params=_cparams(("arbitrary",)),
        name="moe_dispatch",
    )(dest, h2, zeros)


def _ffn_kernel(te_ref, nv_ref, xs_ref, wu_ref, wd_ref, o_ref):
    i = pl.program_id(0)

    @pl.when(i < nv_ref[0])
    def _():
        h = xs_ref[...].astype(BF16)
        hid = jnp.dot(h, wu_ref[0], preferred_element_type=F32)
        a = hid[:, :EXPERT_HIDDEN]
        b = hid[:, EXPERT_HIDDEN:]
        act = a * jax.nn.sigmoid(a) * b
        o_ref[...] = jnp.dot(act.astype(BF16), wd_ref[0], preferred_element_type=F32)

    @pl.when(i >= nv_ref[0])
    def _():
        o_ref[...] = jnp.zeros_like(o_ref)


def _ffn(xs, tile_expert, n_valid, w_up_b, w_down_b):
    n_pad, D = xs.shape
    tr = MOE_TILE
    F2 = w_up_b.shape[-1]
    grid_spec = pltpu.PrefetchScalarGridSpec(
        num_scalar_prefetch=2,
        grid=(n_pad // tr,),
        in_specs=[pl.BlockSpec((tr, D), lambda i, te, nv: (i, 0)),
                  pl.BlockSpec((1, D, F2), lambda i, te, nv: (te[i], 0, 0)),
                  pl.BlockSpec((1, EXPERT_HIDDEN, D), lambda i, te, nv: (te[i], 0, 0))],
        out_specs=pl.BlockSpec((tr, D), lambda i, te, nv: (i, 0)),
    )
    return pl.pallas_call(
        _ffn_kernel,
        grid_spec=grid_spec,
        out_shape=jax.ShapeDtypeStruct((n_pad, D), F32),
        compiler_params=_cparams(("arbitrary",)),
        name="moe_experts",
    )(tile_expert, n_valid, xs, w_up_b, w_down_b)


def _combine_kernel(dcur_ref, dnext_ref, x_ref, route_ref, mod_ref, ye_hbm, o_ref, buf, sem, *, rows, nsteps):
    i = pl.program_id(0)

    def issue(d_ref, slot):
        def body(j, carry):
            for k in range(2):
                pltpu.make_async_copy(ye_hbm.at[pl.ds(d_ref[2 * j + k], 1), :],
                                      buf.at[slot, k, pl.ds(j, 1), :], sem.at[slot, k]).start()
            return carry

        lax.fori_loop(0, rows, body, 0, unroll=8)

    @pl.when(i == 0)
    def _():
        issue(dcur_ref, 0)

    @pl.when(i + 1 < nsteps)
    def _():
        issue(dnext_ref, (i + 1) % 2)

    slot = i % 2
    for k in range(2):
        pltpu.make_async_copy(ye_hbm.at[pl.ds(0, rows), :], buf.at[slot, k], sem.at[slot, k]).wait()
    r = route_ref[...]
    y = r[:, 2:3] * buf[slot, 0] + r[:, 3:4] * buf[slot, 1]
    o_ref[...] = x_ref[...] + mod_ref[0, 5:6, :] * y


def _combine(x2, route, mod, ye, dest, S):
    T, D = x2.shape
    rows = 256
    nsteps = T // rows
    return pl.pallas_call(
        functools.partial(_combine_kernel, rows=rows, nsteps=nsteps),
        grid=(nsteps,),
        in_specs=[pl.BlockSpec((2 * rows,), lambda i: (i,), memory_space=pltpu.SMEM),
                  pl.BlockSpec((2 * rows,), lambda i: (jnp.minimum(i + 1, nsteps - 1),),
                               memory_space=pltpu.SMEM),
                  pl.BlockSpec((rows, D), lambda i: (i, 0)),
                  pl.BlockSpec((rows, LANES), lambda i: (i, 0)),
                  pl.BlockSpec((1, N_MOD, D), lambda i: (i * rows // S, 0, 0)),
                  pl.BlockSpec(memory_space=pl.ANY)],
        out_specs=pl.BlockSpec((rows, D), lambda i: (i, 0)),
        out_shape=jax.ShapeDtypeStruct((T, D), F32),
        scratch_shapes=[pltpu.VMEM((2, 2, rows, D), F32), pltpu.SemaphoreType.DMA((2, 2))],
        compiler_params=_cparams(("arbitrary",)),
        name="moe_combine",
    )(dest, dest, x2, route, mod, ye)


def _routing_tables(route, counts, n_tiles):
    tr = MOE_TILE
    experts = jnp.arange(N_EXPERTS, dtype=jnp.int32)
    counts = counts[0, :N_EXPERTS].astype(jnp.int32)
    padded = ((counts + tr - 1) // tr) * tr
    ends = jnp.cumsum(padded)
    offs = ends - padded
    ids = route[:, 0:2].astype(jnp.int32)
    slot = route[:, 4:6].astype(jnp.int32)
    base = jnp.sum(jnp.where(ids[:, :, None] == experts[None, None, :], offs[None, None, :], 0), axis=-1)
    dest = (base + slot).reshape(-1)
    tile_start = jnp.arange(n_tiles, dtype=jnp.int32) * tr
    tile_expert = jnp.sum((tile_start[:, None] >= ends[None, :]).astype(jnp.int32), axis=-1)
    tile_expert = jnp.minimum(tile_expert, N_EXPERTS - 1)
    n_valid = (ends[-1] // tr).astype(jnp.int32).reshape(1)
    return dest, tile_expert, n_valid


def _reorder_in_cols(w_in):
    cuts = 4 * RET_W + 2 * CONV_W + 3 * ATT_W
    return jnp.concatenate([w_in[..., cuts:], w_in[..., :cuts]], axis=-1)


def kernel(x, c, positions, w_ada, b_ada, g_mix, g_ffn, w_in, b_gate, ret_gn, conv_w, conv_b, conv_ln_g, conv_ln_b, att_q_gain, att_k_gain, att_rel_bias, w_ret_out, w_conv_out, w_att_out, w_out, w_group, b_group, w_inner, b_inner, w_up, w_down):
    B, S, D = x.shape
    L = w_ada.shape[0]
    T = B * S
    assert D == D_MODEL and S % CONV_TILE == 0 and T % 2048 == 0
    n_pad = 2 * T + N_EXPERTS * MOE_TILE
    n_tiles = n_pad // MOE_TILE

    mod_all = _adaln(c, w_ada, b_ada).reshape(L, B, N_MOD, D)
    cs_tab = _rope_table(positions)
    x2 = x.reshape(T, D)
    for l in range(L):
        mod = mod_all[l]
        proj = _inproj(x2, mod, g_mix[l], _reorder_in_cols(w_in[l]).astype(BF16), S)
        ret = _retention(proj, cs_tab, ret_gn[l], B, S)
        conv = _conv(proj, conv_w[l], conv_b[l], conv_ln_g[l], conv_ln_b[l], B, S)
        att = _attention(proj, _attn_bias_mask(att_rel_bias[l]), att_q_gain[l], att_k_gain[l], B, S)
        w_rt = jnp.zeros((D, LANES), F32)
        w_rt = w_rt.at[:, :N_EXPERTS].set(w_inner[l]).at[:, N_EXPERTS:N_EXPERTS + N_GROUPS].set(w_group[l])
        b_rt = jnp.zeros((1, LANES), F32)
        b_rt = b_rt.at[0, :N_EXPERTS].set(b_inner[l]).at[0, N_EXPERTS:N_EXPERTS + N_GROUPS].set(b_group[l])
        x2, h2, route, counts = _merge(x2, proj, ret, conv, att, b_gate[l], mod,
                               w_ret_out[l].astype(BF16), w_conv_out[l].astype(BF16),
                               w_att_out[l].astype(BF16), w_out[l].astype(BF16),
                               g_ffn[l], w_rt, b_rt, S)
        dest, tile_expert, n_valid = _routing_tables(route, counts, n_tiles)
        xs = _dispatch(h2, dest, n_pad)
        ye = _ffn(xs, tile_expert, n_valid,
                  w_up[l].reshape(N_EXPERTS, D, 2 * EXPERT_HIDDEN).astype(BF16),
                  w_down[l].reshape(N_EXPERTS, EXPERT_HIDDEN, D).astype(BF16))
        x2 = _combine(x2, route, mod, ye, dest, S)
    return x2.reshape(B, S, D)
```

```python
import functools
import math

import jax
import jax.numpy as jnp
from jax import lax
from jax.experimental import pallas as pl
from jax.experimental.pallas import tpu as pltpu

F32 = jnp.float32
BF16 = jnp.bfloat16

D_MODEL = 1024
CHUNK = 64
EPS = 1e-6
NEG_INF = -1e30
RET_HEADS = 4
RET_DK = 128
RET_W = 512
ROPE_BASE = 10000.0
CONV_W = 512
CONV_K = 31
ATT_HEADS = 8
ATT_DH = 64
ATT_W = 512
ATT_LEFT_CHUNKS = 8
MAX_REL = 256
N_BRANCH = 3
IN_COLS = 4 * RET_W + 2 * CONV_W + 3 * ATT_W + N_BRANCH * D_MODEL
N_GROUPS = 4
EXPERTS_PER_GROUP = 8
N_EXPERTS = N_GROUPS * EXPERTS_PER_GROUP
EXPERT_HIDDEN = 256
N_MOD = 6

LANES = 128
V7X_VMEM_LIMIT_BYTES = 56 * 1024 * 1024

COL_GATE = 0
COL_RQ, COL_RK, COL_RV, COL_RG = 6, 7, 8, 9
COL_CA, COL_CB = 10, 11
COL_AQ, COL_AK, COL_AV = 12, 13, 14

RET_BLOCK = 256
ATT_QBLOCK = 256
ATT_BAND = ATT_QBLOCK + ATT_LEFT_CHUNKS * CHUNK
ATT_PAD = ATT_LEFT_CHUNKS * CHUNK
ATT_SROWS = 64
CONV_TILE = 512
CONV_HIST = 32
CONV_ROWS = 64
MOE_TILE = 256


def _cparams(sem):
    return pltpu.CompilerParams(dimension_semantics=sem, vmem_limit_bytes=V7X_VMEM_LIMIT_BYTES)


def _adaln_kernel(c_ref, w_ref, b_ref, o_ref):
    c = c_ref[...]
    cs = c * jax.nn.sigmoid(c)
    o_ref[0] = jnp.dot(cs, w_ref[0], preferred_element_type=F32,
                       precision=lax.Precision.HIGHEST) + b_ref[0]


def _adaln(c, w_ada, b_ada):
    L, D, N = w_ada.shape
    B = c.shape[0]
    tn = 1536
    return pl.pallas_call(
        _adaln_kernel,
        grid=(L, N // tn),
        in_specs=[pl.BlockSpec((B, D), lambda l, j: (0, 0)),
                  pl.BlockSpec((1, D, tn), lambda l, j: (l, 0, j)),
                  pl.BlockSpec((1, 1, tn), lambda l, j: (l, 0, j))],
        out_specs=pl.BlockSpec((1, B, tn), lambda l, j: (l, 0, j)),
        out_shape=jax.ShapeDtypeStruct((L, B, N), F32),
        compiler_params=_cparams(("arbitrary", "arbitrary")),
        name="adaln",
    )(c, w_ada, b_ada.reshape(L, 1, N))


def _rope_kernel(pos_ref, inv_ref, ph_ref, o_ref):
    o_ref[...] = jnp.cos(pos_ref[...] * inv_ref[...] - ph_ref[...])


def _rope_table(positions):
    T = positions.size
    half = RET_DK // 2
    inv = ROPE_BASE ** (-jnp.arange(half, dtype=F32) / half)
    inv2 = jnp.concatenate([inv, inv]).reshape(1, LANES)
    phase = jnp.concatenate([jnp.zeros((half,), F32), jnp.full((half,), math.pi / 2, F32)]).reshape(1, LANES)
    posb = jnp.broadcast_to(positions.reshape(T, 1).astype(F32), (T, LANES))
    tm = 2048
    return pl.pallas_call(
        _rope_kernel,
        grid=(T // tm,),
        in_specs=[pl.BlockSpec((tm, LANES), lambda i: (i, 0)),
                  pl.BlockSpec((1, LANES), lambda i: (0, 0)),
                  pl.BlockSpec((1, LANES), lambda i: (0, 0))],
        out_specs=pl.BlockSpec((tm, LANES), lambda i: (i, 0)),
        out_shape=jax.ShapeDtypeStruct((T, LANES), F32),
        compiler_params=_cparams(("arbitrary",)),
        name="rope_table",
    )(posb, inv2, phase)


def _inproj_kernel(x_ref, mod_ref, g_ref, w_ref, o_ref, h_ref):
    @pl.when(pl.program_id(1) == 0)
    def _():
        x = x_ref[...]
        y = x * lax.rsqrt(jnp.mean(x * x, axis=-1, keepdims=True) + EPS) * g_ref[...]
        h = y * (1.0 + mod_ref[0, 1:2, :]) + mod_ref[0, 0:1, :]
        h_ref[...] = h.astype(BF16)

    o_ref[...] = jnp.dot(h_ref[...], w_ref[...], preferred_element_type=F32).astype(BF16)


def _inproj(x2, mod, g_mix, w_in_b, S):
    T, D = x2.shape
    N = w_in_b.shape[1]
    tm, tn = 1024, 1536
    return pl.pallas_call(
        _inproj_kernel,
        grid=(T // tm, N // tn),
        in_specs=[pl.BlockSpec((tm, D), lambda i, j: (i, 0)),
                  pl.BlockSpec((1, N_MOD, D), lambda i, j: (i * tm // S, 0, 0)),
                  pl.BlockSpec((1, D), lambda i, j: (0, 0)),
                  pl.BlockSpec((D, tn), lambda i, j: (0, j))],
        out_specs=pl.BlockSpec((tm, tn), lambda i, j: (i, j)),
        out_shape=jax.ShapeDtypeStruct((T, N), BF16),
        scratch_shapes=[pltpu.VMEM((tm, D), BF16)],
        compiler_params=_cparams(("arbitrary", "arbitrary")),
        name="inproj",
    )(x2, mod, g_mix.reshape(1, D), w_in_b)


def _retention_kernel(q_ref, k_ref, v_ref, g_ref, cs_ref, gn_ref, o_ref, state_ref):
    C = RET_BLOCK

    @pl.when(pl.program_id(1) == 0)
    def _():
        state_ref[...] = jnp.zeros_like(state_ref)

    cs = cs_ref[...]
    csr = pltpu.roll(cs, RET_DK // 2, 1)
    first = lax.broadcasted_iota(jnp.int32, (C, RET_DK), 1) < RET_DK // 2
    cos2 = jnp.where(first, cs, csr)
    sin2 = jnp.where(first, -csr, cs)
    diff = (lax.broadcasted_iota(jnp.int32, (C, C), 0) - lax.broadcasted_iota(jnp.int32, (C, C), 1)).astype(F32)
    row = lax.broadcasted_iota(jnp.int32, (C, 1), 0).astype(F32)
    for h in range(RET_HEADS):
        lg = math.log1p(-(2.0 ** (-5.0 - h)))
        sl = slice(h * RET_DK, (h + 1) * RET_DK)
        qh = q_ref[:, sl].astype(F32)
        kh = k_ref[:, sl].astype(F32)
        vh = v_ref[:, sl]
        qr = (qh * cos2 + pltpu.roll(qh, RET_DK // 2, 1) * sin2).astype(BF16)
        kr = (kh * cos2 + pltpu.roll(kh, RET_DK // 2, 1) * sin2) * (RET_DK ** -0.5)
        decay = jnp.where(diff >= 0.0, jnp.exp(lg * jnp.maximum(diff, 0.0)), 0.0)
        s = lax.dot_general(qr, kr.astype(BF16), (((1,), (1,)), ((), ())),
                            preferred_element_type=F32) * decay
        o = jnp.dot(s.astype(BF16), vh, preferred_element_type=F32)
        st = state_ref[h]
        o = o + jnp.dot(qr, st.astype(BF16), preferred_element_type=F32) * jnp.exp(lg * (row + 1.0))
        kd = (kr * jnp.exp(lg * (C - 1.0 - row))).astype(BF16)
        kv = lax.dot_general(kd, vh, (((0,), (0,)), ((), ())), preferred_element_type=F32)
        state_ref[h] = math.exp(lg * C) * st + kv
        mu = jnp.mean(o, axis=-1, keepdims=True)
        oc = o - mu
        var = jnp.mean(oc * oc, axis=-1, keepdims=True)
        on = oc * lax.rsqrt(var + EPS) * gn_ref[:, sl]
        gh = g_ref[:, sl].astype(F32)
        o_ref[:, sl] = (on * (gh * jax.nn.sigmoid(gh))).astype(BF16)


def _retention(proj, cs_tab, ret_gn, B, S):
    T = B * S
    C = RET_BLOCK
    nblk = S // C

    def col(c):
        return pl.BlockSpec((C, RET_W), lambda b, i, c=c: (b * nblk + i, c))

    return pl.pallas_call(
        _retention_kernel,
        grid=(B, nblk),
        in_specs=[col(COL_RQ), col(COL_RK), col(COL_RV), col(COL_RG),
                  pl.BlockSpec((C, LANES), lambda b, i: (b * nblk + i, 0)),
                  pl.BlockSpec((1, RET_W), lambda b, i: (0, 0))],
        out_specs=pl.BlockSpec((C, RET_W), lambda b, i: (b * nblk + i, 0)),
        out_shape=jax.ShapeDtypeStruct((T, RET_W), BF16),
        scratch_shapes=[pltpu.VMEM((RET_HEADS, RET_DK, RET_DK), F32)],
        compiler_params=_cparams(("arbitrary", "arbitrary")),
        name="retention",
    )(proj, proj, proj, proj, cs_tab, ret_gn.reshape(1, RET_W))


def _conv_kernel(a_ref, b_ref, w_ref, bias_ref, lg_ref, lb_ref, o_ref, z_ref, zs_ref):
    tm = CONV_TILE
    H = CONV_HIST
    SUB = 8

    @pl.when(pl.program_id(1) == 0)
    def _():
        z_ref[0:H, :] = jnp.zeros((H, CONV_W), F32)

    a = a_ref[...].astype(F32)
    b = b_ref[...].astype(F32)
    z_ref[H:H + tm, :] = a * jax.nn.sigmoid(b)
    base = H - (CONV_K - 1)
    n_shift = tm + (base + CONV_K - 2) // SUB * SUB
    for r in range(1, SUB):
        for c0 in range(0, n_shift, 128):
            cs = min(128, n_shift - c0)
            zs_ref[r - 1, c0:c0 + cs, :] = z_ref[c0 + r:c0 + r + cs, :]
    for r0 in range(0, tm, CONV_ROWS):
        acc = jnp.zeros((CONV_ROWS, CONV_W), F32)
        for j in range(CONV_K):
            al, r = (base + j) // SUB * SUB, (base + j) % SUB
            if r == 0:
                tap = z_ref[r0 + al:r0 + al + CONV_ROWS, :]
            else:
                tap = zs_ref[r - 1, r0 + al:r0 + al + CONV_ROWS, :]
            acc = acc + tap * w_ref[j:j + 1, :]
        acc = acc + bias_ref[...]
        mu = jnp.mean(acc, axis=-1, keepdims=True)
        ac = acc - mu
        var = jnp.mean(ac * ac, axis=-1, keepdims=True)
        y = ac * lax.rsqrt(var + EPS) * lg_ref[...] + lb_ref[...]
        o_ref[r0:r0 + CONV_ROWS, :] = (y * jax.nn.sigmoid(y)).astype(BF16)
    z_ref[0:H, :] = z_ref[tm:tm + H, :]


def _conv(proj, conv_w, conv_b, ln_g, ln_b, B, S):
    T = B * S
    tm = CONV_TILE
    nblk = S // tm
    vec = pl.BlockSpec((1, CONV_W), lambda b, i: (0, 0))
    return pl.pallas_call(
        _conv_kernel,
        grid=(B, nblk),
        in_specs=[pl.BlockSpec((tm, CONV_W), lambda b, i: (b * nblk + i, COL_CA)),
                  pl.BlockSpec((tm, CONV_W), lambda b, i: (b * nblk + i, COL_CB)),
                  pl.BlockSpec((CONV_K, CONV_W), lambda b, i: (0, 0)),
                  vec, vec, vec],
        out_specs=pl.BlockSpec((tm, CONV_W), lambda b, i: (b * nblk + i, 0)),
        out_shape=jax.ShapeDtypeStruct((T, CONV_W), BF16),
        scratch_shapes=[pltpu.VMEM((CONV_HIST + tm, CONV_W), F32),
                        pltpu.VMEM((7, CONV_HIST + tm, CONV_W), F32)],
        compiler_params=_cparams(("arbitrary", "arbitrary")),
        name="conv",
    )(proj, proj, conv_w.reshape(CONV_K, CONV_W), conv_b.reshape(1, CONV_W),
      ln_g.reshape(1, CONV_W), ln_b.reshape(1, CONV_W))


def _head_mean_sq(x, hm_ref):
    sq = x * x
    hi = sq.astype(BF16)
    lo = (sq - hi.astype(F32)).astype(BF16)
    tot = jnp.dot(hi, hm_ref[...], preferred_element_type=F32) + jnp.dot(lo, hm_ref[...], preferred_element_type=F32)
    return tot * (1.0 / ATT_DH)


def _attn_kernel(q_ref, k_ref, v_ref, rrow_ref, qg_ref, kg_ref, hm_ref, o_ref, kn_ref, vs_ref, bm_ref, *, S):
    tq = ATT_QBLOCK
    qi = pl.program_id(1)
    heads = [slice(h * ATT_DH, (h + 1) * ATT_DH) for h in range(ATT_HEADS)]

    @pl.when(qi == 0)
    def _():
        r = lax.broadcasted_iota(jnp.int32, (tq, ATT_BAND), 0)
        u = lax.broadcasted_iota(jnp.int32, (tq, ATT_BAND), 1)
        off = u - (r // CHUNK) * CHUNK
        inband = (off >= 0) & (off < (ATT_LEFT_CHUNKS + 1) * CHUNK)
        for h in range(ATT_HEADS):
            gen = jnp.broadcast_to(rrow_ref[h], (tq, rrow_ref.shape[-1]))
            t = pltpu.roll(gen, 0, 1, stride=1, stride_axis=0)
            bm_ref[h] = jnp.where(inband, t[:, :ATT_BAND], NEG_INF)
        rows = 512
        for h in range(ATT_HEADS):
            kn_ref[h, 0:ATT_PAD, :] = jnp.zeros((ATT_PAD, ATT_DH), BF16)
            vs_ref[h, 0:ATT_PAD, :] = jnp.zeros((ATT_PAD, ATT_DH), BF16)
        for r0 in range(0, S, rows):
            k = k_ref[r0:r0 + rows, :].astype(F32)
            kn = (k * lax.rsqrt(_head_mean_sq(k, hm_ref) + EPS) * kg_ref[...]).astype(BF16)
            v = v_ref[r0:r0 + rows, :]
            for h in range(ATT_HEADS):
                kn_ref[h, ATT_PAD + r0:ATT_PAD + r0 + rows, :] = kn[:, heads[h]]
                vs_ref[h, ATT_PAD + r0:ATT_PAD + r0 + rows, :] = v[:, heads[h]]

    start = pl.multiple_of(qi * tq, tq)
    q = q_ref[...].astype(F32)
    qn = (q * lax.rsqrt(_head_mean_sq(q, hm_ref) + EPS) * qg_ref[...]).astype(BF16)

    def attend(mask_pad):
        if mask_pad:
            key_pos = lax.broadcasted_iota(jnp.int32, (ATT_SROWS, ATT_BAND), 1) + start
            valid = key_pos >= ATT_PAD
        outs = []
        for h in range(ATT_HEADS):
            kb = kn_ref[h, pl.ds(start, ATT_BAND), :]
            vb = vs_ref[h, pl.ds(start, ATT_BAND), :]
            s = lax.dot_general(qn[:, heads[h]], kb, (((1,), (1,)), ((), ())), preferred_element_type=F32)
            ps, ls = [], []
            for r0 in range(0, tq, ATT_SROWS):
                ss = s[r0:r0 + ATT_SROWS] + bm_ref[h, r0:r0 + ATT_SROWS, :]
                if mask_pad:
                    ss = jnp.where(valid, ss, NEG_INF)
                m = jnp.max(ss, axis=-1, keepdims=True)
                p = jnp.exp2(ss - m)
                ls.append(jnp.sum(p, axis=-1, keepdims=True))
                ps.append(p.astype(BF16))
            o = jnp.dot(jnp.concatenate(ps, axis=0), vb, preferred_element_type=F32)
            outs.append(o / jnp.concatenate(ls, axis=0))
        o_ref[...] = jnp.concatenate(outs, axis=-1).astype(BF16)

    first = ATT_PAD // tq

    @pl.when(qi < first)
    def _():
        attend(True)

    @pl.when(qi >= first)
    def _():
        attend(False)


def _attn_bias_row(rel_table):
    t = rel_table.astype(F32) * math.log2(math.e)
    edge = jnp.broadcast_to(t[:, 2 * MAX_REL:], (ATT_HEADS, ATT_QBLOCK))
    row = jnp.concatenate([edge, jnp.flip(t[:, 1:], axis=1), edge], axis=1)
    return row.reshape(ATT_HEADS, 1, 2 * ATT_QBLOCK + 2 * MAX_REL)


def _attention(proj, rel_table, q_gain, k_gain, B, S):
    T = B * S
    tq = ATT_QBLOCK
    nblk = S // tq
    gen = _attn_bias_row(rel_table)
    qg = jnp.tile(q_gain.astype(F32) * (ATT_DH ** -0.5 * math.log2(math.e)), ATT_HEADS).reshape(1, ATT_W)
    kg = jnp.tile(k_gain.astype(F32), ATT_HEADS).reshape(1, ATT_W)
    lane_head = jnp.arange(ATT_W) // ATT_DH
    hm = (lane_head[:, None] == lane_head[None, :]).astype(BF16)
    vec = pl.BlockSpec((1, ATT_W), lambda b, i: (0, 0))
    return pl.pallas_call(
        functools.partial(_attn_kernel, S=S),
        grid=(B, nblk),
        in_specs=[pl.BlockSpec((tq, ATT_W), lambda b, i: (b * nblk + i, COL_AQ)),
                  pl.BlockSpec((S, ATT_W), lambda b, i: (b, COL_AK)),
                  pl.BlockSpec((S, ATT_W), lambda b, i: (b, COL_AV)),
                  pl.BlockSpec(gen.shape, lambda b, i: (0, 0, 0)),
                  vec, vec,
                  pl.BlockSpec((ATT_W, ATT_W), lambda b, i: (0, 0))],
        out_specs=pl.BlockSpec((tq, ATT_W), lambda b, i: (b * nblk + i, 0)),
        out_shape=jax.ShapeDtypeStruct((T, ATT_W), BF16),
        scratch_shapes=[pltpu.VMEM((ATT_HEADS, ATT_PAD + S, ATT_DH), BF16),
                        pltpu.VMEM((ATT_HEADS, ATT_PAD + S, ATT_DH), BF16),
                        pltpu.VMEM((ATT_HEADS, tq, ATT_BAND), F32)],
        compiler_params=_cparams(("arbitrary", "arbitrary")),
        name="chunk_attention",
    )(proj, proj, proj, gen, qg, kg, hm)


def _merge_kernel(x_ref, ret_ref, conv_ref, att_ref, g0_ref, g1_ref, g2_ref, bg_ref, mod_ref,
                  wr_ref, wc_ref, wa_ref, wo_ref, gf_ref, wrt_ref, brt_ref,
                  xo_ref, h2_ref, route_ref, cnt_out_ref, tri_ref, cnt_ref):
    D = D_MODEL

    @pl.when(pl.program_id(0) == 0)
    def _():
        n = tri_ref.shape[0]
        tri_ref[...] = (lax.broadcasted_iota(jnp.int32, (n, n), 1)
                        < lax.broadcasted_iota(jnp.int32, (n, n), 0)).astype(BF16)
        cnt_ref[...] = jnp.zeros_like(cnt_ref)

    def gate(g_ref, k):
        return jax.nn.sigmoid(g_ref[...].astype(F32) + bg_ref[:, k * D:(k + 1) * D])

    merged = gate(g0_ref, 0) * jnp.dot(ret_ref[...], wr_ref[...], preferred_element_type=F32)
    merged = merged + gate(g1_ref, 1) * jnp.dot(conv_ref[...], wc_ref[...], preferred_element_type=F32)
    merged = merged + gate(g2_ref, 2) * jnp.dot(att_ref[...], wa_ref[...], preferred_element_type=F32)
    y = jnp.dot(merged.astype(BF16), wo_ref[...], preferred_element_type=F32)
    x = x_ref[...] + mod_ref[0, 2:3, :] * y
    xo_ref[...] = x

    h2 = x * lax.rsqrt(jnp.mean(x * x, axis=-1, keepdims=True) + EPS) * gf_ref[...]
    h2 = h2 * (1.0 + mod_ref[0, 4:5, :]) + mod_ref[0, 3:4, :]
    h2_ref[...] = h2
    h_hi = h2.astype(BF16)
    h_lo = (h2 - h_hi.astype(F32)).astype(BF16)
    both = jnp.dot(h_hi, wrt_ref[...], preferred_element_type=F32)
    logits = (both[:, :LANES] + both[:, LANES:]
              + jnp.dot(h_lo, wrt_ref[:, :LANES], preferred_element_type=F32) + brt_ref[...])
    tm = logits.shape[0]
    lane = lax.broadcasted_iota(jnp.int32, (tm, LANES), 1)
    big = jnp.int32(LANES)
    is_g = (lane >= N_EXPERTS) & (lane < N_EXPERTS + N_GROUPS)
    gl = jnp.where(is_g, logits, -jnp.inf)
    gmax = jnp.max(gl, axis=-1, keepdims=True)
    gsel = jnp.min(jnp.where(gl == gmax, lane - N_EXPERTS, big), axis=-1, keepdims=True)
    p_group = 1.0 / jnp.sum(jnp.where(is_g, jnp.exp(gl - gmax), 0.0), axis=-1, keepdims=True)
    in_grp = (lane >= gsel * EXPERTS_PER_GROUP) & (lane < (gsel + 1) * EXPERTS_PER_GROUP)
    ch = jnp.where(in_grp, logits, -jnp.inf)
    v1 = jnp.max(ch, axis=-1, keepdims=True)
    i1 = jnp.min(jnp.where(ch == v1, lane, big), axis=-1, keepdims=True)
    ch2 = jnp.where(lane == i1, -jnp.inf, ch)
    v2 = jnp.max(ch2, axis=-1, keepdims=True)
    i2 = jnp.min(jnp.where(ch2 == v2, lane, big), axis=-1, keepdims=True)
    e = jnp.exp(v2 - v1)
    w1 = p_group / (1.0 + e)
    w2 = p_group * e / (1.0 + e)
    sel = ((lane == i1) | (lane == i2)).astype(BF16)
    before = jnp.dot(tri_ref[...], sel, preferred_element_type=F32) + cnt_ref[...]
    rank1 = jnp.sum(jnp.where(lane == i1, before, 0.0), axis=-1, keepdims=True)
    rank2 = jnp.sum(jnp.where(lane == i2, before, 0.0), axis=-1, keepdims=True)
    cnt_ref[...] += jnp.sum(sel.astype(F32), axis=0, keepdims=True)
    cnt_out_ref[...] = cnt_ref[...]
    vals = (i1.astype(F32), i2.astype(F32), w1, w2, rank1, rank2)
    route = jnp.zeros((tm, LANES), F32)
    for k, v in enumerate(vals):
        route = jnp.where(lane == k, v, route)
    route_ref[...] = route


def _merge(x2, proj, ret, conv, att, b_gate, mod, wr, wc, wa, wo, g_ffn, w_rt, b_rt, S):
    T, D = x2.shape
    tm = 512
    row = lambda i: (i, 0)
    const = lambda i: (0, 0)
    br = pl.BlockSpec((tm, RET_W), row)
    return pl.pallas_call(
        _merge_kernel,
        grid=(T // tm,),
        in_specs=[pl.BlockSpec((tm, D), row), br, br, br,
                  pl.BlockSpec((tm, D), lambda i: (i, COL_GATE)),
                  pl.BlockSpec((tm, D), lambda i: (i, COL_GATE + 1)),
                  pl.BlockSpec((tm, D), lambda i: (i, COL_GATE + 2)),
                  pl.BlockSpec((1, N_BRANCH * D), const),
                  pl.BlockSpec((1, N_MOD, D), lambda i: (i * tm // S, 0, 0)),
                  pl.BlockSpec((RET_W, D), const), pl.BlockSpec((CONV_W, D), const),
                  pl.BlockSpec((ATT_W, D), const), pl.BlockSpec((D, D), const),
                  pl.BlockSpec((1, D), const),
                  pl.BlockSpec((D, 2 * LANES), const), pl.BlockSpec((1, LANES), const)],
        out_specs=[pl.BlockSpec((tm, D), row), pl.BlockSpec((tm, D), row),
                   pl.BlockSpec((tm, LANES), row), pl.BlockSpec((1, LANES), const)],
        out_shape=[jax.ShapeDtypeStruct((T, D), F32), jax.ShapeDtypeStruct((T, D), F32),
                   jax.ShapeDtypeStruct((T, LANES), F32), jax.ShapeDtypeStruct((1, LANES), F32)],
        scratch_shapes=[pltpu.VMEM((tm, tm), BF16), pltpu.VMEM((1, LANES), F32)],
        compiler_params=_cparams(("arbitrary",)),
        name="merge_router",
    )(x2, ret, conv, att, proj, proj, proj, b_gate.reshape(1, N_BRANCH * D), mod,
      wr, wc, wa, wo, g_ffn.reshape(1, D), w_rt, b_rt)


def _dispatch_kernel(dest_ref, h_ref, xs_in_hbm, xs_hbm, sem, *, rows):
    del xs_in_hbm

    def issue(j, carry):
        for k in range(2):
            pltpu.make_async_copy(h_ref.at[pl.ds(j, 1), :],
                                  xs_hbm.at[pl.ds(dest_ref[2 * j + k], 1), :], sem).start()
        return carry

    lax.fori_loop(0, rows, issue, 0, unroll=8)
    for _ in range(2):
        pltpu.make_async_copy(h_ref, xs_hbm.at[pl.ds(0, rows), :], sem).wait()


def _dispatch(h2, dest, n_pad):
    T, D = h2.shape
    rows = 512
    zeros = jnp.zeros((n_pad, D), h2.dtype)
    return pl.pallas_call(
        functools.partial(_dispatch_kernel, rows=rows),
        grid=(T // rows,),
        in_specs=[pl.BlockSpec((2 * rows,), lambda i: (i,), memory_space=pltpu.SMEM),
                  pl.BlockSpec((rows, D), lambda i: (i, 0)),
                  pl.BlockSpec(memory_space=pl.ANY)],
        out_specs=pl.BlockSpec(memory_space=pl.ANY),
        out_shape=jax.ShapeDtypeStruct((n_pad, D), h2.dtype),
        scratch_shapes=[pltpu.SemaphoreType.DMA(())],
        input_output_aliases={2: 0},
        compiler_params=_cparams(("arbitrary",)),
        name="moe_dispatch",
    )(dest, h2, zeros)


def _ffn_kernel(te_ref, nv_ref, xs_ref, wu_ref, wd_ref, o_ref):
    i = pl.program_id(0)

    @pl.when(i < nv_ref[0])
    def _():
        h = xs_ref[...].astype(BF16)
        hid = jnp.dot(h, wu_ref[0], preferred_element_type=F32)
        a = hid[:, :EXPERT_HIDDEN]
        b = hid[:, EXPERT_HIDDEN:]
        act = a * jax.nn.sigmoid(a) * b
        o_ref[...] = jnp.dot(act.astype(BF16), wd_ref[0], preferred_element_type=F32)

    @pl.when(i >= nv_ref[0])
    def _():
        o_ref[...] = jnp.zeros_like(o_ref)


def _ffn(xs, tile_expert, n_valid, w_up_b, w_down_b):
    n_pad, D = xs.shape
    tr = MOE_TILE
    F2 = w_up_b.shape[-1]
    grid_spec = pltpu.PrefetchScalarGridSpec(
        num_scalar_prefetch=2,
        grid=(n_pad // tr,),
        in_specs=[pl.BlockSpec((tr, D), lambda i, te, nv: (i, 0)),
                  pl.BlockSpec((1, D, F2), lambda i, te, nv: (te[i], 0, 0)),
                  pl.BlockSpec((1, EXPERT_HIDDEN, D), lambda i, te, nv: (te[i], 0, 0))],
        out_specs=pl.BlockSpec((tr, D), lambda i, te, nv: (i, 0)),
    )
    return pl.pallas_call(
        _ffn_kernel,
        grid_spec=grid_spec,
        out_shape=jax.ShapeDtypeStruct((n_pad, D), F32),
        compiler_params=_cparams(("arbitrary",)),
        name="moe_experts",
    )(tile_expert, n_valid, xs, w_up_b, w_down_b)


def _combine_kernel(dcur_ref, dnext_ref, x_ref, route_ref, mod_ref, ye_hbm, o_ref, buf, sem, *, rows, nsteps):
    i = pl.program_id(0)

    def issue(d_ref, slot):
        def body(j, carry):
            for k in range(2):
                pltpu.make_async_copy(ye_hbm.at[pl.ds(d_ref[2 * j + k], 1), :],
                                      buf.at[slot, k, pl.ds(j, 1), :], sem.at[slot, k]).start()
            return carry

        lax.fori_loop(0, rows, body, 0, unroll=8)

    @pl.when(i == 0)
    def _():
        issue(dcur_ref, 0)

    @pl.when(i + 1 < nsteps)
    def _():
        issue(dnext_ref, (i + 1) % 2)

    slot = i % 2
    for k in range(2):
        pltpu.make_async_copy(ye_hbm.at[pl.ds(0, rows), :], buf.at[slot, k], sem.at[slot, k]).wait()
    r = route_ref[...]
    y = r[:, 2:3] * buf[slot, 0] + r[:, 3:4] * buf[slot, 1]
    o_ref[...] = x_ref[...] + mod_ref[0, 5:6, :] * y


def _combine(x2, route, mod, ye, dest, S):
    T, D = x2.shape
    rows = 256
    nsteps = T // rows
    return pl.pallas_call(
        functools.partial(_combine_kernel, rows=rows, nsteps=nsteps),
        grid=(nsteps,),
        in_specs=[pl.BlockSpec((2 * rows,), lambda i: (i,), memory_space=pltpu.SMEM),
                  pl.BlockSpec((2 * rows,), lambda i: (jnp.minimum(i + 1, nsteps - 1),),
                               memory_space=pltpu.SMEM),
                  pl.BlockSpec((rows, D), lambda i: (i, 0)),
                  pl.BlockSpec((rows, LANES), lambda i: (i, 0)),
                  pl.BlockSpec((1, N_MOD, D), lambda i: (i * rows // S, 0, 0)),
                  pl.BlockSpec(memory_space=pl.ANY)],
        out_specs=pl.BlockSpec((rows, D), lambda i: (i, 0)),
        out_shape=jax.ShapeDtypeStruct((T, D), F32),
        scratch_shapes=[pltpu.VMEM((2, 2, rows, D), F32), pltpu.SemaphoreType.DMA((2, 2))],
        compiler_params=_cparams(("arbitrary",)),
        name="moe_combine",
    )(dest, dest, x2, route, mod, ye)


def _routing_tables(route, counts, n_tiles):
    tr = MOE_TILE
    experts = jnp.arange(N_EXPERTS, dtype=jnp.int32)
    counts = counts[0, :N_EXPERTS].astype(jnp.int32)
    padded = ((counts + tr - 1) // tr) * tr
    ends = jnp.cumsum(padded)
    offs = ends - padded
    ids = route[:, 0:2].astype(jnp.int32)
    slot = route[:, 4:6].astype(jnp.int32)
    base = jnp.sum(jnp.where(ids[:, :, None] == experts[None, None, :], offs[None, None, :], 0), axis=-1)
    dest = (base + slot).reshape(-1)
    tile_start = jnp.arange(n_tiles, dtype=jnp.int32) * tr
    tile_expert = jnp.sum((tile_start[:, None] >= ends[None, :]).astype(jnp.int32), axis=-1)
    tile_expert = jnp.minimum(tile_expert, N_EXPERTS - 1)
    n_valid = (ends[-1] // tr).astype(jnp.int32).reshape(1)
    return dest, tile_expert, n_valid


def _reorder_in_cols(w_in):
    cuts = 4 * RET_W + 2 * CONV_W + 3 * ATT_W
    return jnp.concatenate([w_in[..., cuts:], w_in[..., :cuts]], axis=-1)


def kernel(x, c, positions, w_ada, b_ada, g_mix, g_ffn, w_in, b_gate, ret_gn, conv_w, conv_b, conv_ln_g, conv_ln_b, att_q_gain, att_k_gain, att_rel_bias, w_ret_out, w_conv_out, w_att_out, w_out, w_group, b_group, w_inner, b_inner, w_up, w_down):
    B, S, D = x.shape
    L = w_ada.shape[0]
    T = B * S
    assert D == D_MODEL and S % CONV_TILE == 0 and T % 2048 == 0
    n_pad = 2 * T + N_EXPERTS * MOE_TILE
    n_tiles = n_pad // MOE_TILE

    mod_all = _adaln(c, w_ada, b_ada).reshape(L, B, N_MOD, D)
    cs_tab = _rope_table(positions)
    x2 = x.reshape(T, D)
    for l in range(L):
        mod = mod_all[l]
        proj = _inproj(x2, mod, g_mix[l], _reorder_in_cols(w_in[l]).astype(BF16), S)
        ret = _retention(proj, cs_tab, ret_gn[l], B, S)
        conv = _conv(proj, conv_w[l], conv_b[l], conv_ln_g[l], conv_ln_b[l], B, S)
        att = _attention(proj, att_rel_bias[l], att_q_gain[l], att_k_gain[l], B, S)
        w_rt = jnp.zeros((D, LANES), F32)
        w_rt = w_rt.at[:, :N_EXPERTS].set(w_inner[l]).at[:, N_EXPERTS:N_EXPERTS + N_GROUPS].set(w_group[l])
        b_rt = jnp.zeros((1, LANES), F32)
        b_rt = b_rt.at[0, :N_EXPERTS].set(b_inner[l]).at[0, N_EXPERTS:N_EXPERTS + N_GROUPS].set(b_group[l])
        w_rt_hi = w_rt.astype(BF16)
        w_rt2 = jnp.concatenate([w_rt_hi, (w_rt - w_rt_hi.astype(F32)).astype(BF16)], axis=1)
        x2, h2, route, counts = _merge(x2, proj, ret, conv, att, b_gate[l], mod,
                                       w_ret_out[l].astype(BF16), w_conv_out[l].astype(BF16),
                                       w_att_out[l].astype(BF16), w_out[l].astype(BF16),
                                       g_ffn[l], w_rt2, b_rt, S)
        dest, tile_expert, n_valid = _routing_tables(route, counts, n_tiles)
        xs = _dispatch(h2, dest, n_pad)
        ye = _ffn(xs, tile_expert, n_valid,
                  w_up[l].reshape(N_EXPERTS, D, 2 * EXPERT_HIDDEN).astype(BF16),
                  w_down[l].reshape(N_EXPERTS, EXPERT_HIDDEN, D).astype(BF16))
        x2 = _combine(x2, route, mod, ye, dest, S)
    return x2.reshape(B, S, D)
```

```python
import functools
import math

import jax
import jax.numpy as jnp
from jax import lax
from jax.experimental import pallas as pl
from jax.experimental.pallas import tpu as pltpu

F32 = jnp.float32
BF16 = jnp.bfloat16

D_MODEL = 1024
CHUNK = 64
EPS = 1e-6
NEG_INF = -1e30
RET_HEADS = 4
RET_DK = 128
RET_W = 512
ROPE_BASE = 10000.0
CONV_W = 512
CONV_K = 31
ATT_HEADS = 8
ATT_DH = 64
ATT_W = 512
ATT_LEFT_CHUNKS = 8
MAX_REL = 256
N_BRANCH = 3
IN_COLS = 4 * RET_W + 2 * CONV_W + 3 * ATT_W + N_BRANCH * D_MODEL
N_GROUPS = 4
EXPERTS_PER_GROUP = 8
N_EXPERTS = N_GROUPS * EXPERTS_PER_GROUP
EXPERT_HIDDEN = 256
N_MOD = 6

LANES = 128
V7X_VMEM_LIMIT_BYTES = 56 * 1024 * 1024

COL_GATE = 0
COL_RQ, COL_RK, COL_RV, COL_RG = 6, 7, 8, 9
COL_CA, COL_CB = 10, 11
COL_AQ, COL_AK, COL_AV = 12, 13, 14

RET_BLOCK = 256
ATT_QBLOCK = 256
ATT_BAND = ATT_QBLOCK + ATT_LEFT_CHUNKS * CHUNK
ATT_PAD = ATT_LEFT_CHUNKS * CHUNK
ATT_SROWS = 64
CONV_TILE = 512
CONV_HIST = 32
CONV_ROWS = 64
MOE_TILE = 512


def _cparams(sem):
    return pltpu.CompilerParams(dimension_semantics=sem, vmem_limit_bytes=V7X_VMEM_LIMIT_BYTES)


def _adaln_kernel(c_ref, w_ref, b_ref, o_ref):
    c = c_ref[...]
    cs = c * jax.nn.sigmoid(c)
    o_ref[0] = jnp.dot(cs, w_ref[0], preferred_element_type=F32,
                       precision=lax.Precision.HIGHEST) + b_ref[0]


def _adaln(c, w_ada, b_ada):
    L, D, N = w_ada.shape
    B = c.shape[0]
    tn = 1536
    return pl.pallas_call(
        _adaln_kernel,
        grid=(L, N // tn),
        in_specs=[pl.BlockSpec((B, D), lambda l, j: (0, 0)),
                  pl.BlockSpec((1, D, tn), lambda l, j: (l, 0, j)),
                  pl.BlockSpec((1, 1, tn), lambda l, j: (l, 0, j))],
        out_specs=pl.BlockSpec((1, B, tn), lambda l, j: (l, 0, j)),
        out_shape=jax.ShapeDtypeStruct((L, B, N), F32),
        compiler_params=_cparams(("arbitrary", "arbitrary")),
        name="adaln",
    )(c, w_ada, b_ada.reshape(L, 1, N))


def _rope_kernel(pos_ref, inv_ref, ph_ref, o_ref):
    o_ref[...] = jnp.cos(pos_ref[...] * inv_ref[...] - ph_ref[...])


def _rope_table(positions):
    T = positions.size
    half = RET_DK // 2
    inv = ROPE_BASE ** (-jnp.arange(half, dtype=F32) / half)
    inv2 = jnp.concatenate([inv, inv]).reshape(1, LANES)
    phase = jnp.concatenate([jnp.zeros((half,), F32), jnp.full((half,), math.pi / 2, F32)]).reshape(1, LANES)
    posb = jnp.broadcast_to(positions.reshape(T, 1).astype(F32), (T, LANES))
    tm = 2048
    return pl.pallas_call(
        _rope_kernel,
        grid=(T // tm,),
        in_specs=[pl.BlockSpec((tm, LANES), lambda i: (i, 0)),
                  pl.BlockSpec((1, LANES), lambda i: (0, 0)),
                  pl.BlockSpec((1, LANES), lambda i: (0, 0))],
        out_specs=pl.BlockSpec((tm, LANES), lambda i: (i, 0)),
        out_shape=jax.ShapeDtypeStruct((T, LANES), F32),
        compiler_params=_cparams(("arbitrary",)),
        name="rope_table",
    )(posb, inv2, phase)


def _inproj_kernel(x_ref, mod_ref, g_ref, w_ref, o_ref, h_ref):
    @pl.when(pl.program_id(1) == 0)
    def _():
        x = x_ref[...]
        y = x * lax.rsqrt(jnp.mean(x * x, axis=-1, keepdims=True) + EPS) * g_ref[...]
        h = y * (1.0 + mod_ref[0, 1:2, :]) + mod_ref[0, 0:1, :]
        h_ref[...] = h.astype(BF16)

    o_ref[...] = jnp.dot(h_ref[...], w_ref[...], preferred_element_type=F32).astype(BF16)


def _inproj(x2, mod, g_mix, w_in_b, S):
    T, D = x2.shape
    N = w_in_b.shape[1]
    tm, tn = 1024, 1536
    return pl.pallas_call(
        _inproj_kernel,
        grid=(T // tm, N // tn),
        in_specs=[pl.BlockSpec((tm, D), lambda i, j: (i, 0)),
                  pl.BlockSpec((1, N_MOD, D), lambda i, j: (i * tm // S, 0, 0)),
                  pl.BlockSpec((1, D), lambda i, j: (0, 0)),
                  pl.BlockSpec((D, tn), lambda i, j: (0, j))],
        out_specs=pl.BlockSpec((tm, tn), lambda i, j: (i, j)),
        out_shape=jax.ShapeDtypeStruct((T, N), BF16),
        scratch_shapes=[pltpu.VMEM((tm, D), BF16)],
        compiler_params=_cparams(("arbitrary", "arbitrary")),
        name="inproj",
    )(x2, mod, g_mix.reshape(1, D), w_in_b)


def _retention_kernel(q_ref, k_ref, v_ref, g_ref, cs_ref, gn_ref, o_ref, state_ref):
    C = RET_BLOCK

    @pl.when(pl.program_id(1) == 0)
    def _():
        state_ref[...] = jnp.zeros_like(state_ref)

    cs = cs_ref[...]
    csr = pltpu.roll(cs, RET_DK // 2, 1)
    first = lax.broadcasted_iota(jnp.int32, (C, RET_DK), 1) < RET_DK // 2
    cos2 = jnp.where(first, cs, csr)
    sin2 = jnp.where(first, -csr, cs)
    diff = (lax.broadcasted_iota(jnp.int32, (C, C), 0) - lax.broadcasted_iota(jnp.int32, (C, C), 1)).astype(F32)
    row = lax.broadcasted_iota(jnp.int32, (C, 1), 0).astype(F32)
    for h in range(RET_HEADS):
        lg = math.log1p(-(2.0 ** (-5.0 - h)))
        sl = slice(h * RET_DK, (h + 1) * RET_DK)
        qh = q_ref[:, sl].astype(F32)
        kh = k_ref[:, sl].astype(F32)
        vh = v_ref[:, sl]
        qr = (qh * cos2 + pltpu.roll(qh, RET_DK // 2, 1) * sin2).astype(BF16)
        kr = (kh * cos2 + pltpu.roll(kh, RET_DK // 2, 1) * sin2) * (RET_DK ** -0.5)
        decay = jnp.where(diff >= 0.0, jnp.exp(lg * jnp.maximum(diff, 0.0)), 0.0)
        s = lax.dot_general(qr, kr.astype(BF16), (((1,), (1,)), ((), ())),
                            preferred_element_type=F32) * decay
        o = jnp.dot(s.astype(BF16), vh, preferred_element_type=F32)
        st = state_ref[h]
        o = o + jnp.dot(qr, st.astype(BF16), preferred_element_type=F32) * jnp.exp(lg * (row + 1.0))
        kd = (kr * jnp.exp(lg * (C - 1.0 - row))).astype(BF16)
        kv = lax.dot_general(kd, vh, (((0,), (0,)), ((), ())), preferred_element_type=F32)
        state_ref[h] = math.exp(lg * C) * st + kv
        mu = jnp.mean(o, axis=-1, keepdims=True)
        oc = o - mu
        var = jnp.mean(oc * oc, axis=-1, keepdims=True)
        on = oc * lax.rsqrt(var + EPS) * gn_ref[:, sl]
        gh = g_ref[:, sl].astype(F32)
        o_ref[:, sl] = (on * (gh * jax.nn.sigmoid(gh))).astype(BF16)


def _retention(proj, cs_tab, ret_gn, B, S):
    T = B * S
    C = RET_BLOCK
    nblk = S // C

    def col(c):
        return pl.BlockSpec((C, RET_W), lambda b, i, c=c: (b * nblk + i, c))

    return pl.pallas_call(
        _retention_kernel,
        grid=(B, nblk),
        in_specs=[col(COL_RQ), col(COL_RK), col(COL_RV), col(COL_RG),
                  pl.BlockSpec((C, LANES), lambda b, i: (b * nblk + i, 0)),
                  pl.BlockSpec((1, RET_W), lambda b, i: (0, 0))],
        out_specs=pl.BlockSpec((C, RET_W), lambda b, i: (b * nblk + i, 0)),
        out_shape=jax.ShapeDtypeStruct((T, RET_W), BF16),
        scratch_shapes=[pltpu.VMEM((RET_HEADS, RET_DK, RET_DK), F32)],
        compiler_params=_cparams(("arbitrary", "arbitrary")),
        name="retention",
    )(proj, proj, proj, proj, cs_tab, ret_gn.reshape(1, RET_W))


def _conv_kernel(a_ref, b_ref, w_ref, bias_ref, lg_ref, lb_ref, o_ref, z_ref, zs_ref):
    tm = CONV_TILE
    H = CONV_HIST
    SUB = 8

    @pl.when(pl.program_id(1) == 0)
    def _():
        z_ref[0:H, :] = jnp.zeros((H, CONV_W), F32)

    a = a_ref[...].astype(F32)
    b = b_ref[...].astype(F32)
    z_ref[H:H + tm, :] = a * jax.nn.sigmoid(b)
    base = H - (CONV_K - 1)
    n_shift = tm + (base + CONV_K - 2) // SUB * SUB
    for r in range(1, SUB):
        for c0 in range(0, n_shift, 128):
            cs = min(128, n_shift - c0)
            zs_ref[r - 1, c0:c0 + cs, :] = z_ref[c0 + r:c0 + r + cs, :]
    for r0 in range(0, tm, CONV_ROWS):
        acc = jnp.zeros((CONV_ROWS, CONV_W), F32)
        for j in range(CONV_K):
            al, r = (base + j) // SUB * SUB, (base + j) % SUB
            if r == 0:
                tap = z_ref[r0 + al:r0 + al + CONV_ROWS, :]
            else:
                tap = zs_ref[r - 1, r0 + al:r0 + al + CONV_ROWS, :]
            acc = acc + tap * w_ref[j:j + 1, :]
        acc = acc + bias_ref[...]
        mu = jnp.mean(acc, axis=-1, keepdims=True)
        ac = acc - mu
        var = jnp.mean(ac * ac, axis=-1, keepdims=True)
        y = ac * lax.rsqrt(var + EPS) * lg_ref[...] + lb_ref[...]
        o_ref[r0:r0 + CONV_ROWS, :] = (y * jax.nn.sigmoid(y)).astype(BF16)
    z_ref[0:H, :] = z_ref[tm:tm + H, :]


def _conv(proj, conv_w, conv_b, ln_g, ln_b, B, S):
    T = B * S
    tm = CONV_TILE
    nblk = S // tm
    vec = pl.BlockSpec((1, CONV_W), lambda b, i: (0, 0))
    return pl.pallas_call(
        _conv_kernel,
        grid=(B, nblk),
        in_specs=[pl.BlockSpec((tm, CONV_W), lambda b, i: (b * nblk + i, COL_CA)),
                  pl.BlockSpec((tm, CONV_W), lambda b, i: (b * nblk + i, COL_CB)),
                  pl.BlockSpec((CONV_K, CONV_W), lambda b, i: (0, 0)),
                  vec, vec, vec],
        out_specs=pl.BlockSpec((tm, CONV_W), lambda b, i: (b * nblk + i, 0)),
        out_shape=jax.ShapeDtypeStruct((T, CONV_W), BF16),
        scratch_shapes=[pltpu.VMEM((CONV_HIST + tm, CONV_W), F32),
                        pltpu.VMEM((7, CONV_HIST + tm, CONV_W), F32)],
        compiler_params=_cparams(("arbitrary", "arbitrary")),
        name="conv",
    )(proj, proj, conv_w.reshape(CONV_K, CONV_W), conv_b.reshape(1, CONV_W),
      ln_g.reshape(1, CONV_W), ln_b.reshape(1, CONV_W))


def _head_mean_sq(x, hm_ref):
    sq = x * x
    hi = sq.astype(BF16)
    lo = (sq - hi.astype(F32)).astype(BF16)
    tot = jnp.dot(hi, hm_ref[...], preferred_element_type=F32) + jnp.dot(lo, hm_ref[...], preferred_element_type=F32)
    return tot * (1.0 / ATT_DH)


def _attn_kernel(q_ref, k_ref, v_ref, rrow_ref, qg_ref, kg_ref, hm_ref, o_ref, kn_ref, vs_ref, bm_ref, *, S):
    tq = ATT_QBLOCK
    qi = pl.program_id(1)
    heads = [slice(h * ATT_DH, (h + 1) * ATT_DH) for h in range(ATT_HEADS)]

    @pl.when(qi == 0)
    def _():
        r = lax.broadcasted_iota(jnp.int32, (tq, ATT_BAND), 0)
        u = lax.broadcasted_iota(jnp.int32, (tq, ATT_BAND), 1)
        off = u - (r // CHUNK) * CHUNK
        inband = (off >= 0) & (off < (ATT_LEFT_CHUNKS + 1) * CHUNK)
        for h in range(ATT_HEADS):
            gen = jnp.broadcast_to(rrow_ref[h], (tq, rrow_ref.shape[-1]))
            t = pltpu.roll(gen, 0, 1, stride=1, stride_axis=0)
            bm_ref[h] = jnp.where(inband, t[:, :ATT_BAND], NEG_INF)
        rows = 512
        for h in range(ATT_HEADS):
            kn_ref[h, 0:ATT_PAD, :] = jnp.zeros((ATT_PAD, ATT_DH), BF16)
            vs_ref[h, 0:ATT_PAD, :] = jnp.zeros((ATT_PAD, ATT_DH), BF16)
        for r0 in range(0, S, rows):
            k = k_ref[r0:r0 + rows, :].astype(F32)
            kn = (k * lax.rsqrt(_head_mean_sq(k, hm_ref) + EPS) * kg_ref[...]).astype(BF16)
            v = v_ref[r0:r0 + rows, :]
            for h in range(ATT_HEADS):
                kn_ref[h, ATT_PAD + r0:ATT_PAD + r0 + rows, :] = kn[:, heads[h]]
                vs_ref[h, ATT_PAD + r0:ATT_PAD + r0 + rows, :] = v[:, heads[h]]

    start = pl.multiple_of(qi * tq, tq)
    q = q_ref[...].astype(F32)
    qn = (q * lax.rsqrt(_head_mean_sq(q, hm_ref) + EPS) * qg_ref[...]).astype(BF16)

    def attend(mask_pad):
        if mask_pad:
            key_pos = lax.broadcasted_iota(jnp.int32, (ATT_SROWS, ATT_BAND), 1) + start
            valid = key_pos >= ATT_PAD
        outs = []
        for h in range(ATT_HEADS):
            kb = kn_ref[h, pl.ds(start, ATT_BAND), :]
            vb = vs_ref[h, pl.ds(start, ATT_BAND), :]
            s = lax.dot_general(qn[:, heads[h]], kb, (((1,), (1,)), ((), ())), preferred_element_type=F32)
            ps, ls = [], []
            for r0 in range(0, tq, ATT_SROWS):
                ss = s[r0:r0 + ATT_SROWS] + bm_ref[h, r0:r0 + ATT_SROWS, :]
                if mask_pad:
                    ss = jnp.where(valid, ss, NEG_INF)
                m = jnp.max(ss, axis=-1, keepdims=True)
                p = jnp.exp2(ss - m)
                ls.append(jnp.sum(p, axis=-1, keepdims=True))
                ps.append(p.astype(BF16))
            o = jnp.dot(jnp.concatenate(ps, axis=0), vb, preferred_element_type=F32)
            outs.append(o / jnp.concatenate(ls, axis=0))
        o_ref[...] = jnp.concatenate(outs, axis=-1).astype(BF16)

    first = ATT_PAD // tq

    @pl.when(qi < first)
    def _():
        attend(True)

    @pl.when(qi >= first)
    def _():
        attend(False)


def _attn_bias_row(rel_table):
    t = rel_table.astype(F32) * math.log2(math.e)
    edge = jnp.broadcast_to(t[:, 2 * MAX_REL:], (ATT_HEADS, ATT_QBLOCK))
    row = jnp.concatenate([edge, jnp.flip(t[:, 1:], axis=1), edge], axis=1)
    return row.reshape(ATT_HEADS, 1, 2 * ATT_QBLOCK + 2 * MAX_REL)


def _attention(proj, rel_table, q_gain, k_gain, B, S):
    T = B * S
    tq = ATT_QBLOCK
    nblk = S // tq
    gen = _attn_bias_row(rel_table)
    qg = jnp.tile(q_gain.astype(F32) * (ATT_DH ** -0.5 * math.log2(math.e)), ATT_HEADS).reshape(1, ATT_W)
    kg = jnp.tile(k_gain.astype(F32), ATT_HEADS).reshape(1, ATT_W)
    lane_head = jnp.arange(ATT_W) // ATT_DH
    hm = (lane_head[:, None] == lane_head[None, :]).astype(BF16)
    vec = pl.BlockSpec((1, ATT_W), lambda b, i: (0, 0))
    return pl.pallas_call(
        functools.partial(_attn_kernel, S=S),
        grid=(B, nblk),
        in_specs=[pl.BlockSpec((tq, ATT_W), lambda b, i: (b * nblk + i, COL_AQ)),
                  pl.BlockSpec((S, ATT_W), lambda b, i: (b, COL_AK)),
                  pl.BlockSpec((S, ATT_W), lambda b, i: (b, COL_AV)),
                  pl.BlockSpec(gen.shape, lambda b, i: (0, 0, 0)),
                  vec, vec,
                  pl.BlockSpec((ATT_W, ATT_W), lambda b, i: (0, 0))],
        out_specs=pl.BlockSpec((tq, ATT_W), lambda b, i: (b * nblk + i, 0)),
        out_shape=jax.ShapeDtypeStruct((T, ATT_W), BF16),
        scratch_shapes=[pltpu.VMEM((ATT_HEADS, ATT_PAD + S, ATT_DH), BF16),
                        pltpu.VMEM((ATT_HEADS, ATT_PAD + S, ATT_DH), BF16),
                        pltpu.VMEM((ATT_HEADS, tq, ATT_BAND), F32)],
        compiler_params=_cparams(("arbitrary", "arbitrary")),
        name="chunk_attention",
    )(proj, proj, proj, gen, qg, kg, hm)


def _merge_kernel(x_ref, ret_ref, conv_ref, att_ref, g0_ref, g1_ref, g2_ref, bg_ref, mod_ref,
                  wr_ref, wc_ref, wa_ref, wo_ref, gf_ref, wrt_ref, brt_ref,
                  xo_ref, h2_ref, route_ref, cnt_out_ref, tri_ref, cnt_ref):
    D = D_MODEL

    @pl.when(pl.program_id(0) == 0)
    def _():
        n = tri_ref.shape[0]
        tri_ref[...] = (lax.broadcasted_iota(jnp.int32, (n, n), 1)
                        < lax.broadcasted_iota(jnp.int32, (n, n), 0)).astype(BF16)
        cnt_ref[...] = jnp.zeros_like(cnt_ref)

    def gate(g_ref, k):
        return jax.nn.sigmoid(g_ref[...].astype(F32) + bg_ref[:, k * D:(k + 1) * D])

    merged = gate(g0_ref, 0) * jnp.dot(ret_ref[...], wr_ref[...], preferred_element_type=F32)
    merged = merged + gate(g1_ref, 1) * jnp.dot(conv_ref[...], wc_ref[...], preferred_element_type=F32)
    merged = merged + gate(g2_ref, 2) * jnp.dot(att_ref[...], wa_ref[...], preferred_element_type=F32)
    y = jnp.dot(merged.astype(BF16), wo_ref[...], preferred_element_type=F32)
    x = x_ref[...] + mod_ref[0, 2:3, :] * y
    xo_ref[...] = x

    h2 = x * lax.rsqrt(jnp.mean(x * x, axis=-1, keepdims=True) + EPS) * gf_ref[...]
    h2 = h2 * (1.0 + mod_ref[0, 4:5, :]) + mod_ref[0, 3:4, :]
    h2_ref[...] = h2
    h_hi = h2.astype(BF16)
    h_lo = (h2 - h_hi.astype(F32)).astype(BF16)
    both = jnp.dot(h_hi, wrt_ref[...], preferred_element_type=F32)
    logits = (both[:, :LANES] + both[:, LANES:]
              + jnp.dot(h_lo, wrt_ref[:, :LANES], preferred_element_type=F32) + brt_ref[...])
    tm = logits.shape[0]
    lane = lax.broadcasted_iota(jnp.int32, (tm, LANES), 1)
    big = jnp.int32(LANES)
    is_g = (lane >= N_EXPERTS) & (lane < N_EXPERTS + N_GROUPS)
    gl = jnp.where(is_g, logits, -jnp.inf)
    gmax = jnp.max(gl, axis=-1, keepdims=True)
    gsel = jnp.min(jnp.where(gl == gmax, lane - N_EXPERTS, big), axis=-1, keepdims=True)
    p_group = 1.0 / jnp.sum(jnp.where(is_g, jnp.exp(gl - gmax), 0.0), axis=-1, keepdims=True)
    in_grp = (lane >= gsel * EXPERTS_PER_GROUP) & (lane < (gsel + 1) * EXPERTS_PER_GROUP)
    ch = jnp.where(in_grp, logits, -jnp.inf)
    v1 = jnp.max(ch, axis=-1, keepdims=True)
    i1 = jnp.min(jnp.where(ch == v1, lane, big), axis=-1, keepdims=True)
    ch2 = jnp.where(lane == i1, -jnp.inf, ch)
    v2 = jnp.max(ch2, axis=-1, keepdims=True)
    i2 = jnp.min(jnp.where(ch2 == v2, lane, big), axis=-1, keepdims=True)
    e = jnp.exp(v2 - v1)
    w1 = p_group / (1.0 + e)
    w2 = p_group * e / (1.0 + e)
    sel = ((lane == i1) | (lane == i2)).astype(BF16)
    before = jnp.dot(tri_ref[...], sel, preferred_element_type=F32) + cnt_ref[...]
    rank1 = jnp.sum(jnp.where(lane == i1, before, 0.0), axis=-1, keepdims=True)
    rank2 = jnp.sum(jnp.where(lane == i2, before, 0.0), axis=-1, keepdims=True)
    cnt_ref[...] += jnp.sum(sel.astype(F32), axis=0, keepdims=True)
    cnt_out_ref[...] = cnt_ref[...]
    vals = (i1.astype(F32), i2.astype(F32), w1, w2, rank1, rank2)
    route = jnp.zeros((tm, LANES), F32)
    for k, v in enumerate(vals):
        route = jnp.where(lane == k, v, route)
    route_ref[...] = route


def _merge(x2, proj, ret, conv, att, b_gate, mod, wr, wc, wa, wo, g_ffn, w_rt, b_rt, S):
    T, D = x2.shape
    tm = 512
    row = lambda i: (i, 0)
    const = lambda i: (0, 0)
    br = pl.BlockSpec((tm, RET_W), row)
    return pl.pallas_call(
        _merge_kernel,
        grid=(T // tm,),
        in_specs=[pl.BlockSpec((tm, D), row), br, br, br,
                  pl.BlockSpec((tm, D), lambda i: (i, COL_GATE)),
                  pl.BlockSpec((tm, D), lambda i: (i, COL_GATE + 1)),
                  pl.BlockSpec((tm, D), lambda i: (i, COL_GATE + 2)),
                  pl.BlockSpec((1, N_BRANCH * D), const),
                  pl.BlockSpec((1, N_MOD, D), lambda i: (i * tm // S, 0, 0)),
                  pl.BlockSpec((RET_W, D), const), pl.BlockSpec((CONV_W, D), const),
                  pl.BlockSpec((ATT_W, D), const), pl.BlockSpec((D, D), const),
                  pl.BlockSpec((1, D), const),
                  pl.BlockSpec((D, 2 * LANES), const), pl.BlockSpec((1, LANES), const)],
        out_specs=[pl.BlockSpec((tm, D), row), pl.BlockSpec((tm, D), row),
                   pl.BlockSpec((tm, LANES), row), pl.BlockSpec((1, LANES), const)],
        out_shape=[jax.ShapeDtypeStruct((T, D), F32), jax.ShapeDtypeStruct((T, D), F32),
                   jax.ShapeDtypeStruct((T, LANES), F32), jax.ShapeDtypeStruct((1, LANES), F32)],
        scratch_shapes=[pltpu.VMEM((tm, tm), BF16), pltpu.VMEM((1, LANES), F32)],
        compiler_params=_cparams(("arbitrary",)),
        name="merge_router",
    )(x2, ret, conv, att, proj, proj, proj, b_gate.reshape(1, N_BRANCH * D), mod,
      wr, wc, wa, wo, g_ffn.reshape(1, D), w_rt, b_rt)


def _ffn_kernel(te_ref, nv_ref, scur_ref, snext_ref, h_hbm, wu_ref, wd_ref, o_ref, xbuf, sem, *, tr, n_tiles):
    i = pl.program_id(0)
    nv = nv_ref[0]
    slot = i % 2
    F = EXPERT_HIDDEN

    def gather(s_ref, dst_slot, lo, hi):
        for j in range(lo, hi):
            pltpu.make_async_copy(h_hbm.at[pl.ds(s_ref[j], 1), :],
                                  xbuf.at[dst_slot, pl.ds(j, 1), :], sem.at[dst_slot]).start()

    def wait(s):
        pltpu.make_async_copy(h_hbm.at[pl.ds(0, tr), :], xbuf.at[s], sem.at[s]).wait()

    @pl.when(i == 0)
    def _():
        gather(scur_ref, 0, 0, tr)

    @pl.when(i < nv)
    def _():
        wait(slot)
        x = xbuf[slot].astype(BF16)
        n_up, n_down = 2, D_MODEL // F
        cuts = [(c * tr) // (n_up + n_down) for c in range(n_up + n_down + 1)]
        hid = []
        for c in range(n_up):
            gather(snext_ref, 1 - slot, cuts[c], cuts[c + 1])
            hid.append(jnp.dot(x, wu_ref[0, :, c * F:(c + 1) * F], preferred_element_type=F32))
        act = (hid[0] * jax.nn.sigmoid(hid[0]) * hid[1]).astype(BF16)
        for c in range(n_down):
            gather(snext_ref, 1 - slot, cuts[n_up + c], cuts[n_up + c + 1])
            o_ref[:, c * F:(c + 1) * F] = jnp.dot(act, wd_ref[0, :, c * F:(c + 1) * F],
                                                  preferred_element_type=F32)

    @pl.when(i == nv)
    def _():
        wait(slot)

    @pl.when((i >= nv) & (i < n_tiles))
    def _():
        o_ref[...] = jnp.zeros_like(o_ref)


def _ffn(h2, src_tok, tile_expert, n_valid, w_up_b, w_down_b):
    T, D = h2.shape
    tr = MOE_TILE
    n_pad = src_tok.shape[0]
    n_tiles = n_pad // tr
    F2 = w_up_b.shape[-1]
    last = n_tiles - 1
    grid_spec = pltpu.PrefetchScalarGridSpec(
        num_scalar_prefetch=2,
        grid=(n_tiles + 1,),
        in_specs=[pl.BlockSpec((tr,), lambda i, te, nv: (jnp.minimum(i, last),), memory_space=pltpu.SMEM),
                  pl.BlockSpec((tr,), lambda i, te, nv: (jnp.minimum(i + 1, last),), memory_space=pltpu.SMEM),
                  pl.BlockSpec(memory_space=pl.ANY),
                  pl.BlockSpec((1, D, F2), lambda i, te, nv: (te[jnp.minimum(i, last)], 0, 0)),
                  pl.BlockSpec((1, EXPERT_HIDDEN, D), lambda i, te, nv: (te[jnp.minimum(i, last)], 0, 0))],
        out_specs=pl.BlockSpec((tr, D), lambda i, te, nv: (jnp.minimum(i, last), 0)),
        scratch_shapes=[pltpu.VMEM((2, tr, D), F32), pltpu.SemaphoreType.DMA((2,))],
    )
    return pl.pallas_call(
        functools.partial(_ffn_kernel, tr=tr, n_tiles=n_tiles),
        grid_spec=grid_spec,
        out_shape=jax.ShapeDtypeStruct((n_pad, D), F32),
        compiler_params=_cparams(("arbitrary",)),
        name="moe_experts",
    )(tile_expert, n_valid, src_tok, src_tok, h2, w_up_b, w_down_b)


def _combine_kernel(dcur_ref, dnext_ref, x_ref, route_ref, mod_ref, ye_hbm, o_ref, buf, sem, *, rows, nsteps):
    i = pl.program_id(0)

    def issue(d_ref, slot):
        def body(j, carry):
            for k in range(2):
                pltpu.make_async_copy(ye_hbm.at[pl.ds(d_ref[2 * j + k], 1), :],
                                      buf.at[slot, k, pl.ds(j, 1), :], sem.at[slot, k]).start()
            return carry

        lax.fori_loop(0, rows, body, 0, unroll=8)

    @pl.when(i == 0)
    def _():
        issue(dcur_ref, 0)

    @pl.when(i + 1 < nsteps)
    def _():
        issue(dnext_ref, (i + 1) % 2)

    slot = i % 2
    for k in range(2):
        pltpu.make_async_copy(ye_hbm.at[pl.ds(0, rows), :], buf.at[slot, k], sem.at[slot, k]).wait()
    r = route_ref[...]
    y = r[:, 2:3] * buf[slot, 0] + r[:, 3:4] * buf[slot, 1]
    o_ref[...] = x_ref[...] + mod_ref[0, 5:6, :] * y


def _combine(x2, route, mod, ye, dest, S):
    T, D = x2.shape
    rows = 256
    nsteps = T // rows
    return pl.pallas_call(
        functools.partial(_combine_kernel, rows=rows, nsteps=nsteps),
        grid=(nsteps,),
        in_specs=[pl.BlockSpec((2 * rows,), lambda i: (i,), memory_space=pltpu.SMEM),
                  pl.BlockSpec((2 * rows,), lambda i: (jnp.minimum(i + 1, nsteps - 1),),
                               memory_space=pltpu.SMEM),
                  pl.BlockSpec((rows, D), lambda i: (i, 0)),
                  pl.BlockSpec((rows, LANES), lambda i: (i, 0)),
                  pl.BlockSpec((1, N_MOD, D), lambda i: (i * rows // S, 0, 0)),
                  pl.BlockSpec(memory_space=pl.ANY)],
        out_specs=pl.BlockSpec((rows, D), lambda i: (i, 0)),
        out_shape=jax.ShapeDtypeStruct((T, D), F32),
        scratch_shapes=[pltpu.VMEM((2, 2, rows, D), F32), pltpu.SemaphoreType.DMA((2, 2))],
        compiler_params=_cparams(("arbitrary",)),
        name="moe_combine",
    )(dest, dest, x2, route, mod, ye)


def _routing_tables(route, counts, n_tiles):
    tr = MOE_TILE
    experts = jnp.arange(N_EXPERTS, dtype=jnp.int32)
    counts = counts[0, :N_EXPERTS].astype(jnp.int32)
    padded = ((counts + tr - 1) // tr) * tr
    ends = jnp.cumsum(padded)
    offs = ends - padded
    ids = route[:, 0:2].astype(jnp.int32)
    slot = route[:, 4:6].astype(jnp.int32)
    base = jnp.sum(jnp.where(ids[:, :, None] == experts[None, None, :], offs[None, None, :], 0), axis=-1)
    dest = (base + slot).reshape(-1)
    tile_start = jnp.arange(n_tiles, dtype=jnp.int32) * tr
    tile_expert = jnp.sum((tile_start[:, None] >= ends[None, :]).astype(jnp.int32), axis=-1)
    tile_expert = jnp.minimum(tile_expert, N_EXPERTS - 1)
    n_valid = (ends[-1] // tr).astype(jnp.int32).reshape(1)
    n_pairs = dest.shape[0]
    src_tok = jnp.zeros((n_tiles * tr,), jnp.int32).at[dest].set(
        jnp.arange(n_pairs, dtype=jnp.int32) // 2, unique_indices=True)
    return dest, src_tok, tile_expert, n_valid


def _reorder_in_cols(w_in):
    cuts = 4 * RET_W + 2 * CONV_W + 3 * ATT_W
    return jnp.concatenate([w_in[..., cuts:], w_in[..., :cuts]], axis=-1)


def kernel(x, c, positions, w_ada, b_ada, g_mix, g_ffn, w_in, b_gate, ret_gn, conv_w, conv_b, conv_ln_g, conv_ln_b, att_q_gain, att_k_gain, att_rel_bias, w_ret_out, w_conv_out, w_att_out, w_out, w_group, b_group, w_inner, b_inner, w_up, w_down):
    B, S, D = x.shape
    L = w_ada.shape[0]
    T = B * S
    assert D == D_MODEL and S % CONV_TILE == 0 and T % 2048 == 0
    n_pad = 2 * T + N_EXPERTS * MOE_TILE
    n_tiles = n_pad // MOE_TILE

    mod_all = _adaln(c, w_ada, b_ada).reshape(L, B, N_MOD, D)
    cs_tab = _rope_table(positions)
    x2 = x.reshape(T, D)
    for l in range(L):
        mod = mod_all[l]
        proj = _inproj(x2, mod, g_mix[l], _reorder_in_cols(w_in[l]).astype(BF16), S)
        ret = _retention(proj, cs_tab, ret_gn[l], B, S)
        conv = _conv(proj, conv_w[l], conv_b[l], conv_ln_g[l], conv_ln_b[l], B, S)
        att = _attention(proj, att_rel_bias[l], att_q_gain[l], att_k_gain[l], B, S)
        w_rt = jnp.zeros((D, LANES), F32)
        w_rt = w_rt.at[:, :N_EXPERTS].set(w_inner[l]).at[:, N_EXPERTS:N_EXPERTS + N_GROUPS].set(w_group[l])
        b_rt = jnp.zeros((1, LANES), F32)
        b_rt = b_rt.at[0, :N_EXPERTS].set(b_inner[l]).at[0, N_EXPERTS:N_EXPERTS + N_GROUPS].set(b_group[l])
        w_rt_hi = w_rt.astype(BF16)
        w_rt2 = jnp.concatenate([w_rt_hi, (w_rt - w_rt_hi.astype(F32)).astype(BF16)], axis=1)
        x2, h2, route, counts = _merge(x2, proj, ret, conv, att, b_gate[l], mod,
                                       w_ret_out[l].astype(BF16), w_conv_out[l].astype(BF16),
                                       w_att_out[l].astype(BF16), w_out[l].astype(BF16),
                                       g_ffn[l], w_rt2, b_rt, S)
        dest, src_tok, tile_expert, n_valid = _routing_tables(route, counts, n_tiles)
        ye = _ffn(h2, src_tok, tile_expert, n_valid,
                  w_up[l].reshape(N_EXPERTS, D, 2 * EXPERT_HIDDEN).astype(BF16),
                  w_down[l].reshape(N_EXPERTS, EXPERT_HIDDEN, D).astype(BF16))
        x2 = _combine(x2, route, mod, ye, dest, S)
    return x2.reshape(B, S, D)
```

```python
import functools
import math

import jax
import jax.numpy as jnp
from jax import lax
from jax.experimental import pallas as pl
from jax.experimental.pallas import tpu as pltpu

F32 = jnp.float32
BF16 = jnp.bfloat16

D_MODEL = 1024
CHUNK = 64
EPS = 1e-6
NEG_INF = -1e30
RET_HEADS = 4
RET_DK = 128
RET_W = 512
ROPE_BASE = 10000.0
CONV_W = 512
CONV_K = 31
ATT_HEADS = 8
ATT_DH = 64
ATT_W = 512
ATT_LEFT_CHUNKS = 8
MAX_REL = 256
N_BRANCH = 3
IN_COLS = 4 * RET_W + 2 * CONV_W + 3 * ATT_W + N_BRANCH * D_MODEL
N_GROUPS = 4
EXPERTS_PER_GROUP = 8
N_EXPERTS = N_GROUPS * EXPERTS_PER_GROUP
EXPERT_HIDDEN = 256
N_MOD = 6

LANES = 128
V7X_VMEM_LIMIT_BYTES = 56 * 1024 * 1024

COL_GATE = 0
COL_RQ, COL_RK, COL_RV, COL_RG = 6, 7, 8, 9
COL_CA, COL_CB = 10, 11
COL_AQ, COL_AK, COL_AV = 12, 13, 14

RET_BLOCK = 256
ATT_QBLOCK = 256
ATT_BAND = ATT_QBLOCK + ATT_LEFT_CHUNKS * CHUNK
ATT_PAD = ATT_LEFT_CHUNKS * CHUNK
ATT_SROWS = 64
CONV_TILE = 512
CONV_HIST = 32
CONV_ROWS = 64
MOE_TILE = 512
ROUTE_TILE = 512
ROW_ALIGN = 16
SORT_SLOTS = 2 * ROUTE_TILE + N_EXPERTS * ROW_ALIGN
XS_W = D_MODEL + LANES


def _cparams(sem):
    return pltpu.CompilerParams(dimension_semantics=sem, vmem_limit_bytes=V7X_VMEM_LIMIT_BYTES)


def _adaln_kernel(c_ref, w_ref, b_ref, o_ref):
    c = c_ref[...]
    cs = c * jax.nn.sigmoid(c)
    o_ref[0] = jnp.dot(cs, w_ref[0], preferred_element_type=F32,
                       precision=lax.Precision.HIGHEST) + b_ref[0]


def _adaln(c, w_ada, b_ada):
    L, D, N = w_ada.shape
    B = c.shape[0]
    tn = 1536
    return pl.pallas_call(
        _adaln_kernel,
        grid=(L, N // tn),
        in_specs=[pl.BlockSpec((B, D), lambda l, j: (0, 0)),
                  pl.BlockSpec((1, D, tn), lambda l, j: (l, 0, j)),
                  pl.BlockSpec((1, 1, tn), lambda l, j: (l, 0, j))],
        out_specs=pl.BlockSpec((1, B, tn), lambda l, j: (l, 0, j)),
        out_shape=jax.ShapeDtypeStruct((L, B, N), F32),
        compiler_params=_cparams(("arbitrary", "arbitrary")),
        name="adaln",
    )(c, w_ada, b_ada.reshape(L, 1, N))


def _rope_kernel(pos_ref, inv_ref, ph_ref, o_ref):
    o_ref[...] = jnp.cos(pos_ref[...] * inv_ref[...] - ph_ref[...])


def _rope_table(positions):
    T = positions.size
    half = RET_DK // 2
    inv = ROPE_BASE ** (-jnp.arange(half, dtype=F32) / half)
    inv2 = jnp.concatenate([inv, inv]).reshape(1, LANES)
    phase = jnp.concatenate([jnp.zeros((half,), F32), jnp.full((half,), math.pi / 2, F32)]).reshape(1, LANES)
    posb = jnp.broadcast_to(positions.reshape(T, 1).astype(F32), (T, LANES))
    tm = 2048
    return pl.pallas_call(
        _rope_kernel,
        grid=(T // tm,),
        in_specs=[pl.BlockSpec((tm, LANES), lambda i: (i, 0)),
                  pl.BlockSpec((1, LANES), lambda i: (0, 0)),
                  pl.BlockSpec((1, LANES), lambda i: (0, 0))],
        out_specs=pl.BlockSpec((tm, LANES), lambda i: (i, 0)),
        out_shape=jax.ShapeDtypeStruct((T, LANES), F32),
        compiler_params=_cparams(("arbitrary",)),
        name="rope_table",
    )(posb, inv2, phase)


def _inproj_kernel(x_ref, mod_ref, g_ref, w_ref, o_ref, h_ref):
    @pl.when(pl.program_id(1) == 0)
    def _():
        x = x_ref[...]
        y = x * lax.rsqrt(jnp.mean(x * x, axis=-1, keepdims=True) + EPS) * g_ref[...]
        h = y * (1.0 + mod_ref[0, 1:2, :]) + mod_ref[0, 0:1, :]
        h_ref[...] = h.astype(BF16)

    o_ref[...] = jnp.dot(h_ref[...], w_ref[...], preferred_element_type=F32).astype(BF16)


def _inproj(x2, mod, g_mix, w_in_b, S):
    T, D = x2.shape
    N = w_in_b.shape[1]
    tm, tn = 1024, 1536
    return pl.pallas_call(
        _inproj_kernel,
        grid=(T // tm, N // tn),
        in_specs=[pl.BlockSpec((tm, D), lambda i, j: (i, 0)),
                  pl.BlockSpec((1, N_MOD, D), lambda i, j: (i * tm // S, 0, 0)),
                  pl.BlockSpec((1, D), lambda i, j: (0, 0)),
                  pl.BlockSpec((D, tn), lambda i, j: (0, j))],
        out_specs=pl.BlockSpec((tm, tn), lambda i, j: (i, j)),
        out_shape=jax.ShapeDtypeStruct((T, N), BF16),
        scratch_shapes=[pltpu.VMEM((tm, D), BF16)],
        compiler_params=_cparams(("arbitrary", "arbitrary")),
        name="inproj",
    )(x2, mod, g_mix.reshape(1, D), w_in_b)


def _retention_kernel(q_ref, k_ref, v_ref, g_ref, cs_ref, gn_ref, o_ref, state_ref):
    C = RET_BLOCK

    @pl.when(pl.program_id(1) == 0)
    def _():
        state_ref[...] = jnp.zeros_like(state_ref)

    cs = cs_ref[...]
    csr = pltpu.roll(cs, RET_DK // 2, 1)
    first = lax.broadcasted_iota(jnp.int32, (C, RET_DK), 1) < RET_DK // 2
    cos2 = jnp.where(first, cs, csr)
    sin2 = jnp.where(first, -csr, cs)
    diff = (lax.broadcasted_iota(jnp.int32, (C, C), 0) - lax.broadcasted_iota(jnp.int32, (C, C), 1)).astype(F32)
    row = lax.broadcasted_iota(jnp.int32, (C, 1), 0).astype(F32)
    for h in range(RET_HEADS):
        lg = math.log1p(-(2.0 ** (-5.0 - h)))
        sl = slice(h * RET_DK, (h + 1) * RET_DK)
        qh = q_ref[:, sl].astype(F32)
        kh = k_ref[:, sl].astype(F32)
        vh = v_ref[:, sl]
        qr = (qh * cos2 + pltpu.roll(qh, RET_DK // 2, 1) * sin2).astype(BF16)
        kr = (kh * cos2 + pltpu.roll(kh, RET_DK // 2, 1) * sin2) * (RET_DK ** -0.5)
        decay = jnp.where(diff >= 0.0, jnp.exp(lg * jnp.maximum(diff, 0.0)), 0.0)
        s = lax.dot_general(qr, kr.astype(BF16), (((1,), (1,)), ((), ())),
                            preferred_element_type=F32) * decay
        o = jnp.dot(s.astype(BF16), vh, preferred_element_type=F32)
        st = state_ref[h]
        o = o + jnp.dot(qr, st.astype(BF16), preferred_element_type=F32) * jnp.exp(lg * (row + 1.0))
        kd = (kr * jnp.exp(lg * (C - 1.0 - row))).astype(BF16)
        kv = lax.dot_general(kd, vh, (((0,), (0,)), ((), ())), preferred_element_type=F32)
        state_ref[h] = math.exp(lg * C) * st + kv
        mu = jnp.mean(o, axis=-1, keepdims=True)
        oc = o - mu
        var = jnp.mean(oc * oc, axis=-1, keepdims=True)
        on = oc * lax.rsqrt(var + EPS) * gn_ref[:, sl]
        gh = g_ref[:, sl].astype(F32)
        o_ref[:, sl] = (on * (gh * jax.nn.sigmoid(gh))).astype(BF16)


def _retention(proj, cs_tab, ret_gn, B, S):
    T = B * S
    C = RET_BLOCK
    nblk = S // C

    def col(c):
        return pl.BlockSpec((C, RET_W), lambda b, i, c=c: (b * nblk + i, c))

    return pl.pallas_call(
        _retention_kernel,
        grid=(B, nblk),
        in_specs=[col(COL_RQ), col(COL_RK), col(COL_RV), col(COL_RG),
                  pl.BlockSpec((C, LANES), lambda b, i: (b * nblk + i, 0)),
                  pl.BlockSpec((1, RET_W), lambda b, i: (0, 0))],
        out_specs=pl.BlockSpec((C, RET_W), lambda b, i: (b * nblk + i, 0)),
        out_shape=jax.ShapeDtypeStruct((T, RET_W), BF16),
        scratch_shapes=[pltpu.VMEM((RET_HEADS, RET_DK, RET_DK), F32)],
        compiler_params=_cparams(("arbitrary", "arbitrary")),
        name="retention",
    )(proj, proj, proj, proj, cs_tab, ret_gn.reshape(1, RET_W))


def _conv_kernel(a_ref, b_ref, w_ref, bias_ref, lg_ref, lb_ref, o_ref, z_ref, zs_ref):
    tm = CONV_TILE
    H = CONV_HIST
    SUB = 8

    @pl.when(pl.program_id(1) == 0)
    def _():
        z_ref[0:H, :] = jnp.zeros((H, CONV_W), F32)

    a = a_ref[...].astype(F32)
    b = b_ref[...].astype(F32)
    z_ref[H:H + tm, :] = a * jax.nn.sigmoid(b)
    base = H - (CONV_K - 1)
    n_shift = tm + (base + CONV_K - 2) // SUB * SUB
    for r in range(1, SUB):
        for c0 in range(0, n_shift, 128):
            cs = min(128, n_shift - c0)
            zs_ref[r - 1, c0:c0 + cs, :] = z_ref[c0 + r:c0 + r + cs, :]
    for r0 in range(0, tm, CONV_ROWS):
        acc = jnp.zeros((CONV_ROWS, CONV_W), F32)
        for j in range(CONV_K):
            al, r = (base + j) // SUB * SUB, (base + j) % SUB
            if r == 0:
                tap = z_ref[r0 + al:r0 + al + CONV_ROWS, :]
            else:
                tap = zs_ref[r - 1, r0 + al:r0 + al + CONV_ROWS, :]
            acc = acc + tap * w_ref[j:j + 1, :]
        acc = acc + bias_ref[...]
        mu = jnp.mean(acc, axis=-1, keepdims=True)
        ac = acc - mu
        var = jnp.mean(ac * ac, axis=-1, keepdims=True)
        y = ac * lax.rsqrt(var + EPS) * lg_ref[...] + lb_ref[...]
        o_ref[r0:r0 + CONV_ROWS, :] = (y * jax.nn.sigmoid(y)).astype(BF16)
    z_ref[0:H, :] = z_ref[tm:tm + H, :]


def _conv(proj, conv_w, conv_b, ln_g, ln_b, B, S):
    T = B * S
    tm = CONV_TILE
    nblk = S // tm
    vec = pl.BlockSpec((1, CONV_W), lambda b, i: (0, 0))
    return pl.pallas_call(
        _conv_kernel,
        grid=(B, nblk),
        in_specs=[pl.BlockSpec((tm, CONV_W), lambda b, i: (b * nblk + i, COL_CA)),
                  pl.BlockSpec((tm, CONV_W), lambda b, i: (b * nblk + i, COL_CB)),
                  pl.BlockSpec((CONV_K, CONV_W), lambda b, i: (0, 0)),
                  vec, vec, vec],
        out_specs=pl.BlockSpec((tm, CONV_W), lambda b, i: (b * nblk + i, 0)),
        out_shape=jax.ShapeDtypeStruct((T, CONV_W), BF16),
        scratch_shapes=[pltpu.VMEM((CONV_HIST + tm, CONV_W), F32),
                        pltpu.VMEM((7, CONV_HIST + tm, CONV_W), F32)],
        compiler_params=_cparams(("arbitrary", "arbitrary")),
        name="conv",
    )(proj, proj, conv_w.reshape(CONV_K, CONV_W), conv_b.reshape(1, CONV_W),
      ln_g.reshape(1, CONV_W), ln_b.reshape(1, CONV_W))


def _head_mean_sq(x, hm_ref):
    sq = x * x
    hi = sq.astype(BF16)
    lo = (sq - hi.astype(F32)).astype(BF16)
    tot = jnp.dot(hi, hm_ref[...], preferred_element_type=F32) + jnp.dot(lo, hm_ref[...], preferred_element_type=F32)
    return tot * (1.0 / ATT_DH)


def _attn_kernel(q_ref, k_ref, v_ref, rrow_ref, qg_ref, kg_ref, hm_ref, o_ref, kn_ref, vs_ref, bm_ref, *, S):
    tq = ATT_QBLOCK
    qi = pl.program_id(1)
    heads = [slice(h * ATT_DH, (h + 1) * ATT_DH) for h in range(ATT_HEADS)]

    @pl.when(qi == 0)
    def _():
        r = lax.broadcasted_iota(jnp.int32, (tq, ATT_BAND), 0)
        u = lax.broadcasted_iota(jnp.int32, (tq, ATT_BAND), 1)
        off = u - (r // CHUNK) * CHUNK
        inband = (off >= 0) & (off < (ATT_LEFT_CHUNKS + 1) * CHUNK)
        for h in range(ATT_HEADS):
            gen = jnp.broadcast_to(rrow_ref[h], (tq, rrow_ref.shape[-1]))
            t = pltpu.roll(gen, 0, 1, stride=1, stride_axis=0)
            bm_ref[h] = jnp.where(inband, t[:, :ATT_BAND], NEG_INF)
        rows = 512
        for h in range(ATT_HEADS):
            kn_ref[h, 0:ATT_PAD, :] = jnp.zeros((ATT_PAD, ATT_DH), BF16)
            vs_ref[h, 0:ATT_PAD, :] = jnp.zeros((ATT_PAD, ATT_DH), BF16)
        for r0 in range(0, S, rows):
            k = k_ref[r0:r0 + rows, :].astype(F32)
            kn = (k * lax.rsqrt(_head_mean_sq(k, hm_ref) + EPS) * kg_ref[...]).astype(BF16)
            v = v_ref[r0:r0 + rows, :]
            for h in range(ATT_HEADS):
                kn_ref[h, ATT_PAD + r0:ATT_PAD + r0 + rows, :] = kn[:, heads[h]]
                vs_ref[h, ATT_PAD + r0:ATT_PAD + r0 + rows, :] = v[:, heads[h]]

    start = pl.multiple_of(qi * tq, tq)
    q = q_ref[...].astype(F32)
    qn = (q * lax.rsqrt(_head_mean_sq(q, hm_ref) + EPS) * qg_ref[...]).astype(BF16)

    def attend(mask_pad):
        if mask_pad:
            key_pos = lax.broadcasted_iota(jnp.int32, (ATT_SROWS, ATT_BAND), 1) + start
            valid = key_pos >= ATT_PAD
        outs = []
        for h in range(ATT_HEADS):
            kb = kn_ref[h, pl.ds(start, ATT_BAND), :]
            vb = vs_ref[h, pl.ds(start, ATT_BAND), :]
            s = lax.dot_general(qn[:, heads[h]], kb, (((1,), (1,)), ((), ())), preferred_element_type=F32)
            ps, ls = [], []
            for r0 in range(0, tq, ATT_SROWS):
                ss = s[r0:r0 + ATT_SROWS] + bm_ref[h, r0:r0 + ATT_SROWS, :]
                if mask_pad:
                    ss = jnp.where(valid, ss, NEG_INF)
                m = jnp.max(ss, axis=-1, keepdims=True)
                p = jnp.exp2(ss - m)
                ls.append(jnp.sum(p, axis=-1, keepdims=True))
                ps.append(p.astype(BF16))
            o = jnp.dot(jnp.concatenate(ps, axis=0), vb, preferred_element_type=F32)
            outs.append(o / jnp.concatenate(ls, axis=0))
        o_ref[...] = jnp.concatenate(outs, axis=-1).astype(BF16)

    first = ATT_PAD // tq

    @pl.when(qi < first)
    def _():
        attend(True)

    @pl.when(qi >= first)
    def _():
        attend(False)


def _attn_bias_row(rel_table):
    t = rel_table.astype(F32) * math.log2(math.e)
    edge = jnp.broadcast_to(t[:, 2 * MAX_REL:], (ATT_HEADS, ATT_QBLOCK))
    row = jnp.concatenate([edge, jnp.flip(t[:, 1:], axis=1), edge], axis=1)
    return row.reshape(ATT_HEADS, 1, 2 * ATT_QBLOCK + 2 * MAX_REL)


def _attention(proj, rel_table, q_gain, k_gain, B, S):
    T = B * S
    tq = ATT_QBLOCK
    nblk = S // tq
    gen = _attn_bias_row(rel_table)
    qg = jnp.tile(q_gain.astype(F32) * (ATT_DH ** -0.5 * math.log2(math.e)), ATT_HEADS).reshape(1, ATT_W)
    kg = jnp.tile(k_gain.astype(F32), ATT_HEADS).reshape(1, ATT_W)
    lane_head = jnp.arange(ATT_W) // ATT_DH
    hm = (lane_head[:, None] == lane_head[None, :]).astype(BF16)
    vec = pl.BlockSpec((1, ATT_W), lambda b, i: (0, 0))
    return pl.pallas_call(
        functools.partial(_attn_kernel, S=S),
        grid=(B, nblk),
        in_specs=[pl.BlockSpec((tq, ATT_W), lambda b, i: (b * nblk + i, COL_AQ)),
                  pl.BlockSpec((S, ATT_W), lambda b, i: (b, COL_AK)),
                  pl.BlockSpec((S, ATT_W), lambda b, i: (b, COL_AV)),
                  pl.BlockSpec(gen.shape, lambda b, i: (0, 0, 0)),
                  vec, vec,
                  pl.BlockSpec((ATT_W, ATT_W), lambda b, i: (0, 0))],
        out_specs=pl.BlockSpec((tq, ATT_W), lambda b, i: (b * nblk + i, 0)),
        out_shape=jax.ShapeDtypeStruct((T, ATT_W), BF16),
        scratch_shapes=[pltpu.VMEM((ATT_HEADS, ATT_PAD + S, ATT_DH), BF16),
                        pltpu.VMEM((ATT_HEADS, ATT_PAD + S, ATT_DH), BF16),
                        pltpu.VMEM((ATT_HEADS, tq, ATT_BAND), F32)],
        compiler_params=_cparams(("arbitrary", "arbitrary")),
        name="chunk_attention",
    )(proj, proj, proj, gen, qg, kg, hm)


def _merge_kernel(x_ref, ret_ref, conv_ref, att_ref, g0_ref, g1_ref, g2_ref, bg_ref, mod_ref,
                  wr_ref, wc_ref, wa_ref, wo_ref, gf_ref, wrt_ref, brt_ref,
                  xo_ref, h2_ref, route_ref, cnt_out_ref, tri_ref):
    D = D_MODEL

    @pl.when(pl.program_id(0) == 0)
    def _():
        n = tri_ref.shape[0]
        tri_ref[...] = (lax.broadcasted_iota(jnp.int32, (n, n), 1)
                        < lax.broadcasted_iota(jnp.int32, (n, n), 0)).astype(BF16)

    def gate(g_ref, k):
        return jax.nn.sigmoid(g_ref[...].astype(F32) + bg_ref[:, k * D:(k + 1) * D])

    merged = gate(g0_ref, 0) * jnp.dot(ret_ref[...], wr_ref[...], preferred_element_type=F32)
    merged = merged + gate(g1_ref, 1) * jnp.dot(conv_ref[...], wc_ref[...], preferred_element_type=F32)
    merged = merged + gate(g2_ref, 2) * jnp.dot(att_ref[...], wa_ref[...], preferred_element_type=F32)
    y = jnp.dot(merged.astype(BF16), wo_ref[...], preferred_element_type=F32)
    x = x_ref[...] + mod_ref[0, 2:3, :] * y
    xo_ref[...] = x

    h2 = x * lax.rsqrt(jnp.mean(x * x, axis=-1, keepdims=True) + EPS) * gf_ref[...]
    h2 = h2 * (1.0 + mod_ref[0, 4:5, :]) + mod_ref[0, 3:4, :]
    h2_ref[...] = h2.astype(BF16)
    h_hi = h2.astype(BF16)
    h_lo = (h2 - h_hi.astype(F32)).astype(BF16)
    both = jnp.dot(h_hi, wrt_ref[...], preferred_element_type=F32)
    logits = (both[:, :LANES] + both[:, LANES:]
              + jnp.dot(h_lo, wrt_ref[:, :LANES], preferred_element_type=F32) + brt_ref[...])
    tm = logits.shape[0]
    lane = lax.broadcasted_iota(jnp.int32, (tm, LANES), 1)
    big = jnp.int32(LANES)
    is_g = (lane >= N_EXPERTS) & (lane < N_EXPERTS + N_GROUPS)
    gl = jnp.where(is_g, logits, -jnp.inf)
    gmax = jnp.max(gl, axis=-1, keepdims=True)
    gsel = jnp.min(jnp.where(gl == gmax, lane - N_EXPERTS, big), axis=-1, keepdims=True)
    p_group = 1.0 / jnp.sum(jnp.where(is_g, jnp.exp(gl - gmax), 0.0), axis=-1, keepdims=True)
    in_grp = (lane >= gsel * EXPERTS_PER_GROUP) & (lane < (gsel + 1) * EXPERTS_PER_GROUP)
    ch = jnp.where(in_grp, logits, -jnp.inf)
    v1 = jnp.max(ch, axis=-1, keepdims=True)
    i1 = jnp.min(jnp.where(ch == v1, lane, big), axis=-1, keepdims=True)
    ch2 = jnp.where(lane == i1, -jnp.inf, ch)
    v2 = jnp.max(ch2, axis=-1, keepdims=True)
    i2 = jnp.min(jnp.where(ch2 == v2, lane, big), axis=-1, keepdims=True)
    e = jnp.exp(v2 - v1)
    w1 = p_group / (1.0 + e)
    w2 = p_group * e / (1.0 + e)
    sel = ((lane == i1) | (lane == i2)).astype(BF16)
    before = jnp.dot(tri_ref[...], sel, preferred_element_type=F32)
    rank1 = jnp.sum(jnp.where(lane == i1, before, 0.0), axis=-1, keepdims=True)
    rank2 = jnp.sum(jnp.where(lane == i2, before, 0.0), axis=-1, keepdims=True)
    cnt_out_ref[0] = jnp.sum(sel.astype(F32), axis=0, keepdims=True)
    vals = (i1.astype(F32), i2.astype(F32), w1, w2, rank1, rank2)
    route = jnp.zeros((tm, LANES), F32)
    for k, v in enumerate(vals):
        route = jnp.where(lane == k, v, route)
    route_ref[...] = route


def _merge(x2, proj, ret, conv, att, b_gate, mod, wr, wc, wa, wo, g_ffn, w_rt, b_rt, S):
    T, D = x2.shape
    tm = ROUTE_TILE
    row = lambda i: (i, 0)
    const = lambda i: (0, 0)
    br = pl.BlockSpec((tm, RET_W), row)
    return pl.pallas_call(
        _merge_kernel,
        grid=(T // tm,),
        in_specs=[pl.BlockSpec((tm, D), row), br, br, br,
                  pl.BlockSpec((tm, D), lambda i: (i, COL_GATE)),
                  pl.BlockSpec((tm, D), lambda i: (i, COL_GATE + 1)),
                  pl.BlockSpec((tm, D), lambda i: (i, COL_GATE + 2)),
                  pl.BlockSpec((1, N_BRANCH * D), const),
                  pl.BlockSpec((1, N_MOD, D), lambda i: (i * tm // S, 0, 0)),
                  pl.BlockSpec((RET_W, D), const), pl.BlockSpec((CONV_W, D), const),
                  pl.BlockSpec((ATT_W, D), const), pl.BlockSpec((D, D), const),
                  pl.BlockSpec((1, D), const),
                  pl.BlockSpec((D, 2 * LANES), const), pl.BlockSpec((1, LANES), const)],
        out_specs=[pl.BlockSpec((tm, D), row), pl.BlockSpec((tm, D), row),
                   pl.BlockSpec((tm, LANES), row), pl.BlockSpec((1, 1, LANES), lambda i: (i, 0, 0))],
        out_shape=[jax.ShapeDtypeStruct((T, D), F32), jax.ShapeDtypeStruct((T, D), BF16),
                   jax.ShapeDtypeStruct((T, LANES), F32), jax.ShapeDtypeStruct((T // tm, 1, LANES), F32)],
        scratch_shapes=[pltpu.VMEM((tm, tm), BF16)],
        compiler_params=_cparams(("arbitrary",)),
        name="merge_router",
    )(x2, ret, conv, att, proj, proj, proj, b_gate.reshape(1, N_BRANCH * D), mod,
      wr, wc, wa, wo, g_ffn.reshape(1, D), w_rt, b_rt)


def _pair_onehots(route, srcoff_row):
    tm = route.shape[0]
    lane = lax.broadcasted_iota(jnp.int32, (tm, LANES), 1).astype(F32)
    s0 = jnp.sum(jnp.where(lane == route[:, 0:1], srcoff_row, 0.0), axis=-1, keepdims=True) + route[:, 4:5]
    s1 = jnp.sum(jnp.where(lane == route[:, 1:2], srcoff_row, 0.0), axis=-1, keepdims=True) + route[:, 5:6]
    slots = lax.broadcasted_iota(jnp.int32, (tm, SORT_SLOTS), 1).astype(F32)
    return slots == s0, slots == s1


def _chunk_copies(tile, so_ref, nch_ref, dst_ref, make_copy):
    base = tile * N_EXPERTS
    for e in range(N_EXPERTS):
        so = so_ref[base + e]
        d = dst_ref[base + e]

        def body(c, carry, so=so, d=d):
            make_copy(pl.multiple_of(so + c * ROW_ALIGN, ROW_ALIGN),
                      pl.multiple_of(d + c * ROW_ALIGN, ROW_ALIGN)).start()
            return carry

        lax.fori_loop(0, nch_ref[base + e], body, 0)


def _dispatch_kernel(so_ref, nch_ref, dst_ref, ntot_ref, zpos_ref, h_ref, route_ref, sof_ref,
                     xs_hbm, sorted_ref, zero_ref, sem, zsem):
    i = pl.program_id(0)

    @pl.when(i == 0)
    def _():
        zero_ref[...] = jnp.zeros_like(zero_ref)
        for e in range(N_EXPERTS):
            pltpu.make_async_copy(zero_ref, xs_hbm.at[pl.ds(pl.multiple_of(zpos_ref[e], ROW_ALIGN), MOE_TILE), :],
                                  zsem).start()
        for e in range(N_EXPERTS):
            pltpu.make_async_copy(zero_ref, xs_hbm.at[pl.ds(0, MOE_TILE), :], zsem).wait()
        tail = zpos_ref[N_EXPERTS]
        n_tail = (xs_hbm.shape[0] - tail) // MOE_TILE

        def zero_tail(c, carry):
            cp = pltpu.make_async_copy(
                zero_ref, xs_hbm.at[pl.ds(pl.multiple_of(tail + c * MOE_TILE, MOE_TILE), MOE_TILE), :], zsem)
            cp.start()
            cp.wait()
            return carry

        lax.fori_loop(0, n_tail, zero_tail, 0)

    r = route_ref[...]
    p1, p2 = _pair_onehots(r, sof_ref[0])
    tn = (((0,), (0,)), ((), ()))
    sx = lax.dot_general((p1 | p2).astype(BF16), h_ref[...], tn, preferred_element_type=F32)
    sorted_ref[:, :D_MODEL] = sx.astype(BF16)
    lane = lax.broadcasted_iota(jnp.int32, (r.shape[0], LANES), 1)

    def wblock(w):
        hi = w.astype(BF16).astype(F32)
        lo = (w - hi).astype(BF16).astype(F32)
        return jnp.where(lane == 0, hi, jnp.where(lane == 1, lo, 0.0)).astype(BF16)

    sw = (lax.dot_general(p1.astype(BF16), wblock(r[:, 2:3]), tn, preferred_element_type=F32)
          + lax.dot_general(p2.astype(BF16), wblock(r[:, 3:4]), tn, preferred_element_type=F32))
    sorted_ref[:, D_MODEL:] = sw.astype(BF16)

    def make_copy(src_row, dst_row):
        return pltpu.make_async_copy(sorted_ref.at[pl.ds(src_row, ROW_ALIGN), :],
                                     xs_hbm.at[pl.ds(dst_row, ROW_ALIGN), :], sem)

    _chunk_copies(i, so_ref, nch_ref, dst_ref, make_copy)

    def wait_one(c, carry):
        make_copy(0, 0).wait()
        return carry

    lax.fori_loop(0, ntot_ref[i], wait_one, 0)


def _dispatch(h2, route, tabs, n_rows):
    T, D = h2.shape
    tm = ROUTE_TILE
    grid_spec = pltpu.PrefetchScalarGridSpec(
        num_scalar_prefetch=5,
        grid=(T // tm,),
        in_specs=[pl.BlockSpec((tm, D), lambda i, *_: (i, 0)),
                  pl.BlockSpec((tm, LANES), lambda i, *_: (i, 0)),
                  pl.BlockSpec((1, 1, LANES), lambda i, *_: (i, 0, 0))],
        out_specs=pl.BlockSpec(memory_space=pl.ANY),
        scratch_shapes=[pltpu.VMEM((SORT_SLOTS, XS_W), BF16), pltpu.VMEM((MOE_TILE, XS_W), BF16),
                        pltpu.SemaphoreType.DMA(()), pltpu.SemaphoreType.DMA(())],
    )
    return pl.pallas_call(
        _dispatch_kernel,
        grid_spec=grid_spec,
        out_shape=jax.ShapeDtypeStruct((n_rows + MOE_TILE, XS_W), BF16),
        compiler_params=_cparams(("arbitrary",)),
        name="moe_dispatch",
    )(tabs["srcoff"], tabs["nchunk"], tabs["dst"], tabs["ntot"], tabs["zpos"], h2, route, tabs["srcoff_f"])


def _ffn_kernel(te_ref, nv_ref, xs_ref, wu_ref, wd_ref, o_ref):
    i = pl.program_id(0)

    @pl.when(i < nv_ref[0])
    def _():
        x = xs_ref[:, :D_MODEL]
        wv = xs_ref[:, D_MODEL:].astype(F32)
        w = wv[:, 0:1] + wv[:, 1:2]
        hid = jnp.dot(x, wu_ref[0], preferred_element_type=F32)
        a = hid[:, :EXPERT_HIDDEN]
        b = hid[:, EXPERT_HIDDEN:]
        act = a * jax.nn.sigmoid(a) * b * w
        o_ref[...] = jnp.dot(act.astype(BF16), wd_ref[0], preferred_element_type=F32).astype(BF16)

    @pl.when(i >= nv_ref[0])
    def _():
        o_ref[...] = jnp.zeros_like(o_ref)


def _ffn(xs, tile_expert, n_valid, w_up_b, w_down_b, n_rows):
    tr = MOE_TILE
    D = D_MODEL
    F2 = w_up_b.shape[-1]
    grid_spec = pltpu.PrefetchScalarGridSpec(
        num_scalar_prefetch=2,
        grid=(n_rows // tr,),
        in_specs=[pl.BlockSpec((tr, XS_W), lambda i, te, nv: (i, 0)),
                  pl.BlockSpec((1, D, F2), lambda i, te, nv: (te[i], 0, 0)),
                  pl.BlockSpec((1, EXPERT_HIDDEN, D), lambda i, te, nv: (te[i], 0, 0))],
        out_specs=pl.BlockSpec((tr, D), lambda i, te, nv: (i, 0)),
    )
    return pl.pallas_call(
        _ffn_kernel,
        grid_spec=grid_spec,
        out_shape=jax.ShapeDtypeStruct((n_rows, D), BF16),
        compiler_params=_cparams(("arbitrary",)),
        name="moe_experts",
    )(tile_expert, n_valid, xs, w_up_b, w_down_b)


def _combine_kernel(so_ref, nch_ref, dst_ref, ntot_ref, x_ref, route_ref, sof_ref, mod_ref, ye_hbm,
                    o_ref, ys_ref, sem, *, nsteps):
    i = pl.program_id(0)

    def copies(slot):
        def make_copy(src_row, dst_row):
            return pltpu.make_async_copy(ye_hbm.at[pl.ds(dst_row, ROW_ALIGN), :],
                                         ys_ref.at[slot, pl.ds(src_row, ROW_ALIGN), :], sem.at[slot])
        return make_copy

    @pl.when(i == 0)
    def _():
        ys_ref[...] = jnp.zeros_like(ys_ref)
        _chunk_copies(0, so_ref, nch_ref, dst_ref, copies(0))

    @pl.when(i + 1 < nsteps)
    def _():
        _chunk_copies(i + 1, so_ref, nch_ref, dst_ref, copies((i + 1) % 2))

    slot = i % 2

    def wait_one(c, carry):
        copies(slot)(0, 0).wait()
        return carry

    lax.fori_loop(0, ntot_ref[i], wait_one, 0)
    p1, p2 = _pair_onehots(route_ref[...], sof_ref[0])
    y = jnp.dot((p1 | p2).astype(BF16), ys_ref[slot], preferred_element_type=F32)
    o_ref[...] = x_ref[...] + mod_ref[0, 5:6, :] * y


def _combine(x2, route, mod, ye, tabs, S):
    T, D = x2.shape
    tm = ROUTE_TILE
    nsteps = T // tm
    grid_spec = pltpu.PrefetchScalarGridSpec(
        num_scalar_prefetch=4,
        grid=(nsteps,),
        in_specs=[pl.BlockSpec((tm, D), lambda i, *_: (i, 0)),
                  pl.BlockSpec((tm, LANES), lambda i, *_: (i, 0)),
                  pl.BlockSpec((1, 1, LANES), lambda i, *_: (i, 0, 0)),
                  pl.BlockSpec((1, N_MOD, D), lambda i, *_: (i * tm // S, 0, 0)),
                  pl.BlockSpec(memory_space=pl.ANY)],
        out_specs=pl.BlockSpec((tm, D), lambda i, *_: (i, 0)),
        scratch_shapes=[pltpu.VMEM((2, SORT_SLOTS, D), BF16), pltpu.SemaphoreType.DMA((2,))],
    )
    return pl.pallas_call(
        functools.partial(_combine_kernel, nsteps=nsteps),
        grid_spec=grid_spec,
        out_shape=jax.ShapeDtypeStruct((T, D), F32),
        compiler_params=_cparams(("arbitrary",)),
        name="moe_combine",
    )(tabs["srcoff"], tabs["nchunk"], tabs["dst"], tabs["ntot"], x2, route, tabs["srcoff_f"], mod, ye)


def _moe_rows_bound(T):
    nt = T // ROUTE_TILE
    rows = 2 * T + nt * N_EXPERTS * (ROW_ALIGN - 1) + N_EXPERTS * (MOE_TILE - 1)
    return -(-rows // MOE_TILE) * MOE_TILE


def _routing_tables(counts, n_rows):
    tr = MOE_TILE
    c = counts[:, 0, :N_EXPERTS].astype(jnp.int32)
    c16 = (c + ROW_ALIGN - 1) // ROW_ALIGN * ROW_ALIGN
    srcoff = jnp.cumsum(c16, axis=1) - c16
    rows = jnp.sum(c16, axis=0)
    ends = jnp.cumsum((rows + tr - 1) // tr * tr)
    offs = ends - (rows + tr - 1) // tr * tr
    dst = offs[None, :] + jnp.cumsum(c16, axis=0) - c16
    tile_start = jnp.arange(n_rows // tr, dtype=jnp.int32) * tr
    tile_expert = jnp.minimum(jnp.sum((tile_start[:, None] >= ends[None, :]).astype(jnp.int32), axis=-1),
                              N_EXPERTS - 1)
    srcoff_f = jnp.zeros((c.shape[0], 1, LANES), F32).at[:, 0, :N_EXPERTS].set(srcoff.astype(F32))
    return {
        "srcoff": srcoff.reshape(-1), "nchunk": (c16 // ROW_ALIGN).reshape(-1), "dst": dst.reshape(-1),
        "ntot": jnp.sum(c16 // ROW_ALIGN, axis=1), "zpos": jnp.concatenate([offs + rows, ends[-1:]]),
        "srcoff_f": srcoff_f,
        "tile_expert": tile_expert, "n_valid": (ends[-1] // tr).astype(jnp.int32).reshape(1),
    }


def _reorder_in_cols(w_in):
    cuts = 4 * RET_W + 2 * CONV_W + 3 * ATT_W
    return jnp.concatenate([w_in[..., cuts:], w_in[..., :cuts]], axis=-1)


def kernel(x, c, positions, w_ada, b_ada, g_mix, g_ffn, w_in, b_gate, ret_gn, conv_w, conv_b, conv_ln_g, conv_ln_b, att_q_gain, att_k_gain, att_rel_bias, w_ret_out, w_conv_out, w_att_out, w_out, w_group, b_group, w_inner, b_inner, w_up, w_down):
    B, S, D = x.shape
    L = w_ada.shape[0]
    T = B * S
    assert D == D_MODEL and S % 1024 == 0 and T % 2048 == 0
    n_rows = _moe_rows_bound(T)

    mod_all = _adaln(c, w_ada, b_ada).reshape(L, B, N_MOD, D)
    cs_tab = _rope_table(positions)
    x2 = x.reshape(T, D)
    for l in range(L):
        mod = mod_all[l]
        proj = _inproj(x2, mod, g_mix[l], _reorder_in_cols(w_in[l]).astype(BF16), S)
        ret = _retention(proj, cs_tab, ret_gn[l], B, S)
        conv = _conv(proj, conv_w[l], conv_b[l], conv_ln_g[l], conv_ln_b[l], B, S)
        att = _attention(proj, att_rel_bias[l], att_q_gain[l], att_k_gain[l], B, S)
        w_rt = jnp.zeros((D, LANES), F32)
        w_rt = w_rt.at[:, :N_EXPERTS].set(w_inner[l]).at[:, N_EXPERTS:N_EXPERTS + N_GROUPS].set(w_group[l])
        b_rt = jnp.zeros((1, LANES), F32)
        b_rt = b_rt.at[0, :N_EXPERTS].set(b_inner[l]).at[0, N_EXPERTS:N_EXPERTS + N_GROUPS].set(b_group[l])
        w_rt_hi = w_rt.astype(BF16)
        w_rt2 = jnp.concatenate([w_rt_hi, (w_rt - w_rt_hi.astype(F32)).astype(BF16)], axis=1)
        x2, h2, route, counts = _merge(x2, proj, ret, conv, att, b_gate[l], mod,
                                       w_ret_out[l].astype(BF16), w_conv_out[l].astype(BF16),
                                       w_att_out[l].astype(BF16), w_out[l].astype(BF16),
                                       g_ffn[l], w_rt2, b_rt, S)
        tabs = _routing_tables(counts, n_rows)
        xs = _dispatch(h2, route, tabs, n_rows)
        ye = _ffn(xs, tabs["tile_expert"], tabs["n_valid"],
                  w_up[l].reshape(N_EXPERTS, D, 2 * EXPERT_HIDDEN).astype(BF16),
                  w_down[l].reshape(N_EXPERTS, EXPERT_HIDDEN, D).astype(BF16), n_rows)
        x2 = _combine(x2, route, mod, ye, tabs, S)
    return x2.reshape(B, S, D)
```

```python
import functools
import math

import jax
import jax.numpy as jnp
from jax import lax
from jax.experimental import pallas as pl
from jax.experimental.pallas import tpu as pltpu

F32 = jnp.float32
BF16 = jnp.bfloat16

D_MODEL = 1024
CHUNK = 64
EPS = 1e-6
NEG_INF = -1e30
RET_HEADS = 4
RET_DK = 128
RET_W = 512
ROPE_BASE = 10000.0
CONV_W = 512
CONV_K = 31
ATT_HEADS = 8
ATT_DH = 64
ATT_W = 512
ATT_LEFT_CHUNKS = 8
MAX_REL = 256
N_BRANCH = 3
IN_COLS = 4 * RET_W + 2 * CONV_W + 3 * ATT_W + N_BRANCH * D_MODEL
N_GROUPS = 4
EXPERTS_PER_GROUP = 8
N_EXPERTS = N_GROUPS * EXPERTS_PER_GROUP
EXPERT_HIDDEN = 256
N_MOD = 6

LANES = 128
V7X_VMEM_LIMIT_BYTES = 56 * 1024 * 1024

COL_GATE = 0
COL_RQ, COL_RK, COL_RV, COL_RG = 6, 7, 8, 9
COL_CA, COL_CB = 10, 11
COL_AQ, COL_AK, COL_AV = 12, 13, 14

RET_BLOCK = 256
ATT_QBLOCK = 256
ATT_BAND = ATT_QBLOCK + ATT_LEFT_CHUNKS * CHUNK
ATT_PAD = ATT_LEFT_CHUNKS * CHUNK
CONV_TILE = 512
CONV_HIST = 32
CONV_ROWS = 64
MOE_TILE = 512
ROUTE_TILE = 512
ROW_ALIGN = 16
SORT_SLOTS = 2 * ROUTE_TILE + N_EXPERTS * ROW_ALIGN
XS_W = D_MODEL + LANES


def _cparams(sem):
    return pltpu.CompilerParams(dimension_semantics=sem, vmem_limit_bytes=V7X_VMEM_LIMIT_BYTES)


def _adaln_kernel(c_ref, w_ref, b_ref, o_ref):
    c = c_ref[...]
    cs = c * jax.nn.sigmoid(c)
    o_ref[0] = jnp.dot(cs, w_ref[0], preferred_element_type=F32,
                       precision=lax.Precision.HIGHEST) + b_ref[0]


def _adaln(c, w_ada, b_ada):
    L, D, N = w_ada.shape
    B = c.shape[0]
    tn = 1536
    return pl.pallas_call(
        _adaln_kernel,
        grid=(L, N // tn),
        in_specs=[pl.BlockSpec((B, D), lambda l, j: (0, 0)),
                  pl.BlockSpec((1, D, tn), lambda l, j: (l, 0, j)),
                  pl.BlockSpec((1, 1, tn), lambda l, j: (l, 0, j))],
        out_specs=pl.BlockSpec((1, B, tn), lambda l, j: (l, 0, j)),
        out_shape=jax.ShapeDtypeStruct((L, B, N), F32),
        compiler_params=_cparams(("arbitrary", "arbitrary")),
        name="adaln",
    )(c, w_ada, b_ada.reshape(L, 1, N))


def _rope_kernel(pos_ref, inv_ref, ph_ref, o_ref):
    o_ref[...] = jnp.cos(pos_ref[...] * inv_ref[...] - ph_ref[...])


def _rope_table(positions):
    T = positions.size
    half = RET_DK // 2
    inv = ROPE_BASE ** (-jnp.arange(half, dtype=F32) / half)
    inv2 = jnp.concatenate([inv, inv]).reshape(1, LANES)
    phase = jnp.concatenate([jnp.zeros((half,), F32), jnp.full((half,), math.pi / 2, F32)]).reshape(1, LANES)
    posb = jnp.broadcast_to(positions.reshape(T, 1).astype(F32), (T, LANES))
    tm = 2048
    return pl.pallas_call(
        _rope_kernel,
        grid=(T // tm,),
        in_specs=[pl.BlockSpec((tm, LANES), lambda i: (i, 0)),
                  pl.BlockSpec((1, LANES), lambda i: (0, 0)),
                  pl.BlockSpec((1, LANES), lambda i: (0, 0))],
        out_specs=pl.BlockSpec((tm, LANES), lambda i: (i, 0)),
        out_shape=jax.ShapeDtypeStruct((T, LANES), F32),
        compiler_params=_cparams(("arbitrary",)),
        name="rope_table",
    )(posb, inv2, phase)


def _inproj_kernel(x_ref, mod_ref, g_ref, w_ref, o_ref, h_ref):
    @pl.when(pl.program_id(1) == 0)
    def _():
        x = x_ref[...]
        y = x * lax.rsqrt(jnp.mean(x * x, axis=-1, keepdims=True) + EPS) * g_ref[...]
        h = y * (1.0 + mod_ref[0, 1:2, :]) + mod_ref[0, 0:1, :]
        h_ref[...] = h.astype(BF16)

    o_ref[...] = jnp.dot(h_ref[...], w_ref[...], preferred_element_type=F32).astype(BF16)


def _inproj(x2, mod, g_mix, w_in_b, S):
    T, D = x2.shape
    N = w_in_b.shape[1]
    tm, tn = 1024, 1536
    return pl.pallas_call(
        _inproj_kernel,
        grid=(T // tm, N // tn),
        in_specs=[pl.BlockSpec((tm, D), lambda i, j: (i, 0)),
                  pl.BlockSpec((1, N_MOD, D), lambda i, j: (i * tm // S, 0, 0)),
                  pl.BlockSpec((1, D), lambda i, j: (0, 0)),
                  pl.BlockSpec((D, tn), lambda i, j: (0, j))],
        out_specs=pl.BlockSpec((tm, tn), lambda i, j: (i, j)),
        out_shape=jax.ShapeDtypeStruct((T, N), BF16),
        scratch_shapes=[pltpu.VMEM((tm, D), BF16)],
        compiler_params=_cparams(("arbitrary", "arbitrary")),
        name="inproj",
    )(x2, mod, g_mix.reshape(1, D), w_in_b)


def _retention_kernel(q_ref, k_ref, v_ref, g_ref, cs_ref, gn_ref, o_ref, state_ref):
    C = RET_BLOCK

    @pl.when(pl.program_id(1) == 0)
    def _():
        state_ref[...] = jnp.zeros_like(state_ref)

    cs = cs_ref[...]
    csr = pltpu.roll(cs, RET_DK // 2, 1)
    first = lax.broadcasted_iota(jnp.int32, (C, RET_DK), 1) < RET_DK // 2
    cos2 = jnp.where(first, cs, csr)
    sin2 = jnp.where(first, -csr, cs)
    diff = (lax.broadcasted_iota(jnp.int32, (C, C), 0) - lax.broadcasted_iota(jnp.int32, (C, C), 1)).astype(F32)
    row = lax.broadcasted_iota(jnp.int32, (C, 1), 0).astype(F32)
    for h in range(RET_HEADS):
        lg = math.log1p(-(2.0 ** (-5.0 - h)))
        sl = slice(h * RET_DK, (h + 1) * RET_DK)
        qh = q_ref[:, sl].astype(F32)
        kh = k_ref[:, sl].astype(F32)
        vh = v_ref[:, sl]
        qr = (qh * cos2 + pltpu.roll(qh, RET_DK // 2, 1) * sin2).astype(BF16)
        kr = (kh * cos2 + pltpu.roll(kh, RET_DK // 2, 1) * sin2) * (RET_DK ** -0.5)
        decay = jnp.where(diff >= 0.0, jnp.exp(lg * jnp.maximum(diff, 0.0)), 0.0)
        s = lax.dot_general(qr, kr.astype(BF16), (((1,), (1,)), ((), ())),
                            preferred_element_type=F32) * decay
        o = jnp.dot(s.astype(BF16), vh, preferred_element_type=F32)
        st = state_ref[h]
        o = o + jnp.dot(qr, st.astype(BF16), preferred_element_type=F32) * jnp.exp(lg * (row + 1.0))
        kd = (kr * jnp.exp(lg * (C - 1.0 - row))).astype(BF16)
        kv = lax.dot_general(kd, vh, (((0,), (0,)), ((), ())), preferred_element_type=F32)
        state_ref[h] = math.exp(lg * C) * st + kv
        mu = jnp.mean(o, axis=-1, keepdims=True)
        oc = o - mu
        var = jnp.mean(oc * oc, axis=-1, keepdims=True)
        on = oc * lax.rsqrt(var + EPS) * gn_ref[:, sl]
        gh = g_ref[:, sl].astype(F32)
        o_ref[:, sl] = (on * (gh * jax.nn.sigmoid(gh))).astype(BF16)


def _retention(proj, cs_tab, ret_gn, B, S):
    T = B * S
    C = RET_BLOCK
    nblk = S // C

    def col(c):
        return pl.BlockSpec((C, RET_W), lambda b, i, c=c: (b * nblk + i, c))

    return pl.pallas_call(
        _retention_kernel,
        grid=(B, nblk),
        in_specs=[col(COL_RQ), col(COL_RK), col(COL_RV), col(COL_RG),
                  pl.BlockSpec((C, LANES), lambda b, i: (b * nblk + i, 0)),
                  pl.BlockSpec((1, RET_W), lambda b, i: (0, 0))],
        out_specs=pl.BlockSpec((C, RET_W), lambda b, i: (b * nblk + i, 0)),
        out_shape=jax.ShapeDtypeStruct((T, RET_W), BF16),
        scratch_shapes=[pltpu.VMEM((RET_HEADS, RET_DK, RET_DK), F32)],
        compiler_params=_cparams(("arbitrary", "arbitrary")),
        name="retention",
    )(proj, proj, proj, proj, cs_tab, ret_gn.reshape(1, RET_W))


def _conv_kernel(a_ref, b_ref, w_ref, bias_ref, lg_ref, lb_ref, o_ref, z_ref, zs_ref):
    tm = CONV_TILE
    H = CONV_HIST
    SUB = 8

    @pl.when(pl.program_id(1) == 0)
    def _():
        z_ref[0:H, :] = jnp.zeros((H, CONV_W), F32)

    a = a_ref[...].astype(F32)
    b = b_ref[...].astype(F32)
    z_ref[H:H + tm, :] = a * jax.nn.sigmoid(b)
    base = H - (CONV_K - 1)
    n_shift = tm + (base + CONV_K - 2) // SUB * SUB
    for r in range(1, SUB):
        for c0 in range(0, n_shift, 128):
            cs = min(128, n_shift - c0)
            zs_ref[r - 1, c0:c0 + cs, :] = z_ref[c0 + r:c0 + r + cs, :]
    for r0 in range(0, tm, CONV_ROWS):
        acc = jnp.zeros((CONV_ROWS, CONV_W), F32)
        for j in range(CONV_K):
            al, r = (base + j) // SUB * SUB, (base + j) % SUB
            if r == 0:
                tap = z_ref[r0 + al:r0 + al + CONV_ROWS, :]
            else:
                tap = zs_ref[r - 1, r0 + al:r0 + al + CONV_ROWS, :]
            acc = acc + tap * w_ref[j:j + 1, :]
        acc = acc + bias_ref[...]
        mu = jnp.mean(acc, axis=-1, keepdims=True)
        ac = acc - mu
        var = jnp.mean(ac * ac, axis=-1, keepdims=True)
        y = ac * lax.rsqrt(var + EPS) * lg_ref[...] + lb_ref[...]
        o_ref[r0:r0 + CONV_ROWS, :] = (y * jax.nn.sigmoid(y)).astype(BF16)
    z_ref[0:H, :] = z_ref[tm:tm + H, :]


def _conv(proj, conv_w, conv_b, ln_g, ln_b, B, S):
    T = B * S
    tm = CONV_TILE
    nblk = S // tm
    vec = pl.BlockSpec((1, CONV_W), lambda b, i: (0, 0))
    return pl.pallas_call(
        _conv_kernel,
        grid=(B, nblk),
        in_specs=[pl.BlockSpec((tm, CONV_W), lambda b, i: (b * nblk + i, COL_CA)),
                  pl.BlockSpec((tm, CONV_W), lambda b, i: (b * nblk + i, COL_CB)),
                  pl.BlockSpec((CONV_K, CONV_W), lambda b, i: (0, 0)),
                  vec, vec, vec],
        out_specs=pl.BlockSpec((tm, CONV_W), lambda b, i: (b * nblk + i, 0)),
        out_shape=jax.ShapeDtypeStruct((T, CONV_W), BF16),
        scratch_shapes=[pltpu.VMEM((CONV_HIST + tm, CONV_W), F32),
                        pltpu.VMEM((7, CONV_HIST + tm, CONV_W), F32)],
        compiler_params=_cparams(("arbitrary", "arbitrary")),
        name="conv",
    )(proj, proj, conv_w.reshape(CONV_K, CONV_W), conv_b.reshape(1, CONV_W),
      ln_g.reshape(1, CONV_W), ln_b.reshape(1, CONV_W))


def _head_mean_sq(x, hm_ref):
    sq = x * x
    hi = sq.astype(BF16)
    lo = (sq - hi.astype(F32)).astype(BF16)
    tot = jnp.dot(hi, hm_ref[...], preferred_element_type=F32) + jnp.dot(lo, hm_ref[...], preferred_element_type=F32)
    return tot * (1.0 / ATT_DH)


def _attn_kernel(q_ref, k_ref, v_ref, rrow_ref, qg_ref, kg_ref, hm_ref, o_ref, kn_ref, vt_ref, bmt_ref, *, S):
    tq = ATT_QBLOCK
    qi = pl.program_id(1)
    heads = [slice(h * ATT_DH, (h + 1) * ATT_DH) for h in range(ATT_HEADS)]
    nkb = ATT_BAND // tq

    @pl.when((qi == 0) & (pl.program_id(0) == 0))
    def _():
        r = lax.broadcasted_iota(jnp.int32, (tq, ATT_BAND), 0)
        u = lax.broadcasted_iota(jnp.int32, (tq, ATT_BAND), 1)
        off = u - (r // CHUNK) * CHUNK
        inband = (off >= 0) & (off < (ATT_LEFT_CHUNKS + 1) * CHUNK)
        for h in range(ATT_HEADS):
            gen = jnp.broadcast_to(rrow_ref[h], (tq, rrow_ref.shape[-1]))
            t = pltpu.roll(gen, 0, 1, stride=1, stride_axis=0)
            bmt_ref[h] = jnp.where(inband, t[:, :ATT_BAND], NEG_INF).T

    @pl.when(qi == 0)
    def _():
        for h in range(ATT_HEADS):
            kn_ref[h, 0:ATT_PAD, :] = jnp.zeros((ATT_PAD, ATT_DH), BF16)
        for j in range(ATT_PAD // tq):
            vt_ref[j] = jnp.zeros((ATT_W, tq), BF16)
        rows = 512
        for r0 in range(0, S, rows):
            k = k_ref[r0:r0 + rows, :].astype(F32)
            kn = (k * lax.rsqrt(_head_mean_sq(k, hm_ref) + EPS) * kg_ref[...]).astype(BF16)
            for h in range(ATT_HEADS):
                kn_ref[h, ATT_PAD + r0:ATT_PAD + r0 + rows, :] = kn[:, heads[h]]
        for r0 in range(0, S, tq):
            vt_ref[(ATT_PAD + r0) // tq] = v_ref[r0:r0 + tq, :].astype(F32).T.astype(BF16)

    start = pl.multiple_of(qi * tq, tq)
    q = q_ref[...].astype(F32)
    qn = (q * lax.rsqrt(_head_mean_sq(q, hm_ref) + EPS) * qg_ref[...]).astype(BF16)

    def attend(mask_pad):
        if mask_pad:
            key_pos = lax.broadcasted_iota(jnp.int32, (ATT_BAND, LANES), 0) + start
            valid = key_pos >= ATT_PAD
        def scores(h):
            kb = kn_ref[h, pl.ds(start, ATT_BAND), :]
            return lax.dot_general(kb, qn[:, heads[h]], (((1,), (1,)), ((), ())), preferred_element_type=F32)

        def softmax(h, st):
            ps, ls = [], []
            for c0 in range(0, tq, LANES):
                ss = st[:, c0:c0 + LANES] + bmt_ref[h, :, c0:c0 + LANES]
                if mask_pad:
                    ss = jnp.where(valid, ss, NEG_INF)
                m = jnp.max(ss, axis=0, keepdims=True)
                p = jnp.exp2(ss - m)
                ls.append(jnp.sum(p, axis=0, keepdims=True))
                ps.append(p.astype(BF16))
            return jnp.concatenate(ps, axis=1), jnp.concatenate(ls, axis=1)

        def values(h, pt, l):
            ot = jnp.zeros((ATT_DH, tq), F32)
            for t in range(nkb):
                ot = ot + jnp.dot(vt_ref[qi + t, heads[h], :], pt[t * tq:(t + 1) * tq, :],
                                  preferred_element_type=F32)
            return ot / l

        sts, pls, outs = {}, {}, []
        for step in range(ATT_HEADS + 2):
            if step < ATT_HEADS:
                sts[step] = scores(step)
            if 0 <= step - 1 < ATT_HEADS:
                pls[step - 1] = softmax(step - 1, sts.pop(step - 1))
            if 0 <= step - 2 < ATT_HEADS:
                outs.append(values(step - 2, *pls.pop(step - 2)))
        o_ref[...] = jnp.concatenate(outs, axis=0).T.astype(BF16)

    first = ATT_PAD // tq

    @pl.when(qi < first)
    def _():
        attend(True)

    @pl.when(qi >= first)
    def _():
        attend(False)


def _attn_bias_row(rel_table):
    t = rel_table.astype(F32) * math.log2(math.e)
    edge = jnp.broadcast_to(t[:, 2 * MAX_REL:], (ATT_HEADS, ATT_QBLOCK))
    row = jnp.concatenate([edge, jnp.flip(t[:, 1:], axis=1), edge], axis=1)
    return row.reshape(ATT_HEADS, 1, 2 * ATT_QBLOCK + 2 * MAX_REL)


def _attention(proj, rel_table, q_gain, k_gain, B, S):
    T = B * S
    tq = ATT_QBLOCK
    nblk = S // tq
    gen = _attn_bias_row(rel_table)
    qg = jnp.tile(q_gain.astype(F32) * (ATT_DH ** -0.5 * math.log2(math.e)), ATT_HEADS).reshape(1, ATT_W)
    kg = jnp.tile(k_gain.astype(F32), ATT_HEADS).reshape(1, ATT_W)
    lane_head = jnp.arange(ATT_W) // ATT_DH
    hm = (lane_head[:, None] == lane_head[None, :]).astype(BF16)
    vec = pl.BlockSpec((1, ATT_W), lambda b, i: (0, 0))
    return pl.pallas_call(
        functools.partial(_attn_kernel, S=S),
        grid=(B, nblk),
        in_specs=[pl.BlockSpec((tq, ATT_W), lambda b, i: (b * nblk + i, COL_AQ)),
                  pl.BlockSpec((S, ATT_W), lambda b, i: (b, COL_AK)),
                  pl.BlockSpec((S, ATT_W), lambda b, i: (b, COL_AV)),
                  pl.BlockSpec(gen.shape, lambda b, i: (0, 0, 0)),
                  vec, vec,
                  pl.BlockSpec((ATT_W, ATT_W), lambda b, i: (0, 0))],
        out_specs=pl.BlockSpec((tq, ATT_W), lambda b, i: (b * nblk + i, 0)),
        out_shape=jax.ShapeDtypeStruct((T, ATT_W), BF16),
        scratch_shapes=[pltpu.VMEM((ATT_HEADS, ATT_PAD + S, ATT_DH), BF16),
                        pltpu.VMEM(((ATT_PAD + S) // tq, ATT_W, tq), BF16),
                        pltpu.VMEM((ATT_HEADS, ATT_BAND, tq), F32)],
        compiler_params=_cparams(("arbitrary", "arbitrary")),
        name="chunk_attention",
    )(proj, proj, proj, gen, qg, kg, hm)


def _merge_kernel(x_ref, ret_ref, conv_ref, att_ref, g0_ref, g1_ref, g2_ref, bg_ref, mod_ref,
                  wr_ref, wc_ref, wa_ref, wo_ref, gf_ref, wrt_ref, brt_ref,
                  xo_ref, h2_ref, route_ref, cnt_out_ref, tri_ref):
    D = D_MODEL

    @pl.when(pl.program_id(0) == 0)
    def _():
        n = tri_ref.shape[0]
        tri_ref[...] = (lax.broadcasted_iota(jnp.int32, (n, n), 1)
                        < lax.broadcasted_iota(jnp.int32, (n, n), 0)).astype(BF16)

    def gate(g_ref, k):
        return jax.nn.sigmoid(g_ref[...].astype(F32) + bg_ref[:, k * D:(k + 1) * D])

    merged = gate(g0_ref, 0) * jnp.dot(ret_ref[...], wr_ref[...], preferred_element_type=F32)
    merged = merged + gate(g1_ref, 1) * jnp.dot(conv_ref[...], wc_ref[...], preferred_element_type=F32)
    merged = merged + gate(g2_ref, 2) * jnp.dot(att_ref[...], wa_ref[...], preferred_element_type=F32)
    y = jnp.dot(merged.astype(BF16), wo_ref[...], preferred_element_type=F32)
    x = x_ref[...] + mod_ref[0, 2:3, :] * y
    xo_ref[...] = x

    h2 = x * lax.rsqrt(jnp.mean(x * x, axis=-1, keepdims=True) + EPS) * gf_ref[...]
    h2 = h2 * (1.0 + mod_ref[0, 4:5, :]) + mod_ref[0, 3:4, :]
    h2_ref[...] = h2.astype(BF16)
    h_hi = h2.astype(BF16)
    h_lo = (h2 - h_hi.astype(F32)).astype(BF16)
    both = jnp.dot(h_hi, wrt_ref[...], preferred_element_type=F32)
    logits = (both[:, :LANES] + both[:, LANES:]
              + jnp.dot(h_lo, wrt_ref[:, :LANES], preferred_element_type=F32) + brt_ref[...])
    tm = logits.shape[0]
    lane = lax.broadcasted_iota(jnp.int32, (tm, LANES), 1)
    big = jnp.int32(LANES)
    is_g = (lane >= N_EXPERTS) & (lane < N_EXPERTS + N_GROUPS)
    gl = jnp.where(is_g, logits, -jnp.inf)
    gmax = jnp.max(gl, axis=-1, keepdims=True)
    gsel = jnp.min(jnp.where(gl == gmax, lane - N_EXPERTS, big), axis=-1, keepdims=True)
    p_group = 1.0 / jnp.sum(jnp.where(is_g, jnp.exp(gl - gmax), 0.0), axis=-1, keepdims=True)
    in_grp = (lane >= gsel * EXPERTS_PER_GROUP) & (lane < (gsel + 1) * EXPERTS_PER_GROUP)
    ch = jnp.where(in_grp, logits, -jnp.inf)
    v1 = jnp.max(ch, axis=-1, keepdims=True)
    i1 = jnp.min(jnp.where(ch == v1, lane, big), axis=-1, keepdims=True)
    ch2 = jnp.where(lane == i1, -jnp.inf, ch)
    v2 = jnp.max(ch2, axis=-1, keepdims=True)
    i2 = jnp.min(jnp.where(ch2 == v2, lane, big), axis=-1, keepdims=True)
    e = jnp.exp(v2 - v1)
    w1 = p_group / (1.0 + e)
    w2 = p_group * e / (1.0 + e)
    sel = ((lane == i1) | (lane == i2)).astype(BF16)
    before = jnp.dot(tri_ref[...], sel, preferred_element_type=F32)
    rank1 = jnp.sum(jnp.where(lane == i1, before, 0.0), axis=-1, keepdims=True)
    rank2 = jnp.sum(jnp.where(lane == i2, before, 0.0), axis=-1, keepdims=True)
    cnt_out_ref[0] = jnp.sum(sel.astype(F32), axis=0, keepdims=True)
    vals = (i1.astype(F32), i2.astype(F32), w1, w2, rank1, rank2)
    route = jnp.zeros((tm, LANES), F32)
    for k, v in enumerate(vals):
        route = jnp.where(lane == k, v, route)
    route_ref[...] = route


def _merge(x2, proj, ret, conv, att, b_gate, mod, wr, wc, wa, wo, g_ffn, w_rt, b_rt, S):
    T, D = x2.shape
    tm = ROUTE_TILE
    row = lambda i: (i, 0)
    const = lambda i: (0, 0)
    br = pl.BlockSpec((tm, RET_W), row)
    return pl.pallas_call(
        _merge_kernel,
        grid=(T // tm,),
        in_specs=[pl.BlockSpec((tm, D), row), br, br, br,
                  pl.BlockSpec((tm, D), lambda i: (i, COL_GATE)),
                  pl.BlockSpec((tm, D), lambda i: (i, COL_GATE + 1)),
                  pl.BlockSpec((tm, D), lambda i: (i, COL_GATE + 2)),
                  pl.BlockSpec((1, N_BRANCH * D), const),
                  pl.BlockSpec((1, N_MOD, D), lambda i: (i * tm // S, 0, 0)),
                  pl.BlockSpec((RET_W, D), const), pl.BlockSpec((CONV_W, D), const),
                  pl.BlockSpec((ATT_W, D), const), pl.BlockSpec((D, D), const),
                  pl.BlockSpec((1, D), const),
                  pl.BlockSpec((D, 2 * LANES), const), pl.BlockSpec((1, LANES), const)],
        out_specs=[pl.BlockSpec((tm, D), row), pl.BlockSpec((tm, D), row),
                   pl.BlockSpec((tm, LANES), row), pl.BlockSpec((1, 1, LANES), lambda i: (i, 0, 0))],
        out_shape=[jax.ShapeDtypeStruct((T, D), F32), jax.ShapeDtypeStruct((T, D), BF16),
                   jax.ShapeDtypeStruct((T, LANES), F32), jax.ShapeDtypeStruct((T // tm, 1, LANES), F32)],
        scratch_shapes=[pltpu.VMEM((tm, tm), BF16)],
        compiler_params=_cparams(("arbitrary",)),
        name="merge_router",
    )(x2, ret, conv, att, proj, proj, proj, b_gate.reshape(1, N_BRANCH * D), mod,
      wr, wc, wa, wo, g_ffn.reshape(1, D), w_rt, b_rt)


def _pair_onehots(route, srcoff_row):
    tm = route.shape[0]
    lane = lax.broadcasted_iota(jnp.int32, (tm, LANES), 1).astype(F32)
    s0 = jnp.sum(jnp.where(lane == route[:, 0:1], srcoff_row, 0.0), axis=-1, keepdims=True) + route[:, 4:5]
    s1 = jnp.sum(jnp.where(lane == route[:, 1:2], srcoff_row, 0.0), axis=-1, keepdims=True) + route[:, 5:6]
    slots = lax.broadcasted_iota(jnp.int32, (tm, SORT_SLOTS), 1).astype(F32)
    return slots == s0, slots == s1


def _chunk_copies(tile, so_ref, nch_ref, dst_ref, make_copy):
    base = tile * N_EXPERTS
    for e in range(N_EXPERTS):
        so = so_ref[base + e]
        d = dst_ref[base + e]

        def body(c, carry, so=so, d=d):
            make_copy(pl.multiple_of(so + c * ROW_ALIGN, ROW_ALIGN),
                      pl.multiple_of(d + c * ROW_ALIGN, ROW_ALIGN)).start()
            return carry

        lax.fori_loop(0, nch_ref[base + e], body, 0)


def _dispatch_kernel(so_ref, nch_ref, dst_ref, ntot_ref, zpos_ref, h_ref, route_ref, sof_ref,
                     xs_hbm, sorted_ref, zero_ref, sem, zsem, *, nsteps):
    i = pl.program_id(0)

    @pl.when(i == 0)
    def _():
        zero_ref[...] = jnp.zeros_like(zero_ref)
        for e in range(N_EXPERTS):
            pltpu.make_async_copy(zero_ref, xs_hbm.at[pl.ds(pl.multiple_of(zpos_ref[e], ROW_ALIGN), MOE_TILE), :],
                                  zsem).start()
        for e in range(N_EXPERTS):
            pltpu.make_async_copy(zero_ref, xs_hbm.at[pl.ds(0, MOE_TILE), :], zsem).wait()
        tail = zpos_ref[N_EXPERTS]
        n_tail = (xs_hbm.shape[0] - tail) // MOE_TILE

        def zero_tail(c, carry):
            cp = pltpu.make_async_copy(
                zero_ref, xs_hbm.at[pl.ds(pl.multiple_of(tail + c * MOE_TILE, MOE_TILE), MOE_TILE), :], zsem)
            cp.start()
            cp.wait()
            return carry

        lax.fori_loop(0, n_tail, zero_tail, 0)

    r = route_ref[...]
    p1, p2 = _pair_onehots(r, sof_ref[0])
    lane = lax.broadcasted_iota(jnp.int32, (r.shape[0], LANES), 1)

    def split(w):
        hi = w.astype(BF16).astype(F32)
        return hi, (w - hi).astype(BF16).astype(F32)

    extra = jnp.zeros((r.shape[0], LANES), F32)
    for k, v in enumerate((r[:, 0:1], *split(r[:, 2:3]), *split(r[:, 3:4]))):
        extra = jnp.where(lane == k, v, extra)
    hx = jnp.concatenate([h_ref[...], extra.astype(BF16)], axis=1)
    slot = i % 2
    sorted_ref[slot] = lax.dot_general((p1 | p2).astype(BF16), hx, (((0,), (0,)), ((), ())),
                                       preferred_element_type=F32).astype(BF16)

    def copies(s):
        def make_copy(src_row, dst_row):
            return pltpu.make_async_copy(sorted_ref.at[s, pl.ds(src_row, ROW_ALIGN), :],
                                         xs_hbm.at[pl.ds(dst_row, ROW_ALIGN), :], sem.at[s])
        return make_copy

    _chunk_copies(i, so_ref, nch_ref, dst_ref, copies(slot))

    def wait_all(tile, s):
        def wait_one(c, carry):
            copies(s)(0, 0).wait()
            return carry

        lax.fori_loop(0, ntot_ref[tile], wait_one, 0)

    @pl.when(i > 0)
    def _():
        wait_all(i - 1, 1 - slot)

    @pl.when(i == nsteps - 1)
    def _():
        wait_all(i, slot)


def _dispatch(h2, route, tabs, n_rows):
    T, D = h2.shape
    tm = ROUTE_TILE
    grid_spec = pltpu.PrefetchScalarGridSpec(
        num_scalar_prefetch=5,
        grid=(T // tm,),
        in_specs=[pl.BlockSpec((tm, D), lambda i, *_: (i, 0)),
                  pl.BlockSpec((tm, LANES), lambda i, *_: (i, 0)),
                  pl.BlockSpec((1, 1, LANES), lambda i, *_: (i, 0, 0))],
        out_specs=pl.BlockSpec(memory_space=pl.ANY),
        scratch_shapes=[pltpu.VMEM((2, SORT_SLOTS, XS_W), BF16), pltpu.VMEM((MOE_TILE, XS_W), BF16),
                        pltpu.SemaphoreType.DMA((2,)), pltpu.SemaphoreType.DMA(())],
    )
    return pl.pallas_call(
        functools.partial(_dispatch_kernel, nsteps=T // tm),
        grid_spec=grid_spec,
        out_shape=jax.ShapeDtypeStruct((n_rows + MOE_TILE, XS_W), BF16),
        compiler_params=_cparams(("arbitrary",)),
        name="moe_dispatch",
    )(tabs["srcoff"], tabs["nchunk"], tabs["dst"], tabs["ntot"], tabs["zpos"], h2, route, tabs["srcoff_f"])


def _ffn_kernel(te_ref, nv_ref, xs_ref, wu_ref, wd_ref, o_ref):
    i = pl.program_id(0)

    @pl.when(i < nv_ref[0])
    def _():
        x = xs_ref[:, :D_MODEL]
        wv = xs_ref[:, D_MODEL:].astype(F32)
        mine = wv[:, 0:1] == te_ref[i].astype(F32)
        w = jnp.where(mine, wv[:, 1:2] + wv[:, 2:3], wv[:, 3:4] + wv[:, 4:5])
        hid = jnp.dot(x, wu_ref[0], preferred_element_type=F32)
        a = hid[:, :EXPERT_HIDDEN]
        b = hid[:, EXPERT_HIDDEN:]
        act = a * jax.nn.sigmoid(a) * b * w
        o_ref[...] = jnp.dot(act.astype(BF16), wd_ref[0], preferred_element_type=F32).astype(BF16)

    @pl.when(i >= nv_ref[0])
    def _():
        o_ref[...] = jnp.zeros_like(o_ref)


def _ffn(xs, tile_expert, n_valid, w_up_b, w_down_b, n_rows):
    tr = MOE_TILE
    D = D_MODEL
    F2 = w_up_b.shape[-1]
    grid_spec = pltpu.PrefetchScalarGridSpec(
        num_scalar_prefetch=2,
        grid=(n_rows // tr,),
        in_specs=[pl.BlockSpec((tr, XS_W), lambda i, te, nv: (i, 0)),
                  pl.BlockSpec((1, D, F2), lambda i, te, nv: (te[i], 0, 0)),
                  pl.BlockSpec((1, EXPERT_HIDDEN, D), lambda i, te, nv: (te[i], 0, 0))],
        out_specs=pl.BlockSpec((tr, D), lambda i, te, nv: (i, 0)),
    )
    return pl.pallas_call(
        _ffn_kernel,
        grid_spec=grid_spec,
        out_shape=jax.ShapeDtypeStruct((n_rows, D), BF16),
        compiler_params=_cparams(("arbitrary",)),
        name="moe_experts",
    )(tile_expert, n_valid, xs, w_up_b, w_down_b)


def _combine_kernel(so_ref, nch_ref, dst_ref, ntot_ref, x_ref, route_ref, sof_ref, mod_ref, ye_hbm,
                    o_ref, ys_ref, sem, *, nsteps):
    i = pl.program_id(0)

    def copies(slot):
        def make_copy(src_row, dst_row):
            return pltpu.make_async_copy(ye_hbm.at[pl.ds(dst_row, ROW_ALIGN), :],
                                         ys_ref.at[slot, pl.ds(src_row, ROW_ALIGN), :], sem.at[slot])
        return make_copy

    @pl.when(i == 0)
    def _():
        ys_ref[...] = jnp.zeros_like(ys_ref)
        _chunk_copies(0, so_ref, nch_ref, dst_ref, copies(0))

    @pl.when(i + 1 < nsteps)
    def _():
        _chunk_copies(i + 1, so_ref, nch_ref, dst_ref, copies((i + 1) % 2))

    slot = i % 2

    def wait_one(c, carry):
        copies(slot)(0, 0).wait()
        return carry

    lax.fori_loop(0, ntot_ref[i], wait_one, 0)
    p1, p2 = _pair_onehots(route_ref[...], sof_ref[0])
    y = jnp.dot((p1 | p2).astype(BF16), ys_ref[slot], preferred_element_type=F32)
    o_ref[...] = x_ref[...] + mod_ref[0, 5:6, :] * y


def _combine(x2, route, mod, ye, tabs, S):
    T, D = x2.shape
    tm = ROUTE_TILE
    nsteps = T // tm
    grid_spec = pltpu.PrefetchScalarGridSpec(
        num_scalar_prefetch=4,
        grid=(nsteps,),
        in_specs=[pl.BlockSpec((tm, D), lambda i, *_: (i, 0)),
                  pl.BlockSpec((tm, LANES), lambda i, *_: (i, 0)),
                  pl.BlockSpec((1, 1, LANES), lambda i, *_: (i, 0, 0)),
                  pl.BlockSpec((1, N_MOD, D), lambda i, *_: (i * tm // S, 0, 0)),
                  pl.BlockSpec(memory_space=pl.ANY)],
        out_specs=pl.BlockSpec((tm, D), lambda i, *_: (i, 0)),
        scratch_shapes=[pltpu.VMEM((2, SORT_SLOTS, D), BF16), pltpu.SemaphoreType.DMA((2,))],
    )
    return pl.pallas_call(
        functools.partial(_combine_kernel, nsteps=nsteps),
        grid_spec=grid_spec,
        out_shape=jax.ShapeDtypeStruct((T, D), F32),
        compiler_params=_cparams(("arbitrary",)),
        name="moe_combine",
    )(tabs["srcoff"], tabs["nchunk"], tabs["dst"], tabs["ntot"], x2, route, tabs["srcoff_f"], mod, ye)


def _moe_rows_bound(T):
    nt = T // ROUTE_TILE
    rows = 2 * T + nt * N_EXPERTS * (ROW_ALIGN - 1) + N_EXPERTS * (MOE_TILE - 1)
    return -(-rows // MOE_TILE) * MOE_TILE


def _routing_tables(counts, n_rows):
    tr = MOE_TILE
    c = counts[:, 0, :N_EXPERTS].astype(jnp.int32)
    c16 = (c + ROW_ALIGN - 1) // ROW_ALIGN * ROW_ALIGN
    srcoff = jnp.cumsum(c16, axis=1) - c16
    rows = jnp.sum(c16, axis=0)
    ends = jnp.cumsum((rows + tr - 1) // tr * tr)
    offs = ends - (rows + tr - 1) // tr * tr
    dst = offs[None, :] + jnp.cumsum(c16, axis=0) - c16
    tile_start = jnp.arange(n_rows // tr, dtype=jnp.int32) * tr
    tile_expert = jnp.minimum(jnp.sum((tile_start[:, None] >= ends[None, :]).astype(jnp.int32), axis=-1),
                              N_EXPERTS - 1)
    srcoff_f = jnp.zeros((c.shape[0], 1, LANES), F32).at[:, 0, :N_EXPERTS].set(srcoff.astype(F32))
    return {
        "srcoff": srcoff.reshape(-1), "nchunk": (c16 // ROW_ALIGN).reshape(-1), "dst": dst.reshape(-1),
        "ntot": jnp.sum(c16 // ROW_ALIGN, axis=1), "zpos": jnp.concatenate([offs + rows, ends[-1:]]),
        "srcoff_f": srcoff_f,
        "tile_expert": tile_expert, "n_valid": (ends[-1] // tr).astype(jnp.int32).reshape(1),
    }


def _reorder_in_cols(w_in):
    cuts = 4 * RET_W + 2 * CONV_W + 3 * ATT_W
    return jnp.concatenate([w_in[..., cuts:], w_in[..., :cuts]], axis=-1)


def kernel(x, c, positions, w_ada, b_ada, g_mix, g_ffn, w_in, b_gate, ret_gn, conv_w, conv_b, conv_ln_g, conv_ln_b, att_q_gain, att_k_gain, att_rel_bias, w_ret_out, w_conv_out, w_att_out, w_out, w_group, b_group, w_inner, b_inner, w_up, w_down):
    B, S, D = x.shape
    L = w_ada.shape[0]
    T = B * S
    assert D == D_MODEL and S % 1024 == 0 and T % 2048 == 0
    n_rows = _moe_rows_bound(T)

    mod_all = _adaln(c, w_ada, b_ada).reshape(L, B, N_MOD, D)
    cs_tab = _rope_table(positions)
    x2 = x.reshape(T, D)
    for l in range(L):
        mod = mod_all[l]
        proj = _inproj(x2, mod, g_mix[l], _reorder_in_cols(w_in[l]).astype(BF16), S)
        ret = _retention(proj, cs_tab, ret_gn[l], B, S)
        conv = _conv(proj, conv_w[l], conv_b[l], conv_ln_g[l], conv_ln_b[l], B, S)
        att = _attention(proj, att_rel_bias[l], att_q_gain[l], att_k_gain[l], B, S)
        w_rt = jnp.zeros((D, LANES), F32)
        w_rt = w_rt.at[:, :N_EXPERTS].set(w_inner[l]).at[:, N_EXPERTS:N_EXPERTS + N_GROUPS].set(w_group[l])
        b_rt = jnp.zeros((1, LANES), F32)
        b_rt = b_rt.at[0, :N_EXPERTS].set(b_inner[l]).at[0, N_EXPERTS:N_EXPERTS + N_GROUPS].set(b_group[l])
        w_rt_hi = w_rt.astype(BF16)
        w_rt2 = jnp.concatenate([w_rt_hi, (w_rt - w_rt_hi.astype(F32)).astype(BF16)], axis=1)
        x2, h2, route, counts = _merge(x2, proj, ret, conv, att, b_gate[l], mod,
                                       w_ret_out[l].astype(BF16), w_conv_out[l].astype(BF16),
                                       w_att_out[l].astype(BF16), w_out[l].astype(BF16),
                                       g_ffn[l], w_rt2, b_rt, S)
        tabs = _routing_tables(counts, n_rows)
        xs = _dispatch(h2, route, tabs, n_rows)
        ye = _ffn(xs, tabs["tile_expert"], tabs["n_valid"],
                  w_up[l].reshape(N_EXPERTS, D, 2 * EXPERT_HIDDEN).astype(BF16),
                  w_down[l].reshape(N_EXPERTS, EXPERT_HIDDEN, D).astype(BF16), n_rows)
        x2 = _combine(x2, route, mod, ye, tabs, S)
    return x2.reshape(B, S, D)
```

```python
import functools
import math

import jax
import jax.numpy as jnp
from jax import lax
from jax.experimental import pallas as pl
from jax.experimental.pallas import tpu as pltpu

F32 = jnp.float32
BF16 = jnp.bfloat16

D_MODEL = 1024
CHUNK = 64
EPS = 1e-6
NEG_INF = -1e30
RET_HEADS = 4
RET_DK = 128
RET_W = 512
ROPE_BASE = 10000.0
CONV_W = 512
CONV_K = 31
ATT_HEADS = 8
ATT_DH = 64
ATT_W = 512
ATT_LEFT_CHUNKS = 8
MAX_REL = 256
N_BRANCH = 3
IN_COLS = 4 * RET_W + 2 * CONV_W + 3 * ATT_W + N_BRANCH * D_MODEL
N_GROUPS = 4
EXPERTS_PER_GROUP = 8
N_EXPERTS = N_GROUPS * EXPERTS_PER_GROUP
EXPERT_HIDDEN = 256
N_MOD = 6

LANES = 128
V7X_VMEM_LIMIT_BYTES = 56 * 1024 * 1024

COL_GATE = 0
COL_RQ, COL_RK, COL_RV, COL_RG = 6, 7, 8, 9
COL_CA, COL_CB = 10, 11
COL_AQ, COL_AK, COL_AV = 12, 13, 14

RET_BLOCK = 256
ATT_QBLOCK = 256
ATT_BAND = ATT_QBLOCK + ATT_LEFT_CHUNKS * CHUNK
ATT_PAD = ATT_LEFT_CHUNKS * CHUNK
CONV_TILE = 512
CONV_HIST = 32
CONV_ROWS = 64
MOE_TILE = 512
ROUTE_TILE = 512
MERGE_SUB = 512
ROW_ALIGN = 16
SORT_SLOTS = 2 * ROUTE_TILE + N_EXPERTS * ROW_ALIGN
XS_W = D_MODEL + LANES


def _cparams(sem):
    return pltpu.CompilerParams(dimension_semantics=sem, vmem_limit_bytes=V7X_VMEM_LIMIT_BYTES)


def _adaln_kernel(c_ref, w_ref, b_ref, o_ref):
    c = c_ref[...]
    cs = c * jax.nn.sigmoid(c)
    o_ref[0] = jnp.dot(cs, w_ref[0], preferred_element_type=F32,
                       precision=lax.Precision.HIGHEST) + b_ref[0]


def _adaln(c, w_ada, b_ada):
    L, D, N = w_ada.shape
    B = c.shape[0]
    tn = 1536
    return pl.pallas_call(
        _adaln_kernel,
        grid=(L, N // tn),
        in_specs=[pl.BlockSpec((B, D), lambda l, j: (0, 0)),
                  pl.BlockSpec((1, D, tn), lambda l, j: (l, 0, j)),
                  pl.BlockSpec((1, 1, tn), lambda l, j: (l, 0, j))],
        out_specs=pl.BlockSpec((1, B, tn), lambda l, j: (l, 0, j)),
        out_shape=jax.ShapeDtypeStruct((L, B, N), F32),
        compiler_params=_cparams(("arbitrary", "arbitrary")),
        name="adaln",
    )(c, w_ada, b_ada.reshape(L, 1, N))


def _rope_kernel(pos_ref, inv_ref, ph_ref, o_ref):
    o_ref[...] = jnp.cos(pos_ref[...] * inv_ref[...] - ph_ref[...])


def _rope_table(positions):
    T = positions.size
    half = RET_DK // 2
    inv = ROPE_BASE ** (-jnp.arange(half, dtype=F32) / half)
    inv2 = jnp.concatenate([inv, inv]).reshape(1, LANES)
    phase = jnp.concatenate([jnp.zeros((half,), F32), jnp.full((half,), math.pi / 2, F32)]).reshape(1, LANES)
    posb = jnp.broadcast_to(positions.reshape(T, 1).astype(F32), (T, LANES))
    tm = 2048
    return pl.pallas_call(
        _rope_kernel,
        grid=(T // tm,),
        in_specs=[pl.BlockSpec((tm, LANES), lambda i: (i, 0)),
                  pl.BlockSpec((1, LANES), lambda i: (0, 0)),
                  pl.BlockSpec((1, LANES), lambda i: (0, 0))],
        out_specs=pl.BlockSpec((tm, LANES), lambda i: (i, 0)),
        out_shape=jax.ShapeDtypeStruct((T, LANES), F32),
        compiler_params=_cparams(("arbitrary",)),
        name="rope_table",
    )(posb, inv2, phase)


def _modulated_rmsnorm(x, g, mod_ref, shift_row, scale_row):
    y = x * lax.rsqrt(jnp.mean(x * x, axis=-1, keepdims=True) + EPS) * g
    return y * (1.0 + mod_ref[0, scale_row:scale_row + 1, :]) + mod_ref[0, shift_row:shift_row + 1, :]


def _prenorm_kernel(x_ref, mod_ref, g_ref, h_ref):
    h_ref[...] = _modulated_rmsnorm(x_ref[...], g_ref[...], mod_ref, 0, 1).astype(BF16)


def _prenorm(x2, mod, g_mix, S):
    T, D = x2.shape
    tm = 1024
    return pl.pallas_call(
        _prenorm_kernel,
        grid=(T // tm,),
        in_specs=[pl.BlockSpec((tm, D), lambda i: (i, 0)),
                  pl.BlockSpec((1, N_MOD, D), lambda i: (i * tm // S, 0, 0)),
                  pl.BlockSpec((1, D), lambda i: (0, 0))],
        out_specs=pl.BlockSpec((tm, D), lambda i: (i, 0)),
        out_shape=jax.ShapeDtypeStruct((T, D), BF16),
        compiler_params=_cparams(("arbitrary",)),
        name="prenorm",
    )(x2, mod, g_mix.reshape(1, D))


def _inproj_kernel(h_ref, w_ref, o_ref):
    o_ref[...] = jnp.dot(h_ref[...], w_ref[...], preferred_element_type=F32).astype(BF16)


def _inproj(h, w_in_b):
    T, D = h.shape
    N = w_in_b.shape[1]
    tm, tn = 1024, 2560
    return pl.pallas_call(
        _inproj_kernel,
        grid=(T // tm, N // tn),
        in_specs=[pl.BlockSpec((tm, D), lambda i, j: (i, 0)),
                  pl.BlockSpec((D, tn), lambda i, j: (0, j))],
        out_specs=pl.BlockSpec((tm, tn), lambda i, j: (i, j)),
        out_shape=jax.ShapeDtypeStruct((T, N), BF16),
        compiler_params=_cparams(("arbitrary", "arbitrary")),
        name="inproj",
    )(h, w_in_b)


def _retention_kernel(q_ref, k_ref, v_ref, g_ref, cs_ref, gn_ref, o_ref, state_ref, decay_ref):
    C = RET_BLOCK

    @pl.when(pl.program_id(1) == 0)
    def _():
        state_ref[...] = jnp.zeros_like(state_ref)

    @pl.when((pl.program_id(0) == 0) & (pl.program_id(1) == 0))
    def _():
        d = (lax.broadcasted_iota(jnp.int32, (C, C), 0) - lax.broadcasted_iota(jnp.int32, (C, C), 1)).astype(F32)
        for h in range(RET_HEADS):
            lg = math.log1p(-(2.0 ** (-5.0 - h)))
            decay_ref[h] = jnp.where(d >= 0.0, jnp.exp(lg * jnp.maximum(d, 0.0)), 0.0)

    cs = cs_ref[...]
    csr = pltpu.roll(cs, RET_DK // 2, 1)
    first = lax.broadcasted_iota(jnp.int32, (C, RET_DK), 1) < RET_DK // 2
    cos2 = jnp.where(first, cs, csr)
    sin2 = jnp.where(first, -csr, cs)
    row = lax.broadcasted_iota(jnp.int32, (C, 1), 0).astype(F32)
    for h in range(RET_HEADS):
        lg = math.log1p(-(2.0 ** (-5.0 - h)))
        sl = slice(h * RET_DK, (h + 1) * RET_DK)
        qh = q_ref[:, sl].astype(F32)
        kh = k_ref[:, sl].astype(F32)
        vh = v_ref[:, sl]
        qr = (qh * cos2 + pltpu.roll(qh, RET_DK // 2, 1) * sin2).astype(BF16)
        kr = (kh * cos2 + pltpu.roll(kh, RET_DK // 2, 1) * sin2) * (RET_DK ** -0.5)
        s = lax.dot_general(qr, kr.astype(BF16), (((1,), (1,)), ((), ())),
                            preferred_element_type=F32) * decay_ref[h]
        o = jnp.dot(s.astype(BF16), vh, preferred_element_type=F32)
        st = state_ref[h]
        o = o + jnp.dot(qr, st.astype(BF16), preferred_element_type=F32) * jnp.exp(lg * (row + 1.0))
        kd = (kr * jnp.exp(lg * (C - 1.0 - row))).astype(BF16)
        kv = lax.dot_general(kd, vh, (((0,), (0,)), ((), ())), preferred_element_type=F32)
        state_ref[h] = math.exp(lg * C) * st + kv
        mu = jnp.mean(o, axis=-1, keepdims=True)
        oc = o - mu
        var = jnp.mean(oc * oc, axis=-1, keepdims=True)
        on = oc * lax.rsqrt(var + EPS) * gn_ref[:, sl]
        gh = g_ref[:, sl].astype(F32)
        o_ref[:, sl] = (on * (gh * jax.nn.sigmoid(gh))).astype(BF16)


def _retention(proj, cs_tab, ret_gn, B, S):
    T = B * S
    C = RET_BLOCK
    nblk = S // C

    def col(c):
        return pl.BlockSpec((C, RET_W), lambda b, i, c=c: (b * nblk + i, c))

    return pl.pallas_call(
        _retention_kernel,
        grid=(B, nblk),
        in_specs=[col(COL_RQ), col(COL_RK), col(COL_RV), col(COL_RG),
                  pl.BlockSpec((C, LANES), lambda b, i: (b * nblk + i, 0)),
                  pl.BlockSpec((1, RET_W), lambda b, i: (0, 0))],
        out_specs=pl.BlockSpec((C, RET_W), lambda b, i: (b * nblk + i, 0)),
        out_shape=jax.ShapeDtypeStruct((T, RET_W), BF16),
        scratch_shapes=[pltpu.VMEM((RET_HEADS, RET_DK, RET_DK), F32),
                        pltpu.VMEM((RET_HEADS, C, C), F32)],
        compiler_params=_cparams(("arbitrary", "arbitrary")),
        name="retention",
    )(proj, proj, proj, proj, cs_tab, ret_gn.reshape(1, RET_W))


def _conv_kernel(a_ref, b_ref, w_ref, bias_ref, lg_ref, lb_ref, o_ref, z_ref, zs_ref):
    tm = CONV_TILE
    H = CONV_HIST
    SUB = 8

    @pl.when(pl.program_id(1) == 0)
    def _():
        z_ref[0:H, :] = jnp.zeros((H, CONV_W), F32)

    a = a_ref[...].astype(F32)
    b = b_ref[...].astype(F32)
    z_ref[H:H + tm, :] = a * jax.nn.sigmoid(b)
    base = H - (CONV_K - 1)
    n_shift = tm + (base + CONV_K - 2) // SUB * SUB
    for r in range(1, SUB):
        for c0 in range(0, n_shift, 128):
            cs = min(128, n_shift - c0)
            zs_ref[r - 1, c0:c0 + cs, :] = z_ref[c0 + r:c0 + r + cs, :]
    for r0 in range(0, tm, CONV_ROWS):
        acc = jnp.zeros((CONV_ROWS, CONV_W), F32)
        for j in range(CONV_K):
            al, r = (base + j) // SUB * SUB, (base + j) % SUB
            if r == 0:
                tap = z_ref[r0 + al:r0 + al + CONV_ROWS, :]
            else:
                tap = zs_ref[r - 1, r0 + al:r0 + al + CONV_ROWS, :]
            acc = acc + tap * w_ref[j:j + 1, :]
        acc = acc + bias_ref[...]
        mu = jnp.mean(acc, axis=-1, keepdims=True)
        ac = acc - mu
        var = jnp.mean(ac * ac, axis=-1, keepdims=True)
        y = ac * lax.rsqrt(var + EPS) * lg_ref[...] + lb_ref[...]
        o_ref[r0:r0 + CONV_ROWS, :] = (y * jax.nn.sigmoid(y)).astype(BF16)
    z_ref[0:H, :] = z_ref[tm:tm + H, :]


def _conv(proj, conv_w, conv_b, ln_g, ln_b, B, S):
    T = B * S
    tm = CONV_TILE
    nblk = S // tm
    vec = pl.BlockSpec((1, CONV_W), lambda b, i: (0, 0))
    return pl.pallas_call(
        _conv_kernel,
        grid=(B, nblk),
        in_specs=[pl.BlockSpec((tm, CONV_W), lambda b, i: (b * nblk + i, COL_CA)),
                  pl.BlockSpec((tm, CONV_W), lambda b, i: (b * nblk + i, COL_CB)),
                  pl.BlockSpec((CONV_K, CONV_W), lambda b, i: (0, 0)),
                  vec, vec, vec],
        out_specs=pl.BlockSpec((tm, CONV_W), lambda b, i: (b * nblk + i, 0)),
        out_shape=jax.ShapeDtypeStruct((T, CONV_W), BF16),
        scratch_shapes=[pltpu.VMEM((CONV_HIST + tm, CONV_W), F32),
                        pltpu.VMEM((7, CONV_HIST + tm, CONV_W), F32)],
        compiler_params=_cparams(("arbitrary", "arbitrary")),
        name="conv",
    )(proj, proj, conv_w.reshape(CONV_K, CONV_W), conv_b.reshape(1, CONV_W),
      ln_g.reshape(1, CONV_W), ln_b.reshape(1, CONV_W))


def _head_mean_sq(x, hm_ref):
    sq = x * x
    hi = sq.astype(BF16)
    lo = (sq - hi.astype(F32)).astype(BF16)
    tot = jnp.dot(hi, hm_ref[...], preferred_element_type=F32) + jnp.dot(lo, hm_ref[...], preferred_element_type=F32)
    return tot * (1.0 / ATT_DH)


def _attn_kernel(q_ref, k_ref, v_ref, rrow_ref, qg_ref, kg_ref, hm_ref, o_ref, kn_ref, vt_ref, bmt_ref, *, S):
    tq = ATT_QBLOCK
    qi = pl.program_id(1)
    heads = [slice(h * ATT_DH, (h + 1) * ATT_DH) for h in range(ATT_HEADS)]
    nkb = ATT_BAND // tq

    @pl.when((qi == 0) & (pl.program_id(0) == 0))
    def _():
        r = lax.broadcasted_iota(jnp.int32, (tq, ATT_BAND), 0)
        u = lax.broadcasted_iota(jnp.int32, (tq, ATT_BAND), 1)
        off = u - (r // CHUNK) * CHUNK
        inband = (off >= 0) & (off < (ATT_LEFT_CHUNKS + 1) * CHUNK)
        for h in range(ATT_HEADS):
            gen = jnp.broadcast_to(rrow_ref[h], (tq, rrow_ref.shape[-1]))
            t = pltpu.roll(gen, 0, 1, stride=1, stride_axis=0)
            bmt_ref[h] = jnp.where(inband, t[:, :ATT_BAND], NEG_INF).T

    @pl.when(qi == 0)
    def _():
        for h in range(ATT_HEADS):
            kn_ref[h, 0:ATT_PAD, :] = jnp.zeros((ATT_PAD, ATT_DH), BF16)
        for j in range(ATT_PAD // tq):
            vt_ref[j] = jnp.zeros((ATT_W, tq), BF16)
        rows = 512
        for r0 in range(0, S, rows):
            k = k_ref[r0:r0 + rows, :].astype(F32)
            kn = (k * lax.rsqrt(_head_mean_sq(k, hm_ref) + EPS) * kg_ref[...]).astype(BF16)
            for h in range(ATT_HEADS):
                kn_ref[h, ATT_PAD + r0:ATT_PAD + r0 + rows, :] = kn[:, heads[h]]
        for r0 in range(0, S, tq):
            vt_ref[(ATT_PAD + r0) // tq] = v_ref[r0:r0 + tq, :].astype(F32).T.astype(BF16)

    start = pl.multiple_of(qi * tq, tq)
    q = q_ref[...].astype(F32)
    qn = (q * lax.rsqrt(_head_mean_sq(q, hm_ref) + EPS) * qg_ref[...]).astype(BF16)

    def attend(mask_pad):
        if mask_pad:
            key_pos = lax.broadcasted_iota(jnp.int32, (ATT_BAND, LANES), 0) + start
            valid = key_pos >= ATT_PAD
        def scores(h):
            kb = kn_ref[h, pl.ds(start, ATT_BAND), :]
            return lax.dot_general(kb, qn[:, heads[h]], (((1,), (1,)), ((), ())), preferred_element_type=F32)

        def softmax(h, st):
            ps, ls = [], []
            for c0 in range(0, tq, LANES):
                ss = st[:, c0:c0 + LANES] + bmt_ref[h, :, c0:c0 + LANES]
                if mask_pad:
                    ss = jnp.where(valid, ss, NEG_INF)
                m = jnp.max(ss, axis=0, keepdims=True)
                p = jnp.exp2(ss - m)
                ls.append(jnp.sum(p, axis=0, keepdims=True))
                ps.append(p.astype(BF16))
            return jnp.concatenate(ps, axis=1), jnp.concatenate(ls, axis=1)

        def values(h, pt, l):
            ot = jnp.zeros((ATT_DH, tq), F32)
            for t in range(nkb):
                ot = ot + jnp.dot(vt_ref[qi + t, heads[h], :], pt[t * tq:(t + 1) * tq, :],
                                  preferred_element_type=F32)
            return ot / l

        sts, pls, outs = {}, {}, []
        for step in range(ATT_HEADS + 2):
            if step < ATT_HEADS:
                sts[step] = scores(step)
            if 0 <= step - 1 < ATT_HEADS:
                pls[step - 1] = softmax(step - 1, sts.pop(step - 1))
            if 0 <= step - 2 < ATT_HEADS:
                outs.append(values(step - 2, *pls.pop(step - 2)))
        o_ref[...] = jnp.concatenate(outs, axis=0).T.astype(BF16)

    first = ATT_PAD // tq

    @pl.when(qi < first)
    def _():
        attend(True)

    @pl.when(qi >= first)
    def _():
        attend(False)


def _attn_bias_row(rel_table):
    t = rel_table.astype(F32) * math.log2(math.e)
    edge = jnp.broadcast_to(t[:, 2 * MAX_REL:], (ATT_HEADS, ATT_QBLOCK))
    row = jnp.concatenate([edge, jnp.flip(t[:, 1:], axis=1), edge], axis=1)
    return row.reshape(ATT_HEADS, 1, 2 * ATT_QBLOCK + 2 * MAX_REL)


def _attention(proj, rel_table, q_gain, k_gain, B, S):
    T = B * S
    tq = ATT_QBLOCK
    nblk = S // tq
    gen = _attn_bias_row(rel_table)
    qg = jnp.tile(q_gain.astype(F32) * (ATT_DH ** -0.5 * math.log2(math.e)), ATT_HEADS).reshape(1, ATT_W)
    kg = jnp.tile(k_gain.astype(F32), ATT_HEADS).reshape(1, ATT_W)
    lane_head = jnp.arange(ATT_W) // ATT_DH
    hm = (lane_head[:, None] == lane_head[None, :]).astype(BF16)
    vec = pl.BlockSpec((1, ATT_W), lambda b, i: (0, 0))
    return pl.pallas_call(
        functools.partial(_attn_kernel, S=S),
        grid=(B, nblk),
        in_specs=[pl.BlockSpec((tq, ATT_W), lambda b, i: (b * nblk + i, COL_AQ)),
                  pl.BlockSpec((S, ATT_W), lambda b, i: (b, COL_AK)),
                  pl.BlockSpec((S, ATT_W), lambda b, i: (b, COL_AV)),
                  pl.BlockSpec(gen.shape, lambda b, i: (0, 0, 0)),
                  vec, vec,
                  pl.BlockSpec((ATT_W, ATT_W), lambda b, i: (0, 0))],
        out_specs=pl.BlockSpec((tq, ATT_W), lambda b, i: (b * nblk + i, 0)),
        out_shape=jax.ShapeDtypeStruct((T, ATT_W), BF16),
        scratch_shapes=[pltpu.VMEM((ATT_HEADS, ATT_PAD + S, ATT_DH), BF16),
                        pltpu.VMEM(((ATT_PAD + S) // tq, ATT_W, tq), BF16),
                        pltpu.VMEM((ATT_HEADS, ATT_BAND, tq), F32)],
        compiler_params=_cparams(("arbitrary", "arbitrary")),
        name="chunk_attention",
    )(proj, proj, proj, gen, qg, kg, hm)


def _merge_kernel(x_ref, ret_ref, conv_ref, att_ref, g0_ref, g1_ref, g2_ref, bg_ref, mod_ref,
                  wr_ref, wc_ref, wa_ref, wo_ref, gf_ref, wrt_ref, brt_ref,
                  xo_ref, h2_ref, route_ref, cnt_out_ref, tri_ref):
    D = D_MODEL

    @pl.when(pl.program_id(0) == 0)
    def _():
        n = tri_ref.shape[0]
        tri_ref[...] = (lax.broadcasted_iota(jnp.int32, (n, n), 1)
                        < lax.broadcasted_iota(jnp.int32, (n, n), 0)).astype(BF16)

    def mix(rs):
        def gated(g_ref, k, b_ref, w_ref):
            gate = jax.nn.sigmoid(g_ref[rs, :].astype(F32) + bg_ref[:, k * D:(k + 1) * D])
            return gate * jnp.dot(b_ref[rs, :], w_ref[...], preferred_element_type=F32)

        merged = (gated(g0_ref, 0, ret_ref, wr_ref) + gated(g1_ref, 1, conv_ref, wc_ref)
                  + gated(g2_ref, 2, att_ref, wa_ref))
        return merged.astype(BF16)

    def project(rs, merged):
        y = jnp.dot(merged, wo_ref[...], preferred_element_type=F32)
        x = x_ref[rs, :] + mod_ref[0, 2:3, :] * y
        xo_ref[rs, :] = x
        h2 = x * lax.rsqrt(jnp.mean(x * x, axis=-1, keepdims=True) + EPS) * gf_ref[...]
        h2 = h2 * (1.0 + mod_ref[0, 4:5, :]) + mod_ref[0, 3:4, :]
        h_hi = h2.astype(BF16)
        h2_ref[rs, :] = h_hi
        h_lo = (h2 - h_hi.astype(F32)).astype(BF16)
        both = jnp.dot(h_hi, wrt_ref[...], preferred_element_type=F32)
        return (both[:, :LANES] + both[:, LANES:]
                + jnp.dot(h_lo, wrt_ref[:, :LANES], preferred_element_type=F32) + brt_ref[...])

    subs = [slice(r0, r0 + MERGE_SUB) for r0 in range(0, x_ref.shape[0], MERGE_SUB)]
    mixed, parts = {}, []
    for step in range(len(subs) + 1):
        if step < len(subs):
            mixed[step] = mix(subs[step])
        if step >= 1:
            parts.append(project(subs[step - 1], mixed.pop(step - 1)))
    logits = jnp.concatenate(parts, axis=0)
    tm = logits.shape[0]
    lane = lax.broadcasted_iota(jnp.int32, (tm, LANES), 1)
    big = jnp.int32(LANES)
    is_g = (lane >= N_EXPERTS) & (lane < N_EXPERTS + N_GROUPS)
    gl = jnp.where(is_g, logits, -jnp.inf)
    gmax = jnp.max(gl, axis=-1, keepdims=True)
    gsel = jnp.min(jnp.where(gl == gmax, lane - N_EXPERTS, big), axis=-1, keepdims=True)
    p_group = 1.0 / jnp.sum(jnp.where(is_g, jnp.exp(gl - gmax), 0.0), axis=-1, keepdims=True)
    in_grp = (lane >= gsel * EXPERTS_PER_GROUP) & (lane < (gsel + 1) * EXPERTS_PER_GROUP)
    ch = jnp.where(in_grp, logits, -jnp.inf)
    v1 = jnp.max(ch, axis=-1, keepdims=True)
    i1 = jnp.min(jnp.where(ch == v1, lane, big), axis=-1, keepdims=True)
    ch2 = jnp.where(lane == i1, -jnp.inf, ch)
    v2 = jnp.max(ch2, axis=-1, keepdims=True)
    i2 = jnp.min(jnp.where(ch2 == v2, lane, big), axis=-1, keepdims=True)
    e = jnp.exp(v2 - v1)
    w1 = p_group / (1.0 + e)
    w2 = p_group * e / (1.0 + e)
    sel = ((lane == i1) | (lane == i2)).astype(BF16)
    before = jnp.dot(tri_ref[...], sel, preferred_element_type=F32)
    rank1 = jnp.sum(jnp.where(lane == i1, before, 0.0), axis=-1, keepdims=True)
    rank2 = jnp.sum(jnp.where(lane == i2, before, 0.0), axis=-1, keepdims=True)
    cnt_out_ref[0] = jnp.sum(sel.astype(F32), axis=0, keepdims=True)
    vals = (i1.astype(F32), i2.astype(F32), w1, w2, rank1, rank2)
    route = jnp.zeros((tm, LANES), F32)
    for k, v in enumerate(vals):
        route = jnp.where(lane == k, v, route)
    route_ref[...] = route


def _merge(x2, proj, ret, conv, att, b_gate, mod, wr, wc, wa, wo, g_ffn, w_rt, b_rt, S):
    T, D = x2.shape
    tm = ROUTE_TILE
    row = lambda i: (i, 0)
    const = lambda i: (0, 0)
    br = pl.BlockSpec((tm, RET_W), row)
    return pl.pallas_call(
        _merge_kernel,
        grid=(T // tm,),
        in_specs=[pl.BlockSpec((tm, D), row), br, br, br,
                  pl.BlockSpec((tm, D), lambda i: (i, COL_GATE)),
                  pl.BlockSpec((tm, D), lambda i: (i, COL_GATE + 1)),
                  pl.BlockSpec((tm, D), lambda i: (i, COL_GATE + 2)),
                  pl.BlockSpec((1, N_BRANCH * D), const),
                  pl.BlockSpec((1, N_MOD, D), lambda i: (i * tm // S, 0, 0)),
                  pl.BlockSpec((RET_W, D), const), pl.BlockSpec((CONV_W, D), const),
                  pl.BlockSpec((ATT_W, D), const), pl.BlockSpec((D, D), const),
                  pl.BlockSpec((1, D), const),
                  pl.BlockSpec((D, 2 * LANES), const), pl.BlockSpec((1, LANES), const)],
        out_specs=[pl.BlockSpec((tm, D), row), pl.BlockSpec((tm, D), row),
                   pl.BlockSpec((tm, LANES), row), pl.BlockSpec((1, 1, LANES), lambda i: (i, 0, 0))],
        out_shape=[jax.ShapeDtypeStruct((T, D), F32), jax.ShapeDtypeStruct((T, D), BF16),
                   jax.ShapeDtypeStruct((T, LANES), F32), jax.ShapeDtypeStruct((T // tm, 1, LANES), F32)],
        scratch_shapes=[pltpu.VMEM((tm, tm), BF16)],
        compiler_params=_cparams(("arbitrary",)),
        name="merge_router",
    )(x2, ret, conv, att, proj, proj, proj, b_gate.reshape(1, N_BRANCH * D), mod,
      wr, wc, wa, wo, g_ffn.reshape(1, D), w_rt, b_rt)


def _pair_onehots(route, srcoff_row):
    tm = route.shape[0]
    lane = lax.broadcasted_iota(jnp.int32, (tm, LANES), 1).astype(F32)
    s0 = jnp.sum(jnp.where(lane == route[:, 0:1], srcoff_row, 0.0), axis=-1, keepdims=True) + route[:, 4:5]
    s1 = jnp.sum(jnp.where(lane == route[:, 1:2], srcoff_row, 0.0), axis=-1, keepdims=True) + route[:, 5:6]
    slots = lax.broadcasted_iota(jnp.int32, (tm, SORT_SLOTS), 1).astype(F32)
    return slots == s0, slots == s1


def _chunk_copies(tile, so_ref, nch_ref, dst_ref, make_copy):
    base = tile * N_EXPERTS
    for e in range(N_EXPERTS):
        so = so_ref[base + e]
        d = dst_ref[base + e]

        def body(c, carry, so=so, d=d):
            make_copy(pl.multiple_of(so + c * ROW_ALIGN, ROW_ALIGN),
                      pl.multiple_of(d + c * ROW_ALIGN, ROW_ALIGN)).start()
            return carry

        lax.fori_loop(0, nch_ref[base + e], body, 0)


def _dispatch_kernel(so_ref, nch_ref, dst_ref, ntot_ref, zpos_ref, h_ref, route_ref, sof_ref,
                     xs_hbm, sorted_ref, zero_ref, sem, zsem, *, nsteps):
    i = pl.program_id(0)

    @pl.when(i == 0)
    def _():
        zero_ref[...] = jnp.zeros_like(zero_ref)
        for e in range(N_EXPERTS):
            pltpu.make_async_copy(zero_ref, xs_hbm.at[pl.ds(pl.multiple_of(zpos_ref[e], ROW_ALIGN), MOE_TILE), :],
                                  zsem).start()
        for e in range(N_EXPERTS):
            pltpu.make_async_copy(zero_ref, xs_hbm.at[pl.ds(0, MOE_TILE), :], zsem).wait()
        tail = zpos_ref[N_EXPERTS]
        n_tail = (xs_hbm.shape[0] - tail) // MOE_TILE

        def zero_tail(c, carry):
            cp = pltpu.make_async_copy(
                zero_ref, xs_hbm.at[pl.ds(pl.multiple_of(tail + c * MOE_TILE, MOE_TILE), MOE_TILE), :], zsem)
            cp.start()
            cp.wait()
            return carry

        lax.fori_loop(0, n_tail, zero_tail, 0)

    r = route_ref[...]
    p1, p2 = _pair_onehots(r, sof_ref[0])
    lane = lax.broadcasted_iota(jnp.int32, (r.shape[0], LANES), 1)

    def split(w):
        hi = w.astype(BF16).astype(F32)
        return hi, (w - hi).astype(BF16).astype(F32)

    extra = jnp.zeros((r.shape[0], LANES), F32)
    for k, v in enumerate((r[:, 0:1], *split(r[:, 2:3]), *split(r[:, 3:4]))):
        extra = jnp.where(lane == k, v, extra)
    hx = jnp.concatenate([h_ref[...], extra.astype(BF16)], axis=1)
    slot = i % 2
    sorted_ref[slot] = lax.dot_general((p1 | p2).astype(BF16), hx, (((0,), (0,)), ((), ())),
                                       preferred_element_type=F32).astype(BF16)

    def copies(s):
        def make_copy(src_row, dst_row):
            return pltpu.make_async_copy(sorted_ref.at[s, pl.ds(src_row, ROW_ALIGN), :],
                                         xs_hbm.at[pl.ds(dst_row, ROW_ALIGN), :], sem.at[s])
        return make_copy

    _chunk_copies(i, so_ref, nch_ref, dst_ref, copies(slot))

    def wait_all(tile, s):
        def wait_one(c, carry):
            copies(s)(0, 0).wait()
            return carry

        lax.fori_loop(0, ntot_ref[tile], wait_one, 0)

    @pl.when(i > 0)
    def _():
        wait_all(i - 1, 1 - slot)

    @pl.when(i == nsteps - 1)
    def _():
        wait_all(i, slot)


def _dispatch(h2, route, tabs, n_rows):
    T, D = h2.shape
    tm = ROUTE_TILE
    grid_spec = pltpu.PrefetchScalarGridSpec(
        num_scalar_prefetch=5,
        grid=(T // tm,),
        in_specs=[pl.BlockSpec((tm, D), lambda i, *_: (i, 0)),
                  pl.BlockSpec((tm, LANES), lambda i, *_: (i, 0)),
                  pl.BlockSpec((1, 1, LANES), lambda i, *_: (i, 0, 0))],
        out_specs=pl.BlockSpec(memory_space=pl.ANY),
        scratch_shapes=[pltpu.VMEM((2, SORT_SLOTS, XS_W), BF16), pltpu.VMEM((MOE_TILE, XS_W), BF16),
                        pltpu.SemaphoreType.DMA((2,)), pltpu.SemaphoreType.DMA(())],
    )
    return pl.pallas_call(
        functools.partial(_dispatch_kernel, nsteps=T // tm),
        grid_spec=grid_spec,
        out_shape=jax.ShapeDtypeStruct((n_rows + MOE_TILE, XS_W), BF16),
        compiler_params=_cparams(("arbitrary",)),
        name="moe_dispatch",
    )(tabs["srcoff"], tabs["nchunk"], tabs["dst"], tabs["ntot"], tabs["zpos"], h2, route, tabs["srcoff_f"])


def _ffn_kernel(te_ref, nv_ref, xs_ref, wu_ref, wd_ref, o_ref, wub_ref, wdb_ref):
    i = pl.program_id(0)

    @pl.when((i == 0) | (te_ref[i] != te_ref[jnp.maximum(i - 1, 0)]))
    def _():
        wub_ref[...] = wu_ref[0].astype(BF16)
        wdb_ref[...] = wd_ref[0].astype(BF16)

    @pl.when(i < nv_ref[0])
    def _():
        x = xs_ref[:, :D_MODEL]
        wv = xs_ref[:, D_MODEL:].astype(F32)
        mine = wv[:, 0:1] == te_ref[i].astype(F32)
        w = jnp.where(mine, wv[:, 1:2] + wv[:, 2:3], wv[:, 3:4] + wv[:, 4:5])
        hid = jnp.dot(x, wub_ref[...], preferred_element_type=F32)
        a = hid[:, :EXPERT_HIDDEN]
        b = hid[:, EXPERT_HIDDEN:]
        act = a * jax.nn.sigmoid(a) * b * w
        o_ref[...] = jnp.dot(act.astype(BF16), wdb_ref[...], preferred_element_type=F32).astype(BF16)

    @pl.when(i >= nv_ref[0])
    def _():
        o_ref[...] = jnp.zeros_like(o_ref)


def _ffn(xs, tile_expert, n_valid, w_up_b, w_down_b, n_rows):
    tr = MOE_TILE
    D = D_MODEL
    F2 = w_up_b.shape[-1]
    grid_spec = pltpu.PrefetchScalarGridSpec(
        num_scalar_prefetch=2,
        grid=(n_rows // tr,),
        in_specs=[pl.BlockSpec((tr, XS_W), lambda i, te, nv: (i, 0)),
                  pl.BlockSpec((1, D, F2), lambda i, te, nv: (te[i], 0, 0)),
                  pl.BlockSpec((1, EXPERT_HIDDEN, D), lambda i, te, nv: (te[i], 0, 0))],
        out_specs=pl.BlockSpec((tr, D), lambda i, te, nv: (i, 0)),
        scratch_shapes=[pltpu.VMEM((D, F2), BF16), pltpu.VMEM((EXPERT_HIDDEN, D), BF16)],
    )
    return pl.pallas_call(
        _ffn_kernel,
        grid_spec=grid_spec,
        out_shape=jax.ShapeDtypeStruct((n_rows, D), BF16),
        compiler_params=_cparams(("arbitrary",)),
        name="moe_experts",
    )(tile_expert, n_valid, xs, w_up_b, w_down_b)


def _combine_kernel(so_ref, nch_ref, dst_ref, ntot_ref, x_ref, route_ref, sof_ref, mod_ref, ye_hbm, *rest,
                    nsteps, emit_next):
    if emit_next:
        modn_ref, gn_ref, o_ref, hn_ref, ys_ref, sem = rest
    else:
        o_ref, ys_ref, sem = rest
    i = pl.program_id(0)

    def copies(slot):
        def make_copy(src_row, dst_row):
            return pltpu.make_async_copy(ye_hbm.at[pl.ds(dst_row, ROW_ALIGN), :],
                                         ys_ref.at[slot, pl.ds(src_row, ROW_ALIGN), :], sem.at[slot])
        return make_copy

    @pl.when(i == 0)
    def _():
        ys_ref[...] = jnp.zeros_like(ys_ref)
        _chunk_copies(0, so_ref, nch_ref, dst_ref, copies(0))

    @pl.when(i + 1 < nsteps)
    def _():
        _chunk_copies(i + 1, so_ref, nch_ref, dst_ref, copies((i + 1) % 2))

    slot = i % 2

    def wait_one(c, carry):
        copies(slot)(0, 0).wait()
        return carry

    lax.fori_loop(0, ntot_ref[i], wait_one, 0)
    p1, p2 = _pair_onehots(route_ref[...], sof_ref[0])
    y = jnp.dot((p1 | p2).astype(BF16), ys_ref[slot], preferred_element_type=F32)
    x = x_ref[...] + mod_ref[0, 5:6, :] * y
    o_ref[...] = x
    if emit_next:
        hn_ref[...] = _modulated_rmsnorm(x, gn_ref[...], modn_ref, 0, 1).astype(BF16)


def _combine(x2, route, mod, ye, tabs, S, next_mod=None, next_gain=None):
    T, D = x2.shape
    tm = ROUTE_TILE
    nsteps = T // tm
    emit_next = next_mod is not None
    row = pl.BlockSpec((tm, D), lambda i, *_: (i, 0))
    modspec = pl.BlockSpec((1, N_MOD, D), lambda i, *_: (i * tm // S, 0, 0))
    in_specs = [row,
                pl.BlockSpec((tm, LANES), lambda i, *_: (i, 0)),
                pl.BlockSpec((1, 1, LANES), lambda i, *_: (i, 0, 0)),
                modspec,
                pl.BlockSpec(memory_space=pl.ANY)]
    args = [x2, route, tabs["srcoff_f"], mod, ye]
    out_specs, out_shape = row, jax.ShapeDtypeStruct((T, D), F32)
    if emit_next:
        in_specs += [modspec, pl.BlockSpec((1, D), lambda i, *_: (0, 0))]
        args += [next_mod, next_gain.reshape(1, D)]
        out_specs, out_shape = [row, row], [out_shape, jax.ShapeDtypeStruct((T, D), BF16)]
    grid_spec = pltpu.PrefetchScalarGridSpec(
        num_scalar_prefetch=4,
        grid=(nsteps,),
        in_specs=in_specs,
        out_specs=out_specs,
        scratch_shapes=[pltpu.VMEM((2, SORT_SLOTS, D), BF16), pltpu.SemaphoreType.DMA((2,))],
    )
    return pl.pallas_call(
        functools.partial(_combine_kernel, nsteps=nsteps, emit_next=emit_next),
        grid_spec=grid_spec,
        out_shape=out_shape,
        compiler_params=_cparams(("arbitrary",)),
        name="moe_combine",
    )(tabs["srcoff"], tabs["nchunk"], tabs["dst"], tabs["ntot"], *args)


def _moe_rows_bound(T):
    nt = T // ROUTE_TILE
    rows = 2 * T + nt * N_EXPERTS * (ROW_ALIGN - 1) + N_EXPERTS * (MOE_TILE - 1)
    return -(-rows // MOE_TILE) * MOE_TILE


def _routing_tables(counts, n_rows):
    tr = MOE_TILE
    c = counts[:, 0, :N_EXPERTS].astype(jnp.int32)
    c16 = (c + ROW_ALIGN - 1) // ROW_ALIGN * ROW_ALIGN
    srcoff = jnp.cumsum(c16, axis=1) - c16
    rows = jnp.sum(c16, axis=0)
    ends = jnp.cumsum((rows + tr - 1) // tr * tr)
    offs = ends - (rows + tr - 1) // tr * tr
    dst = offs[None, :] + jnp.cumsum(c16, axis=0) - c16
    tile_start = jnp.arange(n_rows // tr, dtype=jnp.int32) * tr
    tile_expert = jnp.minimum(jnp.sum((tile_start[:, None] >= ends[None, :]).astype(jnp.int32), axis=-1),
                              N_EXPERTS - 1)
    srcoff_f = jnp.zeros((c.shape[0], 1, LANES), F32).at[:, 0, :N_EXPERTS].set(srcoff.astype(F32))
    return {
        "srcoff": srcoff.reshape(-1), "nchunk": (c16 // ROW_ALIGN).reshape(-1), "dst": dst.reshape(-1),
        "ntot": jnp.sum(c16 // ROW_ALIGN, axis=1), "zpos": jnp.concatenate([offs + rows, ends[-1:]]),
        "srcoff_f": srcoff_f,
        "tile_expert": tile_expert, "n_valid": (ends[-1] // tr).astype(jnp.int32).reshape(1),
    }


def _reorder_in_cols(w_in):
    cuts = 4 * RET_W + 2 * CONV_W + 3 * ATT_W
    return jnp.concatenate([w_in[..., cuts:], w_in[..., :cuts]], axis=-1)


def kernel(x, c, positions, w_ada, b_ada, g_mix, g_ffn, w_in, b_gate, ret_gn, conv_w, conv_b, conv_ln_g, conv_ln_b, att_q_gain, att_k_gain, att_rel_bias, w_ret_out, w_conv_out, w_att_out, w_out, w_group, b_group, w_inner, b_inner, w_up, w_down):
    B, S, D = x.shape
    L = w_ada.shape[0]
    T = B * S
    assert D == D_MODEL and S % 1024 == 0 and T % 2048 == 0
    n_rows = _moe_rows_bound(T)

    mod_all = _adaln(c, w_ada, b_ada).reshape(L, B, N_MOD, D)
    cs_tab = _rope_table(positions)
    x2 = x.reshape(T, D)
    h = _prenorm(x2, mod_all[0], g_mix[0], S)
    for l in range(L):
        mod = mod_all[l]
        proj = _inproj(h, _reorder_in_cols(w_in[l]).astype(BF16))
        ret = _retention(proj, cs_tab, ret_gn[l], B, S)
        conv = _conv(proj, conv_w[l], conv_b[l], conv_ln_g[l], conv_ln_b[l], B, S)
        att = _attention(proj, att_rel_bias[l], att_q_gain[l], att_k_gain[l], B, S)
        w_rt = jnp.zeros((D, LANES), F32)
        w_rt = w_rt.at[:, :N_EXPERTS].set(w_inner[l]).at[:, N_EXPERTS:N_EXPERTS + N_GROUPS].set(w_group[l])
        b_rt = jnp.zeros((1, LANES), F32)
        b_rt = b_rt.at[0, :N_EXPERTS].set(b_inner[l]).at[0, N_EXPERTS:N_EXPERTS + N_GROUPS].set(b_group[l])
        w_rt_hi = w_rt.astype(BF16)
        w_rt2 = jnp.concatenate([w_rt_hi, (w_rt - w_rt_hi.astype(F32)).astype(BF16)], axis=1)
        x2, h2, route, counts = _merge(x2, proj, ret, conv, att, b_gate[l], mod,
                                       w_ret_out[l].astype(BF16), w_conv_out[l].astype(BF16),
                                       w_att_out[l].astype(BF16), w_out[l].astype(BF16),
                                       g_ffn[l], w_rt2, b_rt, S)
        tabs = _routing_tables(counts, n_rows)
        xs = _dispatch(h2, route, tabs, n_rows)
        ye = _ffn(xs, tabs["tile_expert"], tabs["n_valid"],
                  w_up[l].reshape(N_EXPERTS, D, 2 * EXPERT_HIDDEN),
                  w_down[l].reshape(N_EXPERTS, EXPERT_HIDDEN, D), n_rows)
        if l + 1 < L:
            x2, h = _combine(x2, route, mod, ye, tabs, S, mod_all[l + 1], g_mix[l + 1])
        else:
            x2 = _combine(x2, route, mod, ye, tabs, S)
    return x2.reshape(B, S, D)
```

```python
import functools
import math

import jax
import jax.numpy as jnp
from jax import lax
from jax.experimental import pallas as pl
from jax.experimental.pallas import tpu as pltpu

F32 = jnp.float32
BF16 = jnp.bfloat16

D_MODEL = 1024
CHUNK = 64
EPS = 1e-6
NEG_INF = -1e30
RET_HEADS = 4
RET_DK = 128
RET_W = 512
ROPE_BASE = 10000.0
CONV_W = 512
CONV_K = 31
ATT_HEADS = 8
ATT_DH = 64
ATT_W = 512
ATT_LEFT_CHUNKS = 8
MAX_REL = 256
N_BRANCH = 3
IN_COLS = 4 * RET_W + 2 * CONV_W + 3 * ATT_W + N_BRANCH * D_MODEL
N_GROUPS = 4
EXPERTS_PER_GROUP = 8
N_EXPERTS = N_GROUPS * EXPERTS_PER_GROUP
EXPERT_HIDDEN = 256
N_MOD = 6

LANES = 128
V7X_VMEM_LIMIT_BYTES = 56 * 1024 * 1024

COL_GATE = 0
COL_RQ, COL_RK, COL_RV, COL_RG = 6, 7, 8, 9
COL_CA, COL_CB = 10, 11
COL_AQ, COL_AK, COL_AV = 12, 13, 14

RET_BLOCK = 256
ATT_QBLOCK = 256
ATT_BAND = ATT_QBLOCK + ATT_LEFT_CHUNKS * CHUNK
ATT_PAD = ATT_LEFT_CHUNKS * CHUNK
CONV_TILE = 512
CONV_HIST = 32
CONV_ROWS = 64
MOE_TILE = 512
ROUTE_TILE = 512
MERGE_SUB = 512
ROW_ALIGN = 16
SORT_SLOTS = 2 * ROUTE_TILE + N_EXPERTS * ROW_ALIGN
XS_W = D_MODEL + LANES


def _cparams(sem):
    return pltpu.CompilerParams(dimension_semantics=sem, vmem_limit_bytes=V7X_VMEM_LIMIT_BYTES)


def _adaln_kernel(c_ref, w_ref, b_ref, o_ref):
    c = c_ref[...]
    cs = c * jax.nn.sigmoid(c)
    o_ref[0] = jnp.dot(cs, w_ref[0], preferred_element_type=F32,
                       precision=lax.Precision.HIGHEST) + b_ref[0]


def _adaln(c, w_ada, b_ada):
    L, D, N = w_ada.shape
    B = c.shape[0]
    tn = 1536
    return pl.pallas_call(
        _adaln_kernel,
        grid=(L, N // tn),
        in_specs=[pl.BlockSpec((B, D), lambda l, j: (0, 0)),
                  pl.BlockSpec((1, D, tn), lambda l, j: (l, 0, j)),
                  pl.BlockSpec((1, 1, tn), lambda l, j: (l, 0, j))],
        out_specs=pl.BlockSpec((1, B, tn), lambda l, j: (l, 0, j)),
        out_shape=jax.ShapeDtypeStruct((L, B, N), F32),
        compiler_params=_cparams(("arbitrary", "arbitrary")),
        name="adaln",
    )(c, w_ada, b_ada.reshape(L, 1, N))


def _rope_kernel(pos_ref, inv_ref, ph_ref, o_ref):
    o_ref[...] = jnp.cos(pos_ref[...] * inv_ref[...] - ph_ref[...])


def _rope_table(positions):
    T = positions.size
    half = RET_DK // 2
    inv = ROPE_BASE ** (-jnp.arange(half, dtype=F32) / half)
    inv2 = jnp.concatenate([inv, inv]).reshape(1, LANES)
    phase = jnp.concatenate([jnp.zeros((half,), F32), jnp.full((half,), math.pi / 2, F32)]).reshape(1, LANES)
    posb = jnp.broadcast_to(positions.reshape(T, 1).astype(F32), (T, LANES))
    tm = 2048
    return pl.pallas_call(
        _rope_kernel,
        grid=(T // tm,),
        in_specs=[pl.BlockSpec((tm, LANES), lambda i: (i, 0)),
                  pl.BlockSpec((1, LANES), lambda i: (0, 0)),
                  pl.BlockSpec((1, LANES), lambda i: (0, 0))],
        out_specs=pl.BlockSpec((tm, LANES), lambda i: (i, 0)),
        out_shape=jax.ShapeDtypeStruct((T, LANES), F32),
        compiler_params=_cparams(("arbitrary",)),
        name="rope_table",
    )(posb, inv2, phase)


def _modulated_rmsnorm(x, g, mod_ref, shift_row, scale_row):
    y = x * lax.rsqrt(jnp.mean(x * x, axis=-1, keepdims=True) + EPS) * g
    return y * (1.0 + mod_ref[0, scale_row:scale_row + 1, :]) + mod_ref[0, shift_row:shift_row + 1, :]


def _prenorm_kernel(x_ref, mod_ref, g_ref, h_ref):
    h_ref[...] = _modulated_rmsnorm(x_ref[...], g_ref[...], mod_ref, 0, 1).astype(BF16)


def _prenorm(x2, mod, g_mix, S):
    T, D = x2.shape
    tm = 1024
    return pl.pallas_call(
        _prenorm_kernel,
        grid=(T // tm,),
        in_specs=[pl.BlockSpec((tm, D), lambda i: (i, 0)),
                  pl.BlockSpec((1, N_MOD, D), lambda i: (i * tm // S, 0, 0)),
                  pl.BlockSpec((1, D), lambda i: (0, 0))],
        out_specs=pl.BlockSpec((tm, D), lambda i: (i, 0)),
        out_shape=jax.ShapeDtypeStruct((T, D), BF16),
        compiler_params=_cparams(("arbitrary",)),
        name="prenorm",
    )(x2, mod, g_mix.reshape(1, D))


def _inproj_kernel(h_ref, w_ref, o_ref):
    o_ref[...] = jnp.dot(h_ref[...], w_ref[...], preferred_element_type=F32).astype(BF16)


def _inproj(h, w_in_b):
    T, D = h.shape
    N = w_in_b.shape[1]
    tm, tn = 1024, 2560
    return pl.pallas_call(
        _inproj_kernel,
        grid=(T // tm, N // tn),
        in_specs=[pl.BlockSpec((tm, D), lambda i, j: (i, 0)),
                  pl.BlockSpec((D, tn), lambda i, j: (0, j))],
        out_specs=pl.BlockSpec((tm, tn), lambda i, j: (i, j)),
        out_shape=jax.ShapeDtypeStruct((T, N), BF16),
        compiler_params=_cparams(("arbitrary", "arbitrary")),
        name="inproj",
    )(h, w_in_b)


def _retention_kernel(q_ref, k_ref, v_ref, g_ref, cs_ref, gn_ref, o_ref, state_ref, decay_ref):
    C = RET_BLOCK

    @pl.when(pl.program_id(1) == 0)
    def _():
        state_ref[...] = jnp.zeros_like(state_ref)

    @pl.when((pl.program_id(0) == 0) & (pl.program_id(1) == 0))
    def _():
        d = (lax.broadcasted_iota(jnp.int32, (C, C), 0) - lax.broadcasted_iota(jnp.int32, (C, C), 1)).astype(F32)
        for h in range(RET_HEADS):
            lg = math.log1p(-(2.0 ** (-5.0 - h)))
            decay_ref[h] = jnp.where(d >= 0.0, jnp.exp(lg * jnp.maximum(d, 0.0)), 0.0)

    cs = cs_ref[...]
    csr = pltpu.roll(cs, RET_DK // 2, 1)
    first = lax.broadcasted_iota(jnp.int32, (C, RET_DK), 1) < RET_DK // 2
    cos2 = jnp.where(first, cs, csr)
    sin2 = jnp.where(first, -csr, cs)
    row = lax.broadcasted_iota(jnp.int32, (C, 1), 0).astype(F32)
    for h in range(RET_HEADS):
        lg = math.log1p(-(2.0 ** (-5.0 - h)))
        sl = slice(h * RET_DK, (h + 1) * RET_DK)
        qh = q_ref[:, sl].astype(F32)
        kh = k_ref[:, sl].astype(F32)
        vh = v_ref[:, sl]
        qr = (qh * cos2 + pltpu.roll(qh, RET_DK // 2, 1) * sin2).astype(BF16)
        kr = (kh * cos2 + pltpu.roll(kh, RET_DK // 2, 1) * sin2) * (RET_DK ** -0.5)
        s = lax.dot_general(qr, kr.astype(BF16), (((1,), (1,)), ((), ())),
                            preferred_element_type=F32) * decay_ref[h]
        o = jnp.dot(s.astype(BF16), vh, preferred_element_type=F32)
        st = state_ref[h]
        o = o + jnp.dot(qr, st.astype(BF16), preferred_element_type=F32) * jnp.exp(lg * (row + 1.0))
        kd = (kr * jnp.exp(lg * (C - 1.0 - row))).astype(BF16)
        kv = lax.dot_general(kd, vh, (((0,), (0,)), ((), ())), preferred_element_type=F32)
        state_ref[h] = math.exp(lg * C) * st + kv
        mu = jnp.mean(o, axis=-1, keepdims=True)
        oc = o - mu
        var = jnp.mean(oc * oc, axis=-1, keepdims=True)
        on = oc * lax.rsqrt(var + EPS) * gn_ref[:, sl]
        gh = g_ref[:, sl].astype(F32)
        o_ref[:, sl] = (on * (gh * jax.nn.sigmoid(gh))).astype(BF16)


def _retention(proj, cs_tab, ret_gn, B, S):
    T = B * S
    C = RET_BLOCK
    nblk = S // C

    def col(c):
        return pl.BlockSpec((C, RET_W), lambda b, i, c=c: (b * nblk + i, c))

    return pl.pallas_call(
        _retention_kernel,
        grid=(B, nblk),
        in_specs=[col(COL_RQ), col(COL_RK), col(COL_RV), col(COL_RG),
                  pl.BlockSpec((C, LANES), lambda b, i: (b * nblk + i, 0)),
                  pl.BlockSpec((1, RET_W), lambda b, i: (0, 0))],
        out_specs=pl.BlockSpec((C, RET_W), lambda b, i: (b * nblk + i, 0)),
        out_shape=jax.ShapeDtypeStruct((T, RET_W), BF16),
        scratch_shapes=[pltpu.VMEM((RET_HEADS, RET_DK, RET_DK), F32),
                        pltpu.VMEM((RET_HEADS, C, C), F32)],
        compiler_params=_cparams(("arbitrary", "arbitrary")),
        name="retention",
    )(proj, proj, proj, proj, cs_tab, ret_gn.reshape(1, RET_W))


def _conv_kernel(a_ref, b_ref, w_ref, bias_ref, lg_ref, lb_ref, o_ref, z_ref, zs_ref):
    tm = CONV_TILE
    H = CONV_HIST
    SUB = 8

    @pl.when(pl.program_id(1) == 0)
    def _():
        z_ref[0:H, :] = jnp.zeros((H, CONV_W), F32)

    a = a_ref[...].astype(F32)
    b = b_ref[...].astype(F32)
    z_ref[H:H + tm, :] = a * jax.nn.sigmoid(b)
    base = H - (CONV_K - 1)
    n_shift = tm + (base + CONV_K - 2) // SUB * SUB
    for r in range(1, SUB):
        for c0 in range(0, n_shift, 128):
            cs = min(128, n_shift - c0)
            zs_ref[r - 1, c0:c0 + cs, :] = z_ref[c0 + r:c0 + r + cs, :]
    for r0 in range(0, tm, CONV_ROWS):
        acc = jnp.zeros((CONV_ROWS, CONV_W), F32)
        for j in range(CONV_K):
            al, r = (base + j) // SUB * SUB, (base + j) % SUB
            if r == 0:
                tap = z_ref[r0 + al:r0 + al + CONV_ROWS, :]
            else:
                tap = zs_ref[r - 1, r0 + al:r0 + al + CONV_ROWS, :]
            acc = acc + tap * w_ref[j:j + 1, :]
        acc = acc + bias_ref[...]
        mu = jnp.mean(acc, axis=-1, keepdims=True)
        ac = acc - mu
        var = jnp.mean(ac * ac, axis=-1, keepdims=True)
        y = ac * lax.rsqrt(var + EPS) * lg_ref[...] + lb_ref[...]
        o_ref[r0:r0 + CONV_ROWS, :] = (y * jax.nn.sigmoid(y)).astype(BF16)
    z_ref[0:H, :] = z_ref[tm:tm + H, :]


def _conv(proj, conv_w, conv_b, ln_g, ln_b, B, S):
    T = B * S
    tm = CONV_TILE
    nblk = S // tm
    vec = pl.BlockSpec((1, CONV_W), lambda b, i: (0, 0))
    return pl.pallas_call(
        _conv_kernel,
        grid=(B, nblk),
        in_specs=[pl.BlockSpec((tm, CONV_W), lambda b, i: (b * nblk + i, COL_CA)),
                  pl.BlockSpec((tm, CONV_W), lambda b, i: (b * nblk + i, COL_CB)),
                  pl.BlockSpec((CONV_K, CONV_W), lambda b, i: (0, 0)),
                  vec, vec, vec],
        out_specs=pl.BlockSpec((tm, CONV_W), lambda b, i: (b * nblk + i, 0)),
        out_shape=jax.ShapeDtypeStruct((T, CONV_W), BF16),
        scratch_shapes=[pltpu.VMEM((CONV_HIST + tm, CONV_W), F32),
                        pltpu.VMEM((7, CONV_HIST + tm, CONV_W), F32)],
        compiler_params=_cparams(("arbitrary", "arbitrary")),
        name="conv",
    )(proj, proj, conv_w.reshape(CONV_K, CONV_W), conv_b.reshape(1, CONV_W),
      ln_g.reshape(1, CONV_W), ln_b.reshape(1, CONV_W))


def _head_mean_sq(x, hm_ref):
    sq = x * x
    hi = sq.astype(BF16)
    lo = (sq - hi.astype(F32)).astype(BF16)
    tot = jnp.dot(hi, hm_ref[...], preferred_element_type=F32) + jnp.dot(lo, hm_ref[...], preferred_element_type=F32)
    return tot * (1.0 / ATT_DH)


def _attn_kernel(q_ref, k_ref, v_ref, rrow_ref, qg_ref, kg_ref, hm_ref, o_ref, kn_ref, vt_ref, bmt_ref, *, S):
    tq = ATT_QBLOCK
    qi = pl.program_id(1)
    heads = [slice(h * ATT_DH, (h + 1) * ATT_DH) for h in range(ATT_HEADS)]
    nkb = ATT_BAND // tq

    @pl.when((qi == 0) & (pl.program_id(0) == 0))
    def _():
        r = lax.broadcasted_iota(jnp.int32, (tq, ATT_BAND), 0)
        u = lax.broadcasted_iota(jnp.int32, (tq, ATT_BAND), 1)
        off = u - (r // CHUNK) * CHUNK
        inband = (off >= 0) & (off < (ATT_LEFT_CHUNKS + 1) * CHUNK)
        for h in range(ATT_HEADS):
            gen = jnp.broadcast_to(rrow_ref[h], (tq, rrow_ref.shape[-1]))
            t = pltpu.roll(gen, 0, 1, stride=1, stride_axis=0)
            bmt_ref[h] = jnp.where(inband, t[:, :ATT_BAND], NEG_INF).T

    @pl.when(qi == 0)
    def _():
        for h in range(ATT_HEADS):
            kn_ref[h, 0:ATT_PAD, :] = jnp.zeros((ATT_PAD, ATT_DH), BF16)
        for j in range(ATT_PAD // tq):
            vt_ref[j] = jnp.zeros((ATT_W, tq), BF16)
        rows = 512
        for r0 in range(0, S, rows):
            k = k_ref[r0:r0 + rows, :].astype(F32)
            kn = (k * lax.rsqrt(_head_mean_sq(k, hm_ref) + EPS) * kg_ref[...]).astype(BF16)
            for h in range(ATT_HEADS):
                kn_ref[h, ATT_PAD + r0:ATT_PAD + r0 + rows, :] = kn[:, heads[h]]
        for r0 in range(0, S, tq):
            vt_ref[(ATT_PAD + r0) // tq] = v_ref[r0:r0 + tq, :].astype(F32).T.astype(BF16)

    start = pl.multiple_of(qi * tq, tq)
    q = q_ref[...].astype(F32)
    qn = (q * lax.rsqrt(_head_mean_sq(q, hm_ref) + EPS) * qg_ref[...]).astype(BF16)

    def attend(mask_pad):
        if mask_pad:
            key_pos = lax.broadcasted_iota(jnp.int32, (ATT_BAND, LANES), 0) + start
            valid = key_pos >= ATT_PAD
        def scores(h):
            kb = kn_ref[h, pl.ds(start, ATT_BAND), :]
            return lax.dot_general(kb, qn[:, heads[h]], (((1,), (1,)), ((), ())), preferred_element_type=F32)

        def softmax(h, st):
            ps, ls = [], []
            for c0 in range(0, tq, LANES):
                ss = st[:, c0:c0 + LANES] + bmt_ref[h, :, c0:c0 + LANES]
                if mask_pad:
                    ss = jnp.where(valid, ss, NEG_INF)
                m = jnp.max(ss, axis=0, keepdims=True)
                p = jnp.exp2(ss - m)
                ls.append(jnp.sum(p, axis=0, keepdims=True))
                ps.append(p.astype(BF16))
            return jnp.concatenate(ps, axis=1), jnp.concatenate(ls, axis=1)

        def values(h, pt, l):
            ot = jnp.zeros((ATT_DH, tq), F32)
            for t in range(nkb):
                ot = ot + jnp.dot(vt_ref[qi + t, heads[h], :], pt[t * tq:(t + 1) * tq, :],
                                  preferred_element_type=F32)
            return ot / l

        sts, pls, outs = {}, {}, []
        for step in range(ATT_HEADS + 3):
            if step < ATT_HEADS:
                sts[step] = scores(step)
            if 0 <= step - 2 < ATT_HEADS:
                pls[step - 2] = softmax(step - 2, sts.pop(step - 2))
            if 0 <= step - 3 < ATT_HEADS:
                outs.append(values(step - 3, *pls.pop(step - 3)))
        o_ref[...] = jnp.concatenate(outs, axis=0).T.astype(BF16)

    first = ATT_PAD // tq

    @pl.when(qi < first)
    def _():
        attend(True)

    @pl.when(qi >= first)
    def _():
        attend(False)


def _attn_bias_row(rel_table):
    t = rel_table.astype(F32) * math.log2(math.e)
    edge = jnp.broadcast_to(t[:, 2 * MAX_REL:], (ATT_HEADS, ATT_QBLOCK))
    row = jnp.concatenate([edge, jnp.flip(t[:, 1:], axis=1), edge], axis=1)
    return row.reshape(ATT_HEADS, 1, 2 * ATT_QBLOCK + 2 * MAX_REL)


def _attention(proj, rel_table, q_gain, k_gain, B, S):
    T = B * S
    tq = ATT_QBLOCK
    nblk = S // tq
    gen = _attn_bias_row(rel_table)
    qg = jnp.tile(q_gain.astype(F32) * (ATT_DH ** -0.5 * math.log2(math.e)), ATT_HEADS).reshape(1, ATT_W)
    kg = jnp.tile(k_gain.astype(F32), ATT_HEADS).reshape(1, ATT_W)
    lane_head = jnp.arange(ATT_W) // ATT_DH
    hm = (lane_head[:, None] == lane_head[None, :]).astype(BF16)
    vec = pl.BlockSpec((1, ATT_W), lambda b, i: (0, 0))
    return pl.pallas_call(
        functools.partial(_attn_kernel, S=S),
        grid=(B, nblk),
        in_specs=[pl.BlockSpec((tq, ATT_W), lambda b, i: (b * nblk + i, COL_AQ)),
                  pl.BlockSpec((S, ATT_W), lambda b, i: (b, COL_AK)),
                  pl.BlockSpec((S, ATT_W), lambda b, i: (b, COL_AV)),
                  pl.BlockSpec(gen.shape, lambda b, i: (0, 0, 0)),
                  vec, vec,
                  pl.BlockSpec((ATT_W, ATT_W), lambda b, i: (0, 0))],
        out_specs=pl.BlockSpec((tq, ATT_W), lambda b, i: (b * nblk + i, 0)),
        out_shape=jax.ShapeDtypeStruct((T, ATT_W), BF16),
        scratch_shapes=[pltpu.VMEM((ATT_HEADS, ATT_PAD + S, ATT_DH), BF16),
                        pltpu.VMEM(((ATT_PAD + S) // tq, ATT_W, tq), BF16),
                        pltpu.VMEM((ATT_HEADS, ATT_BAND, tq), F32)],
        compiler_params=_cparams(("arbitrary", "arbitrary")),
        name="chunk_attention",
    )(proj, proj, proj, gen, qg, kg, hm)


def _merge_kernel(x_ref, ret_ref, conv_ref, att_ref, g0_ref, g1_ref, g2_ref, bg_ref, mod_ref,
                  wr_ref, wc_ref, wa_ref, wo_ref, gf_ref, wrt_ref, brt_ref,
                  xo_ref, h2_ref, route_ref, cnt_out_ref, tri_ref):
    D = D_MODEL

    @pl.when(pl.program_id(0) == 0)
    def _():
        n = tri_ref.shape[0]
        tri_ref[...] = (lax.broadcasted_iota(jnp.int32, (n, n), 1)
                        < lax.broadcasted_iota(jnp.int32, (n, n), 0)).astype(BF16)

    def mix(rs):
        def gated(g_ref, k, b_ref, w_ref):
            gate = jax.nn.sigmoid(g_ref[rs, :].astype(F32) + bg_ref[:, k * D:(k + 1) * D])
            return gate * jnp.dot(b_ref[rs, :], w_ref[...], preferred_element_type=F32)

        merged = (gated(g0_ref, 0, ret_ref, wr_ref) + gated(g1_ref, 1, conv_ref, wc_ref)
                  + gated(g2_ref, 2, att_ref, wa_ref))
        return merged.astype(BF16)

    def project(rs, merged):
        y = jnp.dot(merged, wo_ref[...], preferred_element_type=F32)
        x = x_ref[rs, :] + mod_ref[0, 2:3, :] * y
        xo_ref[rs, :] = x
        h2 = x * lax.rsqrt(jnp.mean(x * x, axis=-1, keepdims=True) + EPS) * gf_ref[...]
        h2 = h2 * (1.0 + mod_ref[0, 4:5, :]) + mod_ref[0, 3:4, :]
        h_hi = h2.astype(BF16)
        h2_ref[rs, :] = h_hi
        h_lo = (h2 - h_hi.astype(F32)).astype(BF16)
        both = jnp.dot(h_hi, wrt_ref[...], preferred_element_type=F32)
        return (both[:, :LANES] + both[:, LANES:]
                + jnp.dot(h_lo, wrt_ref[:, :LANES], preferred_element_type=F32) + brt_ref[...])

    subs = [slice(r0, r0 + MERGE_SUB) for r0 in range(0, x_ref.shape[0], MERGE_SUB)]
    mixed, parts = {}, []
    for step in range(len(subs) + 1):
        if step < len(subs):
            mixed[step] = mix(subs[step])
        if step >= 1:
            parts.append(project(subs[step - 1], mixed.pop(step - 1)))
    logits = jnp.concatenate(parts, axis=0)
    tm = logits.shape[0]
    lane = lax.broadcasted_iota(jnp.int32, (tm, LANES), 1)
    big = jnp.int32(LANES)
    is_g = (lane >= N_EXPERTS) & (lane < N_EXPERTS + N_GROUPS)
    gl = jnp.where(is_g, logits, -jnp.inf)
    gmax = jnp.max(gl, axis=-1, keepdims=True)
    gsel = jnp.min(jnp.where(gl == gmax, lane - N_EXPERTS, big), axis=-1, keepdims=True)
    p_group = 1.0 / jnp.sum(jnp.where(is_g, jnp.exp(gl - gmax), 0.0), axis=-1, keepdims=True)
    in_grp = (lane >= gsel * EXPERTS_PER_GROUP) & (lane < (gsel + 1) * EXPERTS_PER_GROUP)
    ch = jnp.where(in_grp, logits, -jnp.inf)
    v1 = jnp.max(ch, axis=-1, keepdims=True)
    i1 = jnp.min(jnp.where(ch == v1, lane, big), axis=-1, keepdims=True)
    ch2 = jnp.where(lane == i1, -jnp.inf, ch)
    v2 = jnp.max(ch2, axis=-1, keepdims=True)
    i2 = jnp.min(jnp.where(ch2 == v2, lane, big), axis=-1, keepdims=True)
    e = jnp.exp(v2 - v1)
    w1 = p_group / (1.0 + e)
    w2 = p_group * e / (1.0 + e)
    sel = ((lane == i1) | (lane == i2)).astype(BF16)
    before = jnp.dot(tri_ref[...], sel, preferred_element_type=F32)
    rank1 = jnp.sum(jnp.where(lane == i1, before, 0.0), axis=-1, keepdims=True)
    rank2 = jnp.sum(jnp.where(lane == i2, before, 0.0), axis=-1, keepdims=True)
    cnt_out_ref[0] = jnp.sum(sel.astype(F32), axis=0, keepdims=True)
    vals = (i1.astype(F32), i2.astype(F32), w1, w2, rank1, rank2)
    route = jnp.zeros((tm, LANES), F32)
    for k, v in enumerate(vals):
        route = jnp.where(lane == k, v, route)
    route_ref[...] = route


def _merge(x2, proj, ret, conv, att, b_gate, mod, wr, wc, wa, wo, g_ffn, w_rt, b_rt, S):
    T, D = x2.shape
    tm = ROUTE_TILE
    row = lambda i: (i, 0)
    const = lambda i: (0, 0)
    br = pl.BlockSpec((tm, RET_W), row)
    return pl.pallas_call(
        _merge_kernel,
        grid=(T // tm,),
        in_specs=[pl.BlockSpec((tm, D), row), br, br, br,
                  pl.BlockSpec((tm, D), lambda i: (i, COL_GATE)),
                  pl.BlockSpec((tm, D), lambda i: (i, COL_GATE + 1)),
                  pl.BlockSpec((tm, D), lambda i: (i, COL_GATE + 2)),
                  pl.BlockSpec((1, N_BRANCH * D), const),
                  pl.BlockSpec((1, N_MOD, D), lambda i: (i * tm // S, 0, 0)),
                  pl.BlockSpec((RET_W, D), const), pl.BlockSpec((CONV_W, D), const),
                  pl.BlockSpec((ATT_W, D), const), pl.BlockSpec((D, D), const),
                  pl.BlockSpec((1, D), const),
                  pl.BlockSpec((D, 2 * LANES), const), pl.BlockSpec((1, LANES), const)],
        out_specs=[pl.BlockSpec((tm, D), row), pl.BlockSpec((tm, D), row),
                   pl.BlockSpec((tm, LANES), row), pl.BlockSpec((1, 1, LANES), lambda i: (i, 0, 0))],
        out_shape=[jax.ShapeDtypeStruct((T, D), F32), jax.ShapeDtypeStruct((T, D), BF16),
                   jax.ShapeDtypeStruct((T, LANES), F32), jax.ShapeDtypeStruct((T // tm, 1, LANES), F32)],
        scratch_shapes=[pltpu.VMEM((tm, tm), BF16)],
        compiler_params=_cparams(("arbitrary",)),
        name="merge_router",
    )(x2, ret, conv, att, proj, proj, proj, b_gate.reshape(1, N_BRANCH * D), mod,
      wr, wc, wa, wo, g_ffn.reshape(1, D), w_rt, b_rt)


def _pair_slots(route, srcoff_row):
    tm = route.shape[0]
    lane = lax.broadcasted_iota(jnp.int32, (tm, LANES), 1).astype(F32)
    s0 = jnp.sum(jnp.where(lane == route[:, 0:1], srcoff_row, 0.0), axis=-1, keepdims=True) + route[:, 4:5]
    s1 = jnp.sum(jnp.where(lane == route[:, 1:2], srcoff_row, 0.0), axis=-1, keepdims=True) + route[:, 5:6]
    return s0, s1


def _chunk_copies(tile, so_ref, nch_ref, dst_ref, make_copy):
    base = tile * N_EXPERTS
    for e in range(N_EXPERTS):
        so = so_ref[base + e]
        d = dst_ref[base + e]

        def body(c, carry, so=so, d=d):
            make_copy(pl.multiple_of(so + c * ROW_ALIGN, ROW_ALIGN),
                      pl.multiple_of(d + c * ROW_ALIGN, ROW_ALIGN)).start()
            return carry

        lax.fori_loop(0, nch_ref[base + e], body, 0)


def _dispatch_kernel(so_ref, nch_ref, dst_ref, ntot_ref, zpos_ref, h_ref, route_ref, sof_ref,
                     xs_hbm, sorted_ref, zero_ref, sem, zsem, *, nsteps):
    i = pl.program_id(0)

    @pl.when(i == 0)
    def _():
        zero_ref[...] = jnp.zeros_like(zero_ref)
        for e in range(N_EXPERTS):
            pltpu.make_async_copy(zero_ref, xs_hbm.at[pl.ds(pl.multiple_of(zpos_ref[e], ROW_ALIGN), MOE_TILE), :],
                                  zsem).start()
        for e in range(N_EXPERTS):
            pltpu.make_async_copy(zero_ref, xs_hbm.at[pl.ds(0, MOE_TILE), :], zsem).wait()
        tail = zpos_ref[N_EXPERTS]
        n_tail = (xs_hbm.shape[0] - tail) // MOE_TILE

        def zero_tail(c, carry):
            cp = pltpu.make_async_copy(
                zero_ref, xs_hbm.at[pl.ds(pl.multiple_of(tail + c * MOE_TILE, MOE_TILE), MOE_TILE), :], zsem)
            cp.start()
            cp.wait()
            return carry

        lax.fori_loop(0, n_tail, zero_tail, 0)

    r = route_ref[...]
    tm = r.shape[0]
    lane = lax.broadcasted_iota(jnp.int32, (tm, LANES), 1)
    s0, s1 = _pair_slots(r, sof_ref[0])
    slots = lax.broadcasted_iota(jnp.int32, (tm, SORT_SLOTS), 1).astype(F32)
    onehot = ((slots == s0) | (slots == s1)).astype(BF16)

    def split(w):
        hi = w.astype(BF16).astype(F32)
        return hi, (w - hi).astype(BF16).astype(F32)

    extra = jnp.zeros((tm, LANES), F32)
    for k, v in enumerate((r[:, 0:1], *split(r[:, 2:3]), *split(r[:, 3:4]))):
        extra = jnp.where(lane == k, v, extra)
    hx = jnp.concatenate([h_ref[...], extra.astype(BF16)], axis=1)
    slot = i % 2
    sorted_ref[slot] = lax.dot_general(onehot, hx, (((0,), (0,)), ((), ())),
                                       preferred_element_type=F32).astype(BF16)

    def copies(s):
        def make_copy(src_row, dst_row):
            return pltpu.make_async_copy(sorted_ref.at[s, pl.ds(src_row, ROW_ALIGN), :],
                                         xs_hbm.at[pl.ds(dst_row, ROW_ALIGN), :], sem.at[s])
        return make_copy

    _chunk_copies(i, so_ref, nch_ref, dst_ref, copies(slot))

    def wait_all(tile, s):
        def wait_one(c, carry):
            copies(s)(0, 0).wait()
            return carry

        lax.fori_loop(0, ntot_ref[tile], wait_one, 0)

    @pl.when(i > 0)
    def _():
        wait_all(i - 1, 1 - slot)

    @pl.when(i == nsteps - 1)
    def _():
        wait_all(i, slot)


def _dispatch(h2, route, tabs, n_rows):
    T, D = h2.shape
    tm = ROUTE_TILE
    grid_spec = pltpu.PrefetchScalarGridSpec(
        num_scalar_prefetch=5,
        grid=(T // tm,),
        in_specs=[pl.BlockSpec((tm, D), lambda i, *_: (i, 0)),
                  pl.BlockSpec((tm, LANES), lambda i, *_: (i, 0)),
                  pl.BlockSpec((1, 1, LANES), lambda i, *_: (i, 0, 0))],
        out_specs=pl.BlockSpec(memory_space=pl.ANY),
        scratch_shapes=[pltpu.VMEM((2, SORT_SLOTS, XS_W), BF16), pltpu.VMEM((MOE_TILE, XS_W), BF16),
                        pltpu.SemaphoreType.DMA((2,)), pltpu.SemaphoreType.DMA(())],
    )
    return pl.pallas_call(
        functools.partial(_dispatch_kernel, nsteps=T // tm),
        grid_spec=grid_spec,
        out_shape=jax.ShapeDtypeStruct((n_rows + MOE_TILE, XS_W), BF16),
        compiler_params=_cparams(("arbitrary",)),
        name="moe_dispatch",
    )(tabs["srcoff"], tabs["nchunk"], tabs["dst"], tabs["ntot"], tabs["zpos"], h2, route, tabs["srcoff_f"])


def _ffn_kernel(te_ref, nv_ref, xs_ref, wu_ref, wd_ref, o_ref, wub_ref, wdb_ref):
    i = pl.program_id(0)

    @pl.when((i == 0) | (te_ref[i] != te_ref[jnp.maximum(i - 1, 0)]))
    def _():
        wub_ref[...] = wu_ref[0].astype(BF16)
        wdb_ref[...] = wd_ref[0].astype(BF16)

    @pl.when(i < nv_ref[0])
    def _():
        x = xs_ref[:, :D_MODEL]
        wv = xs_ref[:, D_MODEL:].astype(F32)
        mine = wv[:, 0:1] == te_ref[i].astype(F32)
        w = jnp.where(mine, wv[:, 1:2] + wv[:, 2:3], wv[:, 3:4] + wv[:, 4:5])
        hid = jnp.dot(x, wub_ref[...], preferred_element_type=F32)
        a = hid[:, :EXPERT_HIDDEN]
        b = hid[:, EXPERT_HIDDEN:]
        act = a * jax.nn.sigmoid(a) * b * w
        o_ref[...] = jnp.dot(act.astype(BF16), wdb_ref[...], preferred_element_type=F32).astype(BF16)

    @pl.when(i >= nv_ref[0])
    def _():
        o_ref[...] = jnp.zeros_like(o_ref)


def _ffn(xs, tile_expert, n_valid, w_up, w_down, layer, n_rows):
    tr = MOE_TILE
    D = D_MODEL
    F2 = w_up.shape[-1]
    first = layer * N_EXPERTS
    grid_spec = pltpu.PrefetchScalarGridSpec(
        num_scalar_prefetch=2,
        grid=(n_rows // tr,),
        in_specs=[pl.BlockSpec((tr, XS_W), lambda i, te, nv: (i, 0)),
                  pl.BlockSpec((1, D, F2), lambda i, te, nv: (first + te[i], 0, 0)),
                  pl.BlockSpec((1, EXPERT_HIDDEN, D), lambda i, te, nv: (first + te[i], 0, 0))],
        out_specs=pl.BlockSpec((tr, D), lambda i, te, nv: (i, 0)),
        scratch_shapes=[pltpu.VMEM((D, F2), BF16), pltpu.VMEM((EXPERT_HIDDEN, D), BF16)],
    )
    return pl.pallas_call(
        _ffn_kernel,
        grid_spec=grid_spec,
        out_shape=jax.ShapeDtypeStruct((n_rows, D), BF16),
        compiler_params=_cparams(("arbitrary",)),
        name="moe_experts",
    )(tile_expert, n_valid, xs, w_up, w_down)


def _combine_kernel(so_ref, nch_ref, dst_ref, ntot_ref, x_ref, route_ref, sof_ref, mod_ref, ye_hbm, *rest,
                    nsteps, emit_next):
    if emit_next:
        modn_ref, gn_ref, o_ref, hn_ref, ys_ref, sem = rest
    else:
        o_ref, ys_ref, sem = rest
    i = pl.program_id(0)

    def copies(slot):
        def make_copy(src_row, dst_row):
            return pltpu.make_async_copy(ye_hbm.at[pl.ds(dst_row, ROW_ALIGN), :],
                                         ys_ref.at[slot, pl.ds(src_row, ROW_ALIGN), :], sem.at[slot])
        return make_copy

    @pl.when(i == 0)
    def _():
        ys_ref[...] = jnp.zeros_like(ys_ref)
        _chunk_copies(0, so_ref, nch_ref, dst_ref, copies(0))

    @pl.when(i + 1 < nsteps)
    def _():
        _chunk_copies(i + 1, so_ref, nch_ref, dst_ref, copies((i + 1) % 2))

    slot = i % 2

    def wait_one(c, carry):
        copies(slot)(0, 0).wait()
        return carry

    lax.fori_loop(0, ntot_ref[i], wait_one, 0)
    s0, s1 = _pair_slots(route_ref[...], sof_ref[0])
    slots = lax.broadcasted_iota(jnp.int32, (s0.shape[0], SORT_SLOTS), 1).astype(F32)
    onehot = ((slots == s0) | (slots == s1)).astype(BF16)
    y = jnp.dot(onehot, ys_ref[slot], preferred_element_type=F32)
    x = x_ref[...] + mod_ref[0, 5:6, :] * y
    o_ref[...] = x
    if emit_next:
        hn_ref[...] = _modulated_rmsnorm(x, gn_ref[...], modn_ref, 0, 1).astype(BF16)


def _combine(x2, route, mod, ye, tabs, S, next_mod=None, next_gain=None):
    T, D = x2.shape
    tm = ROUTE_TILE
    nsteps = T // tm
    emit_next = next_mod is not None
    row = pl.BlockSpec((tm, D), lambda i, *_: (i, 0))
    modspec = pl.BlockSpec((1, N_MOD, D), lambda i, *_: (i * tm // S, 0, 0))
    in_specs = [row,
                pl.BlockSpec((tm, LANES), lambda i, *_: (i, 0)),
                pl.BlockSpec((1, 1, LANES), lambda i, *_: (i, 0, 0)),
                modspec,
                pl.BlockSpec(memory_space=pl.ANY)]
    args = [x2, route, tabs["srcoff_f"], mod, ye]
    out_specs, out_shape = row, jax.ShapeDtypeStruct((T, D), F32)
    if emit_next:
        in_specs += [modspec, pl.BlockSpec((1, D), lambda i, *_: (0, 0))]
        args += [next_mod, next_gain.reshape(1, D)]
        out_specs, out_shape = [row, row], [out_shape, jax.ShapeDtypeStruct((T, D), BF16)]
    grid_spec = pltpu.PrefetchScalarGridSpec(
        num_scalar_prefetch=4,
        grid=(nsteps,),
        in_specs=in_specs,
        out_specs=out_specs,
        scratch_shapes=[pltpu.VMEM((2, SORT_SLOTS, D), BF16), pltpu.SemaphoreType.DMA((2,))],
    )
    return pl.pallas_call(
        functools.partial(_combine_kernel, nsteps=nsteps, emit_next=emit_next),
        grid_spec=grid_spec,
        out_shape=out_shape,
        compiler_params=_cparams(("arbitrary",)),
        name="moe_combine",
    )(tabs["srcoff"], tabs["nchunk"], tabs["dst"], tabs["ntot"], *args)


def _moe_rows_bound(T):
    nt = T // ROUTE_TILE
    rows = 2 * T + nt * N_EXPERTS * (ROW_ALIGN - 1) + N_EXPERTS * (MOE_TILE - 1)
    return -(-rows // MOE_TILE) * MOE_TILE


def _routing_tables(counts, n_rows):
    tr = MOE_TILE
    c = counts[:, 0, :N_EXPERTS].astype(jnp.int32)
    c16 = (c + ROW_ALIGN - 1) // ROW_ALIGN * ROW_ALIGN
    srcoff = jnp.cumsum(c16, axis=1) - c16
    rows = jnp.sum(c16, axis=0)
    ends = jnp.cumsum((rows + tr - 1) // tr * tr)
    offs = ends - (rows + tr - 1) // tr * tr
    dst = offs[None, :] + jnp.cumsum(c16, axis=0) - c16
    tile_start = jnp.arange(n_rows // tr, dtype=jnp.int32) * tr
    tile_expert = jnp.minimum(jnp.sum((tile_start[:, None] >= ends[None, :]).astype(jnp.int32), axis=-1),
                              N_EXPERTS - 1)
    srcoff_f = jnp.zeros((c.shape[0], 1, LANES), F32).at[:, 0, :N_EXPERTS].set(srcoff.astype(F32))
    return {
        "srcoff": srcoff.reshape(-1), "nchunk": (c16 // ROW_ALIGN).reshape(-1), "dst": dst.reshape(-1),
        "ntot": jnp.sum(c16 // ROW_ALIGN, axis=1), "zpos": jnp.concatenate([offs + rows, ends[-1:]]),
        "srcoff_f": srcoff_f,
        "tile_expert": tile_expert, "n_valid": (ends[-1] // tr).astype(jnp.int32).reshape(1),
    }


def _reorder_in_cols(w_in):
    cuts = 4 * RET_W + 2 * CONV_W + 3 * ATT_W
    return jnp.concatenate([w_in[..., cuts:], w_in[..., :cuts]], axis=-1)


def kernel(x, c, positions, w_ada, b_ada, g_mix, g_ffn, w_in, b_gate, ret_gn, conv_w, conv_b, conv_ln_g, conv_ln_b, att_q_gain, att_k_gain, att_rel_bias, w_ret_out, w_conv_out, w_att_out, w_out, w_group, b_group, w_inner, b_inner, w_up, w_down):
    B, S, D = x.shape
    L = w_ada.shape[0]
    T = B * S
    assert D == D_MODEL and S % 1024 == 0 and T % 2048 == 0
    n_rows = _moe_rows_bound(T)

    mod_all = _adaln(c, w_ada, b_ada).reshape(L, B, N_MOD, D)
    cs_tab = _rope_table(positions)
    x2 = x.reshape(T, D)
    h = _prenorm(x2, mod_all[0], g_mix[0], S)
    for l in range(L):
        mod = mod_all[l]
        proj = _inproj(h, _reorder_in_cols(w_in[l]).astype(BF16))
        ret = _retention(proj, cs_tab, ret_gn[l], B, S)
        conv = _conv(proj, conv_w[l], conv_b[l], conv_ln_g[l], conv_ln_b[l], B, S)
        att = _attention(proj, att_rel_bias[l], att_q_gain[l], att_k_gain[l], B, S)
        w_rt = jnp.zeros((D, LANES), F32)
        w_rt = w_rt.at[:, :N_EXPERTS].set(w_inner[l]).at[:, N_EXPERTS:N_EXPERTS + N_GROUPS].set(w_group[l])
        b_rt = jnp.zeros((1, LANES), F32)
        b_rt = b_rt.at[0, :N_EXPERTS].set(b_inner[l]).at[0, N_EXPERTS:N_EXPERTS + N_GROUPS].set(b_group[l])
        w_rt_hi = w_rt.astype(BF16)
        w_rt2 = jnp.concatenate([w_rt_hi, (w_rt - w_rt_hi.astype(F32)).astype(BF16)], axis=1)
        x2, h2, route, counts = _merge(x2, proj, ret, conv, att, b_gate[l], mod,
                                       w_ret_out[l].astype(BF16), w_conv_out[l].astype(BF16),
                                       w_att_out[l].astype(BF16), w_out[l].astype(BF16),
                                       g_ffn[l], w_rt2, b_rt, S)
        tabs = _routing_tables(counts, n_rows)
        xs = _dispatch(h2, route, tabs, n_rows)
        ye = _ffn(xs, tabs["tile_expert"], tabs["n_valid"],
                  w_up.reshape(L * N_EXPERTS, D, 2 * EXPERT_HIDDEN),
                  w_down.reshape(L * N_EXPERTS, EXPERT_HIDDEN, D), l, n_rows)
        if l + 1 < L:
            x2, h = _combine(x2, route, mod, ye, tabs, S, mod_all[l + 1], g_mix[l + 1])
        else:
            x2 = _combine(x2, route, mod, ye, tabs, S)
    return x2.reshape(B, S, D)
```

```python
import functools
import math

import jax
import jax.numpy as jnp
from jax import lax
from jax.experimental import pallas as pl
from jax.experimental.pallas import tpu as pltpu

F32 = jnp.float32
BF16 = jnp.bfloat16

D_MODEL = 1024
CHUNK = 64
EPS = 1e-6
NEG_INF = -1e30
RET_HEADS = 4
RET_DK = 128
RET_W = 512
ROPE_BASE = 10000.0
CONV_W = 512
CONV_K = 31
ATT_HEADS = 8
ATT_DH = 64
ATT_W = 512
ATT_LEFT_CHUNKS = 8
MAX_REL = 256
N_BRANCH = 3
IN_COLS = 4 * RET_W + 2 * CONV_W + 3 * ATT_W + N_BRANCH * D_MODEL
N_GROUPS = 4
EXPERTS_PER_GROUP = 8
N_EXPERTS = N_GROUPS * EXPERTS_PER_GROUP
EXPERT_HIDDEN = 256
N_MOD = 6

LANES = 128
V7X_VMEM_LIMIT_BYTES = 56 * 1024 * 1024

COL_GATE = 0
COL_RQ, COL_RK, COL_RV, COL_RG = 6, 7, 8, 9
COL_CA, COL_CB = 10, 11
COL_AQ, COL_AK, COL_AV = 12, 13, 14

RET_BLOCK = 256
ATT_QBLOCK = 256
ATT_BAND = ATT_QBLOCK + ATT_LEFT_CHUNKS * CHUNK
ATT_PAD = ATT_LEFT_CHUNKS * CHUNK
CONV_TILE = 512
CONV_HIST = 32
CONV_ROWS = 64
MOE_TILE = 512
ROUTE_TILE = 512
MERGE_SUB = 512
ROW_ALIGN = 16
SORT_SLOTS = 2 * ROUTE_TILE + N_EXPERTS * ROW_ALIGN
XS_W = D_MODEL + LANES


def _cparams(sem):
    return pltpu.CompilerParams(dimension_semantics=sem, vmem_limit_bytes=V7X_VMEM_LIMIT_BYTES)


def _adaln_kernel(c_ref, w_ref, b_ref, o_ref):
    c = c_ref[...]
    cs = c * jax.nn.sigmoid(c)
    o_ref[0] = jnp.dot(cs, w_ref[0], preferred_element_type=F32,
                       precision=lax.Precision.HIGHEST) + b_ref[0]


def _adaln(c, w_ada, b_ada):
    L, D, N = w_ada.shape
    B = c.shape[0]
    tn = 1536
    return pl.pallas_call(
        _adaln_kernel,
        grid=(L, N // tn),
        in_specs=[pl.BlockSpec((B, D), lambda l, j: (0, 0)),
                  pl.BlockSpec((1, D, tn), lambda l, j: (l, 0, j)),
                  pl.BlockSpec((1, 1, tn), lambda l, j: (l, 0, j))],
        out_specs=pl.BlockSpec((1, B, tn), lambda l, j: (l, 0, j)),
        out_shape=jax.ShapeDtypeStruct((L, B, N), F32),
        compiler_params=_cparams(("arbitrary", "arbitrary")),
        name="adaln",
    )(c, w_ada, b_ada.reshape(L, 1, N))


def _rope_kernel(pos_ref, inv_ref, ph_ref, o_ref):
    o_ref[...] = jnp.cos(pos_ref[...] * inv_ref[...] - ph_ref[...])


def _rope_table(positions):
    T = positions.size
    half = RET_DK // 2
    inv = ROPE_BASE ** (-jnp.arange(half, dtype=F32) / half)
    inv2 = jnp.concatenate([inv, inv]).reshape(1, LANES)
    phase = jnp.concatenate([jnp.zeros((half,), F32), jnp.full((half,), math.pi / 2, F32)]).reshape(1, LANES)
    posb = jnp.broadcast_to(positions.reshape(T, 1).astype(F32), (T, LANES))
    tm = 2048
    return pl.pallas_call(
        _rope_kernel,
        grid=(T // tm,),
        in_specs=[pl.BlockSpec((tm, LANES), lambda i: (i, 0)),
                  pl.BlockSpec((1, LANES), lambda i: (0, 0)),
                  pl.BlockSpec((1, LANES), lambda i: (0, 0))],
        out_specs=pl.BlockSpec((tm, LANES), lambda i: (i, 0)),
        out_shape=jax.ShapeDtypeStruct((T, LANES), F32),
        compiler_params=_cparams(("arbitrary",)),
        name="rope_table",
    )(posb, inv2, phase)


def _modulated_rmsnorm(x, g, mod_ref, shift_row, scale_row):
    y = x * lax.rsqrt(jnp.mean(x * x, axis=-1, keepdims=True) + EPS) * g
    return y * (1.0 + mod_ref[0, scale_row:scale_row + 1, :]) + mod_ref[0, shift_row:shift_row + 1, :]


def _prenorm_kernel(x_ref, mod_ref, g_ref, h_ref):
    h_ref[...] = _modulated_rmsnorm(x_ref[...], g_ref[...], mod_ref, 0, 1).astype(BF16)


def _prenorm(x2, mod, g_mix, S):
    T, D = x2.shape
    tm = 1024
    return pl.pallas_call(
        _prenorm_kernel,
        grid=(T // tm,),
        in_specs=[pl.BlockSpec((tm, D), lambda i: (i, 0)),
                  pl.BlockSpec((1, N_MOD, D), lambda i: (i * tm // S, 0, 0)),
                  pl.BlockSpec((1, D), lambda i: (0, 0))],
        out_specs=pl.BlockSpec((tm, D), lambda i: (i, 0)),
        out_shape=jax.ShapeDtypeStruct((T, D), BF16),
        compiler_params=_cparams(("arbitrary",)),
        name="prenorm",
    )(x2, mod, g_mix.reshape(1, D))


def _inproj_kernel(h_ref, w_ref, o_ref):
    o_ref[...] = jnp.dot(h_ref[...], w_ref[...], preferred_element_type=F32).astype(BF16)


def _inproj(h, w_in_b):
    T, D = h.shape
    N = w_in_b.shape[1]
    tm, tn = 1024, 2560
    return pl.pallas_call(
        _inproj_kernel,
        grid=(T // tm, N // tn),
        in_specs=[pl.BlockSpec((tm, D), lambda i, j: (i, 0)),
                  pl.BlockSpec((D, tn), lambda i, j: (0, j))],
        out_specs=pl.BlockSpec((tm, tn), lambda i, j: (i, j)),
        out_shape=jax.ShapeDtypeStruct((T, N), BF16),
        compiler_params=_cparams(("arbitrary", "arbitrary")),
        name="inproj",
    )(h, w_in_b)


def _retention_kernel(q_ref, k_ref, v_ref, g_ref, cs_ref, gn_ref, o_ref, state_ref, decay_ref):
    C = RET_BLOCK

    @pl.when(pl.program_id(1) == 0)
    def _():
        state_ref[...] = jnp.zeros_like(state_ref)

    @pl.when((pl.program_id(0) == 0) & (pl.program_id(1) == 0))
    def _():
        d = (lax.broadcasted_iota(jnp.int32, (C, C), 0) - lax.broadcasted_iota(jnp.int32, (C, C), 1)).astype(F32)
        for h in range(RET_HEADS):
            lg = math.log1p(-(2.0 ** (-5.0 - h)))
            decay_ref[h] = jnp.where(d >= 0.0, jnp.exp(lg * jnp.maximum(d, 0.0)), 0.0)

    cs = cs_ref[...]
    csr = pltpu.roll(cs, RET_DK // 2, 1)
    first = lax.broadcasted_iota(jnp.int32, (C, RET_DK), 1) < RET_DK // 2
    cos2 = jnp.where(first, cs, csr)
    sin2 = jnp.where(first, -csr, cs)
    row = lax.broadcasted_iota(jnp.int32, (C, 1), 0).astype(F32)
    for h in range(RET_HEADS):
        lg = math.log1p(-(2.0 ** (-5.0 - h)))
        sl = slice(h * RET_DK, (h + 1) * RET_DK)
        qh = q_ref[:, sl].astype(F32)
        kh = k_ref[:, sl].astype(F32)
        vh = v_ref[:, sl]
        qr = (qh * cos2 + pltpu.roll(qh, RET_DK // 2, 1) * sin2).astype(BF16)
        kr = (kh * cos2 + pltpu.roll(kh, RET_DK // 2, 1) * sin2) * (RET_DK ** -0.5)
        s = lax.dot_general(qr, kr.astype(BF16), (((1,), (1,)), ((), ())),
                            preferred_element_type=F32) * decay_ref[h]
        o = jnp.dot(s.astype(BF16), vh, preferred_element_type=F32)
        st = state_ref[h]
        o = o + jnp.dot(qr, st.astype(BF16), preferred_element_type=F32) * jnp.exp(lg * (row + 1.0))
        kd = (kr * jnp.exp(lg * (C - 1.0 - row))).astype(BF16)
        kv = lax.dot_general(kd, vh, (((0,), (0,)), ((), ())), preferred_element_type=F32)
        state_ref[h] = math.exp(lg * C) * st + kv
        mu = jnp.mean(o, axis=-1, keepdims=True)
        oc = o - mu
        var = jnp.mean(oc * oc, axis=-1, keepdims=True)
        on = oc * lax.rsqrt(var + EPS) * gn_ref[:, sl]
        gh = g_ref[:, sl].astype(F32)
        o_ref[:, sl] = (on * (gh * jax.nn.sigmoid(gh))).astype(BF16)


def _retention(proj, cs_tab, ret_gn, B, S):
    T = B * S
    C = RET_BLOCK
    nblk = S // C

    def col(c):
        return pl.BlockSpec((C, RET_W), lambda b, i, c=c: (b * nblk + i, c))

    return pl.pallas_call(
        _retention_kernel,
        grid=(B, nblk),
        in_specs=[col(COL_RQ), col(COL_RK), col(COL_RV), col(COL_RG),
                  pl.BlockSpec((C, LANES), lambda b, i: (b * nblk + i, 0)),
                  pl.BlockSpec((1, RET_W), lambda b, i: (0, 0))],
        out_specs=pl.BlockSpec((C, RET_W), lambda b, i: (b * nblk + i, 0)),
        out_shape=jax.ShapeDtypeStruct((T, RET_W), BF16),
        scratch_shapes=[pltpu.VMEM((RET_HEADS, RET_DK, RET_DK), F32),
                        pltpu.VMEM((RET_HEADS, C, C), F32)],
        compiler_params=_cparams(("arbitrary", "arbitrary")),
        name="retention",
    )(proj, proj, proj, proj, cs_tab, ret_gn.reshape(1, RET_W))


def _conv_kernel(a_ref, b_ref, w_ref, bias_ref, lg_ref, lb_ref, o_ref, z_ref, zs_ref):
    tm = CONV_TILE
    H = CONV_HIST
    SUB = 8

    @pl.when(pl.program_id(1) == 0)
    def _():
        z_ref[0:H, :] = jnp.zeros((H, CONV_W), F32)

    a = a_ref[...].astype(F32)
    b = b_ref[...].astype(F32)
    z_ref[H:H + tm, :] = a * jax.nn.sigmoid(b)
    base = H - (CONV_K - 1)
    n_shift = tm + (base + CONV_K - 2) // SUB * SUB
    for r in range(1, SUB):
        for c0 in range(0, n_shift, 128):
            cs = min(128, n_shift - c0)
            zs_ref[r - 1, c0:c0 + cs, :] = z_ref[c0 + r:c0 + r + cs, :]
    for r0 in range(0, tm, CONV_ROWS):
        acc = jnp.zeros((CONV_ROWS, CONV_W), F32)
        for j in range(CONV_K):
            al, r = (base + j) // SUB * SUB, (base + j) % SUB
            if r == 0:
                tap = z_ref[r0 + al:r0 + al + CONV_ROWS, :]
            else:
                tap = zs_ref[r - 1, r0 + al:r0 + al + CONV_ROWS, :]
            acc = acc + tap * w_ref[j:j + 1, :]
        acc = acc + bias_ref[...]
        mu = jnp.mean(acc, axis=-1, keepdims=True)
        ac = acc - mu
        var = jnp.mean(ac * ac, axis=-1, keepdims=True)
        y = ac * lax.rsqrt(var + EPS) * lg_ref[...] + lb_ref[...]
        o_ref[r0:r0 + CONV_ROWS, :] = (y * jax.nn.sigmoid(y)).astype(BF16)
    z_ref[0:H, :] = z_ref[tm:tm + H, :]


def _conv(proj, conv_w, conv_b, ln_g, ln_b, B, S):
    T = B * S
    tm = CONV_TILE
    nblk = S // tm
    vec = pl.BlockSpec((1, CONV_W), lambda b, i: (0, 0))
    return pl.pallas_call(
        _conv_kernel,
        grid=(B, nblk),
        in_specs=[pl.BlockSpec((tm, CONV_W), lambda b, i: (b * nblk + i, COL_CA)),
                  pl.BlockSpec((tm, CONV_W), lambda b, i: (b * nblk + i, COL_CB)),
                  pl.BlockSpec((CONV_K, CONV_W), lambda b, i: (0, 0)),
                  vec, vec, vec],
        out_specs=pl.BlockSpec((tm, CONV_W), lambda b, i: (b * nblk + i, 0)),
        out_shape=jax.ShapeDtypeStruct((T, CONV_W), BF16),
        scratch_shapes=[pltpu.VMEM((CONV_HIST + tm, CONV_W), F32),
                        pltpu.VMEM((7, CONV_HIST + tm, CONV_W), F32)],
        compiler_params=_cparams(("arbitrary", "arbitrary")),
        name="conv",
    )(proj, proj, conv_w.reshape(CONV_K, CONV_W), conv_b.reshape(1, CONV_W),
      ln_g.reshape(1, CONV_W), ln_b.reshape(1, CONV_W))


def _head_mean_sq(x, hm_ref):
    sq = x * x
    hi = sq.astype(BF16)
    lo = (sq - hi.astype(F32)).astype(BF16)
    tot = jnp.dot(hi, hm_ref[...], preferred_element_type=F32) + jnp.dot(lo, hm_ref[...], preferred_element_type=F32)
    return tot * (1.0 / ATT_DH)


def _attn_kernel(q_ref, k_ref, v_ref, rrow_ref, qg_ref, kg_ref, hm_ref, o_ref, kn_ref, vt_ref, bmt_ref, *, S):
    tq = ATT_QBLOCK
    qi = pl.program_id(1)
    heads = [slice(h * ATT_DH, (h + 1) * ATT_DH) for h in range(ATT_HEADS)]
    nkb = ATT_BAND // tq

    @pl.when((qi == 0) & (pl.program_id(0) == 0))
    def _():
        r = lax.broadcasted_iota(jnp.int32, (tq, ATT_BAND), 0)
        u = lax.broadcasted_iota(jnp.int32, (tq, ATT_BAND), 1)
        off = u - (r // CHUNK) * CHUNK
        inband = (off >= 0) & (off < (ATT_LEFT_CHUNKS + 1) * CHUNK)
        for h in range(ATT_HEADS):
            gen = jnp.broadcast_to(rrow_ref[h], (tq, rrow_ref.shape[-1]))
            t = pltpu.roll(gen, 0, 1, stride=1, stride_axis=0)
            bmt_ref[h] = jnp.where(inband, t[:, :ATT_BAND], NEG_INF).T

    @pl.when(qi == 0)
    def _():
        for h in range(ATT_HEADS):
            kn_ref[h, 0:ATT_PAD, :] = jnp.zeros((ATT_PAD, ATT_DH), BF16)
        for j in range(ATT_PAD // tq):
            vt_ref[j] = jnp.zeros((ATT_W, tq), BF16)
        rows = 512
        for r0 in range(0, S, rows):
            k = k_ref[r0:r0 + rows, :].astype(F32)
            kn = (k * lax.rsqrt(_head_mean_sq(k, hm_ref) + EPS) * kg_ref[...]).astype(BF16)
            for h in range(ATT_HEADS):
                kn_ref[h, ATT_PAD + r0:ATT_PAD + r0 + rows, :] = kn[:, heads[h]]
        for r0 in range(0, S, tq):
            vt_ref[(ATT_PAD + r0) // tq] = v_ref[r0:r0 + tq, :].astype(F32).T.astype(BF16)

    start = pl.multiple_of(qi * tq, tq)
    q = q_ref[...].astype(F32)
    qn = (q * lax.rsqrt(_head_mean_sq(q, hm_ref) + EPS) * qg_ref[...]).astype(BF16)

    def attend(mask_pad):
        if mask_pad:
            key_pos = lax.broadcasted_iota(jnp.int32, (ATT_BAND, LANES), 0) + start
            valid = key_pos >= ATT_PAD
        def scores(h):
            kb = kn_ref[h, pl.ds(start, ATT_BAND), :]
            return lax.dot_general(kb, qn[:, heads[h]], (((1,), (1,)), ((), ())), preferred_element_type=F32)

        def softmax(h, st):
            ps, ls = [], []
            for c0 in range(0, tq, LANES):
                ss = st[:, c0:c0 + LANES] + bmt_ref[h, :, c0:c0 + LANES]
                if mask_pad:
                    ss = jnp.where(valid, ss, NEG_INF)
                m = jnp.max(ss, axis=0, keepdims=True)
                p = jnp.exp2(ss - m)
                ls.append(jnp.sum(p, axis=0, keepdims=True))
                ps.append(p.astype(BF16))
            return jnp.concatenate(ps, axis=1), jnp.concatenate(ls, axis=1)

        def values(h, pt, l):
            ot = jnp.zeros((ATT_DH, tq), F32)
            for t in range(nkb):
                ot = ot + jnp.dot(vt_ref[qi + t, heads[h], :], pt[t * tq:(t + 1) * tq, :],
                                  preferred_element_type=F32)
            return ot / l

        sts, pls, outs = {}, {}, []
        for step in range(ATT_HEADS + 2):
            if step < ATT_HEADS:
                sts[step] = scores(step)
            if 0 <= step - 1 < ATT_HEADS:
                pls[step - 1] = softmax(step - 1, sts.pop(step - 1))
            if 0 <= step - 2 < ATT_HEADS:
                outs.append(values(step - 2, *pls.pop(step - 2)))
        o_ref[...] = jnp.concatenate(outs, axis=0).T.astype(BF16)

    first = ATT_PAD // tq

    @pl.when(qi < first)
    def _():
        attend(True)

    @pl.when(qi >= first)
    def _():
        attend(False)


def _attn_bias_row(rel_table):
    t = rel_table.astype(F32) * math.log2(math.e)
    edge = jnp.broadcast_to(t[:, 2 * MAX_REL:], (ATT_HEADS, ATT_QBLOCK))
    row = jnp.concatenate([edge, jnp.flip(t[:, 1:], axis=1), edge], axis=1)
    return row.reshape(ATT_HEADS, 1, 2 * ATT_QBLOCK + 2 * MAX_REL)


def _attention(proj, rel_table, q_gain, k_gain, B, S):
    T = B * S
    tq = ATT_QBLOCK
    nblk = S // tq
    gen = _attn_bias_row(rel_table)
    qg = jnp.tile(q_gain.astype(F32) * (ATT_DH ** -0.5 * math.log2(math.e)), ATT_HEADS).reshape(1, ATT_W)
    kg = jnp.tile(k_gain.astype(F32), ATT_HEADS).reshape(1, ATT_W)
    lane_head = jnp.arange(ATT_W) // ATT_DH
    hm = (lane_head[:, None] == lane_head[None, :]).astype(BF16)
    vec = pl.BlockSpec((1, ATT_W), lambda b, i: (0, 0))
    return pl.pallas_call(
        functools.partial(_attn_kernel, S=S),
        grid=(B, nblk),
        in_specs=[pl.BlockSpec((tq, ATT_W), lambda b, i: (b * nblk + i, COL_AQ)),
                  pl.BlockSpec((S, ATT_W), lambda b, i: (b, COL_AK)),
                  pl.BlockSpec((S, ATT_W), lambda b, i: (b, COL_AV)),
                  pl.BlockSpec(gen.shape, lambda b, i: (0, 0, 0)),
                  vec, vec,
                  pl.BlockSpec((ATT_W, ATT_W), lambda b, i: (0, 0))],
        out_specs=pl.BlockSpec((tq, ATT_W), lambda b, i: (b * nblk + i, 0)),
        out_shape=jax.ShapeDtypeStruct((T, ATT_W), BF16),
        scratch_shapes=[pltpu.VMEM((ATT_HEADS, ATT_PAD + S, ATT_DH), BF16),
                        pltpu.VMEM(((ATT_PAD + S) // tq, ATT_W, tq), BF16),
                        pltpu.VMEM((ATT_HEADS, ATT_BAND, tq), F32)],
        compiler_params=_cparams(("arbitrary", "arbitrary")),
        name="chunk_attention",
    )(proj, proj, proj, gen, qg, kg, hm)


def _merge_kernel(x_ref, ret_ref, conv_ref, att_ref, g0_ref, g1_ref, g2_ref, bg_ref, mod_ref,
                  wr_ref, wc_ref, wa_ref, wo_ref, gf_ref, wrt_ref, brt_ref,
                  xo_ref, route_ref, cnt_out_ref, sof_out_ref, xs_ref, tri_ref, h2_ref):
    D = D_MODEL

    @pl.when(pl.program_id(0) == 0)
    def _():
        n = tri_ref.shape[0]
        tri_ref[...] = (lax.broadcasted_iota(jnp.int32, (n, n), 1)
                        < lax.broadcasted_iota(jnp.int32, (n, n), 0)).astype(BF16)

    def mix(rs):
        def gated(g_ref, k, b_ref, w_ref):
            gate = jax.nn.sigmoid(g_ref[rs, :].astype(F32) + bg_ref[:, k * D:(k + 1) * D])
            return gate * jnp.dot(b_ref[rs, :], w_ref[...], preferred_element_type=F32)

        merged = (gated(g0_ref, 0, ret_ref, wr_ref) + gated(g1_ref, 1, conv_ref, wc_ref)
                  + gated(g2_ref, 2, att_ref, wa_ref))
        return merged.astype(BF16)

    def project(rs, merged):
        y = jnp.dot(merged, wo_ref[...], preferred_element_type=F32)
        x = x_ref[rs, :] + mod_ref[0, 2:3, :] * y
        xo_ref[rs, :] = x
        h2 = x * lax.rsqrt(jnp.mean(x * x, axis=-1, keepdims=True) + EPS) * gf_ref[...]
        h2 = h2 * (1.0 + mod_ref[0, 4:5, :]) + mod_ref[0, 3:4, :]
        h_hi = h2.astype(BF16)
        h2_ref[rs, :] = h_hi
        h_lo = (h2 - h_hi.astype(F32)).astype(BF16)
        both = jnp.dot(h_hi, wrt_ref[...], preferred_element_type=F32)
        return (both[:, :LANES] + both[:, LANES:]
                + jnp.dot(h_lo, wrt_ref[:, :LANES], preferred_element_type=F32) + brt_ref[...])

    subs = [slice(r0, r0 + MERGE_SUB) for r0 in range(0, x_ref.shape[0], MERGE_SUB)]
    mixed, parts = {}, []
    for step in range(len(subs) + 1):
        if step < len(subs):
            mixed[step] = mix(subs[step])
        if step >= 1:
            parts.append(project(subs[step - 1], mixed.pop(step - 1)))
    logits = jnp.concatenate(parts, axis=0)
    tm = logits.shape[0]
    lane = lax.broadcasted_iota(jnp.int32, (tm, LANES), 1)
    big = jnp.int32(LANES)
    is_g = (lane >= N_EXPERTS) & (lane < N_EXPERTS + N_GROUPS)
    gl = jnp.where(is_g, logits, -jnp.inf)
    gmax = jnp.max(gl, axis=-1, keepdims=True)
    gsel = jnp.min(jnp.where(gl == gmax, lane - N_EXPERTS, big), axis=-1, keepdims=True)
    p_group = 1.0 / jnp.sum(jnp.where(is_g, jnp.exp(gl - gmax), 0.0), axis=-1, keepdims=True)
    in_grp = (lane >= gsel * EXPERTS_PER_GROUP) & (lane < (gsel + 1) * EXPERTS_PER_GROUP)
    ch = jnp.where(in_grp, logits, -jnp.inf)
    v1 = jnp.max(ch, axis=-1, keepdims=True)
    i1 = jnp.min(jnp.where(ch == v1, lane, big), axis=-1, keepdims=True)
    ch2 = jnp.where(lane == i1, -jnp.inf, ch)
    v2 = jnp.max(ch2, axis=-1, keepdims=True)
    i2 = jnp.min(jnp.where(ch2 == v2, lane, big), axis=-1, keepdims=True)
    e = jnp.exp(v2 - v1)
    w1 = p_group / (1.0 + e)
    w2 = p_group * e / (1.0 + e)
    sel = ((lane == i1) | (lane == i2)).astype(BF16)
    before = jnp.dot(tri_ref[...], sel, preferred_element_type=F32)
    rank1 = jnp.sum(jnp.where(lane == i1, before, 0.0), axis=-1, keepdims=True)
    rank2 = jnp.sum(jnp.where(lane == i2, before, 0.0), axis=-1, keepdims=True)
    cnt = jnp.sum(sel.astype(F32), axis=0, keepdims=True)
    cnt_out_ref[0] = cnt
    vals = (i1.astype(F32), i2.astype(F32), w1, w2, rank1, rank2)
    route = jnp.zeros((tm, LANES), F32)
    for k, v in enumerate(vals):
        route = jnp.where(lane == k, v, route)
    route_ref[...] = route

    units = jnp.floor((cnt + (ROW_ALIGN - 1.0)) * (1.0 / ROW_ALIGN))
    before_lane = (lax.broadcasted_iota(jnp.int32, (LANES, LANES), 0)
                   < lax.broadcasted_iota(jnp.int32, (LANES, LANES), 1)).astype(BF16)
    srcoff = jnp.dot(jnp.broadcast_to(units, (8, LANES)).astype(BF16), before_lane,
                     preferred_element_type=F32)[0:1] * float(ROW_ALIGN)
    sof_out_ref[0] = srcoff
    s0 = jnp.sum(jnp.where(lane == i1, srcoff, 0.0), axis=-1, keepdims=True) + rank1
    s1 = jnp.sum(jnp.where(lane == i2, srcoff, 0.0), axis=-1, keepdims=True) + rank2
    slots = lax.broadcasted_iota(jnp.int32, (tm, SORT_SLOTS), 1).astype(F32)
    onehot = ((slots == s0) | (slots == s1)).astype(BF16)

    def split(w):
        hi = w.astype(BF16).astype(F32)
        return hi, (w - hi).astype(BF16).astype(F32)

    extra = jnp.zeros((tm, LANES), F32)
    for k, v in enumerate((i1.astype(F32), *split(w1), *split(w2))):
        extra = jnp.where(lane == k, v, extra)
    hx = jnp.concatenate([h2_ref[...], extra.astype(BF16)], axis=1)
    xs_ref[...] = lax.dot_general(onehot, hx, (((0,), (0,)), ((), ())),
                                  preferred_element_type=F32).astype(BF16)


def _merge(x2, proj, ret, conv, att, b_gate, mod, wr, wc, wa, wo, g_ffn, w_rt, b_rt, S):
    T, D = x2.shape
    tm = ROUTE_TILE
    row = lambda i: (i, 0)
    const = lambda i: (0, 0)
    br = pl.BlockSpec((tm, RET_W), row)
    return pl.pallas_call(
        _merge_kernel,
        grid=(T // tm,),
        in_specs=[pl.BlockSpec((tm, D), row), br, br, br,
                  pl.BlockSpec((tm, D), lambda i: (i, COL_GATE)),
                  pl.BlockSpec((tm, D), lambda i: (i, COL_GATE + 1)),
                  pl.BlockSpec((tm, D), lambda i: (i, COL_GATE + 2)),
                  pl.BlockSpec((1, N_BRANCH * D), const),
                  pl.BlockSpec((1, N_MOD, D), lambda i: (i * tm // S, 0, 0)),
                  pl.BlockSpec((RET_W, D), const), pl.BlockSpec((CONV_W, D), const),
                  pl.BlockSpec((ATT_W, D), const), pl.BlockSpec((D, D), const),
                  pl.BlockSpec((1, D), const),
                  pl.BlockSpec((D, 2 * LANES), const), pl.BlockSpec((1, LANES), const)],
        out_specs=[pl.BlockSpec((tm, D), row), pl.BlockSpec((tm, LANES), row),
                   pl.BlockSpec((1, 1, LANES), lambda i: (i, 0, 0)),
                   pl.BlockSpec((1, 1, LANES), lambda i: (i, 0, 0)),
                   pl.BlockSpec((SORT_SLOTS, XS_W), row)],
        out_shape=[jax.ShapeDtypeStruct((T, D), F32), jax.ShapeDtypeStruct((T, LANES), F32),
                   jax.ShapeDtypeStruct((T // tm, 1, LANES), F32),
                   jax.ShapeDtypeStruct((T // tm, 1, LANES), F32),
                   jax.ShapeDtypeStruct((T // tm * SORT_SLOTS, XS_W), BF16)],
        scratch_shapes=[pltpu.VMEM((tm, tm), BF16), pltpu.VMEM((tm, D), BF16)],
        compiler_params=_cparams(("arbitrary",)),
        name="merge_router",
    )(x2, ret, conv, att, proj, proj, proj, b_gate.reshape(1, N_BRANCH * D), mod,
      wr, wc, wa, wo, g_ffn.reshape(1, D), w_rt, b_rt)


def _pair_slots(route, srcoff_row):
    tm = route.shape[0]
    lane = lax.broadcasted_iota(jnp.int32, (tm, LANES), 1).astype(F32)
    s0 = jnp.sum(jnp.where(lane == route[:, 0:1], srcoff_row, 0.0), axis=-1, keepdims=True) + route[:, 4:5]
    s1 = jnp.sum(jnp.where(lane == route[:, 1:2], srcoff_row, 0.0), axis=-1, keepdims=True) + route[:, 5:6]
    return s0, s1


def _chunk_copies(tile, so_ref, nch_ref, dst_ref, make_copy):
    base = tile * N_EXPERTS
    for e in range(N_EXPERTS):
        so = so_ref[base + e]
        d = dst_ref[base + e]

        def body(c, carry, so=so, d=d):
            make_copy(pl.multiple_of(so + c * ROW_ALIGN, ROW_ALIGN),
                      pl.multiple_of(d + c * ROW_ALIGN, ROW_ALIGN)).start()
            return carry

        lax.fori_loop(0, nch_ref[base + e], body, 0)


def _ffn_kernel(te_ref, nv_ref, cs_ref, xs_hbm, wu_ref, wd_ref, o_ref, xbuf, wub_ref, wdb_ref, sem, *, n_tiles):
    i = pl.program_id(0)
    chunks = MOE_TILE // ROW_ALIGN

    def gather(tile, s):
        for j in range(chunks):
            src = pl.multiple_of(cs_ref[tile * chunks + j], ROW_ALIGN)
            pltpu.make_async_copy(xs_hbm.at[pl.ds(src, ROW_ALIGN), :],
                                  xbuf.at[s, pl.ds(j * ROW_ALIGN, ROW_ALIGN), :], sem.at[s]).start()

    @pl.when(i == 0)
    def _():
        gather(0, 0)

    @pl.when(i + 1 < n_tiles)
    def _():
        gather(i + 1, (i + 1) % 2)

    slot = i % 2
    pltpu.make_async_copy(xs_hbm.at[pl.ds(0, MOE_TILE), :], xbuf.at[slot], sem.at[slot]).wait()
    xs_ref = xbuf.at[slot]

    @pl.when((i == 0) | (te_ref[i] != te_ref[jnp.maximum(i - 1, 0)]))
    def _():
        wub_ref[...] = wu_ref[0].astype(BF16)
        wdb_ref[...] = wd_ref[0].astype(BF16)

    @pl.when(i < nv_ref[0])
    def _():
        x = xs_ref[:, :D_MODEL]
        wv = xs_ref[:, D_MODEL:].astype(F32)
        mine = wv[:, 0:1] == te_ref[i].astype(F32)
        w = jnp.where(mine, wv[:, 1:2] + wv[:, 2:3], wv[:, 3:4] + wv[:, 4:5])
        hid = jnp.dot(x, wub_ref[...], preferred_element_type=F32)
        a = hid[:, :EXPERT_HIDDEN]
        b = hid[:, EXPERT_HIDDEN:]
        act = a * jax.nn.sigmoid(a) * b * w
        o_ref[...] = jnp.dot(act.astype(BF16), wdb_ref[...], preferred_element_type=F32).astype(BF16)

    @pl.when(i >= nv_ref[0])
    def _():
        o_ref[...] = jnp.zeros_like(o_ref)


def _ffn(xs, tile_expert, n_valid, chunk_src, w_up, w_down, layer, n_rows):
    tr = MOE_TILE
    D = D_MODEL
    F2 = w_up.shape[-1]
    first = layer * N_EXPERTS
    grid_spec = pltpu.PrefetchScalarGridSpec(
        num_scalar_prefetch=3,
        grid=(n_rows // tr,),
        in_specs=[pl.BlockSpec(memory_space=pl.ANY),
                  pl.BlockSpec((1, D, F2), lambda i, te, nv, cs: (first + te[i], 0, 0)),
                  pl.BlockSpec((1, EXPERT_HIDDEN, D), lambda i, te, nv, cs: (first + te[i], 0, 0))],
        out_specs=pl.BlockSpec((tr, D), lambda i, te, nv, cs: (i, 0)),
        scratch_shapes=[pltpu.VMEM((2, tr, XS_W), BF16), pltpu.VMEM((D, F2), BF16),
                        pltpu.VMEM((EXPERT_HIDDEN, D), BF16), pltpu.SemaphoreType.DMA((2,))],
    )
    return pl.pallas_call(
        functools.partial(_ffn_kernel, n_tiles=n_rows // tr),
        grid_spec=grid_spec,
        out_shape=jax.ShapeDtypeStruct((n_rows, D), BF16),
        compiler_params=_cparams(("arbitrary",)),
        name="moe_experts",
    )(tile_expert, n_valid, chunk_src, xs, w_up, w_down)


def _combine_kernel(so_ref, nch_ref, dst_ref, ntot_ref, x_ref, route_ref, sof_ref, mod_ref, ye_hbm, *rest,
                    nsteps, emit_next):
    if emit_next:
        modn_ref, gn_ref, o_ref, hn_ref, ys_ref, sem = rest
    else:
        o_ref, ys_ref, sem = rest
    i = pl.program_id(0)

    def copies(slot):
        def make_copy(src_row, dst_row):
            return pltpu.make_async_copy(ye_hbm.at[pl.ds(dst_row, ROW_ALIGN), :],
                                         ys_ref.at[slot, pl.ds(src_row, ROW_ALIGN), :], sem.at[slot])
        return make_copy

    @pl.when(i == 0)
    def _():
        ys_ref[...] = jnp.zeros_like(ys_ref)
        _chunk_copies(0, so_ref, nch_ref, dst_ref, copies(0))

    @pl.when(i + 1 < nsteps)
    def _():
        _chunk_copies(i + 1, so_ref, nch_ref, dst_ref, copies((i + 1) % 2))

    slot = i % 2

    def wait_one(c, carry):
        copies(slot)(0, 0).wait()
        return carry

    lax.fori_loop(0, ntot_ref[i], wait_one, 0)
    s0, s1 = _pair_slots(route_ref[...], sof_ref[0])
    slots = lax.broadcasted_iota(jnp.int32, (s0.shape[0], SORT_SLOTS), 1).astype(F32)
    onehot = ((slots == s0) | (slots == s1)).astype(BF16)
    y = jnp.dot(onehot, ys_ref[slot], preferred_element_type=F32)
    x = x_ref[...] + mod_ref[0, 5:6, :] * y
    o_ref[...] = x
    if emit_next:
        hn_ref[...] = _modulated_rmsnorm(x, gn_ref[...], modn_ref, 0, 1).astype(BF16)


def _combine(x2, route, srcoff_f, mod, ye, tabs, S, next_mod=None, next_gain=None):
    T, D = x2.shape
    tm = ROUTE_TILE
    nsteps = T // tm
    emit_next = next_mod is not None
    row = pl.BlockSpec((tm, D), lambda i, *_: (i, 0))
    modspec = pl.BlockSpec((1, N_MOD, D), lambda i, *_: (i * tm // S, 0, 0))
    in_specs = [row,
                pl.BlockSpec((tm, LANES), lambda i, *_: (i, 0)),
                pl.BlockSpec((1, 1, LANES), lambda i, *_: (i, 0, 0)),
                modspec,
                pl.BlockSpec(memory_space=pl.ANY)]
    args = [x2, route, srcoff_f, mod, ye]
    out_specs, out_shape = row, jax.ShapeDtypeStruct((T, D), F32)
    if emit_next:
        in_specs += [modspec, pl.BlockSpec((1, D), lambda i, *_: (0, 0))]
        args += [next_mod, next_gain.reshape(1, D)]
        out_specs, out_shape = [row, row], [out_shape, jax.ShapeDtypeStruct((T, D), BF16)]
    grid_spec = pltpu.PrefetchScalarGridSpec(
        num_scalar_prefetch=4,
        grid=(nsteps,),
        in_specs=in_specs,
        out_specs=out_specs,
        scratch_shapes=[pltpu.VMEM((2, SORT_SLOTS, D), BF16), pltpu.SemaphoreType.DMA((2,))],
    )
    return pl.pallas_call(
        functools.partial(_combine_kernel, nsteps=nsteps, emit_next=emit_next),
        grid_spec=grid_spec,
        out_shape=out_shape,
        compiler_params=_cparams(("arbitrary",)),
        name="moe_combine",
    )(tabs["srcoff"], tabs["nchunk"], tabs["dst"], tabs["ntot"], *args)


def _moe_rows_bound(T):
    nt = T // ROUTE_TILE
    rows = 2 * T + nt * N_EXPERTS * (ROW_ALIGN - 1) + N_EXPERTS * (MOE_TILE - 1)
    return -(-rows // MOE_TILE) * MOE_TILE


def _routing_tables(counts, n_rows):
    tr = MOE_TILE
    c = counts[:, 0, :N_EXPERTS].astype(jnp.int32)
    c16 = (c + ROW_ALIGN - 1) // ROW_ALIGN * ROW_ALIGN
    srcoff = jnp.cumsum(c16, axis=1) - c16
    rows = jnp.sum(c16, axis=0)
    ends = jnp.cumsum((rows + tr - 1) // tr * tr)
    offs = ends - (rows + tr - 1) // tr * tr
    dst = offs[None, :] + jnp.cumsum(c16, axis=0) - c16
    tile_start = jnp.arange(n_rows // tr, dtype=jnp.int32) * tr
    tile_expert = jnp.minimum(jnp.sum((tile_start[:, None] >= ends[None, :]).astype(jnp.int32), axis=-1),
                              N_EXPERTS - 1)
    nchunk = c16 // ROW_ALIGN
    starts = (dst // ROW_ALIGN).T.reshape(-1)
    lens = nchunk.T.reshape(-1)
    src0 = (jnp.arange(c.shape[0], dtype=jnp.int32)[:, None] * SORT_SLOTS + srcoff).T.reshape(-1)
    k = jnp.arange(n_rows // ROW_ALIGN, dtype=jnp.int32)
    seg = jnp.maximum(jnp.searchsorted(starts, k, side="right").astype(jnp.int32) - 1, 0)
    within = k - starts[seg]
    chunk_src = jnp.where(within < lens[seg], src0[seg] + within * ROW_ALIGN, SORT_SLOTS - ROW_ALIGN)
    return {
        "srcoff": srcoff.reshape(-1), "nchunk": nchunk.reshape(-1), "dst": dst.reshape(-1),
        "ntot": jnp.sum(nchunk, axis=1), "chunk_src": chunk_src,
        "tile_expert": tile_expert, "n_valid": (ends[-1] // tr).astype(jnp.int32).reshape(1),
    }


def _reorder_in_cols(w_in):
    cuts = 4 * RET_W + 2 * CONV_W + 3 * ATT_W
    return jnp.concatenate([w_in[..., cuts:], w_in[..., :cuts]], axis=-1)


def kernel(x, c, positions, w_ada, b_ada, g_mix, g_ffn, w_in, b_gate, ret_gn, conv_w, conv_b, conv_ln_g, conv_ln_b, att_q_gain, att_k_gain, att_rel_bias, w_ret_out, w_conv_out, w_att_out, w_out, w_group, b_group, w_inner, b_inner, w_up, w_down):
    B, S, D = x.shape
    L = w_ada.shape[0]
    T = B * S
    assert D == D_MODEL and S % 1024 == 0 and T % 2048 == 0
    n_rows = _moe_rows_bound(T)

    mod_all = _adaln(c, w_ada, b_ada).reshape(L, B, N_MOD, D)
    cs_tab = _rope_table(positions)
    x2 = x.reshape(T, D)
    h = _prenorm(x2, mod_all[0], g_mix[0], S)
    for l in range(L):
        mod = mod_all[l]
        proj = _inproj(h, _reorder_in_cols(w_in[l]).astype(BF16))
        ret = _retention(proj, cs_tab, ret_gn[l], B, S)
        conv = _conv(proj, conv_w[l], conv_b[l], conv_ln_g[l], conv_ln_b[l], B, S)
        att = _attention(proj, att_rel_bias[l], att_q_gain[l], att_k_gain[l], B, S)
        w_rt = jnp.zeros((D, LANES), F32)
        w_rt = w_rt.at[:, :N_EXPERTS].set(w_inner[l]).at[:, N_EXPERTS:N_EXPERTS + N_GROUPS].set(w_group[l])
        b_rt = jnp.zeros((1, LANES), F32)
        b_rt = b_rt.at[0, :N_EXPERTS].set(b_inner[l]).at[0, N_EXPERTS:N_EXPERTS + N_GROUPS].set(b_group[l])
        w_rt_hi = w_rt.astype(BF16)
        w_rt2 = jnp.concatenate([w_rt_hi, (w_rt - w_rt_hi.astype(F32)).astype(BF16)], axis=1)
        x2, route, counts, srcoff_f, xs = _merge(x2, proj, ret, conv, att, b_gate[l], mod,
                                       w_ret_out[l].astype(BF16), w_conv_out[l].astype(BF16),
                                       w_att_out[l].astype(BF16), w_out[l].astype(BF16),
                                       g_ffn[l], w_rt2, b_rt, S)
        tabs = _routing_tables(counts, n_rows)
        ye = _ffn(xs, tabs["tile_expert"], tabs["n_valid"], tabs["chunk_src"],
                  w_up.reshape(L * N_EXPERTS, D, 2 * EXPERT_HIDDEN),
                  w_down.reshape(L * N_EXPERTS, EXPERT_HIDDEN, D), l, n_rows)
        if l + 1 < L:
            x2, h = _combine(x2, route, srcoff_f, mod, ye, tabs, S, mod_all[l + 1], g_mix[l + 1])
        else:
            x2 = _combine(x2, route, srcoff_f, mod, ye, tabs, S)
    return x2.reshape(B, S, D)
```

```python
import functools
import math

import jax
import jax.numpy as jnp
from jax import lax
from jax.experimental import pallas as pl
from jax.experimental.pallas import tpu as pltpu

F32 = jnp.float32
BF16 = jnp.bfloat16

D_MODEL = 1024
CHUNK = 64
EPS = 1e-6
NEG_INF = -1e30
RET_HEADS = 4
RET_DK = 128
RET_W = 512
ROPE_BASE = 10000.0
CONV_W = 512
CONV_K = 31
ATT_HEADS = 8
ATT_DH = 64
ATT_W = 512
ATT_LEFT_CHUNKS = 8
MAX_REL = 256
N_BRANCH = 3
IN_COLS = 4 * RET_W + 2 * CONV_W + 3 * ATT_W + N_BRANCH * D_MODEL
N_GROUPS = 4
EXPERTS_PER_GROUP = 8
N_EXPERTS = N_GROUPS * EXPERTS_PER_GROUP
EXPERT_HIDDEN = 256
N_MOD = 6

LANES = 128
V7X_VMEM_LIMIT_BYTES = 56 * 1024 * 1024

COL_GATE = 0
COL_RQ, COL_RK, COL_RV, COL_RG = 6, 7, 8, 9
COL_CA, COL_CB = 10, 11
COL_AQ, COL_AK, COL_AV = 12, 13, 14

RET_BLOCK = 256
ATT_QBLOCK = 256
ATT_BAND = ATT_QBLOCK + ATT_LEFT_CHUNKS * CHUNK
ATT_PAD = ATT_LEFT_CHUNKS * CHUNK
CONV_TILE = 512
CONV_HIST = 32
CONV_ROWS = 64
MOE_TILE = 512
ROUTE_TILE = 512
MERGE_SUB = 512
ROW_ALIGN = 16
SORT_SLOTS = 2 * ROUTE_TILE + N_EXPERTS * ROW_ALIGN
XS_W = D_MODEL + LANES


def _cparams(sem):
    return pltpu.CompilerParams(dimension_semantics=sem, vmem_limit_bytes=V7X_VMEM_LIMIT_BYTES)


def _adaln_kernel(c_ref, w_ref, b_ref, o_ref):
    c = c_ref[...]
    cs = c * jax.nn.sigmoid(c)
    o_ref[0] = jnp.dot(cs, w_ref[0], preferred_element_type=F32,
                       precision=lax.Precision.HIGHEST) + b_ref[0]


def _adaln(c, w_ada, b_ada):
    L, D, N = w_ada.shape
    B = c.shape[0]
    tn = 1536
    return pl.pallas_call(
        _adaln_kernel,
        grid=(L, N // tn),
        in_specs=[pl.BlockSpec((B, D), lambda l, j: (0, 0)),
                  pl.BlockSpec((1, D, tn), lambda l, j: (l, 0, j)),
                  pl.BlockSpec((1, 1, tn), lambda l, j: (l, 0, j))],
        out_specs=pl.BlockSpec((1, B, tn), lambda l, j: (l, 0, j)),
        out_shape=jax.ShapeDtypeStruct((L, B, N), F32),
        compiler_params=_cparams(("arbitrary", "arbitrary")),
        name="adaln",
    )(c, w_ada, b_ada.reshape(L, 1, N))


def _rope_kernel(pos_ref, inv_ref, ph_ref, o_ref):
    o_ref[...] = jnp.cos(pos_ref[...] * inv_ref[...] - ph_ref[...])


def _rope_table(positions):
    T = positions.size
    half = RET_DK // 2
    inv = ROPE_BASE ** (-jnp.arange(half, dtype=F32) / half)
    inv2 = jnp.concatenate([inv, inv]).reshape(1, LANES)
    phase = jnp.concatenate([jnp.zeros((half,), F32), jnp.full((half,), math.pi / 2, F32)]).reshape(1, LANES)
    posb = jnp.broadcast_to(positions.reshape(T, 1).astype(F32), (T, LANES))
    tm = 2048
    return pl.pallas_call(
        _rope_kernel,
        grid=(T // tm,),
        in_specs=[pl.BlockSpec((tm, LANES), lambda i: (i, 0)),
                  pl.BlockSpec((1, LANES), lambda i: (0, 0)),
                  pl.BlockSpec((1, LANES), lambda i: (0, 0))],
        out_specs=pl.BlockSpec((tm, LANES), lambda i: (i, 0)),
        out_shape=jax.ShapeDtypeStruct((T, LANES), F32),
        compiler_params=_cparams(("arbitrary",)),
        name="rope_table",
    )(posb, inv2, phase)


def _modulated_rmsnorm(x, g, mod_ref, shift_row, scale_row):
    y = x * lax.rsqrt(jnp.mean(x * x, axis=-1, keepdims=True) + EPS) * g
    return y * (1.0 + mod_ref[0, scale_row:scale_row + 1, :]) + mod_ref[0, shift_row:shift_row + 1, :]


def _prenorm_kernel(x_ref, mod_ref, g_ref, h_ref):
    h_ref[...] = _modulated_rmsnorm(x_ref[...], g_ref[...], mod_ref, 0, 1).astype(BF16)


def _prenorm(x2, mod, g_mix, S):
    T, D = x2.shape
    tm = 1024
    return pl.pallas_call(
        _prenorm_kernel,
        grid=(T // tm,),
        in_specs=[pl.BlockSpec((tm, D), lambda i: (i, 0)),
                  pl.BlockSpec((1, N_MOD, D), lambda i: (i * tm // S, 0, 0)),
                  pl.BlockSpec((1, D), lambda i: (0, 0))],
        out_specs=pl.BlockSpec((tm, D), lambda i: (i, 0)),
        out_shape=jax.ShapeDtypeStruct((T, D), BF16),
        compiler_params=_cparams(("arbitrary",)),
        name="prenorm",
    )(x2, mod, g_mix.reshape(1, D))


def _inproj_kernel(h_ref, w_ref, o_ref):
    o_ref[...] = jnp.dot(h_ref[...], w_ref[...], preferred_element_type=F32).astype(BF16)


def _inproj(h, w_in_b):
    T, D = h.shape
    N = w_in_b.shape[1]
    tm, tn = 1024, 2560
    return pl.pallas_call(
        _inproj_kernel,
        grid=(T // tm, N // tn),
        in_specs=[pl.BlockSpec((tm, D), lambda i, j: (i, 0)),
                  pl.BlockSpec((D, tn), lambda i, j: (0, j))],
        out_specs=pl.BlockSpec((tm, tn), lambda i, j: (i, j)),
        out_shape=jax.ShapeDtypeStruct((T, N), BF16),
        compiler_params=_cparams(("arbitrary", "arbitrary")),
        name="inproj",
    )(h, w_in_b)


def _retention_kernel(q_ref, k_ref, v_ref, g_ref, cs_ref, gn_ref, o_ref, state_ref, decay_ref):
    C = RET_BLOCK

    @pl.when(pl.program_id(1) == 0)
    def _():
        state_ref[...] = jnp.zeros_like(state_ref)

    @pl.when((pl.program_id(0) == 0) & (pl.program_id(1) == 0))
    def _():
        d = (lax.broadcasted_iota(jnp.int32, (C, C), 0) - lax.broadcasted_iota(jnp.int32, (C, C), 1)).astype(F32)
        for h in range(RET_HEADS):
            lg = math.log1p(-(2.0 ** (-5.0 - h)))
            decay_ref[h] = jnp.where(d >= 0.0, jnp.exp(lg * jnp.maximum(d, 0.0)), 0.0)

    cs = cs_ref[...]
    csr = pltpu.roll(cs, RET_DK // 2, 1)
    first = lax.broadcasted_iota(jnp.int32, (C, RET_DK), 1) < RET_DK // 2
    cos2 = jnp.where(first, cs, csr)
    sin2 = jnp.where(first, -csr, cs)
    row = lax.broadcasted_iota(jnp.int32, (C, 1), 0).astype(F32)
    for h in range(RET_HEADS):
        lg = math.log1p(-(2.0 ** (-5.0 - h)))
        sl = slice(h * RET_DK, (h + 1) * RET_DK)
        qh = q_ref[:, sl].astype(F32)
        kh = k_ref[:, sl].astype(F32)
        vh = v_ref[:, sl]
        qr = (qh * cos2 + pltpu.roll(qh, RET_DK // 2, 1) * sin2).astype(BF16)
        kr = (kh * cos2 + pltpu.roll(kh, RET_DK // 2, 1) * sin2) * (RET_DK ** -0.5)
        s = lax.dot_general(qr, kr.astype(BF16), (((1,), (1,)), ((), ())),
                            preferred_element_type=F32) * decay_ref[h]
        o = jnp.dot(s.astype(BF16), vh, preferred_element_type=F32)
        st = state_ref[h]
        o = o + jnp.dot(qr, st.astype(BF16), preferred_element_type=F32) * jnp.exp(lg * (row + 1.0))
        kd = (kr * jnp.exp(lg * (C - 1.0 - row))).astype(BF16)
        kv = lax.dot_general(kd, vh, (((0,), (0,)), ((), ())), preferred_element_type=F32)
        state_ref[h] = math.exp(lg * C) * st + kv
        mu = jnp.mean(o, axis=-1, keepdims=True)
        oc = o - mu
        var = jnp.mean(oc * oc, axis=-1, keepdims=True)
        on = oc * lax.rsqrt(var + EPS) * gn_ref[:, sl]
        gh = g_ref[:, sl].astype(F32)
        o_ref[:, sl] = (on * (gh * jax.nn.sigmoid(gh))).astype(BF16)


def _retention(proj, cs_tab, ret_gn, B, S):
    T = B * S
    C = RET_BLOCK
    nblk = S // C

    def col(c):
        return pl.BlockSpec((C, RET_W), lambda b, i, c=c: (b * nblk + i, c))

    return pl.pallas_call(
        _retention_kernel,
        grid=(B, nblk),
        in_specs=[col(COL_RQ), col(COL_RK), col(COL_RV), col(COL_RG),
                  pl.BlockSpec((C, LANES), lambda b, i: (b * nblk + i, 0)),
                  pl.BlockSpec((1, RET_W), lambda b, i: (0, 0))],
        out_specs=pl.BlockSpec((C, RET_W), lambda b, i: (b * nblk + i, 0)),
        out_shape=jax.ShapeDtypeStruct((T, RET_W), BF16),
        scratch_shapes=[pltpu.VMEM((RET_HEADS, RET_DK, RET_DK), F32),
                        pltpu.VMEM((RET_HEADS, C, C), F32)],
        compiler_params=_cparams(("arbitrary", "arbitrary")),
        name="retention",
    )(proj, proj, proj, proj, cs_tab, ret_gn.reshape(1, RET_W))


def _conv_kernel(a_ref, b_ref, w_ref, bias_ref, lg_ref, lb_ref, o_ref, z_ref, zs_ref):
    tm = CONV_TILE
    H = CONV_HIST
    SUB = 8

    @pl.when(pl.program_id(1) == 0)
    def _():
        z_ref[0:H, :] = jnp.zeros((H, CONV_W), F32)

    a = a_ref[...].astype(F32)
    b = b_ref[...].astype(F32)
    z_ref[H:H + tm, :] = a * jax.nn.sigmoid(b)
    base = H - (CONV_K - 1)
    n_shift = tm + (base + CONV_K - 2) // SUB * SUB
    for r in range(1, SUB):
        for c0 in range(0, n_shift, 128):
            cs = min(128, n_shift - c0)
            zs_ref[r - 1, c0:c0 + cs, :] = z_ref[c0 + r:c0 + r + cs, :]
    for r0 in range(0, tm, CONV_ROWS):
        acc = jnp.zeros((CONV_ROWS, CONV_W), F32)
        for j in range(CONV_K):
            al, r = (base + j) // SUB * SUB, (base + j) % SUB
            if r == 0:
                tap = z_ref[r0 + al:r0 + al + CONV_ROWS, :]
            else:
                tap = zs_ref[r - 1, r0 + al:r0 + al + CONV_ROWS, :]
            acc = acc + tap * w_ref[j:j + 1, :]
        acc = acc + bias_ref[...]
        mu = jnp.mean(acc, axis=-1, keepdims=True)
        ac = acc - mu
        var = jnp.mean(ac * ac, axis=-1, keepdims=True)
        y = ac * lax.rsqrt(var + EPS) * lg_ref[...] + lb_ref[...]
        o_ref[r0:r0 + CONV_ROWS, :] = (y * jax.nn.sigmoid(y)).astype(BF16)
    z_ref[0:H, :] = z_ref[tm:tm + H, :]


def _conv(proj, conv_w, conv_b, ln_g, ln_b, B, S):
    T = B * S
    tm = CONV_TILE
    nblk = S // tm
    vec = pl.BlockSpec((1, CONV_W), lambda b, i: (0, 0))
    return pl.pallas_call(
        _conv_kernel,
        grid=(B, nblk),
        in_specs=[pl.BlockSpec((tm, CONV_W), lambda b, i: (b * nblk + i, COL_CA)),
                  pl.BlockSpec((tm, CONV_W), lambda b, i: (b * nblk + i, COL_CB)),
                  pl.BlockSpec((CONV_K, CONV_W), lambda b, i: (0, 0)),
                  vec, vec, vec],
        out_specs=pl.BlockSpec((tm, CONV_W), lambda b, i: (b * nblk + i, 0)),
        out_shape=jax.ShapeDtypeStruct((T, CONV_W), BF16),
        scratch_shapes=[pltpu.VMEM((CONV_HIST + tm, CONV_W), F32),
                        pltpu.VMEM((7, CONV_HIST + tm, CONV_W), F32)],
        compiler_params=_cparams(("arbitrary", "arbitrary")),
        name="conv",
    )(proj, proj, conv_w.reshape(CONV_K, CONV_W), conv_b.reshape(1, CONV_W),
      ln_g.reshape(1, CONV_W), ln_b.reshape(1, CONV_W))


def _head_mean_sq(x, hm_ref):
    sq = x * x
    hi = sq.astype(BF16)
    lo = (sq - hi.astype(F32)).astype(BF16)
    tot = jnp.dot(hi, hm_ref[...], preferred_element_type=F32) + jnp.dot(lo, hm_ref[...], preferred_element_type=F32)
    return tot * (1.0 / ATT_DH)


def _attn_kernel(q_ref, k_ref, v_ref, rrow_ref, qg_ref, kg_ref, hm_ref, o_ref, kn_ref, vt_ref, bmt_ref, *, S):
    tq = ATT_QBLOCK
    qi = pl.program_id(1)
    heads = [slice(h * ATT_DH, (h + 1) * ATT_DH) for h in range(ATT_HEADS)]
    nkb = ATT_BAND // tq

    @pl.when((qi == 0) & (pl.program_id(0) == 0))
    def _():
        r = lax.broadcasted_iota(jnp.int32, (tq, ATT_BAND), 0)
        u = lax.broadcasted_iota(jnp.int32, (tq, ATT_BAND), 1)
        off = u - (r // CHUNK) * CHUNK
        inband = (off >= 0) & (off < (ATT_LEFT_CHUNKS + 1) * CHUNK)
        for h in range(ATT_HEADS):
            gen = jnp.broadcast_to(rrow_ref[h], (tq, rrow_ref.shape[-1]))
            t = pltpu.roll(gen, 0, 1, stride=1, stride_axis=0)
            bmt_ref[h] = jnp.where(inband, t[:, :ATT_BAND], NEG_INF).T

    @pl.when(qi == 0)
    def _():
        for h in range(ATT_HEADS):
            kn_ref[h, 0:ATT_PAD, :] = jnp.zeros((ATT_PAD, ATT_DH), BF16)
        for j in range(ATT_PAD // tq):
            vt_ref[j] = jnp.zeros((ATT_W, tq), BF16)
        rows = 512
        for r0 in range(0, S, rows):
            k = k_ref[r0:r0 + rows, :].astype(F32)
            kn = (k * lax.rsqrt(_head_mean_sq(k, hm_ref) + EPS) * kg_ref[...]).astype(BF16)
            for h in range(ATT_HEADS):
                kn_ref[h, ATT_PAD + r0:ATT_PAD + r0 + rows, :] = kn[:, heads[h]]
        for r0 in range(0, S, tq):
            vt_ref[(ATT_PAD + r0) // tq] = v_ref[r0:r0 + tq, :].astype(F32).T.astype(BF16)

    start = pl.multiple_of(qi * tq, tq)
    q = q_ref[...].astype(F32)
    qn = (q * lax.rsqrt(_head_mean_sq(q, hm_ref) + EPS) * qg_ref[...]).astype(BF16)

    def attend(mask_pad):
        if mask_pad:
            key_pos = lax.broadcasted_iota(jnp.int32, (ATT_BAND, LANES), 0) + start
            valid = key_pos >= ATT_PAD
        def scores(h):
            kb = kn_ref[h, pl.ds(start, ATT_BAND), :]
            return lax.dot_general(kb, qn[:, heads[h]], (((1,), (1,)), ((), ())), preferred_element_type=F32)

        def softmax(h, st):
            ps, ls = [], []
            for c0 in range(0, tq, LANES):
                ss = st[:, c0:c0 + LANES] + bmt_ref[h, :, c0:c0 + LANES]
                if mask_pad:
                    ss = jnp.where(valid, ss, NEG_INF)
                m = jnp.max(ss, axis=0, keepdims=True)
                p = jnp.exp2(ss - m)
                ls.append(jnp.sum(p, axis=0, keepdims=True))
                ps.append(p.astype(BF16))
            return jnp.concatenate(ps, axis=1), jnp.concatenate(ls, axis=1)

        def values(h, pt, l):
            ot = jnp.zeros((ATT_DH, tq), F32)
            for t in range(nkb):
                ot = ot + jnp.dot(vt_ref[qi + t, heads[h], :], pt[t * tq:(t + 1) * tq, :],
                                  preferred_element_type=F32)
            return ot / l

        sts, pls, outs = {}, {}, []
        for step in range(ATT_HEADS + 2):
            if step < ATT_HEADS:
                sts[step] = scores(step)
            if 0 <= step - 1 < ATT_HEADS:
                pls[step - 1] = softmax(step - 1, sts.pop(step - 1))
            if 0 <= step - 2 < ATT_HEADS:
                outs.append(values(step - 2, *pls.pop(step - 2)))
        o_ref[...] = jnp.concatenate(outs, axis=0).T.astype(BF16)

    first = ATT_PAD // tq

    @pl.when(qi < first)
    def _():
        attend(True)

    @pl.when(qi >= first)
    def _():
        attend(False)


def _attn_bias_row(rel_table):
    t = rel_table.astype(F32) * math.log2(math.e)
    edge = jnp.broadcast_to(t[:, 2 * MAX_REL:], (ATT_HEADS, ATT_QBLOCK))
    row = jnp.concatenate([edge, jnp.flip(t[:, 1:], axis=1), edge], axis=1)
    return row.reshape(ATT_HEADS, 1, 2 * ATT_QBLOCK + 2 * MAX_REL)


def _attention(proj, rel_table, q_gain, k_gain, B, S):
    T = B * S
    tq = ATT_QBLOCK
    nblk = S // tq
    gen = _attn_bias_row(rel_table)
    qg = jnp.tile(q_gain.astype(F32) * (ATT_DH ** -0.5 * math.log2(math.e)), ATT_HEADS).reshape(1, ATT_W)
    kg = jnp.tile(k_gain.astype(F32), ATT_HEADS).reshape(1, ATT_W)
    lane_head = jnp.arange(ATT_W) // ATT_DH
    hm = (lane_head[:, None] == lane_head[None, :]).astype(BF16)
    vec = pl.BlockSpec((1, ATT_W), lambda b, i: (0, 0))
    return pl.pallas_call(
        functools.partial(_attn_kernel, S=S),
        grid=(B, nblk),
        in_specs=[pl.BlockSpec((tq, ATT_W), lambda b, i: (b * nblk + i, COL_AQ)),
                  pl.BlockSpec((S, ATT_W), lambda b, i: (b, COL_AK)),
                  pl.BlockSpec((S, ATT_W), lambda b, i: (b, COL_AV)),
                  pl.BlockSpec(gen.shape, lambda b, i: (0, 0, 0)),
                  vec, vec,
                  pl.BlockSpec((ATT_W, ATT_W), lambda b, i: (0, 0))],
        out_specs=pl.BlockSpec((tq, ATT_W), lambda b, i: (b * nblk + i, 0)),
        out_shape=jax.ShapeDtypeStruct((T, ATT_W), BF16),
        scratch_shapes=[pltpu.VMEM((ATT_HEADS, ATT_PAD + S, ATT_DH), BF16),
                        pltpu.VMEM(((ATT_PAD + S) // tq, ATT_W, tq), BF16),
                        pltpu.VMEM((ATT_HEADS, ATT_BAND, tq), F32)],
        compiler_params=_cparams(("arbitrary", "arbitrary")),
        name="chunk_attention",
    )(proj, proj, proj, gen, qg, kg, hm)


def _merge_kernel(x_ref, ret_ref, conv_ref, att_ref, g0_ref, g1_ref, g2_ref, bg_ref, mod_ref,
                  wr_ref, wc_ref, wa_ref, wo_ref, gf_ref, wrt_ref, brt_ref,
                  xo_ref, h2_ref, route_ref, cnt_out_ref, tri_ref):
    D = D_MODEL

    @pl.when(pl.program_id(0) == 0)
    def _():
        n = tri_ref.shape[0]
        tri_ref[...] = (lax.broadcasted_iota(jnp.int32, (n, n), 1)
                        < lax.broadcasted_iota(jnp.int32, (n, n), 0)).astype(BF16)

    def mix(rs):
        def gated(g_ref, k, b_ref, w_ref):
            gate = jax.nn.sigmoid(g_ref[rs, :].astype(F32) + bg_ref[:, k * D:(k + 1) * D])
            return gate * jnp.dot(b_ref[rs, :], w_ref[...], preferred_element_type=F32)

        merged = (gated(g0_ref, 0, ret_ref, wr_ref) + gated(g1_ref, 1, conv_ref, wc_ref)
                  + gated(g2_ref, 2, att_ref, wa_ref))
        return merged.astype(BF16)

    def project(rs, merged):
        y = jnp.dot(merged, wo_ref[...], preferred_element_type=F32)
        x = x_ref[rs, :] + mod_ref[0, 2:3, :] * y
        xo_ref[rs, :] = x
        h2 = x * lax.rsqrt(jnp.mean(x * x, axis=-1, keepdims=True) + EPS) * gf_ref[...]
        h2 = h2 * (1.0 + mod_ref[0, 4:5, :]) + mod_ref[0, 3:4, :]
        h_hi = h2.astype(BF16)
        h2_ref[rs, :] = h_hi
        h_lo = (h2 - h_hi.astype(F32)).astype(BF16)
        both = jnp.dot(h_hi, wrt_ref[...], preferred_element_type=F32)
        return (both[:, :LANES] + both[:, LANES:]
                + jnp.dot(h_lo, wrt_ref[:, :LANES], preferred_element_type=F32) + brt_ref[...])

    subs = [slice(r0, r0 + MERGE_SUB) for r0 in range(0, x_ref.shape[0], MERGE_SUB)]
    mixed, parts = {}, []
    for step in range(len(subs) + 1):
        if step < len(subs):
            mixed[step] = mix(subs[step])
        if step >= 1:
            parts.append(project(subs[step - 1], mixed.pop(step - 1)))
    logits = jnp.concatenate(parts, axis=0)
    tm = logits.shape[0]
    lane = lax.broadcasted_iota(jnp.int32, (tm, LANES), 1)
    big = jnp.int32(LANES)
    is_g = (lane >= N_EXPERTS) & (lane < N_EXPERTS + N_GROUPS)
    gl = jnp.where(is_g, logits, -jnp.inf)
    gmax = jnp.max(gl, axis=-1, keepdims=True)
    gsel = jnp.min(jnp.where(gl == gmax, lane - N_EXPERTS, big), axis=-1, keepdims=True)
    p_group = 1.0 / jnp.sum(jnp.where(is_g, jnp.exp(gl - gmax), 0.0), axis=-1, keepdims=True)
    in_grp = (lane >= gsel * EXPERTS_PER_GROUP) & (lane < (gsel + 1) * EXPERTS_PER_GROUP)
    ch = jnp.where(in_grp, logits, -jnp.inf)
    v1 = jnp.max(ch, axis=-1, keepdims=True)
    i1 = jnp.min(jnp.where(ch == v1, lane, big), axis=-1, keepdims=True)
    ch2 = jnp.where(lane == i1, -jnp.inf, ch)
    v2 = jnp.max(ch2, axis=-1, keepdims=True)
    i2 = jnp.min(jnp.where(ch2 == v2, lane, big), axis=-1, keepdims=True)
    e = jnp.exp(v2 - v1)
    w1 = p_group / (1.0 + e)
    w2 = p_group * e / (1.0 + e)
    sel = ((lane == i1) | (lane == i2)).astype(BF16)
    before = jnp.dot(tri_ref[...], sel, preferred_element_type=F32)
    rank1 = jnp.sum(jnp.where(lane == i1, before, 0.0), axis=-1, keepdims=True)
    rank2 = jnp.sum(jnp.where(lane == i2, before, 0.0), axis=-1, keepdims=True)
    cnt_out_ref[0] = jnp.sum(sel.astype(F32), axis=0, keepdims=True)
    vals = (i1.astype(F32), i2.astype(F32), w1, w2, rank1, rank2)
    route = jnp.zeros((tm, LANES), F32)
    for k, v in enumerate(vals):
        route = jnp.where(lane == k, v, route)
    route_ref[...] = route


def _merge(x2, proj, ret, conv, att, b_gate, mod, wr, wc, wa, wo, g_ffn, w_rt, b_rt, S):
    T, D = x2.shape
    tm = ROUTE_TILE
    row = lambda i: (i, 0)
    const = lambda i: (0, 0)
    br = pl.BlockSpec((tm, RET_W), row)
    return pl.pallas_call(
        _merge_kernel,
        grid=(T // tm,),
        in_specs=[pl.BlockSpec((tm, D), row), br, br, br,
                  pl.BlockSpec((tm, D), lambda i: (i, COL_GATE)),
                  pl.BlockSpec((tm, D), lambda i: (i, COL_GATE + 1)),
                  pl.BlockSpec((tm, D), lambda i: (i, COL_GATE + 2)),
                  pl.BlockSpec((1, N_BRANCH * D), const),
                  pl.BlockSpec((1, N_MOD, D), lambda i: (i * tm // S, 0, 0)),
                  pl.BlockSpec((RET_W, D), const), pl.BlockSpec((CONV_W, D), const),
                  pl.BlockSpec((ATT_W, D), const), pl.BlockSpec((D, D), const),
                  pl.BlockSpec((1, D), const),
                  pl.BlockSpec((D, 2 * LANES), const), pl.BlockSpec((1, LANES), const)],
        out_specs=[pl.BlockSpec((tm, D), row), pl.BlockSpec((tm, D), row),
                   pl.BlockSpec((tm, LANES), row), pl.BlockSpec((1, 1, LANES), lambda i: (i, 0, 0))],
        out_shape=[jax.ShapeDtypeStruct((T, D), F32), jax.ShapeDtypeStruct((T, D), BF16),
                   jax.ShapeDtypeStruct((T, LANES), F32), jax.ShapeDtypeStruct((T // tm, 1, LANES), F32)],
        scratch_shapes=[pltpu.VMEM((tm, tm), BF16)],
        compiler_params=_cparams(("arbitrary",)),
        name="merge_router",
    )(x2, ret, conv, att, proj, proj, proj, b_gate.reshape(1, N_BRANCH * D), mod,
      wr, wc, wa, wo, g_ffn.reshape(1, D), w_rt, b_rt)


def _pair_slots(route, srcoff_row):
    tm = route.shape[0]
    lane = lax.broadcasted_iota(jnp.int32, (tm, LANES), 1).astype(F32)
    s0 = jnp.sum(jnp.where(lane == route[:, 0:1], srcoff_row, 0.0), axis=-1, keepdims=True) + route[:, 4:5]
    s1 = jnp.sum(jnp.where(lane == route[:, 1:2], srcoff_row, 0.0), axis=-1, keepdims=True) + route[:, 5:6]
    return s0, s1


def _chunk_copies(tile, so_ref, nch_ref, dst_ref, make_copy):
    base = tile * N_EXPERTS
    for e in range(N_EXPERTS):
        so = so_ref[base + e]
        d = dst_ref[base + e]

        def body(c, carry, so=so, d=d):
            make_copy(pl.multiple_of(so + c * ROW_ALIGN, ROW_ALIGN),
                      pl.multiple_of(d + c * ROW_ALIGN, ROW_ALIGN)).start()
            return carry

        lax.fori_loop(0, nch_ref[base + e], body, 0)


def _wait_chunks(n, make_wait):
    bit = 1
    while bit * ROW_ALIGN <= SORT_SLOTS:
        @pl.when((n & bit) != 0)
        def _(bit=bit):
            make_wait(bit * ROW_ALIGN).wait()

        bit *= 2


def _dispatch_kernel(so_ref, nch_ref, dst_ref, ntot_ref, zpos_ref, h_ref, route_ref, sof_ref,
                     xs_hbm, sorted_ref, zero_ref, sem, zsem, *, nsteps):
    i = pl.program_id(0)

    @pl.when(i == 0)
    def _():
        zero_ref[...] = jnp.zeros_like(zero_ref)
        for e in range(N_EXPERTS):
            pltpu.make_async_copy(zero_ref, xs_hbm.at[pl.ds(pl.multiple_of(zpos_ref[e], ROW_ALIGN), MOE_TILE), :],
                                  zsem).start()
        for e in range(N_EXPERTS):
            pltpu.make_async_copy(zero_ref, xs_hbm.at[pl.ds(0, MOE_TILE), :], zsem).wait()
        tail = zpos_ref[N_EXPERTS]
        n_tail = (xs_hbm.shape[0] - tail) // MOE_TILE

        def zero_tail(c, carry):
            cp = pltpu.make_async_copy(
                zero_ref, xs_hbm.at[pl.ds(pl.multiple_of(tail + c * MOE_TILE, MOE_TILE), MOE_TILE), :], zsem)
            cp.start()
            cp.wait()
            return carry

        lax.fori_loop(0, n_tail, zero_tail, 0)

    r = route_ref[...]
    tm = r.shape[0]
    lane = lax.broadcasted_iota(jnp.int32, (tm, LANES), 1)
    s0, s1 = _pair_slots(r, sof_ref[0])
    slots = lax.broadcasted_iota(jnp.int32, (tm, SORT_SLOTS), 1).astype(F32)
    onehot = ((slots == s0) | (slots == s1)).astype(BF16)

    def split(w):
        hi = w.astype(BF16).astype(F32)
        return hi, (w - hi).astype(BF16).astype(F32)

    extra = jnp.zeros((tm, LANES), F32)
    for k, v in enumerate((r[:, 0:1], *split(r[:, 2:3]), *split(r[:, 3:4]))):
        extra = jnp.where(lane == k, v, extra)
    hx = jnp.concatenate([h_ref[...], extra.astype(BF16)], axis=1)
    slot = i % 2
    sorted_ref[slot] = lax.dot_general(onehot, hx, (((0,), (0,)), ((), ())),
                                       preferred_element_type=F32).astype(BF16)

    def copies(s):
        def make_copy(src_row, dst_row):
            return pltpu.make_async_copy(sorted_ref.at[s, pl.ds(src_row, ROW_ALIGN), :],
                                         xs_hbm.at[pl.ds(dst_row, ROW_ALIGN), :], sem.at[s])
        return make_copy

    _chunk_copies(i, so_ref, nch_ref, dst_ref, copies(slot))

    def wait_all(tile, s):
        _wait_chunks(ntot_ref[tile], lambda rows: pltpu.make_async_copy(
            sorted_ref.at[s, pl.ds(0, rows), :], xs_hbm.at[pl.ds(0, rows), :], sem.at[s]))

    @pl.when(i > 0)
    def _():
        wait_all(i - 1, 1 - slot)

    @pl.when(i == nsteps - 1)
    def _():
        wait_all(i, slot)


def _dispatch(h2, route, tabs, n_rows):
    T, D = h2.shape
    tm = ROUTE_TILE
    grid_spec = pltpu.PrefetchScalarGridSpec(
        num_scalar_prefetch=5,
        grid=(T // tm,),
        in_specs=[pl.BlockSpec((tm, D), lambda i, *_: (i, 0)),
                  pl.BlockSpec((tm, LANES), lambda i, *_: (i, 0)),
                  pl.BlockSpec((1, 1, LANES), lambda i, *_: (i, 0, 0))],
        out_specs=pl.BlockSpec(memory_space=pl.ANY),
        scratch_shapes=[pltpu.VMEM((2, SORT_SLOTS, XS_W), BF16), pltpu.VMEM((MOE_TILE, XS_W), BF16),
                        pltpu.SemaphoreType.DMA((2,)), pltpu.SemaphoreType.DMA(())],
    )
    return pl.pallas_call(
        functools.partial(_dispatch_kernel, nsteps=T // tm),
        grid_spec=grid_spec,
        out_shape=jax.ShapeDtypeStruct((n_rows + MOE_TILE, XS_W), BF16),
        compiler_params=_cparams(("arbitrary",)),
        name="moe_dispatch",
    )(tabs["srcoff"], tabs["nchunk"], tabs["dst"], tabs["ntot"], tabs["zpos"], h2, route, tabs["srcoff_f"])


def _ffn_kernel(te_ref, nv_ref, xs_ref, wu_ref, wd_ref, o_ref, wub_ref, wdb_ref):
    i = pl.program_id(0)

    @pl.when((i == 0) | (te_ref[i] != te_ref[jnp.maximum(i - 1, 0)]))
    def _():
        wub_ref[...] = wu_ref[0].astype(BF16)
        wdb_ref[...] = wd_ref[0].astype(BF16)

    @pl.when(i < nv_ref[0])
    def _():
        x = xs_ref[:, :D_MODEL]
        wv = xs_ref[:, D_MODEL:].astype(F32)
        mine = wv[:, 0:1] == te_ref[i].astype(F32)
        w = jnp.where(mine, wv[:, 1:2] + wv[:, 2:3], wv[:, 3:4] + wv[:, 4:5])
        hid = jnp.dot(x, wub_ref[...], preferred_element_type=F32)
        a = hid[:, :EXPERT_HIDDEN]
        b = hid[:, EXPERT_HIDDEN:]
        act = a * jax.nn.sigmoid(a) * b * w
        o_ref[...] = jnp.dot(act.astype(BF16), wdb_ref[...], preferred_element_type=F32).astype(BF16)

    @pl.when(i >= nv_ref[0])
    def _():
        o_ref[...] = jnp.zeros_like(o_ref)


def _ffn(xs, tile_expert, n_valid, w_up, w_down, layer, n_rows):
    tr = MOE_TILE
    D = D_MODEL
    F2 = w_up.shape[-1]
    first = layer * N_EXPERTS
    grid_spec = pltpu.PrefetchScalarGridSpec(
        num_scalar_prefetch=2,
        grid=(n_rows // tr,),
        in_specs=[pl.BlockSpec((tr, XS_W), lambda i, te, nv: (i, 0)),
                  pl.BlockSpec((1, D, F2), lambda i, te, nv: (first + te[i], 0, 0)),
                  pl.BlockSpec((1, EXPERT_HIDDEN, D), lambda i, te, nv: (first + te[i], 0, 0))],
        out_specs=pl.BlockSpec((tr, D), lambda i, te, nv: (i, 0)),
        scratch_shapes=[pltpu.VMEM((D, F2), BF16), pltpu.VMEM((EXPERT_HIDDEN, D), BF16)],
    )
    return pl.pallas_call(
        _ffn_kernel,
        grid_spec=grid_spec,
        out_shape=jax.ShapeDtypeStruct((n_rows, D), BF16),
        compiler_params=_cparams(("arbitrary",)),
        name="moe_experts",
    )(tile_expert, n_valid, xs, w_up, w_down)


def _combine_kernel(so_ref, nch_ref, dst_ref, ntot_ref, x_ref, route_ref, sof_ref, mod_ref, ye_hbm, *rest,
                    nsteps, emit_next):
    if emit_next:
        modn_ref, gn_ref, o_ref, hn_ref, ys_ref, sem = rest
    else:
        o_ref, ys_ref, sem = rest
    i = pl.program_id(0)

    def copies(slot):
        def make_copy(src_row, dst_row):
            return pltpu.make_async_copy(ye_hbm.at[pl.ds(dst_row, ROW_ALIGN), :],
                                         ys_ref.at[slot, pl.ds(src_row, ROW_ALIGN), :], sem.at[slot])
        return make_copy

    @pl.when(i == 0)
    def _():
        ys_ref[...] = jnp.zeros_like(ys_ref)
        _chunk_copies(0, so_ref, nch_ref, dst_ref, copies(0))

    @pl.when(i + 1 < nsteps)
    def _():
        _chunk_copies(i + 1, so_ref, nch_ref, dst_ref, copies((i + 1) % 2))

    slot = i % 2
    _wait_chunks(ntot_ref[i], lambda rows: pltpu.make_async_copy(
        ye_hbm.at[pl.ds(0, rows), :], ys_ref.at[slot, pl.ds(0, rows), :], sem.at[slot]))
    s0, s1 = _pair_slots(route_ref[...], sof_ref[0])
    slots = lax.broadcasted_iota(jnp.int32, (s0.shape[0], SORT_SLOTS), 1).astype(F32)
    onehot = ((slots == s0) | (slots == s1)).astype(BF16)
    y = jnp.dot(onehot, ys_ref[slot], preferred_element_type=F32)
    x = x_ref[...] + mod_ref[0, 5:6, :] * y
    o_ref[...] = x
    if emit_next:
        hn_ref[...] = _modulated_rmsnorm(x, gn_ref[...], modn_ref, 0, 1).astype(BF16)


def _combine(x2, route, mod, ye, tabs, S, next_mod=None, next_gain=None):
    T, D = x2.shape
    tm = ROUTE_TILE
    nsteps = T // tm
    emit_next = next_mod is not None
    row = pl.BlockSpec((tm, D), lambda i, *_: (i, 0))
    modspec = pl.BlockSpec((1, N_MOD, D), lambda i, *_: (i * tm // S, 0, 0))
    in_specs = [row,
                pl.BlockSpec((tm, LANES), lambda i, *_: (i, 0)),
                pl.BlockSpec((1, 1, LANES), lambda i, *_: (i, 0, 0)),
                modspec,
                pl.BlockSpec(memory_space=pl.ANY)]
    args = [x2, route, tabs["srcoff_f"], mod, ye]
    out_specs, out_shape = row, jax.ShapeDtypeStruct((T, D), F32)
    if emit_next:
        in_specs += [modspec, pl.BlockSpec((1, D), lambda i, *_: (0, 0))]
        args += [next_mod, next_gain.reshape(1, D)]
        out_specs, out_shape = [row, row], [out_shape, jax.ShapeDtypeStruct((T, D), BF16)]
    grid_spec = pltpu.PrefetchScalarGridSpec(
        num_scalar_prefetch=4,
        grid=(nsteps,),
        in_specs=in_specs,
        out_specs=out_specs,
        scratch_shapes=[pltpu.VMEM((2, SORT_SLOTS, D), BF16), pltpu.SemaphoreType.DMA((2,))],
    )
    return pl.pallas_call(
        functools.partial(_combine_kernel, nsteps=nsteps, emit_next=emit_next),
        grid_spec=grid_spec,
        out_shape=out_shape,
        compiler_params=_cparams(("arbitrary",)),
        name="moe_combine",
    )(tabs["srcoff"], tabs["nchunk"], tabs["dst"], tabs["ntot"], *args)


def _moe_rows_bound(T):
    nt = T // ROUTE_TILE
    rows = 2 * T + nt * N_EXPERTS * (ROW_ALIGN - 1) + N_EXPERTS * (MOE_TILE - 1)
    return -(-rows // MOE_TILE) * MOE_TILE


def _routing_tables(counts, n_rows):
    tr = MOE_TILE
    c = counts[:, 0, :N_EXPERTS].astype(jnp.int32)
    c16 = (c + ROW_ALIGN - 1) // ROW_ALIGN * ROW_ALIGN
    srcoff = jnp.cumsum(c16, axis=1) - c16
    rows = jnp.sum(c16, axis=0)
    ends = jnp.cumsum((rows + tr - 1) // tr * tr)
    offs = ends - (rows + tr - 1) // tr * tr
    dst = offs[None, :] + jnp.cumsum(c16, axis=0) - c16
    tile_start = jnp.arange(n_rows // tr, dtype=jnp.int32) * tr
    tile_expert = jnp.minimum(jnp.sum((tile_start[:, None] >= ends[None, :]).astype(jnp.int32), axis=-1),
                              N_EXPERTS - 1)
    srcoff_f = jnp.zeros((c.shape[0], 1, LANES), F32).at[:, 0, :N_EXPERTS].set(srcoff.astype(F32))
    return {
        "srcoff": srcoff.reshape(-1), "nchunk": (c16 // ROW_ALIGN).reshape(-1), "dst": dst.reshape(-1),
        "ntot": jnp.sum(c16 // ROW_ALIGN, axis=1), "zpos": jnp.concatenate([offs + rows, ends[-1:]]),
        "srcoff_f": srcoff_f,
        "tile_expert": tile_expert, "n_valid": (ends[-1] // tr).astype(jnp.int32).reshape(1),
    }


def _reorder_in_cols(w_in):
    cuts = 4 * RET_W + 2 * CONV_W + 3 * ATT_W
    return jnp.concatenate([w_in[..., cuts:], w_in[..., :cuts]], axis=-1)


def kernel(x, c, positions, w_ada, b_ada, g_mix, g_ffn, w_in, b_gate, ret_gn, conv_w, conv_b, conv_ln_g, conv_ln_b, att_q_gain, att_k_gain, att_rel_bias, w_ret_out, w_conv_out, w_att_out, w_out, w_group, b_group, w_inner, b_inner, w_up, w_down):
    B, S, D = x.shape
    L = w_ada.shape[0]
    T = B * S
    assert D == D_MODEL and S % 1024 == 0 and T % 2048 == 0
    n_rows = _moe_rows_bound(T)

    mod_all = _adaln(c, w_ada, b_ada).reshape(L, B, N_MOD, D)
    cs_tab = _rope_table(positions)
    x2 = x.reshape(T, D)
    h = _prenorm(x2, mod_all[0], g_mix[0], S)
    for l in range(L):
        mod = mod_all[l]
        proj = _inproj(h, _reorder_in_cols(w_in[l]).astype(BF16))
        ret = _retention(proj, cs_tab, ret_gn[l], B, S)
        conv = _conv(proj, conv_w[l], conv_b[l], conv_ln_g[l], conv_ln_b[l], B, S)
        att = _attention(proj, att_rel_bias[l], att_q_gain[l], att_k_gain[l], B, S)
        w_rt = jnp.zeros((D, LANES), F32)
        w_rt = w_rt.at[:, :N_EXPERTS].set(w_inner[l]).at[:, N_EXPERTS:N_EXPERTS + N_GROUPS].set(w_group[l])
        b_rt = jnp.zeros((1, LANES), F32)
        b_rt = b_rt.at[0, :N_EXPERTS].set(b_inner[l]).at[0, N_EXPERTS:N_EXPERTS + N_GROUPS].set(b_group[l])
        w_rt_hi = w_rt.astype(BF16)
        w_rt2 = jnp.concatenate([w_rt_hi, (w_rt - w_rt_hi.astype(F32)).astype(BF16)], axis=1)
        x2, h2, route, counts = _merge(x2, proj, ret, conv, att, b_gate[l], mod,
                                       w_ret_out[l].astype(BF16), w_conv_out[l].astype(BF16),
                                       w_att_out[l].astype(BF16), w_out[l].astype(BF16),
                                       g_ffn[l], w_rt2, b_rt, S)
        tabs = _routing_tables(counts, n_rows)
        xs = _dispatch(h2, route, tabs, n_rows)
        ye = _ffn(xs, tabs["tile_expert"], tabs["n_valid"],
                  w_up.reshape(L * N_EXPERTS, D, 2 * EXPERT_HIDDEN),
                  w_down.reshape(L * N_EXPERTS, EXPERT_HIDDEN, D), l, n_rows)
        if l + 1 < L:
            x2, h = _combine(x2, route, mod, ye, tabs, S, mod_all[l + 1], g_mix[l + 1])
        else:
            x2 = _combine(x2, route, mod, ye, tabs, S)
    return x2.reshape(B, S, D)
```

```python
import functools
import math

import jax
import jax.numpy as jnp
from jax import lax
from jax.experimental import pallas as pl
from jax.experimental.pallas import tpu as pltpu

F32 = jnp.float32
BF16 = jnp.bfloat16

D_MODEL = 1024
CHUNK = 64
EPS = 1e-6
NEG_INF = -1e30
RET_HEADS = 4
RET_DK = 128
RET_W = 512
ROPE_BASE = 10000.0
CONV_W = 512
CONV_K = 31
ATT_HEADS = 8
ATT_DH = 64
ATT_W = 512
ATT_LEFT_CHUNKS = 8
MAX_REL = 256
N_BRANCH = 3
IN_COLS = 4 * RET_W + 2 * CONV_W + 3 * ATT_W + N_BRANCH * D_MODEL
N_GROUPS = 4
EXPERTS_PER_GROUP = 8
N_EXPERTS = N_GROUPS * EXPERTS_PER_GROUP
EXPERT_HIDDEN = 256
N_MOD = 6

LANES = 128
V7X_VMEM_LIMIT_BYTES = 56 * 1024 * 1024

COL_GATE = 0
COL_RQ, COL_RK, COL_RV, COL_RG = 6, 7, 8, 9
COL_CA, COL_CB = 10, 11
COL_AQ, COL_AK, COL_AV = 12, 13, 14

RET_BLOCK = 256
ATT_QBLOCK = 256
ATT_BAND = ATT_QBLOCK + ATT_LEFT_CHUNKS * CHUNK
ATT_PAD = ATT_LEFT_CHUNKS * CHUNK
CONV_TILE = 512
CONV_HIST = 32
CONV_ROWS = 64
MOE_TILE = 512
ROUTE_TILE = 512
ROW_ALIGN = 16
SORT_SLOTS = 2 * ROUTE_TILE + N_EXPERTS * ROW_ALIGN
XS_W = D_MODEL + LANES


def _cparams(sem):
    return pltpu.CompilerParams(dimension_semantics=sem, vmem_limit_bytes=V7X_VMEM_LIMIT_BYTES)


def _adaln_kernel(c_ref, w_ref, b_ref, o_ref):
    c = c_ref[...]
    cs = c * jax.nn.sigmoid(c)
    o_ref[0] = jnp.dot(cs, w_ref[0], preferred_element_type=F32,
                       precision=lax.Precision.HIGHEST) + b_ref[0]


def _adaln(c, w_ada, b_ada):
    L, D, N = w_ada.shape
    B = c.shape[0]
    tn = 1536
    return pl.pallas_call(
        _adaln_kernel,
        grid=(L, N // tn),
        in_specs=[pl.BlockSpec((B, D), lambda l, j: (0, 0)),
                  pl.BlockSpec((1, D, tn), lambda l, j: (l, 0, j)),
                  pl.BlockSpec((1, 1, tn), lambda l, j: (l, 0, j))],
        out_specs=pl.BlockSpec((1, B, tn), lambda l, j: (l, 0, j)),
        out_shape=jax.ShapeDtypeStruct((L, B, N), F32),
        compiler_params=_cparams(("arbitrary", "arbitrary")),
        name="adaln",
    )(c, w_ada, b_ada.reshape(L, 1, N))


def _rope_kernel(pos_ref, inv_ref, ph_ref, o_ref):
    o_ref[...] = jnp.cos(pos_ref[...] * inv_ref[...] - ph_ref[...])


def _rope_table(positions):
    T = positions.size
    half = RET_DK // 2
    inv = ROPE_BASE ** (-jnp.arange(half, dtype=F32) / half)
    inv2 = jnp.concatenate([inv, inv]).reshape(1, LANES)
    phase = jnp.concatenate([jnp.zeros((half,), F32), jnp.full((half,), math.pi / 2, F32)]).reshape(1, LANES)
    posb = jnp.broadcast_to(positions.reshape(T, 1).astype(F32), (T, LANES))
    tm = 2048
    return pl.pallas_call(
        _rope_kernel,
        grid=(T // tm,),
        in_specs=[pl.BlockSpec((tm, LANES), lambda i: (i, 0)),
                  pl.BlockSpec((1, LANES), lambda i: (0, 0)),
                  pl.BlockSpec((1, LANES), lambda i: (0, 0))],
        out_specs=pl.BlockSpec((tm, LANES), lambda i: (i, 0)),
        out_shape=jax.ShapeDtypeStruct((T, LANES), F32),
        compiler_params=_cparams(("arbitrary",)),
        name="rope_table",
    )(posb, inv2, phase)


def _modulated_rmsnorm(x, g, mod_ref, shift_row, scale_row):
    y = x * lax.rsqrt(jnp.mean(x * x, axis=-1, keepdims=True) + EPS) * g
    return y * (1.0 + mod_ref[0, scale_row:scale_row + 1, :]) + mod_ref[0, shift_row:shift_row + 1, :]


def _prenorm_kernel(x_ref, mod_ref, g_ref, h_ref):
    h_ref[...] = _modulated_rmsnorm(x_ref[...], g_ref[...], mod_ref, 0, 1).astype(BF16)


def _prenorm(x2, mod, g_mix, S):
    T, D = x2.shape
    tm = 1024
    return pl.pallas_call(
        _prenorm_kernel,
        grid=(T // tm,),
        in_specs=[pl.BlockSpec((tm, D), lambda i: (i, 0)),
                  pl.BlockSpec((1, N_MOD, D), lambda i: (i * tm // S, 0, 0)),
                  pl.BlockSpec((1, D), lambda i: (0, 0))],
        out_specs=pl.BlockSpec((tm, D), lambda i: (i, 0)),
        out_shape=jax.ShapeDtypeStruct((T, D), BF16),
        compiler_params=_cparams(("arbitrary",)),
        name="prenorm",
    )(x2, mod, g_mix.reshape(1, D))


def _inproj_kernel(h_ref, w_ref, o_ref):
    o_ref[...] = jnp.dot(h_ref[...], w_ref[...], preferred_element_type=F32).astype(BF16)


def _inproj(h, w_in_b):
    T, D = h.shape
    N = w_in_b.shape[1]
    tm, tn = 2048, 2560
    return pl.pallas_call(
        _inproj_kernel,
        grid=(T // tm, N // tn),
        in_specs=[pl.BlockSpec((tm, D), lambda i, j: (i, 0)),
                  pl.BlockSpec((D, tn), lambda i, j: (0, j))],
        out_specs=pl.BlockSpec((tm, tn), lambda i, j: (i, j)),
        out_shape=jax.ShapeDtypeStruct((T, N), BF16),
        compiler_params=_cparams(("arbitrary", "arbitrary")),
        name="inproj",
    )(h, w_in_b)


def _retention_kernel(q_ref, k_ref, v_ref, g_ref, cs_ref, gn_ref, o_ref, state_ref, decay_ref):
    C = RET_BLOCK

    @pl.when(pl.program_id(1) == 0)
    def _():
        state_ref[...] = jnp.zeros_like(state_ref)

    @pl.when((pl.program_id(0) == 0) & (pl.program_id(1) == 0))
    def _():
        d = (lax.broadcasted_iota(jnp.int32, (C, C), 0) - lax.broadcasted_iota(jnp.int32, (C, C), 1)).astype(F32)
        for h in range(RET_HEADS):
            lg = math.log1p(-(2.0 ** (-5.0 - h)))
            decay_ref[h] = jnp.where(d >= 0.0, jnp.exp(lg * jnp.maximum(d, 0.0)), 0.0)

    cs = cs_ref[...]
    csr = pltpu.roll(cs, RET_DK // 2, 1)
    first = lax.broadcasted_iota(jnp.int32, (C, RET_DK), 1) < RET_DK // 2
    cos2 = jnp.where(first, cs, csr)
    sin2 = jnp.where(first, -csr, cs)
    row = lax.broadcasted_iota(jnp.int32, (C, 1), 0).astype(F32)
    for h in range(RET_HEADS):
        lg = math.log1p(-(2.0 ** (-5.0 - h)))
        sl = slice(h * RET_DK, (h + 1) * RET_DK)
        qh = q_ref[:, sl].astype(F32)
        kh = k_ref[:, sl].astype(F32)
        vh = v_ref[:, sl]
        qr = (qh * cos2 + pltpu.roll(qh, RET_DK // 2, 1) * sin2).astype(BF16)
        kr = (kh * cos2 + pltpu.roll(kh, RET_DK // 2, 1) * sin2) * (RET_DK ** -0.5)
        s = lax.dot_general(qr, kr.astype(BF16), (((1,), (1,)), ((), ())),
                            preferred_element_type=F32) * decay_ref[h]
        o = jnp.dot(s.astype(BF16), vh, preferred_element_type=F32)
        st = state_ref[h]
        o = o + jnp.dot(qr, st.astype(BF16), preferred_element_type=F32) * jnp.exp(lg * (row + 1.0))
        kd = (kr * jnp.exp(lg * (C - 1.0 - row))).astype(BF16)
        kv = lax.dot_general(kd, vh, (((0,), (0,)), ((), ())), preferred_element_type=F32)
        state_ref[h] = math.exp(lg * C) * st + kv
        mu = jnp.mean(o, axis=-1, keepdims=True)
        oc = o - mu
        var = jnp.mean(oc * oc, axis=-1, keepdims=True)
        on = oc * lax.rsqrt(var + EPS) * gn_ref[:, sl]
        gh = g_ref[:, sl].astype(F32)
        o_ref[:, sl] = (on * (gh * jax.nn.sigmoid(gh))).astype(BF16)


def _retention(proj, cs_tab, ret_gn, B, S):
    T = B * S
    C = RET_BLOCK
    nblk = S // C

    def col(c):
        return pl.BlockSpec((C, RET_W), lambda b, i, c=c: (b * nblk + i, c))

    return pl.pallas_call(
        _retention_kernel,
        grid=(B, nblk),
        in_specs=[col(COL_RQ), col(COL_RK), col(COL_RV), col(COL_RG),
                  pl.BlockSpec((C, LANES), lambda b, i: (b * nblk + i, 0)),
                  pl.BlockSpec((1, RET_W), lambda b, i: (0, 0))],
        out_specs=pl.BlockSpec((C, RET_W), lambda b, i: (b * nblk + i, 0)),
        out_shape=jax.ShapeDtypeStruct((T, RET_W), BF16),
        scratch_shapes=[pltpu.VMEM((RET_HEADS, RET_DK, RET_DK), F32),
                        pltpu.VMEM((RET_HEADS, C, C), F32)],
        compiler_params=_cparams(("arbitrary", "arbitrary")),
        name="retention",
    )(proj, proj, proj, proj, cs_tab, ret_gn.reshape(1, RET_W))


def _conv_kernel(a_ref, b_ref, w_ref, bias_ref, lg_ref, lb_ref, o_ref, z_ref, zs_ref):
    tm = CONV_TILE
    H = CONV_HIST
    SUB = 8

    @pl.when(pl.program_id(1) == 0)
    def _():
        z_ref[0:H, :] = jnp.zeros((H, CONV_W), F32)

    a = a_ref[...].astype(F32)
    b = b_ref[...].astype(F32)
    z_ref[H:H + tm, :] = a * jax.nn.sigmoid(b)
    base = H - (CONV_K - 1)
    n_shift = tm + (base + CONV_K - 2) // SUB * SUB
    for r in range(1, SUB):
        for c0 in range(0, n_shift, 128):
            cs = min(128, n_shift - c0)
            zs_ref[r - 1, c0:c0 + cs, :] = z_ref[c0 + r:c0 + r + cs, :]
    for r0 in range(0, tm, CONV_ROWS):
        acc = jnp.zeros((CONV_ROWS, CONV_W), F32)
        for j in range(CONV_K):
            al, r = (base + j) // SUB * SUB, (base + j) % SUB
            if r == 0:
                tap = z_ref[r0 + al:r0 + al + CONV_ROWS, :]
            else:
                tap = zs_ref[r - 1, r0 + al:r0 + al + CONV_ROWS, :]
            acc = acc + tap * w_ref[j:j + 1, :]
        acc = acc + bias_ref[...]
        mu = jnp.mean(acc, axis=-1, keepdims=True)
        ac = acc - mu
        var = jnp.mean(ac * ac, axis=-1, keepdims=True)
        y = ac * lax.rsqrt(var + EPS) * lg_ref[...] + lb_ref[...]
        o_ref[r0:r0 + CONV_ROWS, :] = (y * jax.nn.sigmoid(y)).astype(BF16)
    z_ref[0:H, :] = z_ref[tm:tm + H, :]


def _conv(proj, conv_w, conv_b, ln_g, ln_b, B, S):
    T = B * S
    tm = CONV_TILE
    nblk = S // tm
    vec = pl.BlockSpec((1, CONV_W), lambda b, i: (0, 0))
    return pl.pallas_call(
        _conv_kernel,
        grid=(B, nblk),
        in_specs=[pl.BlockSpec((tm, CONV_W), lambda b, i: (b * nblk + i, COL_CA)),
                  pl.BlockSpec((tm, CONV_W), lambda b, i: (b * nblk + i, COL_CB)),
                  pl.BlockSpec((CONV_K, CONV_W), lambda b, i: (0, 0)),
                  vec, vec, vec],
        out_specs=pl.BlockSpec((tm, CONV_W), lambda b, i: (b * nblk + i, 0)),
        out_shape=jax.ShapeDtypeStruct((T, CONV_W), BF16),
        scratch_shapes=[pltpu.VMEM((CONV_HIST + tm, CONV_W), F32),
                        pltpu.VMEM((7, CONV_HIST + tm, CONV_W), F32)],
        compiler_params=_cparams(("arbitrary", "arbitrary")),
        name="conv",
    )(proj, proj, conv_w.reshape(CONV_K, CONV_W), conv_b.reshape(1, CONV_W),
      ln_g.reshape(1, CONV_W), ln_b.reshape(1, CONV_W))


def _head_mean_sq(x, hm_ref):
    sq = x * x
    hi = sq.astype(BF16)
    lo = (sq - hi.astype(F32)).astype(BF16)
    tot = jnp.dot(hi, hm_ref[...], preferred_element_type=F32) + jnp.dot(lo, hm_ref[...], preferred_element_type=F32)
    return tot * (1.0 / ATT_DH)


def _attn_kernel(q_ref, k_ref, v_ref, rrow_ref, qg_ref, kg_ref, hm_ref, o_ref, kn_ref, vt_ref, bmt_ref, *, S):
    tq = ATT_QBLOCK
    qi = pl.program_id(1)
    heads = [slice(h * ATT_DH, (h + 1) * ATT_DH) for h in range(ATT_HEADS)]
    nkb = ATT_BAND // tq

    @pl.when((qi == 0) & (pl.program_id(0) == 0))
    def _():
        r = lax.broadcasted_iota(jnp.int32, (tq, ATT_BAND), 0)
        u = lax.broadcasted_iota(jnp.int32, (tq, ATT_BAND), 1)
        off = u - (r // CHUNK) * CHUNK
        inband = (off >= 0) & (off < (ATT_LEFT_CHUNKS + 1) * CHUNK)
        for h in range(ATT_HEADS):
            gen = jnp.broadcast_to(rrow_ref[h], (tq, rrow_ref.shape[-1]))
            t = pltpu.roll(gen, 0, 1, stride=1, stride_axis=0)
            bmt_ref[h] = jnp.where(inband, t[:, :ATT_BAND], NEG_INF).T

    @pl.when(qi == 0)
    def _():
        for h in range(ATT_HEADS):
            kn_ref[h, 0:ATT_PAD, :] = jnp.zeros((ATT_PAD, ATT_DH), BF16)
        for j in range(ATT_PAD // tq):
            vt_ref[j] = jnp.zeros((ATT_W, tq), BF16)
        rows = 512
        for r0 in range(0, S, rows):
            k = k_ref[r0:r0 + rows, :].astype(F32)
            kn = (k * lax.rsqrt(_head_mean_sq(k, hm_ref) + EPS) * kg_ref[...]).astype(BF16)
            for h in range(ATT_HEADS):
                kn_ref[h, ATT_PAD + r0:ATT_PAD + r0 + rows, :] = kn[:, heads[h]]
        for r0 in range(0, S, tq):
            vt_ref[(ATT_PAD + r0) // tq] = v_ref[r0:r0 + tq, :].astype(F32).T.astype(BF16)

    start = pl.multiple_of(qi * tq, tq)
    q = q_ref[...].astype(F32)
    qn = (q * lax.rsqrt(_head_mean_sq(q, hm_ref) + EPS) * qg_ref[...]).astype(BF16)

    def attend(mask_pad):
        if mask_pad:
            key_pos = lax.broadcasted_iota(jnp.int32, (ATT_BAND, LANES), 0) + start
            valid = key_pos >= ATT_PAD
        def scores(h):
            kb = kn_ref[h, pl.ds(start, ATT_BAND), :]
            return lax.dot_general(kb, qn[:, heads[h]], (((1,), (1,)), ((), ())), preferred_element_type=F32)

        def softmax(h, st):
            ps, ls = [], []
            for c0 in range(0, tq, LANES):
                ss = st[:, c0:c0 + LANES] + bmt_ref[h, :, c0:c0 + LANES]
                if mask_pad:
                    ss = jnp.where(valid, ss, NEG_INF)
                m = jnp.max(ss, axis=0, keepdims=True)
                p = jnp.exp2(ss - m)
                ls.append(jnp.sum(p, axis=0, keepdims=True))
                ps.append(p.astype(BF16))
            return jnp.concatenate(ps, axis=1), jnp.concatenate(ls, axis=1)

        def values(h, pt, l):
            ot = jnp.zeros((ATT_DH, tq), F32)
            for t in range(nkb):
                ot = ot + jnp.dot(vt_ref[qi + t, heads[h], :], pt[t * tq:(t + 1) * tq, :],
                                  preferred_element_type=F32)
            return ot / l

        sts, pls, outs = {}, {}, []
        for step in range(ATT_HEADS + 2):
            if step < ATT_HEADS:
                sts[step] = scores(step)
            if 0 <= step - 1 < ATT_HEADS:
                pls[step - 1] = softmax(step - 1, sts.pop(step - 1))
            if 0 <= step - 2 < ATT_HEADS:
                outs.append(values(step - 2, *pls.pop(step - 2)))
        o_ref[...] = jnp.concatenate(outs, axis=0).T.astype(BF16)

    first = ATT_PAD // tq

    @pl.when(qi < first)
    def _():
        attend(True)

    @pl.when(qi >= first)
    def _():
        attend(False)


def _attn_bias_row(rel_table):
    t = rel_table.astype(F32) * math.log2(math.e)
    edge = jnp.broadcast_to(t[:, 2 * MAX_REL:], (ATT_HEADS, ATT_QBLOCK))
    row = jnp.concatenate([edge, jnp.flip(t[:, 1:], axis=1), edge], axis=1)
    return row.reshape(ATT_HEADS, 1, 2 * ATT_QBLOCK + 2 * MAX_REL)


def _attention(proj, rel_table, q_gain, k_gain, B, S):
    T = B * S
    tq = ATT_QBLOCK
    nblk = S // tq
    gen = _attn_bias_row(rel_table)
    qg = jnp.tile(q_gain.astype(F32) * (ATT_DH ** -0.5 * math.log2(math.e)), ATT_HEADS).reshape(1, ATT_W)
    kg = jnp.tile(k_gain.astype(F32), ATT_HEADS).reshape(1, ATT_W)
    lane_head = jnp.arange(ATT_W) // ATT_DH
    hm = (lane_head[:, None] == lane_head[None, :]).astype(BF16)
    vec = pl.BlockSpec((1, ATT_W), lambda b, i: (0, 0))
    return pl.pallas_call(
        functools.partial(_attn_kernel, S=S),
        grid=(B, nblk),
        in_specs=[pl.BlockSpec((tq, ATT_W), lambda b, i: (b * nblk + i, COL_AQ)),
                  pl.BlockSpec((S, ATT_W), lambda b, i: (b, COL_AK)),
                  pl.BlockSpec((S, ATT_W), lambda b, i: (b, COL_AV)),
                  pl.BlockSpec(gen.shape, lambda b, i: (0, 0, 0)),
                  vec, vec,
                  pl.BlockSpec((ATT_W, ATT_W), lambda b, i: (0, 0))],
        out_specs=pl.BlockSpec((tq, ATT_W), lambda b, i: (b * nblk + i, 0)),
        out_shape=jax.ShapeDtypeStruct((T, ATT_W), BF16),
        scratch_shapes=[pltpu.VMEM((ATT_HEADS, ATT_PAD + S, ATT_DH), BF16),
                        pltpu.VMEM(((ATT_PAD + S) // tq, ATT_W, tq), BF16),
                        pltpu.VMEM((ATT_HEADS, ATT_BAND, tq), F32)],
        compiler_params=_cparams(("arbitrary", "arbitrary")),
        name="chunk_attention",
    )(proj, proj, proj, gen, qg, kg, hm)


def _merge_kernel(x_ref, ret_ref, conv_ref, att_ref, g0_ref, g1_ref, g2_ref, bg_ref, mod_ref,
                  wr_ref, wc_ref, wa_ref, wo_ref, gf_ref, wrt_ref, brt_ref,
                  xo_ref, h2_ref, route_ref, cnt_out_ref, tri_ref):
    D = D_MODEL

    @pl.when(pl.program_id(0) == 0)
    def _():
        n = tri_ref.shape[0]
        tri_ref[...] = (lax.broadcasted_iota(jnp.int32, (n, n), 1)
                        < lax.broadcasted_iota(jnp.int32, (n, n), 0)).astype(BF16)

    def gated(g_ref, k, b_ref, w_ref):
        gate = jax.nn.sigmoid(g_ref[...].astype(F32) + bg_ref[:, k * D:(k + 1) * D])
        return gate * jnp.dot(b_ref[...], w_ref[...], preferred_element_type=F32)

    merged = gated(g0_ref, 0, ret_ref, wr_ref) + gated(g1_ref, 1, conv_ref, wc_ref) + gated(g2_ref, 2, att_ref, wa_ref)
    y = jnp.dot(merged.astype(BF16), wo_ref[...], preferred_element_type=F32)
    x = x_ref[...] + mod_ref[0, 2:3, :] * y
    xo_ref[...] = x

    h2 = _modulated_rmsnorm(x, gf_ref[...], mod_ref, 3, 4)
    h_hi = h2.astype(BF16)
    h2_ref[...] = h_hi
    h_lo = (h2 - h_hi.astype(F32)).astype(BF16)
    both = jnp.dot(h_hi, wrt_ref[...], preferred_element_type=F32)
    logits = (both[:, :LANES] + both[:, LANES:]
              + jnp.dot(h_lo, wrt_ref[:, :LANES], preferred_element_type=F32) + brt_ref[...])
    tm = logits.shape[0]
    lane = lax.broadcasted_iota(jnp.int32, (tm, LANES), 1)
    big = jnp.int32(LANES)
    is_g = (lane >= N_EXPERTS) & (lane < N_EXPERTS + N_GROUPS)
    gl = jnp.where(is_g, logits, -jnp.inf)
    gmax = jnp.max(gl, axis=-1, keepdims=True)
    gsel = jnp.min(jnp.where(gl == gmax, lane - N_EXPERTS, big), axis=-1, keepdims=True)
    p_group = 1.0 / jnp.sum(jnp.where(is_g, jnp.exp(gl - gmax), 0.0), axis=-1, keepdims=True)
    in_grp = (lane >= gsel * EXPERTS_PER_GROUP) & (lane < (gsel + 1) * EXPERTS_PER_GROUP)
    ch = jnp.where(in_grp, logits, -jnp.inf)
    v1 = jnp.max(ch, axis=-1, keepdims=True)
    i1 = jnp.min(jnp.where(ch == v1, lane, big), axis=-1, keepdims=True)
    ch2 = jnp.where(lane == i1, -jnp.inf, ch)
    v2 = jnp.max(ch2, axis=-1, keepdims=True)
    i2 = jnp.min(jnp.where(ch2 == v2, lane, big), axis=-1, keepdims=True)
    e = jnp.exp(v2 - v1)
    w1 = p_group / (1.0 + e)
    w2 = p_group * e / (1.0 + e)
    sel = ((lane == i1) | (lane == i2)).astype(BF16)
    before = jnp.dot(tri_ref[...], sel, preferred_element_type=F32)
    rank1 = jnp.sum(jnp.where(lane == i1, before, 0.0), axis=-1, keepdims=True)
    rank2 = jnp.sum(jnp.where(lane == i2, before, 0.0), axis=-1, keepdims=True)
    cnt_out_ref[0] = jnp.sum(sel.astype(F32), axis=0, keepdims=True)
    vals = (i1.astype(F32), i2.astype(F32), w1, w2, rank1, rank2)
    route = jnp.zeros((tm, LANES), F32)
    for k, v in enumerate(vals):
        route = jnp.where(lane == k, v, route)
    route_ref[...] = route


def _merge(x2, proj, ret, conv, att, b_gate, mod, wr, wc, wa, wo, g_ffn, w_rt, b_rt, S):
    T, D = x2.shape
    tm = ROUTE_TILE
    row = lambda i: (i, 0)
    const = lambda i: (0, 0)
    br = pl.BlockSpec((tm, RET_W), row)
    return pl.pallas_call(
        _merge_kernel,
        grid=(T // tm,),
        in_specs=[pl.BlockSpec((tm, D), row), br, br, br,
                  pl.BlockSpec((tm, D), lambda i: (i, COL_GATE)),
                  pl.BlockSpec((tm, D), lambda i: (i, COL_GATE + 1)),
                  pl.BlockSpec((tm, D), lambda i: (i, COL_GATE + 2)),
                  pl.BlockSpec((1, N_BRANCH * D), const),
                  pl.BlockSpec((1, N_MOD, D), lambda i: (i * tm // S, 0, 0)),
                  pl.BlockSpec((RET_W, D), const), pl.BlockSpec((CONV_W, D), const),
                  pl.BlockSpec((ATT_W, D), const), pl.BlockSpec((D, D), const),
                  pl.BlockSpec((1, D), const),
                  pl.BlockSpec((D, 2 * LANES), const), pl.BlockSpec((1, LANES), const)],
        out_specs=[pl.BlockSpec((tm, D), row), pl.BlockSpec((tm, D), row),
                   pl.BlockSpec((tm, LANES), row), pl.BlockSpec((1, 1, LANES), lambda i: (i, 0, 0))],
        out_shape=[jax.ShapeDtypeStruct((T, D), F32), jax.ShapeDtypeStruct((T, D), BF16),
                   jax.ShapeDtypeStruct((T, LANES), F32), jax.ShapeDtypeStruct((T // tm, 1, LANES), F32)],
        scratch_shapes=[pltpu.VMEM((tm, tm), BF16)],
        compiler_params=_cparams(("arbitrary",)),
        name="merge_router",
    )(x2, ret, conv, att, proj, proj, proj, b_gate.reshape(1, N_BRANCH * D), mod,
      wr, wc, wa, wo, g_ffn.reshape(1, D), w_rt, b_rt)


def _pair_slots(route, srcoff_row):
    tm = route.shape[0]
    lane = lax.broadcasted_iota(jnp.int32, (tm, LANES), 1).astype(F32)
    s0 = jnp.sum(jnp.where(lane == route[:, 0:1], srcoff_row, 0.0), axis=-1, keepdims=True) + route[:, 4:5]
    s1 = jnp.sum(jnp.where(lane == route[:, 1:2], srcoff_row, 0.0), axis=-1, keepdims=True) + route[:, 5:6]
    return s0, s1


def _chunk_copies(tile, so_ref, nch_ref, dst_ref, make_copy):
    base = tile * N_EXPERTS
    for e in range(N_EXPERTS):
        so = so_ref[base + e]
        d = dst_ref[base + e]

        def body(c, carry, so=so, d=d):
            make_copy(pl.multiple_of(so + c * ROW_ALIGN, ROW_ALIGN),
                      pl.multiple_of(d + c * ROW_ALIGN, ROW_ALIGN)).start()
            return carry

        lax.fori_loop(0, nch_ref[base + e], body, 0)


def _wait_chunks(n, make_wait):
    bit = 1
    while bit * ROW_ALIGN <= SORT_SLOTS:
        @pl.when((n & bit) != 0)
        def _(bit=bit):
            make_wait(bit * ROW_ALIGN).wait()

        bit *= 2


def _dispatch_kernel(so_ref, nch_ref, dst_ref, ntot_ref, zpos_ref, h_ref, route_ref, sof_ref,
                     xs_hbm, sorted_ref, zero_ref, sem, zsem, *, nsteps):
    i = pl.program_id(0)

    @pl.when(i == 0)
    def _():
        zero_ref[...] = jnp.zeros_like(zero_ref)
        for e in range(N_EXPERTS):
            pltpu.make_async_copy(zero_ref, xs_hbm.at[pl.ds(pl.multiple_of(zpos_ref[e], ROW_ALIGN), MOE_TILE), :],
                                  zsem).start()
        for e in range(N_EXPERTS):
            pltpu.make_async_copy(zero_ref, xs_hbm.at[pl.ds(0, MOE_TILE), :], zsem).wait()
        tail = zpos_ref[N_EXPERTS]
        n_tail = (xs_hbm.shape[0] - tail) // MOE_TILE

        def zero_tail(c, carry):
            cp = pltpu.make_async_copy(
                zero_ref, xs_hbm.at[pl.ds(pl.multiple_of(tail + c * MOE_TILE, MOE_TILE), MOE_TILE), :], zsem)
            cp.start()
            cp.wait()
            return carry

        lax.fori_loop(0, n_tail, zero_tail, 0)

    r = route_ref[...]
    tm = r.shape[0]
    lane = lax.broadcasted_iota(jnp.int32, (tm, LANES), 1)
    s0, s1 = _pair_slots(r, sof_ref[0])
    slots = lax.broadcasted_iota(jnp.int32, (tm, SORT_SLOTS), 1).astype(F32)
    onehot = ((slots == s0) | (slots == s1)).astype(BF16)

    def split(w):
        hi = w.astype(BF16).astype(F32)
        return hi, (w - hi).astype(BF16).astype(F32)

    extra = jnp.zeros((tm, LANES), F32)
    for k, v in enumerate((r[:, 0:1], *split(r[:, 2:3]), *split(r[:, 3:4]))):
        extra = jnp.where(lane == k, v, extra)
    hx = jnp.concatenate([h_ref[...], extra.astype(BF16)], axis=1)
    slot = i % 2
    sorted_ref[slot] = lax.dot_general(onehot, hx, (((0,), (0,)), ((), ())),
                                       preferred_element_type=F32).astype(BF16)

    def copies(s):
        def make_copy(src_row, dst_row):
            return pltpu.make_async_copy(sorted_ref.at[s, pl.ds(src_row, ROW_ALIGN), :],
                                         xs_hbm.at[pl.ds(dst_row, ROW_ALIGN), :], sem.at[s])
        return make_copy

    _chunk_copies(i, so_ref, nch_ref, dst_ref, copies(slot))

    def wait_all(tile, s):
        _wait_chunks(ntot_ref[tile], lambda rows: pltpu.make_async_copy(
            sorted_ref.at[s, pl.ds(0, rows), :], xs_hbm.at[pl.ds(0, rows), :], sem.at[s]))

    @pl.when(i > 0)
    def _():
        wait_all(i - 1, 1 - slot)

    @pl.when(i == nsteps - 1)
    def _():
        wait_all(i, slot)


def _dispatch(h2, route, tabs, n_rows):
    T, D = h2.shape
    tm = ROUTE_TILE
    grid_spec = pltpu.PrefetchScalarGridSpec(
        num_scalar_prefetch=5,
        grid=(T // tm,),
        in_specs=[pl.BlockSpec((tm, D), lambda i, *_: (i, 0)),
                  pl.BlockSpec((tm, LANES), lambda i, *_: (i, 0)),
                  pl.BlockSpec((1, 1, LANES), lambda i, *_: (i, 0, 0))],
        out_specs=pl.BlockSpec(memory_space=pl.ANY),
        scratch_shapes=[pltpu.VMEM((2, SORT_SLOTS, XS_W), BF16), pltpu.VMEM((MOE_TILE, XS_W), BF16),
                        pltpu.SemaphoreType.DMA((2,)), pltpu.SemaphoreType.DMA(())],
    )
    return pl.pallas_call(
        functools.partial(_dispatch_kernel, nsteps=T // tm),
        grid_spec=grid_spec,
        out_shape=jax.ShapeDtypeStruct((n_rows + MOE_TILE, XS_W), BF16),
        compiler_params=_cparams(("arbitrary",)),
        name="moe_dispatch",
    )(tabs["srcoff"], tabs["nchunk"], tabs["dst"], tabs["ntot"], tabs["zpos"], h2, route, tabs["srcoff_f"])


def _ffn_kernel(te_ref, nv_ref, xs_ref, wu_ref, wd_ref, o_ref, wub_ref, wdb_ref):
    i = pl.program_id(0)

    @pl.when((i == 0) | (te_ref[i] != te_ref[jnp.maximum(i - 1, 0)]))
    def _():
        wub_ref[...] = wu_ref[0].astype(BF16)
        wdb_ref[...] = wd_ref[0].astype(BF16)

    @pl.when(i < nv_ref[0])
    def _():
        x = xs_ref[:, :D_MODEL]
        wv = xs_ref[:, D_MODEL:].astype(F32)
        mine = wv[:, 0:1] == te_ref[i].astype(F32)
        w = jnp.where(mine, wv[:, 1:2] + wv[:, 2:3], wv[:, 3:4] + wv[:, 4:5])
        hid = jnp.dot(x, wub_ref[...], preferred_element_type=F32)
        a = hid[:, :EXPERT_HIDDEN]
        b = hid[:, EXPERT_HIDDEN:]
        act = a * jax.nn.sigmoid(a) * b * w
        o_ref[...] = jnp.dot(act.astype(BF16), wdb_ref[...], preferred_element_type=F32).astype(BF16)

    @pl.when(i >= nv_ref[0])
    def _():
        o_ref[...] = jnp.zeros_like(o_ref)


def _ffn(xs, tile_expert, n_valid, w_up, w_down, layer, n_rows):
    tr = MOE_TILE
    D = D_MODEL
    F2 = w_up.shape[-1]
    first = layer * N_EXPERTS
    grid_spec = pltpu.PrefetchScalarGridSpec(
        num_scalar_prefetch=2,
        grid=(n_rows // tr,),
        in_specs=[pl.BlockSpec((tr, XS_W), lambda i, te, nv: (i, 0)),
                  pl.BlockSpec((1, D, F2), lambda i, te, nv: (first + te[i], 0, 0)),
                  pl.BlockSpec((1, EXPERT_HIDDEN, D), lambda i, te, nv: (first + te[i], 0, 0))],
        out_specs=pl.BlockSpec((tr, D), lambda i, te, nv: (i, 0)),
        scratch_shapes=[pltpu.VMEM((D, F2), BF16), pltpu.VMEM((EXPERT_HIDDEN, D), BF16)],
    )
    return pl.pallas_call(
        _ffn_kernel,
        grid_spec=grid_spec,
        out_shape=jax.ShapeDtypeStruct((n_rows, D), BF16),
        compiler_params=_cparams(("arbitrary",)),
        name="moe_experts",
    )(tile_expert, n_valid, xs, w_up, w_down)


def _combine_kernel(so_ref, nch_ref, dst_ref, ntot_ref, x_ref, route_ref, sof_ref, mod_ref, ye_hbm, *rest,
                    nsteps, emit_next):
    if emit_next:
        modn_ref, gn_ref, o_ref, hn_ref, ys_ref, sem = rest
    else:
        o_ref, ys_ref, sem = rest
    i = pl.program_id(0)

    def copies(slot):
        def make_copy(src_row, dst_row):
            return pltpu.make_async_copy(ye_hbm.at[pl.ds(dst_row, ROW_ALIGN), :],
                                         ys_ref.at[slot, pl.ds(src_row, ROW_ALIGN), :], sem.at[slot])
        return make_copy

    @pl.when(i == 0)
    def _():
        ys_ref[...] = jnp.zeros_like(ys_ref)
        _chunk_copies(0, so_ref, nch_ref, dst_ref, copies(0))

    @pl.when(i + 1 < nsteps)
    def _():
        _chunk_copies(i + 1, so_ref, nch_ref, dst_ref, copies((i + 1) % 2))

    slot = i % 2
    _wait_chunks(ntot_ref[i], lambda rows: pltpu.make_async_copy(
        ye_hbm.at[pl.ds(0, rows), :], ys_ref.at[slot, pl.ds(0, rows), :], sem.at[slot]))
    s0, s1 = _pair_slots(route_ref[...], sof_ref[0])
    slots = lax.broadcasted_iota(jnp.int32, (s0.shape[0], SORT_SLOTS), 1).astype(F32)
    onehot = ((slots == s0) | (slots == s1)).astype(BF16)
    y = jnp.dot(onehot, ys_ref[slot], preferred_element_type=F32)
    x = x_ref[...] + mod_ref[0, 5:6, :] * y
    o_ref[...] = x
    if emit_next:
        hn_ref[...] = _modulated_rmsnorm(x, gn_ref[...], modn_ref, 0, 1).astype(BF16)


def _combine(x2, route, mod, ye, tabs, S, next_mod=None, next_gain=None):
    T, D = x2.shape
    tm = ROUTE_TILE
    nsteps = T // tm
    emit_next = next_mod is not None
    row = pl.BlockSpec((tm, D), lambda i, *_: (i, 0))
    modspec = pl.BlockSpec((1, N_MOD, D), lambda i, *_: (i * tm // S, 0, 0))
    in_specs = [row,
                pl.BlockSpec((tm, LANES), lambda i, *_: (i, 0)),
                pl.BlockSpec((1, 1, LANES), lambda i, *_: (i, 0, 0)),
                modspec,
                pl.BlockSpec(memory_space=pl.ANY)]
    args = [x2, route, tabs["srcoff_f"], mod, ye]
    out_specs, out_shape = row, jax.ShapeDtypeStruct((T, D), F32)
    if emit_next:
        in_specs += [modspec, pl.BlockSpec((1, D), lambda i, *_: (0, 0))]
        args += [next_mod, next_gain.reshape(1, D)]
        out_specs, out_shape = [row, row], [out_shape, jax.ShapeDtypeStruct((T, D), BF16)]
    grid_spec = pltpu.PrefetchScalarGridSpec(
        num_scalar_prefetch=4,
        grid=(nsteps,),
        in_specs=in_specs,
        out_specs=out_specs,
        scratch_shapes=[pltpu.VMEM((2, SORT_SLOTS, D), BF16), pltpu.SemaphoreType.DMA((2,))],
    )
    return pl.pallas_call(
        functools.partial(_combine_kernel, nsteps=nsteps, emit_next=emit_next),
        grid_spec=grid_spec,
        out_shape=out_shape,
        compiler_params=_cparams(("arbitrary",)),
        name="moe_combine",
    )(tabs["srcoff"], tabs["nchunk"], tabs["dst"], tabs["ntot"], *args)


def _moe_rows_bound(T):
    nt = T // ROUTE_TILE
    rows = 2 * T + nt * N_EXPERTS * (ROW_ALIGN - 1) + N_EXPERTS * (MOE_TILE - 1)
    return -(-rows // MOE_TILE) * MOE_TILE


def _routing_tables(counts, n_rows):
    tr = MOE_TILE
    c = counts[:, 0, :N_EXPERTS].astype(jnp.int32)
    c16 = (c + ROW_ALIGN - 1) // ROW_ALIGN * ROW_ALIGN
    srcoff = jnp.cumsum(c16, axis=1) - c16
    rows = jnp.sum(c16, axis=0)
    ends = jnp.cumsum((rows + tr - 1) // tr * tr)
    offs = ends - (rows + tr - 1) // tr * tr
    dst = offs[None, :] + jnp.cumsum(c16, axis=0) - c16
    tile_start = jnp.arange(n_rows // tr, dtype=jnp.int32) * tr
    tile_expert = jnp.minimum(jnp.sum((tile_start[:, None] >= ends[None, :]).astype(jnp.int32), axis=-1),
                              N_EXPERTS - 1)
    srcoff_f = jnp.zeros((c.shape[0], 1, LANES), F32).at[:, 0, :N_EXPERTS].set(srcoff.astype(F32))
    return {
        "srcoff": srcoff.reshape(-1), "nchunk": (c16 // ROW_ALIGN).reshape(-1), "dst": dst.reshape(-1),
        "ntot": jnp.sum(c16 // ROW_ALIGN, axis=1), "zpos": jnp.concatenate([offs + rows, ends[-1:]]),
        "srcoff_f": srcoff_f,
        "tile_expert": tile_expert, "n_valid": (ends[-1] // tr).astype(jnp.int32).reshape(1),
    }


def _reorder_in_cols(w_in):
    cuts = 4 * RET_W + 2 * CONV_W + 3 * ATT_W
    return jnp.concatenate([w_in[..., cuts:], w_in[..., :cuts]], axis=-1)


def kernel(x, c, positions, w_ada, b_ada, g_mix, g_ffn, w_in, b_gate, ret_gn, conv_w, conv_b, conv_ln_g, conv_ln_b, att_q_gain, att_k_gain, att_rel_bias, w_ret_out, w_conv_out, w_att_out, w_out, w_group, b_group, w_inner, b_inner, w_up, w_down):
    B, S, D = x.shape
    L = w_ada.shape[0]
    T = B * S
    assert D == D_MODEL and S % 1024 == 0 and T % 2048 == 0
    n_rows = _moe_rows_bound(T)

    mod_all = _adaln(c, w_ada, b_ada).reshape(L, B, N_MOD, D)
    cs_tab = _rope_table(positions)
    x2 = x.reshape(T, D)
    h = _prenorm(x2, mod_all[0], g_mix[0], S)
    for l in range(L):
        mod = mod_all[l]
        proj = _inproj(h, _reorder_in_cols(w_in[l]).astype(BF16))
        ret = _retention(proj, cs_tab, ret_gn[l], B, S)
        conv = _conv(proj, conv_w[l], conv_b[l], conv_ln_g[l], conv_ln_b[l], B, S)
        att = _attention(proj, att_rel_bias[l], att_q_gain[l], att_k_gain[l], B, S)
        w_rt = jnp.zeros((D, LANES), F32)
        w_rt = w_rt.at[:, :N_EXPERTS].set(w_inner[l]).at[:, N_EXPERTS:N_EXPERTS + N_GROUPS].set(w_group[l])
        b_rt = jnp.zeros((1, LANES), F32)
        b_rt = b_rt.at[0, :N_EXPERTS].set(b_inner[l]).at[0, N_EXPERTS:N_EXPERTS + N_GROUPS].set(b_group[l])
        w_rt_hi = w_rt.astype(BF16)
        w_rt2 = jnp.concatenate([w_rt_hi, (w_rt - w_rt_hi.astype(F32)).astype(BF16)], axis=1)
        x2, h2, route, counts = _merge(x2, proj, ret, conv, att, b_gate[l], mod,
                                       w_ret_out[l].astype(BF16), w_conv_out[l].astype(BF16),
                                       w_att_out[l].astype(BF16), w_out[l].astype(BF16),
                                       g_ffn[l], w_rt2, b_rt, S)
        tabs = _routing_tables(counts, n_rows)
        xs = _dispatch(h2, route, tabs, n_rows)
        ye = _ffn(xs, tabs["tile_expert"], tabs["n_valid"],
                  w_up.reshape(L * N_EXPERTS, D, 2 * EXPERT_HIDDEN),
                  w_down.reshape(L * N_EXPERTS, EXPERT_HIDDEN, D), l, n_rows)
        if l + 1 < L:
            x2, h = _combine(x2, route, mod, ye, tabs, S, mod_all[l + 1], g_mix[l + 1])
        else:
            x2 = _combine(x2, route, mod, ye, tabs, S)
    return x2.reshape(B, S, D)
```

```python
import functools
import math

import jax
import jax.numpy as jnp
from jax import lax
from jax.experimental import pallas as pl
from jax.experimental.pallas import tpu as pltpu

F32 = jnp.float32
BF16 = jnp.bfloat16

D_MODEL = 1024
CHUNK = 64
EPS = 1e-6
NEG_INF = -1e30
RET_HEADS = 4
RET_DK = 128
RET_W = 512
ROPE_BASE = 10000.0
CONV_W = 512
CONV_K = 31
ATT_HEADS = 8
ATT_DH = 64
ATT_W = 512
ATT_LEFT_CHUNKS = 8
MAX_REL = 256
N_BRANCH = 3
IN_COLS = 4 * RET_W + 2 * CONV_W + 3 * ATT_W + N_BRANCH * D_MODEL
N_GROUPS = 4
EXPERTS_PER_GROUP = 8
N_EXPERTS = N_GROUPS * EXPERTS_PER_GROUP
EXPERT_HIDDEN = 256
N_MOD = 6

LANES = 128
V7X_VMEM_LIMIT_BYTES = 56 * 1024 * 1024

COL_GATE = 0
COL_RQ, COL_RK, COL_RV, COL_RG = 6, 7, 8, 9
COL_CA, COL_CB = 10, 11
COL_AQ, COL_AK, COL_AV = 12, 13, 14

RET_BLOCK = 256
ATT_QBLOCK = 256
ATT_BAND = ATT_QBLOCK + ATT_LEFT_CHUNKS * CHUNK
ATT_PAD = ATT_LEFT_CHUNKS * CHUNK
CONV_TILE = 512
CONV_HIST = 32
CONV_ROWS = 64
MOE_TILE = 512
ROUTE_TILE = 512
ROW_ALIGN = 16
SORT_SLOTS = 2 * ROUTE_TILE + N_EXPERTS * ROW_ALIGN
XS_W = D_MODEL + LANES


def _cparams(sem):
    return pltpu.CompilerParams(dimension_semantics=sem, vmem_limit_bytes=V7X_VMEM_LIMIT_BYTES)


def _adaln_kernel(c_ref, w_ref, b_ref, o_ref):
    c = c_ref[...]
    cs = c * jax.nn.sigmoid(c)
    o_ref[0] = jnp.dot(cs, w_ref[0], preferred_element_type=F32,
                       precision=lax.Precision.HIGHEST) + b_ref[0]


def _adaln(c, w_ada, b_ada):
    L, D, N = w_ada.shape
    B = c.shape[0]
    tn = 1536
    return pl.pallas_call(
        _adaln_kernel,
        grid=(L, N // tn),
        in_specs=[pl.BlockSpec((B, D), lambda l, j: (0, 0)),
                  pl.BlockSpec((1, D, tn), lambda l, j: (l, 0, j)),
                  pl.BlockSpec((1, 1, tn), lambda l, j: (l, 0, j))],
        out_specs=pl.BlockSpec((1, B, tn), lambda l, j: (l, 0, j)),
        out_shape=jax.ShapeDtypeStruct((L, B, N), F32),
        compiler_params=_cparams(("arbitrary", "arbitrary")),
        name="adaln",
    )(c, w_ada, b_ada.reshape(L, 1, N))


def _rope_kernel(pos_ref, inv_ref, ph_ref, o_ref):
    o_ref[...] = jnp.cos(pos_ref[...] * inv_ref[...] - ph_ref[...])


def _rope_table(positions):
    T = positions.size
    half = RET_DK // 2
    inv = ROPE_BASE ** (-jnp.arange(half, dtype=F32) / half)
    inv2 = jnp.concatenate([inv, inv]).reshape(1, LANES)
    phase = jnp.concatenate([jnp.zeros((half,), F32), jnp.full((half,), math.pi / 2, F32)]).reshape(1, LANES)
    posb = jnp.broadcast_to(positions.reshape(T, 1).astype(F32), (T, LANES))
    tm = 2048
    return pl.pallas_call(
        _rope_kernel,
        grid=(T // tm,),
        in_specs=[pl.BlockSpec((tm, LANES), lambda i: (i, 0)),
                  pl.BlockSpec((1, LANES), lambda i: (0, 0)),
                  pl.BlockSpec((1, LANES), lambda i: (0, 0))],
        out_specs=pl.BlockSpec((tm, LANES), lambda i: (i, 0)),
        out_shape=jax.ShapeDtypeStruct((T, LANES), F32),
        compiler_params=_cparams(("arbitrary",)),
        name="rope_table",
    )(posb, inv2, phase)


def _modulated_rmsnorm(x, g, mod_ref, shift_row, scale_row):
    y = x * lax.rsqrt(jnp.mean(x * x, axis=-1, keepdims=True) + EPS) * g
    return y * (1.0 + mod_ref[0, scale_row:scale_row + 1, :]) + mod_ref[0, shift_row:shift_row + 1, :]


INPROJ_TM, INPROJ_TN = 2048, 2560


def _inproj_kernel(h_ref, w_ref, o_ref):
    o_ref[...] = jnp.dot(h_ref[...], w_ref[...], preferred_element_type=F32).astype(BF16)


def _inproj(h, w_in_b):
    T, D = h.shape
    N = w_in_b.shape[1]
    tm, tn = INPROJ_TM, INPROJ_TN
    return pl.pallas_call(
        _inproj_kernel,
        grid=(T // tm, N // tn),
        in_specs=[pl.BlockSpec((tm, D), lambda i, j: (i, 0)),
                  pl.BlockSpec((D, tn), lambda i, j: (0, j))],
        out_specs=pl.BlockSpec((tm, tn), lambda i, j: (i, j)),
        out_shape=jax.ShapeDtypeStruct((T, N), BF16),
        compiler_params=_cparams(("arbitrary", "arbitrary")),
        name="inproj",
    )(h, w_in_b)


def _inproj_norm_kernel(x_ref, mod_ref, g_ref, w_ref, o_ref, h_ref):
    @pl.when(pl.program_id(1) == 0)
    def _():
        h_ref[...] = _modulated_rmsnorm(x_ref[...], g_ref[...], mod_ref, 0, 1).astype(BF16)

    o_ref[...] = jnp.dot(h_ref[...], w_ref[...], preferred_element_type=F32).astype(BF16)


def _inproj_norm(x2, mod, g_mix, w_in_b, S):
    T, D = x2.shape
    N = w_in_b.shape[1]
    tm, tn = INPROJ_TM // 2, INPROJ_TN
    return pl.pallas_call(
        _inproj_norm_kernel,
        grid=(T // tm, N // tn),
        in_specs=[pl.BlockSpec((tm, D), lambda i, j: (i, 0)),
                  pl.BlockSpec((1, N_MOD, D), lambda i, j: (i * tm // S, 0, 0)),
                  pl.BlockSpec((1, D), lambda i, j: (0, 0)),
                  pl.BlockSpec((D, tn), lambda i, j: (0, j))],
        out_specs=pl.BlockSpec((tm, tn), lambda i, j: (i, j)),
        out_shape=jax.ShapeDtypeStruct((T, N), BF16),
        scratch_shapes=[pltpu.VMEM((tm, D), BF16)],
        compiler_params=_cparams(("arbitrary", "arbitrary")),
        name="inproj_norm",
    )(x2, mod, g_mix.reshape(1, D), w_in_b)


def _retention_kernel(q_ref, k_ref, v_ref, g_ref, cs_ref, gn_ref, o_ref, state_ref, decay_ref):
    C = RET_BLOCK

    @pl.when(pl.program_id(1) == 0)
    def _():
        state_ref[...] = jnp.zeros_like(state_ref)

    @pl.when((pl.program_id(0) == 0) & (pl.program_id(1) == 0))
    def _():
        d = (lax.broadcasted_iota(jnp.int32, (C, C), 0) - lax.broadcasted_iota(jnp.int32, (C, C), 1)).astype(F32)
        for h in range(RET_HEADS):
            lg = math.log1p(-(2.0 ** (-5.0 - h)))
            decay_ref[h] = jnp.where(d >= 0.0, jnp.exp(lg * jnp.maximum(d, 0.0)), 0.0)

    cs = cs_ref[...]
    csr = pltpu.roll(cs, RET_DK // 2, 1)
    first = lax.broadcasted_iota(jnp.int32, (C, RET_DK), 1) < RET_DK // 2
    cos2 = jnp.where(first, cs, csr)
    sin2 = jnp.where(first, -csr, cs)
    row = lax.broadcasted_iota(jnp.int32, (C, 1), 0).astype(F32)
    for h in range(RET_HEADS):
        lg = math.log1p(-(2.0 ** (-5.0 - h)))
        sl = slice(h * RET_DK, (h + 1) * RET_DK)
        qh = q_ref[:, sl].astype(F32)
        kh = k_ref[:, sl].astype(F32)
        vh = v_ref[:, sl]
        qr = (qh * cos2 + pltpu.roll(qh, RET_DK // 2, 1) * sin2).astype(BF16)
        kr = (kh * cos2 + pltpu.roll(kh, RET_DK // 2, 1) * sin2) * (RET_DK ** -0.5)
        s = lax.dot_general(qr, kr.astype(BF16), (((1,), (1,)), ((), ())),
                            preferred_element_type=F32) * decay_ref[h]
        o = jnp.dot(s.astype(BF16), vh, preferred_element_type=F32)
        st = state_ref[h]
        o = o + jnp.dot(qr, st.astype(BF16), preferred_element_type=F32) * jnp.exp(lg * (row + 1.0))
        kd = (kr * jnp.exp(lg * (C - 1.0 - row))).astype(BF16)
        kv = lax.dot_general(kd, vh, (((0,), (0,)), ((), ())), preferred_element_type=F32)
        state_ref[h] = math.exp(lg * C) * st + kv
        mu = jnp.mean(o, axis=-1, keepdims=True)
        oc = o - mu
        var = jnp.mean(oc * oc, axis=-1, keepdims=True)
        on = oc * lax.rsqrt(var + EPS) * gn_ref[:, sl]
        gh = g_ref[:, sl].astype(F32)
        o_ref[:, sl] = (on * (gh * jax.nn.sigmoid(gh))).astype(BF16)


def _retention(proj, cs_tab, ret_gn, B, S):
    T = B * S
    C = RET_BLOCK
    nblk = S // C

    def col(c):
        return pl.BlockSpec((C, RET_W), lambda b, i, c=c: (b * nblk + i, c))

    return pl.pallas_call(
        _retention_kernel,
        grid=(B, nblk),
        in_specs=[col(COL_RQ), col(COL_RK), col(COL_RV), col(COL_RG),
                  pl.BlockSpec((C, LANES), lambda b, i: (b * nblk + i, 0)),
                  pl.BlockSpec((1, RET_W), lambda b, i: (0, 0))],
        out_specs=pl.BlockSpec((C, RET_W), lambda b, i: (b * nblk + i, 0)),
        out_shape=jax.ShapeDtypeStruct((T, RET_W), BF16),
        scratch_shapes=[pltpu.VMEM((RET_HEADS, RET_DK, RET_DK), F32),
                        pltpu.VMEM((RET_HEADS, C, C), F32)],
        compiler_params=_cparams(("arbitrary", "arbitrary")),
        name="retention",
    )(proj, proj, proj, proj, cs_tab, ret_gn.reshape(1, RET_W))


def _conv_kernel(a_ref, b_ref, w_ref, bias_ref, lg_ref, lb_ref, o_ref, z_ref, zs_ref):
    tm = CONV_TILE
    H = CONV_HIST
    SUB = 8

    @pl.when(pl.program_id(1) == 0)
    def _():
        z_ref[0:H, :] = jnp.zeros((H, CONV_W), F32)

    a = a_ref[...].astype(F32)
    b = b_ref[...].astype(F32)
    z_ref[H:H + tm, :] = a * jax.nn.sigmoid(b)
    base = H - (CONV_K - 1)
    n_shift = tm + (base + CONV_K - 2) // SUB * SUB
    for r in range(1, SUB):
        for c0 in range(0, n_shift, 128):
            cs = min(128, n_shift - c0)
            zs_ref[r - 1, c0:c0 + cs, :] = z_ref[c0 + r:c0 + r + cs, :]
    for r0 in range(0, tm, CONV_ROWS):
        acc = jnp.zeros((CONV_ROWS, CONV_W), F32)
        for j in range(CONV_K):
            al, r = (base + j) // SUB * SUB, (base + j) % SUB
            if r == 0:
                tap = z_ref[r0 + al:r0 + al + CONV_ROWS, :]
            else:
                tap = zs_ref[r - 1, r0 + al:r0 + al + CONV_ROWS, :]
            acc = acc + tap * w_ref[j:j + 1, :]
        acc = acc + bias_ref[...]
        mu = jnp.mean(acc, axis=-1, keepdims=True)
        ac = acc - mu
        var = jnp.mean(ac * ac, axis=-1, keepdims=True)
        y = ac * lax.rsqrt(var + EPS) * lg_ref[...] + lb_ref[...]
        o_ref[r0:r0 + CONV_ROWS, :] = (y * jax.nn.sigmoid(y)).astype(BF16)
    z_ref[0:H, :] = z_ref[tm:tm + H, :]


def _conv(proj, conv_w, conv_b, ln_g, ln_b, B, S):
    T = B * S
    tm = CONV_TILE
    nblk = S // tm
    vec = pl.BlockSpec((1, CONV_W), lambda b, i: (0, 0))
    return pl.pallas_call(
        _conv_kernel,
        grid=(B, nblk),
        in_specs=[pl.BlockSpec((tm, CONV_W), lambda b, i: (b * nblk + i, COL_CA)),
                  pl.BlockSpec((tm, CONV_W), lambda b, i: (b * nblk + i, COL_CB)),
                  pl.BlockSpec((CONV_K, CONV_W), lambda b, i: (0, 0)),
                  vec, vec, vec],
        out_specs=pl.BlockSpec((tm, CONV_W), lambda b, i: (b * nblk + i, 0)),
        out_shape=jax.ShapeDtypeStruct((T, CONV_W), BF16),
        scratch_shapes=[pltpu.VMEM((CONV_HIST + tm, CONV_W), F32),
                        pltpu.VMEM((7, CONV_HIST + tm, CONV_W), F32)],
        compiler_params=_cparams(("arbitrary", "arbitrary")),
        name="conv",
    )(proj, proj, conv_w.reshape(CONV_K, CONV_W), conv_b.reshape(1, CONV_W),
      ln_g.reshape(1, CONV_W), ln_b.reshape(1, CONV_W))


def _head_mean_sq(x, hm_ref):
    sq = x * x
    hi = sq.astype(BF16)
    lo = (sq - hi.astype(F32)).astype(BF16)
    tot = jnp.dot(hi, hm_ref[...], preferred_element_type=F32) + jnp.dot(lo, hm_ref[...], preferred_element_type=F32)
    return tot * (1.0 / ATT_DH)


def _attn_kernel(q_ref, k_ref, v_ref, rrow_ref, qg_ref, kg_ref, hm_ref, o_ref, kn_ref, vt_ref, bmt_ref, *, S):
    tq = ATT_QBLOCK
    qi = pl.program_id(1)
    heads = [slice(h * ATT_DH, (h + 1) * ATT_DH) for h in range(ATT_HEADS)]
    nkb = ATT_BAND // tq

    @pl.when((qi == 0) & (pl.program_id(0) == 0))
    def _():
        r = lax.broadcasted_iota(jnp.int32, (tq, ATT_BAND), 0)
        u = lax.broadcasted_iota(jnp.int32, (tq, ATT_BAND), 1)
        off = u - (r // CHUNK) * CHUNK
        inband = (off >= 0) & (off < (ATT_LEFT_CHUNKS + 1) * CHUNK)
        for h in range(ATT_HEADS):
            gen = jnp.broadcast_to(rrow_ref[h], (tq, rrow_ref.shape[-1]))
            t = pltpu.roll(gen, 0, 1, stride=1, stride_axis=0)
            bmt_ref[h] = jnp.where(inband, t[:, :ATT_BAND], NEG_INF).T

    @pl.when(qi == 0)
    def _():
        for h in range(ATT_HEADS):
            kn_ref[h, 0:ATT_PAD, :] = jnp.zeros((ATT_PAD, ATT_DH), BF16)
        for j in range(ATT_PAD // tq):
            vt_ref[j] = jnp.zeros((ATT_W, tq), BF16)
        rows = 512
        for r0 in range(0, S, rows):
            k = k_ref[r0:r0 + rows, :].astype(F32)
            kn = (k * lax.rsqrt(_head_mean_sq(k, hm_ref) + EPS) * kg_ref[...]).astype(BF16)
            for h in range(ATT_HEADS):
                kn_ref[h, ATT_PAD + r0:ATT_PAD + r0 + rows, :] = kn[:, heads[h]]
        for r0 in range(0, S, tq):
            vt_ref[(ATT_PAD + r0) // tq] = v_ref[r0:r0 + tq, :].astype(F32).T.astype(BF16)

    start = pl.multiple_of(qi * tq, tq)
    q = q_ref[...].astype(F32)
    qn = (q * lax.rsqrt(_head_mean_sq(q, hm_ref) + EPS) * qg_ref[...]).astype(BF16)

    def attend(mask_pad):
        if mask_pad:
            key_pos = lax.broadcasted_iota(jnp.int32, (ATT_BAND, LANES), 0) + start
            valid = key_pos >= ATT_PAD
        def scores(h):
            kb = kn_ref[h, pl.ds(start, ATT_BAND), :]
            return lax.dot_general(kb, qn[:, heads[h]], (((1,), (1,)), ((), ())), preferred_element_type=F32)

        def softmax(h, st):
            ps, ls = [], []
            for c0 in range(0, tq, LANES):
                ss = st[:, c0:c0 + LANES] + bmt_ref[h, :, c0:c0 + LANES]
                if mask_pad:
                    ss = jnp.where(valid, ss, NEG_INF)
                m = jnp.max(ss, axis=0, keepdims=True)
                p = jnp.exp2(ss - m)
                ls.append(jnp.sum(p, axis=0, keepdims=True))
                ps.append(p.astype(BF16))
            return jnp.concatenate(ps, axis=1), jnp.concatenate(ls, axis=1)

        def values(h, pt, l):
            ot = jnp.zeros((ATT_DH, tq), F32)
            for t in range(nkb):
                ot = ot + jnp.dot(vt_ref[qi + t, heads[h], :], pt[t * tq:(t + 1) * tq, :],
                                  preferred_element_type=F32)
            return ot / l

        sts, pls, outs = {}, {}, []
        for step in range(ATT_HEADS + 2):
            if step < ATT_HEADS:
                sts[step] = scores(step)
            if 0 <= step - 1 < ATT_HEADS:
                pls[step - 1] = softmax(step - 1, sts.pop(step - 1))
            if 0 <= step - 2 < ATT_HEADS:
                outs.append(values(step - 2, *pls.pop(step - 2)))
        o_ref[...] = jnp.concatenate(outs, axis=0).T.astype(BF16)

    first = ATT_PAD // tq

    @pl.when(qi < first)
    def _():
        attend(True)

    @pl.when(qi >= first)
    def _():
        attend(False)


def _attn_bias_row(rel_table):
    t = rel_table.astype(F32) * math.log2(math.e)
    edge = jnp.broadcast_to(t[:, 2 * MAX_REL:], (ATT_HEADS, ATT_QBLOCK))
    row = jnp.concatenate([edge, jnp.flip(t[:, 1:], axis=1), edge], axis=1)
    return row.reshape(ATT_HEADS, 1, 2 * ATT_QBLOCK + 2 * MAX_REL)


def _attention(proj, rel_table, q_gain, k_gain, B, S):
    T = B * S
    tq = ATT_QBLOCK
    nblk = S // tq
    gen = _attn_bias_row(rel_table)
    qg = jnp.tile(q_gain.astype(F32) * (ATT_DH ** -0.5 * math.log2(math.e)), ATT_HEADS).reshape(1, ATT_W)
    kg = jnp.tile(k_gain.astype(F32), ATT_HEADS).reshape(1, ATT_W)
    lane_head = jnp.arange(ATT_W) // ATT_DH
    hm = (lane_head[:, None] == lane_head[None, :]).astype(BF16)
    vec = pl.BlockSpec((1, ATT_W), lambda b, i: (0, 0))
    return pl.pallas_call(
        functools.partial(_attn_kernel, S=S),
        grid=(B, nblk),
        in_specs=[pl.BlockSpec((tq, ATT_W), lambda b, i: (b * nblk + i, COL_AQ)),
                  pl.BlockSpec((S, ATT_W), lambda b, i: (b, COL_AK)),
                  pl.BlockSpec((S, ATT_W), lambda b, i: (b, COL_AV)),
                  pl.BlockSpec(gen.shape, lambda b, i: (0, 0, 0)),
                  vec, vec,
                  pl.BlockSpec((ATT_W, ATT_W), lambda b, i: (0, 0))],
        out_specs=pl.BlockSpec((tq, ATT_W), lambda b, i: (b * nblk + i, 0)),
        out_shape=jax.ShapeDtypeStruct((T, ATT_W), BF16),
        scratch_shapes=[pltpu.VMEM((ATT_HEADS, ATT_PAD + S, ATT_DH), BF16),
                        pltpu.VMEM(((ATT_PAD + S) // tq, ATT_W, tq), BF16),
                        pltpu.VMEM((ATT_HEADS, ATT_BAND, tq), F32)],
        compiler_params=_cparams(("arbitrary", "arbitrary")),
        name="chunk_attention",
    )(proj, proj, proj, gen, qg, kg, hm)


def _merge_kernel(x_ref, ret_ref, conv_ref, att_ref, g0_ref, g1_ref, g2_ref, bg_ref, mod_ref,
                  wr_ref, wc_ref, wa_ref, wo_ref, gf_ref, wrt_ref, brt_ref,
                  xo_ref, h2_ref, route_ref, cnt_out_ref, tri_ref):
    D = D_MODEL

    @pl.when(pl.program_id(0) == 0)
    def _():
        n = tri_ref.shape[0]
        tri_ref[...] = (lax.broadcasted_iota(jnp.int32, (n, n), 1)
                        < lax.broadcasted_iota(jnp.int32, (n, n), 0)).astype(BF16)

    def gated(g_ref, k, b_ref, w_ref):
        gate = jax.nn.sigmoid(g_ref[...].astype(F32) + bg_ref[:, k * D:(k + 1) * D])
        return gate * jnp.dot(b_ref[...], w_ref[...], preferred_element_type=F32)

    merged = gated(g0_ref, 0, ret_ref, wr_ref) + gated(g1_ref, 1, conv_ref, wc_ref) + gated(g2_ref, 2, att_ref, wa_ref)
    y = jnp.dot(merged.astype(BF16), wo_ref[...], preferred_element_type=F32)
    x = x_ref[...] + mod_ref[0, 2:3, :] * y
    xo_ref[...] = x

    h2 = _modulated_rmsnorm(x, gf_ref[...], mod_ref, 3, 4)
    h_hi = h2.astype(BF16)
    h2_ref[...] = h_hi
    h_lo = (h2 - h_hi.astype(F32)).astype(BF16)
    both = jnp.dot(h_hi, wrt_ref[...], preferred_element_type=F32)
    logits = (both[:, :LANES] + both[:, LANES:]
              + jnp.dot(h_lo, wrt_ref[:, :LANES], preferred_element_type=F32) + brt_ref[...])
    tm = logits.shape[0]
    lane = lax.broadcasted_iota(jnp.int32, (tm, LANES), 1)
    big = jnp.int32(LANES)
    is_g = (lane >= N_EXPERTS) & (lane < N_EXPERTS + N_GROUPS)
    gl = jnp.where(is_g, logits, -jnp.inf)
    gmax = jnp.max(gl, axis=-1, keepdims=True)
    gsel = jnp.min(jnp.where(gl == gmax, lane - N_EXPERTS, big), axis=-1, keepdims=True)
    p_group = 1.0 / jnp.sum(jnp.where(is_g, jnp.exp(gl - gmax), 0.0), axis=-1, keepdims=True)
    in_grp = (lane >= gsel * EXPERTS_PER_GROUP) & (lane < (gsel + 1) * EXPERTS_PER_GROUP)
    ch = jnp.where(in_grp, logits, -jnp.inf)
    v1 = jnp.max(ch, axis=-1, keepdims=True)
    i1 = jnp.min(jnp.where(ch == v1, lane, big), axis=-1, keepdims=True)
    ch2 = jnp.where(lane == i1, -jnp.inf, ch)
    v2 = jnp.max(ch2, axis=-1, keepdims=True)
    i2 = jnp.min(jnp.where(ch2 == v2, lane, big), axis=-1, keepdims=True)
    e = jnp.exp(v2 - v1)
    w1 = p_group / (1.0 + e)
    w2 = p_group * e / (1.0 + e)
    sel = ((lane == i1) | (lane == i2)).astype(BF16)
    before = jnp.dot(tri_ref[...], sel, preferred_element_type=F32)
    rank1 = jnp.sum(jnp.where(lane == i1, before, 0.0), axis=-1, keepdims=True)
    rank2 = jnp.sum(jnp.where(lane == i2, before, 0.0), axis=-1, keepdims=True)
    cnt_out_ref[0] = jnp.sum(sel.astype(F32), axis=0, keepdims=True)
    vals = (i1.astype(F32), i2.astype(F32), w1, w2, rank1, rank2)
    route = jnp.zeros((tm, LANES), F32)
    for k, v in enumerate(vals):
        route = jnp.where(lane == k, v, route)
    route_ref[...] = route


def _merge(x2, proj, ret, conv, att, b_gate, mod, wr, wc, wa, wo, g_ffn, w_rt, b_rt, S):
    T, D = x2.shape
    tm = ROUTE_TILE
    row = lambda i: (i, 0)
    const = lambda i: (0, 0)
    br = pl.BlockSpec((tm, RET_W), row)
    return pl.pallas_call(
        _merge_kernel,
        grid=(T // tm,),
        in_specs=[pl.BlockSpec((tm, D), row), br, br, br,
                  pl.BlockSpec((tm, D), lambda i: (i, COL_GATE)),
                  pl.BlockSpec((tm, D), lambda i: (i, COL_GATE + 1)),
                  pl.BlockSpec((tm, D), lambda i: (i, COL_GATE + 2)),
                  pl.BlockSpec((1, N_BRANCH * D), const),
                  pl.BlockSpec((1, N_MOD, D), lambda i: (i * tm // S, 0, 0)),
                  pl.BlockSpec((RET_W, D), const), pl.BlockSpec((CONV_W, D), const),
                  pl.BlockSpec((ATT_W, D), const), pl.BlockSpec((D, D), const),
                  pl.BlockSpec((1, D), const),
                  pl.BlockSpec((D, 2 * LANES), const), pl.BlockSpec((1, LANES), const)],
        out_specs=[pl.BlockSpec((tm, D), row), pl.BlockSpec((tm, D), row),
                   pl.BlockSpec((tm, LANES), row), pl.BlockSpec((1, 1, LANES), lambda i: (i, 0, 0))],
        out_shape=[jax.ShapeDtypeStruct((T, D), F32), jax.ShapeDtypeStruct((T, D), BF16),
                   jax.ShapeDtypeStruct((T, LANES), F32), jax.ShapeDtypeStruct((T // tm, 1, LANES), F32)],
        scratch_shapes=[pltpu.VMEM((tm, tm), BF16)],
        compiler_params=_cparams(("arbitrary",)),
        name="merge_router",
    )(x2, ret, conv, att, proj, proj, proj, b_gate.reshape(1, N_BRANCH * D), mod,
      wr, wc, wa, wo, g_ffn.reshape(1, D), w_rt, b_rt)


def _pair_slots(route, srcoff_row):
    tm = route.shape[0]
    lane = lax.broadcasted_iota(jnp.int32, (tm, LANES), 1).astype(F32)
    s0 = jnp.sum(jnp.where(lane == route[:, 0:1], srcoff_row, 0.0), axis=-1, keepdims=True) + route[:, 4:5]
    s1 = jnp.sum(jnp.where(lane == route[:, 1:2], srcoff_row, 0.0), axis=-1, keepdims=True) + route[:, 5:6]
    return s0, s1


def _chunk_copies(tile, so_ref, nch_ref, dst_ref, make_copy):
    base = tile * N_EXPERTS
    for e in range(N_EXPERTS):
        so = so_ref[base + e]
        d = dst_ref[base + e]

        def body(c, carry, so=so, d=d):
            make_copy(pl.multiple_of(so + c * ROW_ALIGN, ROW_ALIGN),
                      pl.multiple_of(d + c * ROW_ALIGN, ROW_ALIGN)).start()
            return carry

        lax.fori_loop(0, nch_ref[base + e], body, 0)


def _wait_chunks(n, make_wait):
    bit = 1
    while bit * ROW_ALIGN <= SORT_SLOTS:
        @pl.when((n & bit) != 0)
        def _(bit=bit):
            make_wait(bit * ROW_ALIGN).wait()

        bit *= 2


def _dispatch_kernel(so_ref, nch_ref, dst_ref, ntot_ref, zpos_ref, h_ref, route_ref, sof_ref,
                     xs_hbm, sorted_ref, zero_ref, sem, zsem, *, nsteps):
    i = pl.program_id(0)

    @pl.when(i == 0)
    def _():
        zero_ref[...] = jnp.zeros_like(zero_ref)
        for e in range(N_EXPERTS):
            pltpu.make_async_copy(zero_ref, xs_hbm.at[pl.ds(pl.multiple_of(zpos_ref[e], ROW_ALIGN), MOE_TILE), :],
                                  zsem).start()
        for e in range(N_EXPERTS):
            pltpu.make_async_copy(zero_ref, xs_hbm.at[pl.ds(0, MOE_TILE), :], zsem).wait()
        tail = zpos_ref[N_EXPERTS]
        n_tail = (xs_hbm.shape[0] - tail) // MOE_TILE

        def zero_tail(c, carry):
            cp = pltpu.make_async_copy(
                zero_ref, xs_hbm.at[pl.ds(pl.multiple_of(tail + c * MOE_TILE, MOE_TILE), MOE_TILE), :], zsem)
            cp.start()
            cp.wait()
            return carry

        lax.fori_loop(0, n_tail, zero_tail, 0)

    r = route_ref[...]
    tm = r.shape[0]
    lane = lax.broadcasted_iota(jnp.int32, (tm, LANES), 1)
    s0, s1 = _pair_slots(r, sof_ref[0])
    slots = lax.broadcasted_iota(jnp.int32, (tm, SORT_SLOTS), 1).astype(F32)
    onehot = ((slots == s0) | (slots == s1)).astype(BF16)

    def split(w):
        hi = w.astype(BF16).astype(F32)
        return hi, (w - hi).astype(BF16).astype(F32)

    extra = jnp.zeros((tm, LANES), F32)
    for k, v in enumerate((r[:, 0:1], *split(r[:, 2:3]), *split(r[:, 3:4]))):
        extra = jnp.where(lane == k, v, extra)
    hx = jnp.concatenate([h_ref[...], extra.astype(BF16)], axis=1)
    slot = i % 2
    sorted_ref[slot] = lax.dot_general(onehot, hx, (((0,), (0,)), ((), ())),
                                       preferred_element_type=F32).astype(BF16)

    def copies(s):
        def make_copy(src_row, dst_row):
            return pltpu.make_async_copy(sorted_ref.at[s, pl.ds(src_row, ROW_ALIGN), :],
                                         xs_hbm.at[pl.ds(dst_row, ROW_ALIGN), :], sem.at[s])
        return make_copy

    _chunk_copies(i, so_ref, nch_ref, dst_ref, copies(slot))

    def wait_all(tile, s):
        _wait_chunks(ntot_ref[tile], lambda rows: pltpu.make_async_copy(
            sorted_ref.at[s, pl.ds(0, rows), :], xs_hbm.at[pl.ds(0, rows), :], sem.at[s]))

    @pl.when(i > 0)
    def _():
        wait_all(i - 1, 1 - slot)

    @pl.when(i == nsteps - 1)
    def _():
        wait_all(i, slot)


def _dispatch(h2, route, tabs, n_rows):
    T, D = h2.shape
    tm = ROUTE_TILE
    grid_spec = pltpu.PrefetchScalarGridSpec(
        num_scalar_prefetch=5,
        grid=(T // tm,),
        in_specs=[pl.BlockSpec((tm, D), lambda i, *_: (i, 0)),
                  pl.BlockSpec((tm, LANES), lambda i, *_: (i, 0)),
                  pl.BlockSpec((1, 1, LANES), lambda i, *_: (i, 0, 0))],
        out_specs=pl.BlockSpec(memory_space=pl.ANY),
        scratch_shapes=[pltpu.VMEM((2, SORT_SLOTS, XS_W), BF16), pltpu.VMEM((MOE_TILE, XS_W), BF16),
                        pltpu.SemaphoreType.DMA((2,)), pltpu.SemaphoreType.DMA(())],
    )
    return pl.pallas_call(
        functools.partial(_dispatch_kernel, nsteps=T // tm),
        grid_spec=grid_spec,
        out_shape=jax.ShapeDtypeStruct((n_rows + MOE_TILE, XS_W), BF16),
        compiler_params=_cparams(("arbitrary",)),
        name="moe_dispatch",
    )(tabs["srcoff"], tabs["nchunk"], tabs["dst"], tabs["ntot"], tabs["zpos"], h2, route, tabs["srcoff_f"])


def _ffn_kernel(te_ref, nv_ref, xs_ref, wu_ref, wd_ref, o_ref, wub_ref, wdb_ref):
    i = pl.program_id(0)

    @pl.when((i == 0) | (te_ref[i] != te_ref[jnp.maximum(i - 1, 0)]))
    def _():
        wub_ref[...] = wu_ref[0].astype(BF16)
        wdb_ref[...] = wd_ref[0].astype(BF16)

    @pl.when(i < nv_ref[0])
    def _():
        x = xs_ref[:, :D_MODEL]
        wv = xs_ref[:, D_MODEL:].astype(F32)
        mine = wv[:, 0:1] == te_ref[i].astype(F32)
        w = jnp.where(mine, wv[:, 1:2] + wv[:, 2:3], wv[:, 3:4] + wv[:, 4:5])
        hid = jnp.dot(x, wub_ref[...], preferred_element_type=F32)
        a = hid[:, :EXPERT_HIDDEN]
        b = hid[:, EXPERT_HIDDEN:]
        act = a * jax.nn.sigmoid(a) * b * w
        o_ref[...] = jnp.dot(act.astype(BF16), wdb_ref[...], preferred_element_type=F32).astype(BF16)

    @pl.when(i >= nv_ref[0])
    def _():
        o_ref[...] = jnp.zeros_like(o_ref)


def _ffn(xs, tile_expert, n_valid, w_up, w_down, layer, n_rows):
    tr = MOE_TILE
    D = D_MODEL
    F2 = w_up.shape[-1]
    first = layer * N_EXPERTS
    grid_spec = pltpu.PrefetchScalarGridSpec(
        num_scalar_prefetch=2,
        grid=(n_rows // tr,),
        in_specs=[pl.BlockSpec((tr, XS_W), lambda i, te, nv: (i, 0)),
                  pl.BlockSpec((1, D, F2), lambda i, te, nv: (first + te[i], 0, 0)),
                  pl.BlockSpec((1, EXPERT_HIDDEN, D), lambda i, te, nv: (first + te[i], 0, 0))],
        out_specs=pl.BlockSpec((tr, D), lambda i, te, nv: (i, 0)),
        scratch_shapes=[pltpu.VMEM((D, F2), BF16), pltpu.VMEM((EXPERT_HIDDEN, D), BF16)],
    )
    return pl.pallas_call(
        _ffn_kernel,
        grid_spec=grid_spec,
        out_shape=jax.ShapeDtypeStruct((n_rows, D), BF16),
        compiler_params=_cparams(("arbitrary",)),
        name="moe_experts",
    )(tile_expert, n_valid, xs, w_up, w_down)


def _combine_kernel(so_ref, nch_ref, dst_ref, ntot_ref, x_ref, route_ref, sof_ref, mod_ref, ye_hbm, *rest,
                    nsteps, emit_next):
    if emit_next:
        modn_ref, gn_ref, o_ref, hn_ref, ys_ref, sem = rest
    else:
        o_ref, ys_ref, sem = rest
    i = pl.program_id(0)

    def copies(slot):
        def make_copy(src_row, dst_row):
            return pltpu.make_async_copy(ye_hbm.at[pl.ds(dst_row, ROW_ALIGN), :],
                                         ys_ref.at[slot, pl.ds(src_row, ROW_ALIGN), :], sem.at[slot])
        return make_copy

    @pl.when(i == 0)
    def _():
        ys_ref[...] = jnp.zeros_like(ys_ref)
        _chunk_copies(0, so_ref, nch_ref, dst_ref, copies(0))

    @pl.when(i + 1 < nsteps)
    def _():
        _chunk_copies(i + 1, so_ref, nch_ref, dst_ref, copies((i + 1) % 2))

    slot = i % 2
    _wait_chunks(ntot_ref[i], lambda rows: pltpu.make_async_copy(
        ye_hbm.at[pl.ds(0, rows), :], ys_ref.at[slot, pl.ds(0, rows), :], sem.at[slot]))
    s0, s1 = _pair_slots(route_ref[...], sof_ref[0])
    slots = lax.broadcasted_iota(jnp.int32, (s0.shape[0], SORT_SLOTS), 1).astype(F32)
    onehot = ((slots == s0) | (slots == s1)).astype(BF16)
    y = jnp.dot(onehot, ys_ref[slot], preferred_element_type=F32)
    x = x_ref[...] + mod_ref[0, 5:6, :] * y
    o_ref[...] = x
    if emit_next:
        hn_ref[...] = _modulated_rmsnorm(x, gn_ref[...], modn_ref, 0, 1).astype(BF16)


def _combine(x2, route, mod, ye, tabs, S, next_mod=None, next_gain=None):
    T, D = x2.shape
    tm = ROUTE_TILE
    nsteps = T // tm
    emit_next = next_mod is not None
    row = pl.BlockSpec((tm, D), lambda i, *_: (i, 0))
    modspec = pl.BlockSpec((1, N_MOD, D), lambda i, *_: (i * tm // S, 0, 0))
    in_specs = [row,
                pl.BlockSpec((tm, LANES), lambda i, *_: (i, 0)),
                pl.BlockSpec((1, 1, LANES), lambda i, *_: (i, 0, 0)),
                modspec,
                pl.BlockSpec(memory_space=pl.ANY)]
    args = [x2, route, tabs["srcoff_f"], mod, ye]
    out_specs, out_shape = row, jax.ShapeDtypeStruct((T, D), F32)
    if emit_next:
        in_specs += [modspec, pl.BlockSpec((1, D), lambda i, *_: (0, 0))]
        args += [next_mod, next_gain.reshape(1, D)]
        out_specs, out_shape = [row, row], [out_shape, jax.ShapeDtypeStruct((T, D), BF16)]
    grid_spec = pltpu.PrefetchScalarGridSpec(
        num_scalar_prefetch=4,
        grid=(nsteps,),
        in_specs=in_specs,
        out_specs=out_specs,
        scratch_shapes=[pltpu.VMEM((2, SORT_SLOTS, D), BF16), pltpu.SemaphoreType.DMA((2,))],
    )
    return pl.pallas_call(
        functools.partial(_combine_kernel, nsteps=nsteps, emit_next=emit_next),
        grid_spec=grid_spec,
        out_shape=out_shape,
        compiler_params=_cparams(("arbitrary",)),
        name="moe_combine",
    )(tabs["srcoff"], tabs["nchunk"], tabs["dst"], tabs["ntot"], *args)


def _moe_rows_bound(T):
    nt = T // ROUTE_TILE
    rows = 2 * T + nt * N_EXPERTS * (ROW_ALIGN - 1) + N_EXPERTS * (MOE_TILE - 1)
    return -(-rows // MOE_TILE) * MOE_TILE


def _routing_tables(counts, n_rows):
    tr = MOE_TILE
    c = counts[:, 0, :N_EXPERTS].astype(jnp.int32)
    c16 = (c + ROW_ALIGN - 1) // ROW_ALIGN * ROW_ALIGN
    srcoff = jnp.cumsum(c16, axis=1) - c16
    rows = jnp.sum(c16, axis=0)
    ends = jnp.cumsum((rows + tr - 1) // tr * tr)
    offs = ends - (rows + tr - 1) // tr * tr
    dst = offs[None, :] + jnp.cumsum(c16, axis=0) - c16
    tile_start = jnp.arange(n_rows // tr, dtype=jnp.int32) * tr
    tile_expert = jnp.minimum(jnp.sum((tile_start[:, None] >= ends[None, :]).astype(jnp.int32), axis=-1),
                              N_EXPERTS - 1)
    srcoff_f = jnp.zeros((c.shape[0], 1, LANES), F32).at[:, 0, :N_EXPERTS].set(srcoff.astype(F32))
    return {
        "srcoff": srcoff.reshape(-1), "nchunk": (c16 // ROW_ALIGN).reshape(-1), "dst": dst.reshape(-1),
        "ntot": jnp.sum(c16 // ROW_ALIGN, axis=1), "zpos": jnp.concatenate([offs + rows, ends[-1:]]),
        "srcoff_f": srcoff_f,
        "tile_expert": tile_expert, "n_valid": (ends[-1] // tr).astype(jnp.int32).reshape(1),
    }


def _reorder_in_cols(w_in):
    cuts = 4 * RET_W + 2 * CONV_W + 3 * ATT_W
    return jnp.concatenate([w_in[..., cuts:], w_in[..., :cuts]], axis=-1)


def kernel(x, c, positions, w_ada, b_ada, g_mix, g_ffn, w_in, b_gate, ret_gn, conv_w, conv_b, conv_ln_g, conv_ln_b, att_q_gain, att_k_gain, att_rel_bias, w_ret_out, w_conv_out, w_att_out, w_out, w_group, b_group, w_inner, b_inner, w_up, w_down):
    B, S, D = x.shape
    L = w_ada.shape[0]
    T = B * S
    assert D == D_MODEL and S % INPROJ_TM == 0 and T % 2048 == 0
    n_rows = _moe_rows_bound(T)

    mod_all = _adaln(c, w_ada, b_ada).reshape(L, B, N_MOD, D)
    cs_tab = _rope_table(positions)
    x2 = x.reshape(T, D)
    h = None
    for l in range(L):
        mod = mod_all[l]
        w_in_b = _reorder_in_cols(w_in[l]).astype(BF16)
        proj = _inproj_norm(x2, mod, g_mix[l], w_in_b, S) if h is None else _inproj(h, w_in_b)
        ret = _retention(proj, cs_tab, ret_gn[l], B, S)
        conv = _conv(proj, conv_w[l], conv_b[l], conv_ln_g[l], conv_ln_b[l], B, S)
        att = _attention(proj, att_rel_bias[l], att_q_gain[l], att_k_gain[l], B, S)
        pad = LANES - N_EXPERTS - N_GROUPS
        w_rt = jnp.concatenate([w_inner[l], w_group[l], jnp.zeros((D, pad), F32)], axis=1)
        b_rt = jnp.concatenate([b_inner[l], b_group[l], jnp.zeros((pad,), F32)]).reshape(1, LANES)
        w_rt_hi = w_rt.astype(BF16)
        w_rt2 = jnp.concatenate([w_rt_hi, (w_rt - w_rt_hi.astype(F32)).astype(BF16)], axis=1)
        x2, h2, route, counts = _merge(x2, proj, ret, conv, att, b_gate[l], mod,
                                       w_ret_out[l].astype(BF16), w_conv_out[l].astype(BF16),
                                       w_att_out[l].astype(BF16), w_out[l].astype(BF16),
                                       g_ffn[l], w_rt2, b_rt, S)
        tabs = _routing_tables(counts, n_rows)
        xs = _dispatch(h2, route, tabs, n_rows)
        ye = _ffn(xs, tabs["tile_expert"], tabs["n_valid"],
                  w_up.reshape(L * N_EXPERTS, D, 2 * EXPERT_HIDDEN),
                  w_down.reshape(L * N_EXPERTS, EXPERT_HIDDEN, D), l, n_rows)
        if l + 1 < L:
            x2, h = _combine(x2, route, mod, ye, tabs, S, mod_all[l + 1], g_mix[l + 1])
        else:
            x2 = _combine(x2, route, mod, ye, tabs, S)
    return x2.reshape(B, S, D)
```

```python
import functools
import math

import jax
import jax.numpy as jnp
from jax import lax
from jax.experimental import pallas as pl
from jax.experimental.pallas import tpu as pltpu

F32 = jnp.float32
BF16 = jnp.bfloat16

D_MODEL = 1024
CHUNK = 64
EPS = 1e-6
NEG_INF = -1e30
RET_HEADS = 4
RET_DK = 128
RET_W = 512
ROPE_BASE = 10000.0
CONV_W = 512
CONV_K = 31
ATT_HEADS = 8
ATT_DH = 64
ATT_W = 512
ATT_LEFT_CHUNKS = 8
MAX_REL = 256
N_BRANCH = 3
IN_COLS = 4 * RET_W + 2 * CONV_W + 3 * ATT_W + N_BRANCH * D_MODEL
N_GROUPS = 4
EXPERTS_PER_GROUP = 8
N_EXPERTS = N_GROUPS * EXPERTS_PER_GROUP
EXPERT_HIDDEN = 256
N_MOD = 6

LANES = 128
V7X_VMEM_LIMIT_BYTES = 56 * 1024 * 1024

COL_GATE = 0
COL_RQ, COL_RK, COL_RV, COL_RG = 6, 7, 8, 9
COL_CA, COL_CB = 10, 11
COL_AQ, COL_AK, COL_AV = 12, 13, 14

RET_BLOCK = 256
ATT_QBLOCK = 256
ATT_BAND = ATT_QBLOCK + ATT_LEFT_CHUNKS * CHUNK
ATT_PAD = ATT_LEFT_CHUNKS * CHUNK
CONV_TILE = 512
CONV_HIST = 32
CONV_ROWS = 64
MOE_TILE = 512
ROUTE_TILE = 512
ROW_ALIGN = 16
SORT_SLOTS = 2 * ROUTE_TILE + N_EXPERTS * ROW_ALIGN
XS_W = D_MODEL + LANES


def _cparams(sem):
    return pltpu.CompilerParams(dimension_semantics=sem, vmem_limit_bytes=V7X_VMEM_LIMIT_BYTES)


def _adaln_kernel(c_ref, w_ref, b_ref, o_ref):
    c = c_ref[...]
    cs = c * jax.nn.sigmoid(c)
    w = w_ref[0]

    def split(v):
        hi = v.astype(BF16)
        return hi, (v - hi.astype(F32)).astype(BF16)

    (c_hi, c_lo), (w_hi, w_lo) = split(cs), split(w)
    o_ref[0] = (jnp.dot(c_hi, w_hi, preferred_element_type=F32) + jnp.dot(c_lo, w_hi, preferred_element_type=F32)
                + jnp.dot(c_hi, w_lo, preferred_element_type=F32) + b_ref[0])


def _adaln(c, w_ada, b_ada):
    L, D, N = w_ada.shape
    B = c.shape[0]
    tn = 1536
    return pl.pallas_call(
        _adaln_kernel,
        grid=(L, N // tn),
        in_specs=[pl.BlockSpec((B, D), lambda l, j: (0, 0)),
                  pl.BlockSpec((1, D, tn), lambda l, j: (l, 0, j)),
                  pl.BlockSpec((1, 1, tn), lambda l, j: (l, 0, j))],
        out_specs=pl.BlockSpec((1, B, tn), lambda l, j: (l, 0, j)),
        out_shape=jax.ShapeDtypeStruct((L, B, N), F32),
        compiler_params=_cparams(("arbitrary", "arbitrary")),
        name="adaln",
    )(c, w_ada, b_ada.reshape(L, 1, N))


def _rope_kernel(pos_ref, inv_ref, ph_ref, o_ref):
    o_ref[...] = jnp.cos(pos_ref[...] * inv_ref[...] - ph_ref[...])


def _rope_table(positions):
    T = positions.size
    half = RET_DK // 2
    inv = ROPE_BASE ** (-jnp.arange(half, dtype=F32) / half)
    inv2 = jnp.concatenate([inv, inv]).reshape(1, LANES)
    phase = jnp.concatenate([jnp.zeros((half,), F32), jnp.full((half,), math.pi / 2, F32)]).reshape(1, LANES)
    posb = jnp.broadcast_to(positions.reshape(T, 1).astype(F32), (T, LANES))
    tm = 2048
    return pl.pallas_call(
        _rope_kernel,
        grid=(T // tm,),
        in_specs=[pl.BlockSpec((tm, LANES), lambda i: (i, 0)),
                  pl.BlockSpec((1, LANES), lambda i: (0, 0)),
                  pl.BlockSpec((1, LANES), lambda i: (0, 0))],
        out_specs=pl.BlockSpec((tm, LANES), lambda i: (i, 0)),
        out_shape=jax.ShapeDtypeStruct((T, LANES), F32),
        compiler_params=_cparams(("arbitrary",)),
        name="rope_table",
    )(posb, inv2, phase)


def _modulated_rmsnorm(x, g, mod_ref, shift_row, scale_row):
    y = x * lax.rsqrt(jnp.mean(x * x, axis=-1, keepdims=True) + EPS) * g
    return y * (1.0 + mod_ref[0, scale_row:scale_row + 1, :]) + mod_ref[0, shift_row:shift_row + 1, :]


INPROJ_TM, INPROJ_TN = 2048, 2560


def _inproj_kernel(h_ref, w_ref, o_ref):
    o_ref[...] = jnp.dot(h_ref[...], w_ref[...], preferred_element_type=F32).astype(BF16)


def _inproj(h, w_in_b):
    T, D = h.shape
    N = w_in_b.shape[1]
    tm, tn = INPROJ_TM, INPROJ_TN
    return pl.pallas_call(
        _inproj_kernel,
        grid=(T // tm, N // tn),
        in_specs=[pl.BlockSpec((tm, D), lambda i, j: (i, 0)),
                  pl.BlockSpec((D, tn), lambda i, j: (0, j))],
        out_specs=pl.BlockSpec((tm, tn), lambda i, j: (i, j)),
        out_shape=jax.ShapeDtypeStruct((T, N), BF16),
        compiler_params=_cparams(("arbitrary", "arbitrary")),
        name="inproj",
    )(h, w_in_b)


def _inproj_norm_kernel(x_ref, mod_ref, g_ref, w_ref, o_ref, h_ref):
    @pl.when(pl.program_id(1) == 0)
    def _():
        h_ref[...] = _modulated_rmsnorm(x_ref[...], g_ref[...], mod_ref, 0, 1).astype(BF16)

    o_ref[...] = jnp.dot(h_ref[...], w_ref[...], preferred_element_type=F32).astype(BF16)


def _inproj_norm(x2, mod, g_mix, w_in_b, S):
    T, D = x2.shape
    N = w_in_b.shape[1]
    tm, tn = INPROJ_TM // 2, INPROJ_TN
    return pl.pallas_call(
        _inproj_norm_kernel,
        grid=(T // tm, N // tn),
        in_specs=[pl.BlockSpec((tm, D), lambda i, j: (i, 0)),
                  pl.BlockSpec((1, N_MOD, D), lambda i, j: (i * tm // S, 0, 0)),
                  pl.BlockSpec((1, D), lambda i, j: (0, 0)),
                  pl.BlockSpec((D, tn), lambda i, j: (0, j))],
        out_specs=pl.BlockSpec((tm, tn), lambda i, j: (i, j)),
        out_shape=jax.ShapeDtypeStruct((T, N), BF16),
        scratch_shapes=[pltpu.VMEM((tm, D), BF16)],
        compiler_params=_cparams(("arbitrary", "arbitrary")),
        name="inproj_norm",
    )(x2, mod, g_mix.reshape(1, D), w_in_b)


def _retention_kernel(q_ref, k_ref, v_ref, g_ref, cs_ref, gn_ref, o_ref, state_ref, decay_ref):
    C = RET_BLOCK

    @pl.when(pl.program_id(1) == 0)
    def _():
        state_ref[...] = jnp.zeros_like(state_ref)

    @pl.when((pl.program_id(0) == 0) & (pl.program_id(1) == 0))
    def _():
        d = (lax.broadcasted_iota(jnp.int32, (C, C), 0) - lax.broadcasted_iota(jnp.int32, (C, C), 1)).astype(F32)
        for h in range(RET_HEADS):
            lg = math.log1p(-(2.0 ** (-5.0 - h)))
            decay_ref[h] = jnp.where(d >= 0.0, jnp.exp(lg * jnp.maximum(d, 0.0)), 0.0)

    cs = cs_ref[...]
    csr = pltpu.roll(cs, RET_DK // 2, 1)
    first = lax.broadcasted_iota(jnp.int32, (C, RET_DK), 1) < RET_DK // 2
    cos2 = jnp.where(first, cs, csr)
    sin2 = jnp.where(first, -csr, cs)
    row = lax.broadcasted_iota(jnp.int32, (C, 1), 0).astype(F32)
    for h in range(RET_HEADS):
        lg = math.log1p(-(2.0 ** (-5.0 - h)))
        sl = slice(h * RET_DK, (h + 1) * RET_DK)
        qh = q_ref[:, sl].astype(F32)
        kh = k_ref[:, sl].astype(F32)
        vh = v_ref[:, sl]
        qr = (qh * cos2 + pltpu.roll(qh, RET_DK // 2, 1) * sin2).astype(BF16)
        kr = (kh * cos2 + pltpu.roll(kh, RET_DK // 2, 1) * sin2) * (RET_DK ** -0.5)
        s = lax.dot_general(qr, kr.astype(BF16), (((1,), (1,)), ((), ())),
                            preferred_element_type=F32) * decay_ref[h]
        o = jnp.dot(s.astype(BF16), vh, preferred_element_type=F32)
        st = state_ref[h]
        o = o + jnp.dot(qr, st.astype(BF16), preferred_element_type=F32) * jnp.exp(lg * (row + 1.0))
        kd = (kr * jnp.exp(lg * (C - 1.0 - row))).astype(BF16)
        kv = lax.dot_general(kd, vh, (((0,), (0,)), ((), ())), preferred_element_type=F32)
        state_ref[h] = math.exp(lg * C) * st + kv
        mu = jnp.mean(o, axis=-1, keepdims=True)
        oc = o - mu
        var = jnp.mean(oc * oc, axis=-1, keepdims=True)
        on = oc * lax.rsqrt(var + EPS) * gn_ref[:, sl]
        gh = g_ref[:, sl].astype(F32)
        o_ref[:, sl] = (on * (gh * jax.nn.sigmoid(gh))).astype(BF16)


def _retention(proj, cs_tab, ret_gn, B, S):
    T = B * S
    C = RET_BLOCK
    nblk = S // C

    def col(c):
        return pl.BlockSpec((C, RET_W), lambda b, i, c=c: (b * nblk + i, c))

    return pl.pallas_call(
        _retention_kernel,
        grid=(B, nblk),
        in_specs=[col(COL_RQ), col(COL_RK), col(COL_RV), col(COL_RG),
                  pl.BlockSpec((C, LANES), lambda b, i: (b * nblk + i, 0)),
                  pl.BlockSpec((1, RET_W), lambda b, i: (0, 0))],
        out_specs=pl.BlockSpec((C, RET_W), lambda b, i: (b * nblk + i, 0)),
        out_shape=jax.ShapeDtypeStruct((T, RET_W), BF16),
        scratch_shapes=[pltpu.VMEM((RET_HEADS, RET_DK, RET_DK), F32),
                        pltpu.VMEM((RET_HEADS, C, C), F32)],
        compiler_params=_cparams(("arbitrary", "arbitrary")),
        name="retention",
    )(proj, proj, proj, proj, cs_tab, ret_gn.reshape(1, RET_W))


def _conv_kernel(a_ref, b_ref, w_ref, bias_ref, lg_ref, lb_ref, o_ref, z_ref, zs_ref):
    tm = CONV_TILE
    H = CONV_HIST
    SUB = 8

    @pl.when(pl.program_id(1) == 0)
    def _():
        z_ref[0:H, :] = jnp.zeros((H, CONV_W), F32)

    a = a_ref[...].astype(F32)
    b = b_ref[...].astype(F32)
    z_ref[H:H + tm, :] = a * jax.nn.sigmoid(b)
    base = H - (CONV_K - 1)
    n_shift = tm + (base + CONV_K - 2) // SUB * SUB
    for r in range(1, SUB):
        for c0 in range(0, n_shift, 128):
            cs = min(128, n_shift - c0)
            zs_ref[r - 1, c0:c0 + cs, :] = z_ref[c0 + r:c0 + r + cs, :]
    for r0 in range(0, tm, CONV_ROWS):
        acc = jnp.zeros((CONV_ROWS, CONV_W), F32)
        for j in range(CONV_K):
            al, r = (base + j) // SUB * SUB, (base + j) % SUB
            if r == 0:
                tap = z_ref[r0 + al:r0 + al + CONV_ROWS, :]
            else:
                tap = zs_ref[r - 1, r0 + al:r0 + al + CONV_ROWS, :]
            acc = acc + tap * w_ref[j:j + 1, :]
        acc = acc + bias_ref[...]
        mu = jnp.mean(acc, axis=-1, keepdims=True)
        ac = acc - mu
        var = jnp.mean(ac * ac, axis=-1, keepdims=True)
        y = ac * lax.rsqrt(var + EPS) * lg_ref[...] + lb_ref[...]
        o_ref[r0:r0 + CONV_ROWS, :] = (y * jax.nn.sigmoid(y)).astype(BF16)
    z_ref[0:H, :] = z_ref[tm:tm + H, :]


def _conv(proj, conv_w, conv_b, ln_g, ln_b, B, S):
    T = B * S
    tm = CONV_TILE
    nblk = S // tm
    vec = pl.BlockSpec((1, CONV_W), lambda b, i: (0, 0))
    return pl.pallas_call(
        _conv_kernel,
        grid=(B, nblk),
        in_specs=[pl.BlockSpec((tm, CONV_W), lambda b, i: (b * nblk + i, COL_CA)),
                  pl.BlockSpec((tm, CONV_W), lambda b, i: (b * nblk + i, COL_CB)),
                  pl.BlockSpec((CONV_K, CONV_W), lambda b, i: (0, 0)),
                  vec, vec, vec],
        out_specs=pl.BlockSpec((tm, CONV_W), lambda b, i: (b * nblk + i, 0)),
        out_shape=jax.ShapeDtypeStruct((T, CONV_W), BF16),
        scratch_shapes=[pltpu.VMEM((CONV_HIST + tm, CONV_W), F32),
                        pltpu.VMEM((7, CONV_HIST + tm, CONV_W), F32)],
        compiler_params=_cparams(("arbitrary", "arbitrary")),
        name="conv",
    )(proj, proj, conv_w.reshape(CONV_K, CONV_W), conv_b.reshape(1, CONV_W),
      ln_g.reshape(1, CONV_W), ln_b.reshape(1, CONV_W))


def _head_mean_sq(x, hm_ref):
    sq = x * x
    hi = sq.astype(BF16)
    lo = (sq - hi.astype(F32)).astype(BF16)
    tot = jnp.dot(hi, hm_ref[...], preferred_element_type=F32) + jnp.dot(lo, hm_ref[...], preferred_element_type=F32)
    return tot * (1.0 / ATT_DH)


def _attn_kernel(q_ref, k_ref, v_ref, rrow_ref, qg_ref, kg_ref, hm_ref, o_ref, kn_ref, vt_ref, bmt_ref, *, S):
    tq = ATT_QBLOCK
    qi = pl.program_id(1)
    heads = [slice(h * ATT_DH, (h + 1) * ATT_DH) for h in range(ATT_HEADS)]
    nkb = ATT_BAND // tq

    @pl.when((qi == 0) & (pl.program_id(0) == 0))
    def _():
        r = lax.broadcasted_iota(jnp.int32, (tq, ATT_BAND), 0)
        u = lax.broadcasted_iota(jnp.int32, (tq, ATT_BAND), 1)
        off = u - (r // CHUNK) * CHUNK
        inband = (off >= 0) & (off < (ATT_LEFT_CHUNKS + 1) * CHUNK)
        for h in range(ATT_HEADS):
            gen = jnp.broadcast_to(rrow_ref[h], (tq, rrow_ref.shape[-1]))
            t = pltpu.roll(gen, 0, 1, stride=1, stride_axis=0)
            bmt_ref[h] = jnp.where(inband, t[:, :ATT_BAND], NEG_INF).T

    @pl.when(qi == 0)
    def _():
        for h in range(ATT_HEADS):
            kn_ref[h, 0:ATT_PAD, :] = jnp.zeros((ATT_PAD, ATT_DH), BF16)
        for j in range(ATT_PAD // tq):
            vt_ref[j] = jnp.zeros((ATT_W, tq), BF16)
        rows = 512
        for r0 in range(0, S, rows):
            k = k_ref[r0:r0 + rows, :].astype(F32)
            kn = (k * lax.rsqrt(_head_mean_sq(k, hm_ref) + EPS) * kg_ref[...]).astype(BF16)
            for h in range(ATT_HEADS):
                kn_ref[h, ATT_PAD + r0:ATT_PAD + r0 + rows, :] = kn[:, heads[h]]
        for r0 in range(0, S, tq):
            vt_ref[(ATT_PAD + r0) // tq] = v_ref[r0:r0 + tq, :].astype(F32).T.astype(BF16)

    start = pl.multiple_of(qi * tq, tq)
    q = q_ref[...].astype(F32)
    qn = (q * lax.rsqrt(_head_mean_sq(q, hm_ref) + EPS) * qg_ref[...]).astype(BF16)

    def attend(mask_pad):
        if mask_pad:
            key_pos = lax.broadcasted_iota(jnp.int32, (ATT_BAND, LANES), 0) + start
            valid = key_pos >= ATT_PAD
        def scores(h):
            kb = kn_ref[h, pl.ds(start, ATT_BAND), :]
            return lax.dot_general(kb, qn[:, heads[h]], (((1,), (1,)), ((), ())), preferred_element_type=F32)

        def softmax(h, st):
            ps, ls = [], []
            for c0 in range(0, tq, LANES):
                ss = st[:, c0:c0 + LANES] + bmt_ref[h, :, c0:c0 + LANES]
                if mask_pad:
                    ss = jnp.where(valid, ss, NEG_INF)
                m = jnp.max(ss, axis=0, keepdims=True)
                p = jnp.exp2(ss - m)
                ls.append(jnp.sum(p, axis=0, keepdims=True))
                ps.append(p.astype(BF16))
            return jnp.concatenate(ps, axis=1), jnp.concatenate(ls, axis=1)

        def values(h, pt, l):
            ot = jnp.zeros((ATT_DH, tq), F32)
            for t in range(nkb):
                ot = ot + jnp.dot(vt_ref[qi + t, heads[h], :], pt[t * tq:(t + 1) * tq, :],
                                  preferred_element_type=F32)
            return ot / l

        sts, pls, outs = {}, {}, []
        for step in range(ATT_HEADS + 2):
            if step < ATT_HEADS:
                sts[step] = scores(step)
            if 0 <= step - 1 < ATT_HEADS:
                pls[step - 1] = softmax(step - 1, sts.pop(step - 1))
            if 0 <= step - 2 < ATT_HEADS:
                outs.append(values(step - 2, *pls.pop(step - 2)))
        o_ref[...] = jnp.concatenate(outs, axis=0).T.astype(BF16)

    first = ATT_PAD // tq

    @pl.when(qi < first)
    def _():
        attend(True)

    @pl.when(qi >= first)
    def _():
        attend(False)


def _attn_bias_row(rel_table):
    t = rel_table.astype(F32) * math.log2(math.e)
    edge = jnp.broadcast_to(t[:, 2 * MAX_REL:], (ATT_HEADS, ATT_QBLOCK))
    row = jnp.concatenate([edge, jnp.flip(t[:, 1:], axis=1), edge], axis=1)
    return row.reshape(ATT_HEADS, 1, 2 * ATT_QBLOCK + 2 * MAX_REL)


def _attention(proj, rel_table, q_gain, k_gain, B, S):
    T = B * S
    tq = ATT_QBLOCK
    nblk = S // tq
    gen = _attn_bias_row(rel_table)
    qg = jnp.tile(q_gain.astype(F32) * (ATT_DH ** -0.5 * math.log2(math.e)), ATT_HEADS).reshape(1, ATT_W)
    kg = jnp.tile(k_gain.astype(F32), ATT_HEADS).reshape(1, ATT_W)
    lane_head = jnp.arange(ATT_W) // ATT_DH
    hm = (lane_head[:, None] == lane_head[None, :]).astype(BF16)
    vec = pl.BlockSpec((1, ATT_W), lambda b, i: (0, 0))
    return pl.pallas_call(
        functools.partial(_attn_kernel, S=S),
        grid=(B, nblk),
        in_specs=[pl.BlockSpec((tq, ATT_W), lambda b, i: (b * nblk + i, COL_AQ)),
                  pl.BlockSpec((S, ATT_W), lambda b, i: (b, COL_AK)),
                  pl.BlockSpec((S, ATT_W), lambda b, i: (b, COL_AV)),
                  pl.BlockSpec(gen.shape, lambda b, i: (0, 0, 0)),
                  vec, vec,
                  pl.BlockSpec((ATT_W, ATT_W), lambda b, i: (0, 0))],
        out_specs=pl.BlockSpec((tq, ATT_W), lambda b, i: (b * nblk + i, 0)),
        out_shape=jax.ShapeDtypeStruct((T, ATT_W), BF16),
        scratch_shapes=[pltpu.VMEM((ATT_HEADS, ATT_PAD + S, ATT_DH), BF16),
                        pltpu.VMEM(((ATT_PAD + S) // tq, ATT_W, tq), BF16),
                        pltpu.VMEM((ATT_HEADS, ATT_BAND, tq), F32)],
        compiler_params=_cparams(("arbitrary", "arbitrary")),
        name="chunk_attention",
    )(proj, proj, proj, gen, qg, kg, hm)


def _merge_kernel(x_ref, ret_ref, conv_ref, att_ref, g0_ref, g1_ref, g2_ref, bg_ref, mod_ref,
                  wr_ref, wc_ref, wa_ref, wo_ref, gf_ref, wrt_ref, brt_ref,
                  xo_ref, h2_ref, route_ref, cnt_out_ref, tri_ref):
    D = D_MODEL

    @pl.when(pl.program_id(0) == 0)
    def _():
        n = tri_ref.shape[0]
        tri_ref[...] = (lax.broadcasted_iota(jnp.int32, (n, n), 1)
                        < lax.broadcasted_iota(jnp.int32, (n, n), 0)).astype(BF16)

    def gated(g_ref, k, b_ref, w_ref):
        gate = jax.nn.sigmoid(g_ref[...].astype(F32) + bg_ref[:, k * D:(k + 1) * D])
        return gate * jnp.dot(b_ref[...], w_ref[...], preferred_element_type=F32)

    merged = gated(g0_ref, 0, ret_ref, wr_ref) + gated(g1_ref, 1, conv_ref, wc_ref) + gated(g2_ref, 2, att_ref, wa_ref)
    y = jnp.dot(merged.astype(BF16), wo_ref[...], preferred_element_type=F32)
    x = x_ref[...] + mod_ref[0, 2:3, :] * y
    xo_ref[...] = x

    h2 = _modulated_rmsnorm(x, gf_ref[...], mod_ref, 3, 4)
    h_hi = h2.astype(BF16)
    h2_ref[...] = h_hi
    h_lo = (h2 - h_hi.astype(F32)).astype(BF16)
    both = jnp.dot(h_hi, wrt_ref[...], preferred_element_type=F32)
    logits = (both[:, :LANES] + both[:, LANES:]
              + jnp.dot(h_lo, wrt_ref[:, :LANES], preferred_element_type=F32) + brt_ref[...])
    tm = logits.shape[0]
    lane = lax.broadcasted_iota(jnp.int32, (tm, LANES), 1)
    big = jnp.int32(LANES)
    is_g = (lane >= N_EXPERTS) & (lane < N_EXPERTS + N_GROUPS)
    gl = jnp.where(is_g, logits, -jnp.inf)
    gmax = jnp.max(gl, axis=-1, keepdims=True)
    gsel = jnp.min(jnp.where(gl == gmax, lane - N_EXPERTS, big), axis=-1, keepdims=True)
    p_group = 1.0 / jnp.sum(jnp.where(is_g, jnp.exp(gl - gmax), 0.0), axis=-1, keepdims=True)
    in_grp = (lane >= gsel * EXPERTS_PER_GROUP) & (lane < (gsel + 1) * EXPERTS_PER_GROUP)
    ch = jnp.where(in_grp, logits, -jnp.inf)
    v1 = jnp.max(ch, axis=-1, keepdims=True)
    i1 = jnp.min(jnp.where(ch == v1, lane, big), axis=-1, keepdims=True)
    ch2 = jnp.where(lane == i1, -jnp.inf, ch)
    v2 = jnp.max(ch2, axis=-1, keepdims=True)
    i2 = jnp.min(jnp.where(ch2 == v2, lane, big), axis=-1, keepdims=True)
    e = jnp.exp(v2 - v1)
    w1 = p_group / (1.0 + e)
    w2 = p_group * e / (1.0 + e)
    sel = ((lane == i1) | (lane == i2)).astype(BF16)
    before = jnp.dot(tri_ref[...], sel, preferred_element_type=F32)
    rank1 = jnp.sum(jnp.where(lane == i1, before, 0.0), axis=-1, keepdims=True)
    rank2 = jnp.sum(jnp.where(lane == i2, before, 0.0), axis=-1, keepdims=True)
    cnt_out_ref[0] = jnp.sum(sel.astype(F32), axis=0, keepdims=True)
    vals = (i1.astype(F32), i2.astype(F32), w1, w2, rank1, rank2)
    route = jnp.zeros((tm, LANES), F32)
    for k, v in enumerate(vals):
        route = jnp.where(lane == k, v, route)
    route_ref[...] = route


def _merge(x2, proj, ret, conv, att, b_gate, mod, wr, wc, wa, wo, g_ffn, w_rt, b_rt, S):
    T, D = x2.shape
    tm = ROUTE_TILE
    row = lambda i: (i, 0)
    const = lambda i: (0, 0)
    br = pl.BlockSpec((tm, RET_W), row)
    return pl.pallas_call(
        _merge_kernel,
        grid=(T // tm,),
        in_specs=[pl.BlockSpec((tm, D), row), br, br, br,
                  pl.BlockSpec((tm, D), lambda i: (i, COL_GATE)),
                  pl.BlockSpec((tm, D), lambda i: (i, COL_GATE + 1)),
                  pl.BlockSpec((tm, D), lambda i: (i, COL_GATE + 2)),
                  pl.BlockSpec((1, N_BRANCH * D), const),
                  pl.BlockSpec((1, N_MOD, D), lambda i: (i * tm // S, 0, 0)),
                  pl.BlockSpec((RET_W, D), const), pl.BlockSpec((CONV_W, D), const),
                  pl.BlockSpec((ATT_W, D), const), pl.BlockSpec((D, D), const),
                  pl.BlockSpec((1, D), const),
                  pl.BlockSpec((D, 2 * LANES), const), pl.BlockSpec((1, LANES), const)],
        out_specs=[pl.BlockSpec((tm, D), row), pl.BlockSpec((tm, D), row),
                   pl.BlockSpec((tm, LANES), row), pl.BlockSpec((1, 1, LANES), lambda i: (i, 0, 0))],
        out_shape=[jax.ShapeDtypeStruct((T, D), F32), jax.ShapeDtypeStruct((T, D), BF16),
                   jax.ShapeDtypeStruct((T, LANES), F32), jax.ShapeDtypeStruct((T // tm, 1, LANES), F32)],
        scratch_shapes=[pltpu.VMEM((tm, tm), BF16)],
        compiler_params=_cparams(("arbitrary",)),
        name="merge_router",
    )(x2, ret, conv, att, proj, proj, proj, b_gate.reshape(1, N_BRANCH * D), mod,
      wr, wc, wa, wo, g_ffn.reshape(1, D), w_rt, b_rt)


def _pair_slots(route, srcoff_row):
    tm = route.shape[0]
    lane = lax.broadcasted_iota(jnp.int32, (tm, LANES), 1).astype(F32)
    s0 = jnp.sum(jnp.where(lane == route[:, 0:1], srcoff_row, 0.0), axis=-1, keepdims=True) + route[:, 4:5]
    s1 = jnp.sum(jnp.where(lane == route[:, 1:2], srcoff_row, 0.0), axis=-1, keepdims=True) + route[:, 5:6]
    return s0, s1


def _chunk_copies(tile, so_ref, nch_ref, dst_ref, make_copy):
    base = tile * N_EXPERTS
    for e in range(N_EXPERTS):
        so = so_ref[base + e]
        d = dst_ref[base + e]

        def body(c, carry, so=so, d=d):
            make_copy(pl.multiple_of(so + c * ROW_ALIGN, ROW_ALIGN),
                      pl.multiple_of(d + c * ROW_ALIGN, ROW_ALIGN)).start()
            return carry

        lax.fori_loop(0, nch_ref[base + e], body, 0)


def _wait_chunks(n, make_wait):
    bit = 1
    while bit * ROW_ALIGN <= SORT_SLOTS:
        @pl.when((n & bit) != 0)
        def _(bit=bit):
            make_wait(bit * ROW_ALIGN).wait()

        bit *= 2


def _dispatch_kernel(so_ref, nch_ref, dst_ref, ntot_ref, zpos_ref, h_ref, route_ref, sof_ref,
                     xs_hbm, sorted_ref, zero_ref, sem, zsem, *, nsteps):
    i = pl.program_id(0)

    @pl.when(i == 0)
    def _():
        zero_ref[...] = jnp.zeros_like(zero_ref)
        for e in range(N_EXPERTS):
            pltpu.make_async_copy(zero_ref, xs_hbm.at[pl.ds(pl.multiple_of(zpos_ref[e], ROW_ALIGN), MOE_TILE), :],
                                  zsem).start()
        for e in range(N_EXPERTS):
            pltpu.make_async_copy(zero_ref, xs_hbm.at[pl.ds(0, MOE_TILE), :], zsem).wait()
        tail = zpos_ref[N_EXPERTS]
        n_tail = (xs_hbm.shape[0] - tail) // MOE_TILE

        def zero_tail(c, carry):
            cp = pltpu.make_async_copy(
                zero_ref, xs_hbm.at[pl.ds(pl.multiple_of(tail + c * MOE_TILE, MOE_TILE), MOE_TILE), :], zsem)
            cp.start()
            cp.wait()
            return carry

        lax.fori_loop(0, n_tail, zero_tail, 0)

    r = route_ref[...]
    tm = r.shape[0]
    lane = lax.broadcasted_iota(jnp.int32, (tm, LANES), 1)
    s0, s1 = _pair_slots(r, sof_ref[0])
    slots = lax.broadcasted_iota(jnp.int32, (tm, SORT_SLOTS), 1).astype(F32)
    onehot = ((slots == s0) | (slots == s1)).astype(BF16)

    def split(w):
        hi = w.astype(BF16).astype(F32)
        return hi, (w - hi).astype(BF16).astype(F32)

    extra = jnp.zeros((tm, LANES), F32)
    for k, v in enumerate((r[:, 0:1], *split(r[:, 2:3]), *split(r[:, 3:4]))):
        extra = jnp.where(lane == k, v, extra)
    hx = jnp.concatenate([h_ref[...], extra.astype(BF16)], axis=1)
    slot = i % 2
    sorted_ref[slot] = lax.dot_general(onehot, hx, (((0,), (0,)), ((), ())),
                                       preferred_element_type=F32).astype(BF16)

    def copies(s):
        def make_copy(src_row, dst_row):
            return pltpu.make_async_copy(sorted_ref.at[s, pl.ds(src_row, ROW_ALIGN), :],
                                         xs_hbm.at[pl.ds(dst_row, ROW_ALIGN), :], sem.at[s])
        return make_copy

    _chunk_copies(i, so_ref, nch_ref, dst_ref, copies(slot))

    def wait_all(tile, s):
        _wait_chunks(ntot_ref[tile], lambda rows: pltpu.make_async_copy(
            sorted_ref.at[s, pl.ds(0, rows), :], xs_hbm.at[pl.ds(0, rows), :], sem.at[s]))

    @pl.when(i > 0)
    def _():
        wait_all(i - 1, 1 - slot)

    @pl.when(i == nsteps - 1)
    def _():
        wait_all(i, slot)


def _dispatch(h2, route, tabs, n_rows):
    T, D = h2.shape
    tm = ROUTE_TILE
    grid_spec = pltpu.PrefetchScalarGridSpec(
        num_scalar_prefetch=5,
        grid=(T // tm,),
        in_specs=[pl.BlockSpec((tm, D), lambda i, *_: (i, 0)),
                  pl.BlockSpec((tm, LANES), lambda i, *_: (i, 0)),
                  pl.BlockSpec((1, 1, LANES), lambda i, *_: (i, 0, 0))],
        out_specs=pl.BlockSpec(memory_space=pl.ANY),
        scratch_shapes=[pltpu.VMEM((2, SORT_SLOTS, XS_W), BF16), pltpu.VMEM((MOE_TILE, XS_W), BF16),
                        pltpu.SemaphoreType.DMA((2,)), pltpu.SemaphoreType.DMA(())],
    )
    return pl.pallas_call(
        functools.partial(_dispatch_kernel, nsteps=T // tm),
        grid_spec=grid_spec,
        out_shape=jax.ShapeDtypeStruct((n_rows + MOE_TILE, XS_W), BF16),
        compiler_params=_cparams(("arbitrary",)),
        name="moe_dispatch",
    )(tabs["srcoff"], tabs["nchunk"], tabs["dst"], tabs["ntot"], tabs["zpos"], h2, route, tabs["srcoff_f"])


def _ffn_kernel(te_ref, nv_ref, xs_ref, wu_ref, wd_ref, o_ref, wub_ref, wdb_ref):
    i = pl.program_id(0)

    @pl.when((i == 0) | (te_ref[i] != te_ref[jnp.maximum(i - 1, 0)]))
    def _():
        wub_ref[...] = wu_ref[0].astype(BF16)
        wdb_ref[...] = wd_ref[0].astype(BF16)

    @pl.when(i < nv_ref[0])
    def _():
        x = xs_ref[:, :D_MODEL]
        wv = xs_ref[:, D_MODEL:].astype(F32)
        mine = wv[:, 0:1] == te_ref[i].astype(F32)
        w = jnp.where(mine, wv[:, 1:2] + wv[:, 2:3], wv[:, 3:4] + wv[:, 4:5])
        hid = jnp.dot(x, wub_ref[...], preferred_element_type=F32)
        a = hid[:, :EXPERT_HIDDEN]
        b = hid[:, EXPERT_HIDDEN:]
        act = a * jax.nn.sigmoid(a) * b * w
        o_ref[...] = jnp.dot(act.astype(BF16), wdb_ref[...], preferred_element_type=F32).astype(BF16)

    @pl.when(i >= nv_ref[0])
    def _():
        o_ref[...] = jnp.zeros_like(o_ref)


def _ffn(xs, tile_expert, n_valid, w_up, w_down, layer, n_rows):
    tr = MOE_TILE
    D = D_MODEL
    F2 = w_up.shape[-1]
    first = layer * N_EXPERTS
    grid_spec = pltpu.PrefetchScalarGridSpec(
        num_scalar_prefetch=2,
        grid=(n_rows // tr,),
        in_specs=[pl.BlockSpec((tr, XS_W), lambda i, te, nv: (jnp.minimum(i, nv[0] - 1), 0)),
                  pl.BlockSpec((1, D, F2), lambda i, te, nv: (first + te[i], 0, 0)),
                  pl.BlockSpec((1, EXPERT_HIDDEN, D), lambda i, te, nv: (first + te[i], 0, 0))],
        out_specs=pl.BlockSpec((tr, D), lambda i, te, nv: (i, 0)),
        scratch_shapes=[pltpu.VMEM((D, F2), BF16), pltpu.VMEM((EXPERT_HIDDEN, D), BF16)],
    )
    return pl.pallas_call(
        _ffn_kernel,
        grid_spec=grid_spec,
        out_shape=jax.ShapeDtypeStruct((n_rows, D), BF16),
        compiler_params=_cparams(("arbitrary",)),
        name="moe_experts",
    )(tile_expert, n_valid, xs, w_up, w_down)


def _combine_kernel(so_ref, nch_ref, dst_ref, ntot_ref, x_ref, route_ref, sof_ref, mod_ref, ye_hbm, *rest,
                    nsteps, emit_next):
    if emit_next:
        modn_ref, gn_ref, o_ref, hn_ref, ys_ref, sem = rest
    else:
        o_ref, ys_ref, sem = rest
    i = pl.program_id(0)

    def copies(slot):
        def make_copy(src_row, dst_row):
            return pltpu.make_async_copy(ye_hbm.at[pl.ds(dst_row, ROW_ALIGN), :],
                                         ys_ref.at[slot, pl.ds(src_row, ROW_ALIGN), :], sem.at[slot])
        return make_copy

    @pl.when(i == 0)
    def _():
        ys_ref[...] = jnp.zeros_like(ys_ref)
        _chunk_copies(0, so_ref, nch_ref, dst_ref, copies(0))

    @pl.when(i + 1 < nsteps)
    def _():
        _chunk_copies(i + 1, so_ref, nch_ref, dst_ref, copies((i + 1) % 2))

    slot = i % 2
    _wait_chunks(ntot_ref[i], lambda rows: pltpu.make_async_copy(
        ye_hbm.at[pl.ds(0, rows), :], ys_ref.at[slot, pl.ds(0, rows), :], sem.at[slot]))
    s0, s1 = _pair_slots(route_ref[...], sof_ref[0])
    slots = lax.broadcasted_iota(jnp.int32, (s0.shape[0], SORT_SLOTS), 1).astype(F32)
    onehot = ((slots == s0) | (slots == s1)).astype(BF16)
    y = jnp.dot(onehot, ys_ref[slot], preferred_element_type=F32)
    x = x_ref[...] + mod_ref[0, 5:6, :] * y
    o_ref[...] = x
    if emit_next:
        hn_ref[...] = _modulated_rmsnorm(x, gn_ref[...], modn_ref, 0, 1).astype(BF16)


def _combine(x2, route, mod, ye, tabs, S, next_mod=None, next_gain=None):
    T, D = x2.shape
    tm = ROUTE_TILE
    nsteps = T // tm
    emit_next = next_mod is not None
    row = pl.BlockSpec((tm, D), lambda i, *_: (i, 0))
    modspec = pl.BlockSpec((1, N_MOD, D), lambda i, *_: (i * tm // S, 0, 0))
    in_specs = [row,
                pl.BlockSpec((tm, LANES), lambda i, *_: (i, 0)),
                pl.BlockSpec((1, 1, LANES), lambda i, *_: (i, 0, 0)),
                modspec,
                pl.BlockSpec(memory_space=pl.ANY)]
    args = [x2, route, tabs["srcoff_f"], mod, ye]
    out_specs, out_shape = row, jax.ShapeDtypeStruct((T, D), F32)
    if emit_next:
        in_specs += [modspec, pl.BlockSpec((1, D), lambda i, *_: (0, 0))]
        args += [next_mod, next_gain.reshape(1, D)]
        out_specs, out_shape = [row, row], [out_shape, jax.ShapeDtypeStruct((T, D), BF16)]
    grid_spec = pltpu.PrefetchScalarGridSpec(
        num_scalar_prefetch=4,
        grid=(nsteps,),
        in_specs=in_specs,
        out_specs=out_specs,
        scratch_shapes=[pltpu.VMEM((2, SORT_SLOTS, D), BF16), pltpu.SemaphoreType.DMA((2,))],
    )
    return pl.pallas_call(
        functools.partial(_combine_kernel, nsteps=nsteps, emit_next=emit_next),
        grid_spec=grid_spec,
        out_shape=out_shape,
        compiler_params=_cparams(("arbitrary",)),
        name="moe_combine",
    )(tabs["srcoff"], tabs["nchunk"], tabs["dst"], tabs["ntot"], *args)


def _moe_rows_bound(T):
    nt = T // ROUTE_TILE
    rows = 2 * T + nt * N_EXPERTS * (ROW_ALIGN - 1) + N_EXPERTS * (MOE_TILE - 1)
    return -(-rows // MOE_TILE) * MOE_TILE


def _routing_tables(counts, n_rows):
    tr = MOE_TILE
    c = counts[:, 0, :N_EXPERTS].astype(jnp.int32)
    c16 = (c + ROW_ALIGN - 1) // ROW_ALIGN * ROW_ALIGN
    srcoff = jnp.cumsum(c16, axis=1) - c16
    rows = jnp.sum(c16, axis=0)
    ends = jnp.cumsum((rows + tr - 1) // tr * tr)
    offs = ends - (rows + tr - 1) // tr * tr
    dst = offs[None, :] + jnp.cumsum(c16, axis=0) - c16
    tile_start = jnp.arange(n_rows // tr, dtype=jnp.int32) * tr
    tile_expert = jnp.minimum(jnp.sum((tile_start[:, None] >= ends[None, :]).astype(jnp.int32), axis=-1),
                              N_EXPERTS - 1)
    srcoff_f = jnp.zeros((c.shape[0], 1, LANES), F32).at[:, 0, :N_EXPERTS].set(srcoff.astype(F32))
    return {
        "srcoff": srcoff.reshape(-1), "nchunk": (c16 // ROW_ALIGN).reshape(-1), "dst": dst.reshape(-1),
        "ntot": jnp.sum(c16 // ROW_ALIGN, axis=1), "zpos": jnp.concatenate([offs + rows, ends[-1:]]),
        "srcoff_f": srcoff_f,
        "tile_expert": tile_expert, "n_valid": (ends[-1] // tr).astype(jnp.int32).reshape(1),
    }


def _reorder_in_cols(w_in):
    cuts = 4 * RET_W + 2 * CONV_W + 3 * ATT_W
    return jnp.concatenate([w_in[..., cuts:], w_in[..., :cuts]], axis=-1)


def kernel(x, c, positions, w_ada, b_ada, g_mix, g_ffn, w_in, b_gate, ret_gn, conv_w, conv_b, conv_ln_g, conv_ln_b, att_q_gain, att_k_gain, att_rel_bias, w_ret_out, w_conv_out, w_att_out, w_out, w_group, b_group, w_inner, b_inner, w_up, w_down):
    B, S, D = x.shape
    L = w_ada.shape[0]
    T = B * S
    assert D == D_MODEL and S % INPROJ_TM == 0 and T % 2048 == 0
    n_rows = _moe_rows_bound(T)

    mod_all = _adaln(c, w_ada, b_ada).reshape(L, B, N_MOD, D)
    cs_tab = _rope_table(positions)
    x2 = x.reshape(T, D)
    h = None
    for l in range(L):
        mod = mod_all[l]
        w_in_b = _reorder_in_cols(w_in[l]).astype(BF16)
        proj = _inproj_norm(x2, mod, g_mix[l], w_in_b, S) if h is None else _inproj(h, w_in_b)
        ret = _retention(proj, cs_tab, ret_gn[l], B, S)
        conv = _conv(proj, conv_w[l], conv_b[l], conv_ln_g[l], conv_ln_b[l], B, S)
        att = _attention(proj, att_rel_bias[l], att_q_gain[l], att_k_gain[l], B, S)
        pad = LANES - N_EXPERTS - N_GROUPS
        w_rt = jnp.concatenate([w_inner[l], w_group[l], jnp.zeros((D, pad), F32)], axis=1)
        b_rt = jnp.concatenate([b_inner[l], b_group[l], jnp.zeros((pad,), F32)]).reshape(1, LANES)
        w_rt_hi = w_rt.astype(BF16)
        w_rt2 = jnp.concatenate([w_rt_hi, (w_rt - w_rt_hi.astype(F32)).astype(BF16)], axis=1)
        x2, h2, route, counts = _merge(x2, proj, ret, conv, att, b_gate[l], mod,
                                       w_ret_out[l].astype(BF16), w_conv_out[l].astype(BF16),
                                       w_att_out[l].astype(BF16), w_out[l].astype(BF16),
                                       g_ffn[l], w_rt2, b_rt, S)
        tabs = _routing_tables(counts, n_rows)
        xs = _dispatch(h2, route, tabs, n_rows)
        ye = _ffn(xs, tabs["tile_expert"], tabs["n_valid"],
                  w_up.reshape(L * N_EXPERTS, D, 2 * EXPERT_HIDDEN),
                  w_down.reshape(L * N_EXPERTS, EXPERT_HIDDEN, D), l, n_rows)
        if l + 1 < L:
            x2, h = _combine(x2, route, mod, ye, tabs, S, mod_all[l + 1], g_mix[l + 1])
        else:
            x2 = _combine(x2, route, mod, ye, tabs, S)
    return x2.reshape(B, S, D)
```

```python
import functools
import math

import jax
import jax.numpy as jnp
from jax import lax
from jax.experimental import pallas as pl
from jax.experimental.pallas import tpu as pltpu

F32 = jnp.float32
BF16 = jnp.bfloat16

D_MODEL = 1024
CHUNK = 64
EPS = 1e-6
NEG_INF = -1e30
RET_HEADS = 4
RET_DK = 128
RET_W = 512
ROPE_BASE = 10000.0
CONV_W = 512
CONV_K = 31
ATT_HEADS = 8
ATT_DH = 64
ATT_W = 512
ATT_LEFT_CHUNKS = 8
MAX_REL = 256
N_BRANCH = 3
IN_COLS = 4 * RET_W + 2 * CONV_W + 3 * ATT_W + N_BRANCH * D_MODEL
N_GROUPS = 4
EXPERTS_PER_GROUP = 8
N_EXPERTS = N_GROUPS * EXPERTS_PER_GROUP
EXPERT_HIDDEN = 256
N_MOD = 6

LANES = 128
V7X_VMEM_LIMIT_BYTES = 56 * 1024 * 1024

COL_GATE = 0
COL_RQ, COL_RK, COL_RV, COL_RG = 6, 7, 8, 9
COL_CA, COL_CB = 10, 11
COL_AQ, COL_AK, COL_AV = 12, 13, 14

RET_BLOCK = 256
ATT_QBLOCK = 256
ATT_BAND = ATT_QBLOCK + ATT_LEFT_CHUNKS * CHUNK
ATT_PAD = ATT_LEFT_CHUNKS * CHUNK
CONV_TILE = 512
CONV_HIST = 32
CONV_ROWS = 64
MOE_TILE = 512
ROUTE_TILE = 512
ROW_ALIGN = 16
SORT_SLOTS = 2 * ROUTE_TILE + N_EXPERTS * ROW_ALIGN
XS_W = D_MODEL + LANES


def _cparams(sem):
    return pltpu.CompilerParams(dimension_semantics=sem, vmem_limit_bytes=V7X_VMEM_LIMIT_BYTES)


def _adaln_kernel(c_ref, w_ref, b_ref, o_ref):
    c = c_ref[...]
    cs = c * jax.nn.sigmoid(c)
    w = w_ref[0]

    def split(v):
        hi = v.astype(BF16)
        return hi, (v - hi.astype(F32)).astype(BF16)

    (c_hi, c_lo), (w_hi, w_lo) = split(cs), split(w)
    o_ref[0] = (jnp.dot(c_hi, w_hi, preferred_element_type=F32) + jnp.dot(c_lo, w_hi, preferred_element_type=F32)
                + jnp.dot(c_hi, w_lo, preferred_element_type=F32) + b_ref[0])


def _adaln(c, w_ada, b_ada):
    L, D, N = w_ada.shape
    B = c.shape[0]
    tn = 1536
    return pl.pallas_call(
        _adaln_kernel,
        grid=(L, N // tn),
        in_specs=[pl.BlockSpec((B, D), lambda l, j: (0, 0)),
                  pl.BlockSpec((1, D, tn), lambda l, j: (l, 0, j)),
                  pl.BlockSpec((1, 1, tn), lambda l, j: (l, 0, j))],
        out_specs=pl.BlockSpec((1, B, tn), lambda l, j: (l, 0, j)),
        out_shape=jax.ShapeDtypeStruct((L, B, N), F32),
        compiler_params=_cparams(("arbitrary", "arbitrary")),
        name="adaln",
    )(c, w_ada, b_ada.reshape(L, 1, N))


def _rope_kernel(pos_ref, inv_ref, ph_ref, o_ref):
    o_ref[...] = jnp.cos(pos_ref[...] * inv_ref[...] - ph_ref[...])


def _rope_table(positions):
    T = positions.size
    half = RET_DK // 2
    inv = ROPE_BASE ** (-jnp.arange(half, dtype=F32) / half)
    inv2 = jnp.concatenate([inv, inv]).reshape(1, LANES)
    phase = jnp.concatenate([jnp.zeros((half,), F32), jnp.full((half,), math.pi / 2, F32)]).reshape(1, LANES)
    posb = jnp.broadcast_to(positions.reshape(T, 1).astype(F32), (T, LANES))
    tm = 2048
    return pl.pallas_call(
        _rope_kernel,
        grid=(T // tm,),
        in_specs=[pl.BlockSpec((tm, LANES), lambda i: (i, 0)),
                  pl.BlockSpec((1, LANES), lambda i: (0, 0)),
                  pl.BlockSpec((1, LANES), lambda i: (0, 0))],
        out_specs=pl.BlockSpec((tm, LANES), lambda i: (i, 0)),
        out_shape=jax.ShapeDtypeStruct((T, LANES), F32),
        compiler_params=_cparams(("arbitrary",)),
        name="rope_table",
    )(posb, inv2, phase)


def _modulated_rmsnorm(x, g, mod_ref, shift_row, scale_row):
    y = x * lax.rsqrt(jnp.mean(x * x, axis=-1, keepdims=True) + EPS) * g
    return y * (1.0 + mod_ref[0, scale_row:scale_row + 1, :]) + mod_ref[0, shift_row:shift_row + 1, :]


INPROJ_TM, INPROJ_TN = 2048, 2560


def _inproj_kernel(h_ref, w_ref, o_ref):
    o_ref[...] = jnp.dot(h_ref[...], w_ref[...], preferred_element_type=F32).astype(BF16)


def _inproj(h, w_in_b):
    T, D = h.shape
    N = w_in_b.shape[1]
    tm, tn = INPROJ_TM, INPROJ_TN
    return pl.pallas_call(
        _inproj_kernel,
        grid=(T // tm, N // tn),
        in_specs=[pl.BlockSpec((tm, D), lambda i, j: (i, 0)),
                  pl.BlockSpec((D, tn), lambda i, j: (0, j))],
        out_specs=pl.BlockSpec((tm, tn), lambda i, j: (i, j)),
        out_shape=jax.ShapeDtypeStruct((T, N), BF16),
        compiler_params=_cparams(("arbitrary", "arbitrary")),
        name="inproj",
    )(h, w_in_b)


def _inproj_norm_kernel(x_ref, mod_ref, g_ref, w_ref, o_ref, h_ref):
    @pl.when(pl.program_id(1) == 0)
    def _():
        h_ref[...] = _modulated_rmsnorm(x_ref[...], g_ref[...], mod_ref, 0, 1).astype(BF16)

    o_ref[...] = jnp.dot(h_ref[...], w_ref[...], preferred_element_type=F32).astype(BF16)


def _inproj_norm(x2, mod, g_mix, w_in_b, S):
    T, D = x2.shape
    N = w_in_b.shape[1]
    tm, tn = INPROJ_TM // 2, INPROJ_TN
    return pl.pallas_call(
        _inproj_norm_kernel,
        grid=(T // tm, N // tn),
        in_specs=[pl.BlockSpec((tm, D), lambda i, j: (i, 0)),
                  pl.BlockSpec((1, N_MOD, D), lambda i, j: (i * tm // S, 0, 0)),
                  pl.BlockSpec((1, D), lambda i, j: (0, 0)),
                  pl.BlockSpec((D, tn), lambda i, j: (0, j))],
        out_specs=pl.BlockSpec((tm, tn), lambda i, j: (i, j)),
        out_shape=jax.ShapeDtypeStruct((T, N), BF16),
        scratch_shapes=[pltpu.VMEM((tm, D), BF16)],
        compiler_params=_cparams(("arbitrary", "arbitrary")),
        name="inproj_norm",
    )(x2, mod, g_mix.reshape(1, D), w_in_b)


def _retention_kernel(q_ref, k_ref, v_ref, g_ref, cs_ref, gn_ref, o_ref, state_ref, decay_ref):
    C = RET_BLOCK

    @pl.when(pl.program_id(1) == 0)
    def _():
        state_ref[...] = jnp.zeros_like(state_ref)

    @pl.when((pl.program_id(0) == 0) & (pl.program_id(1) == 0))
    def _():
        d = (lax.broadcasted_iota(jnp.int32, (C, C), 0) - lax.broadcasted_iota(jnp.int32, (C, C), 1)).astype(F32)
        for h in range(RET_HEADS):
            lg = math.log1p(-(2.0 ** (-5.0 - h)))
            decay_ref[h] = jnp.where(d >= 0.0, jnp.exp(lg * jnp.maximum(d, 0.0)), 0.0)

    cs = cs_ref[...]
    csr = pltpu.roll(cs, RET_DK // 2, 1)
    first = lax.broadcasted_iota(jnp.int32, (C, RET_DK), 1) < RET_DK // 2
    cos2 = jnp.where(first, cs, csr)
    sin2 = jnp.where(first, -csr, cs)
    row = lax.broadcasted_iota(jnp.int32, (C, 1), 0).astype(F32)
    for h in range(RET_HEADS):
        lg = math.log1p(-(2.0 ** (-5.0 - h)))
        sl = slice(h * RET_DK, (h + 1) * RET_DK)
        qh = q_ref[:, sl].astype(F32)
        kh = k_ref[:, sl].astype(F32)
        vh = v_ref[:, sl]
        qr = (qh * cos2 + pltpu.roll(qh, RET_DK // 2, 1) * sin2).astype(BF16)
        kr = (kh * cos2 + pltpu.roll(kh, RET_DK // 2, 1) * sin2) * (RET_DK ** -0.5)
        s = lax.dot_general(qr, kr.astype(BF16), (((1,), (1,)), ((), ())),
                            preferred_element_type=F32) * decay_ref[h]
        o = jnp.dot(s.astype(BF16), vh, preferred_element_type=F32)
        st = state_ref[h]
        o = o + jnp.dot(qr, st.astype(BF16), preferred_element_type=F32) * jnp.exp(lg * (row + 1.0))
        kd = (kr * jnp.exp(lg * (C - 1.0 - row))).astype(BF16)
        kv = lax.dot_general(kd, vh, (((0,), (0,)), ((), ())), preferred_element_type=F32)
        state_ref[h] = math.exp(lg * C) * st + kv
        mu = jnp.mean(o, axis=-1, keepdims=True)
        oc = o - mu
        var = jnp.mean(oc * oc, axis=-1, keepdims=True)
        on = oc * lax.rsqrt(var + EPS) * gn_ref[:, sl]
        gh = g_ref[:, sl].astype(F32)
        o_ref[:, sl] = (on * (gh * jax.nn.sigmoid(gh))).astype(BF16)


def _retention(proj, cs_tab, ret_gn, B, S):
    T = B * S
    C = RET_BLOCK
    nblk = S // C

    def col(c):
        return pl.BlockSpec((C, RET_W), lambda b, i, c=c: (b * nblk + i, c))

    return pl.pallas_call(
        _retention_kernel,
        grid=(B, nblk),
        in_specs=[col(COL_RQ), col(COL_RK), col(COL_RV), col(COL_RG),
                  pl.BlockSpec((C, LANES), lambda b, i: (b * nblk + i, 0)),
                  pl.BlockSpec((1, RET_W), lambda b, i: (0, 0))],
        out_specs=pl.BlockSpec((C, RET_W), lambda b, i: (b * nblk + i, 0)),
        out_shape=jax.ShapeDtypeStruct((T, RET_W), BF16),
        scratch_shapes=[pltpu.VMEM((RET_HEADS, RET_DK, RET_DK), F32),
                        pltpu.VMEM((RET_HEADS, C, C), F32)],
        compiler_params=_cparams(("arbitrary", "arbitrary")),
        name="retention",
    )(proj, proj, proj, proj, cs_tab, ret_gn.reshape(1, RET_W))


def _conv_kernel(a_ref, b_ref, w_ref, bias_ref, lg_ref, lb_ref, o_ref, z_ref, zs_ref):
    tm = CONV_TILE
    H = CONV_HIST
    SUB = 8

    @pl.when(pl.program_id(1) == 0)
    def _():
        z_ref[0:H, :] = jnp.zeros((H, CONV_W), F32)

    a = a_ref[...].astype(F32)
    b = b_ref[...].astype(F32)
    z_ref[H:H + tm, :] = a * jax.nn.sigmoid(b)
    base = H - (CONV_K - 1)
    n_shift = tm + (base + CONV_K - 2) // SUB * SUB
    for r in range(1, SUB):
        for c0 in range(0, n_shift, 128):
            cs = min(128, n_shift - c0)
            zs_ref[r - 1, c0:c0 + cs, :] = z_ref[c0 + r:c0 + r + cs, :]
    for r0 in range(0, tm, CONV_ROWS):
        acc = jnp.zeros((CONV_ROWS, CONV_W), F32)
        for j in range(CONV_K):
            al, r = (base + j) // SUB * SUB, (base + j) % SUB
            if r == 0:
                tap = z_ref[r0 + al:r0 + al + CONV_ROWS, :]
            else:
                tap = zs_ref[r - 1, r0 + al:r0 + al + CONV_ROWS, :]
            acc = acc + tap * w_ref[j:j + 1, :]
        acc = acc + bias_ref[...]
        mu = jnp.mean(acc, axis=-1, keepdims=True)
        ac = acc - mu
        var = jnp.mean(ac * ac, axis=-1, keepdims=True)
        y = ac * lax.rsqrt(var + EPS) * lg_ref[...] + lb_ref[...]
        o_ref[r0:r0 + CONV_ROWS, :] = (y * jax.nn.sigmoid(y)).astype(BF16)
    z_ref[0:H, :] = z_ref[tm:tm + H, :]


def _conv(proj, conv_w, conv_b, ln_g, ln_b, B, S):
    T = B * S
    tm = CONV_TILE
    nblk = S // tm
    vec = pl.BlockSpec((1, CONV_W), lambda b, i: (0, 0))
    return pl.pallas_call(
        _conv_kernel,
        grid=(B, nblk),
        in_specs=[pl.BlockSpec((tm, CONV_W), lambda b, i: (b * nblk + i, COL_CA)),
                  pl.BlockSpec((tm, CONV_W), lambda b, i: (b * nblk + i, COL_CB)),
                  pl.BlockSpec((CONV_K, CONV_W), lambda b, i: (0, 0)),
                  vec, vec, vec],
        out_specs=pl.BlockSpec((tm, CONV_W), lambda b, i: (b * nblk + i, 0)),
        out_shape=jax.ShapeDtypeStruct((T, CONV_W), BF16),
        scratch_shapes=[pltpu.VMEM((CONV_HIST + tm, CONV_W), F32),
                        pltpu.VMEM((7, CONV_HIST + tm, CONV_W), F32)],
        compiler_params=_cparams(("arbitrary", "arbitrary")),
        name="conv",
    )(proj, proj, conv_w.reshape(CONV_K, CONV_W), conv_b.reshape(1, CONV_W),
      ln_g.reshape(1, CONV_W), ln_b.reshape(1, CONV_W))


def _head_mean_sq(x, hm_ref):
    sq = x * x
    hi = sq.astype(BF16)
    lo = (sq - hi.astype(F32)).astype(BF16)
    tot = jnp.dot(hi, hm_ref[...], preferred_element_type=F32) + jnp.dot(lo, hm_ref[...], preferred_element_type=F32)
    return tot * (1.0 / ATT_DH)


def _attn_kernel(q_ref, k_ref, v_ref, rrow_ref, qg_ref, kg_ref, hm_ref, o_ref, kn_ref, vt_ref, bmt_ref, *, S):
    tq = ATT_QBLOCK
    qi = pl.program_id(1)
    heads = [slice(h * ATT_DH, (h + 1) * ATT_DH) for h in range(ATT_HEADS)]
    nkb = ATT_BAND // tq

    @pl.when((qi == 0) & (pl.program_id(0) == 0))
    def _():
        r = lax.broadcasted_iota(jnp.int32, (tq, ATT_BAND), 0)
        u = lax.broadcasted_iota(jnp.int32, (tq, ATT_BAND), 1)
        off = u - (r // CHUNK) * CHUNK
        inband = (off >= 0) & (off < (ATT_LEFT_CHUNKS + 1) * CHUNK)
        for h in range(ATT_HEADS):
            gen = jnp.broadcast_to(rrow_ref[h], (tq, rrow_ref.shape[-1]))
            t = pltpu.roll(gen, 0, 1, stride=1, stride_axis=0)
            bmt_ref[h] = jnp.where(inband, t[:, :ATT_BAND], NEG_INF).T

    @pl.when(qi == 0)
    def _():
        for h in range(ATT_HEADS):
            kn_ref[h, 0:ATT_PAD, :] = jnp.zeros((ATT_PAD, ATT_DH), BF16)
        for j in range(ATT_PAD // tq):
            vt_ref[j] = jnp.zeros((ATT_W, tq), BF16)
        rows = 512
        for r0 in range(0, S, rows):
            k = k_ref[r0:r0 + rows, :].astype(F32)
            kn = (k * lax.rsqrt(_head_mean_sq(k, hm_ref) + EPS) * kg_ref[...]).astype(BF16)
            for h in range(ATT_HEADS):
                kn_ref[h, ATT_PAD + r0:ATT_PAD + r0 + rows, :] = kn[:, heads[h]]
        for r0 in range(0, S, tq):
            vt_ref[(ATT_PAD + r0) // tq] = v_ref[r0:r0 + tq, :].astype(F32).T.astype(BF16)

    start = pl.multiple_of(qi * tq, tq)
    q = q_ref[...].astype(F32)
    qn = (q * lax.rsqrt(_head_mean_sq(q, hm_ref) + EPS) * qg_ref[...]).astype(BF16)

    def attend(mask_pad):
        if mask_pad:
            key_pos = lax.broadcasted_iota(jnp.int32, (ATT_BAND, LANES), 0) + start
            valid = key_pos >= ATT_PAD
        def scores(h):
            kb = kn_ref[h, pl.ds(start, ATT_BAND), :]
            return lax.dot_general(kb, qn[:, heads[h]], (((1,), (1,)), ((), ())), preferred_element_type=F32)

        def softmax(h, st):
            ps, ls = [], []
            for c0 in range(0, tq, LANES):
                ss = st[:, c0:c0 + LANES] + bmt_ref[h, :, c0:c0 + LANES]
                if mask_pad:
                    ss = jnp.where(valid, ss, NEG_INF)
                m = jnp.max(ss, axis=0, keepdims=True)
                p = jnp.exp2(ss - m)
                ls.append(jnp.sum(p, axis=0, keepdims=True))
                ps.append(p.astype(BF16))
            return jnp.concatenate(ps, axis=1), jnp.concatenate(ls, axis=1)

        def values(h, pt, l):
            ot = jnp.zeros((ATT_DH, tq), F32)
            for t in range(nkb):
                ot = ot + jnp.dot(vt_ref[qi + t, heads[h], :], pt[t * tq:(t + 1) * tq, :],
                                  preferred_element_type=F32)
            return ot / l

        sts, pls, outs = {}, {}, []
        for step in range(ATT_HEADS + 2):
            if step < ATT_HEADS:
                sts[step] = scores(step)
            if 0 <= step - 1 < ATT_HEADS:
                pls[step - 1] = softmax(step - 1, sts.pop(step - 1))
            if 0 <= step - 2 < ATT_HEADS:
                outs.append(values(step - 2, *pls.pop(step - 2)))
        o_ref[...] = jnp.concatenate(outs, axis=0).T.astype(BF16)

    first = ATT_PAD // tq

    @pl.when(qi < first)
    def _():
        attend(True)

    @pl.when(qi >= first)
    def _():
        attend(False)


def _attn_bias_row(rel_table):
    t = rel_table.astype(F32) * math.log2(math.e)
    edge = jnp.broadcast_to(t[:, 2 * MAX_REL:], (ATT_HEADS, ATT_QBLOCK))
    row = jnp.concatenate([edge, jnp.flip(t[:, 1:], axis=1), edge], axis=1)
    return row.reshape(ATT_HEADS, 1, 2 * ATT_QBLOCK + 2 * MAX_REL)


def _attention(proj, rel_table, q_gain, k_gain, B, S):
    T = B * S
    tq = ATT_QBLOCK
    nblk = S // tq
    gen = _attn_bias_row(rel_table)
    qg = jnp.tile(q_gain.astype(F32) * (ATT_DH ** -0.5 * math.log2(math.e)), ATT_HEADS).reshape(1, ATT_W)
    kg = jnp.tile(k_gain.astype(F32), ATT_HEADS).reshape(1, ATT_W)
    lane_head = jnp.arange(ATT_W) // ATT_DH
    hm = (lane_head[:, None] == lane_head[None, :]).astype(BF16)
    vec = pl.BlockSpec((1, ATT_W), lambda b, i: (0, 0))
    return pl.pallas_call(
        functools.partial(_attn_kernel, S=S),
        grid=(B, nblk),
        in_specs=[pl.BlockSpec((tq, ATT_W), lambda b, i: (b * nblk + i, COL_AQ)),
                  pl.BlockSpec((S, ATT_W), lambda b, i: (b, COL_AK)),
                  pl.BlockSpec((S, ATT_W), lambda b, i: (b, COL_AV)),
                  pl.BlockSpec(gen.shape, lambda b, i: (0, 0, 0)),
                  vec, vec,
                  pl.BlockSpec((ATT_W, ATT_W), lambda b, i: (0, 0))],
        out_specs=pl.BlockSpec((tq, ATT_W), lambda b, i: (b * nblk + i, 0)),
        out_shape=jax.ShapeDtypeStruct((T, ATT_W), BF16),
        scratch_shapes=[pltpu.VMEM((ATT_HEADS, ATT_PAD + S, ATT_DH), BF16),
                        pltpu.VMEM(((ATT_PAD + S) // tq, ATT_W, tq), BF16),
                        pltpu.VMEM((ATT_HEADS, ATT_BAND, tq), F32)],
        compiler_params=_cparams(("arbitrary", "arbitrary")),
        name="chunk_attention",
    )(proj, proj, proj, gen, qg, kg, hm)


def _merge_kernel(x_ref, ret_ref, conv_ref, att_ref, g0_ref, g1_ref, g2_ref, bg_ref, mod_ref,
                  wr_ref, wc_ref, wa_ref, wo_ref, gf_ref, wrt_ref, brt_ref,
                  xo_ref, h2_ref, route_ref, cnt_out_ref, tri_ref):
    D = D_MODEL

    @pl.when(pl.program_id(0) == 0)
    def _():
        n = tri_ref.shape[0]
        tri_ref[...] = (lax.broadcasted_iota(jnp.int32, (n, n), 1)
                        < lax.broadcasted_iota(jnp.int32, (n, n), 0)).astype(BF16)

    def gated(g_ref, k, b_ref, w_ref):
        gate = jax.nn.sigmoid(g_ref[...].astype(F32) + bg_ref[:, k * D:(k + 1) * D])
        return gate * jnp.dot(b_ref[...], w_ref[...], preferred_element_type=F32)

    merged = gated(g0_ref, 0, ret_ref, wr_ref) + gated(g1_ref, 1, conv_ref, wc_ref) + gated(g2_ref, 2, att_ref, wa_ref)
    y = jnp.dot(merged.astype(BF16), wo_ref[...], preferred_element_type=F32)
    x = x_ref[...] + mod_ref[0, 2:3, :] * y
    xo_ref[...] = x

    h2 = _modulated_rmsnorm(x, gf_ref[...], mod_ref, 3, 4)
    h_hi = h2.astype(BF16)
    h2_ref[...] = h_hi
    h_lo = (h2 - h_hi.astype(F32)).astype(BF16)
    both = jnp.dot(h_hi, wrt_ref[...], preferred_element_type=F32)
    logits = (both[:, :LANES] + both[:, LANES:]
              + jnp.dot(h_lo, wrt_ref[:, :LANES], preferred_element_type=F32) + brt_ref[...])
    tm = logits.shape[0]
    lane = lax.broadcasted_iota(jnp.int32, (tm, LANES), 1)
    big = jnp.int32(LANES)
    is_g = (lane >= N_EXPERTS) & (lane < N_EXPERTS + N_GROUPS)
    gl = jnp.where(is_g, logits, -jnp.inf)
    gmax = jnp.max(gl, axis=-1, keepdims=True)
    gsel = jnp.min(jnp.where(gl == gmax, lane - N_EXPERTS, big), axis=-1, keepdims=True)
    p_group = 1.0 / jnp.sum(jnp.where(is_g, jnp.exp(gl - gmax), 0.0), axis=-1, keepdims=True)
    in_grp = (lane >= gsel * EXPERTS_PER_GROUP) & (lane < (gsel + 1) * EXPERTS_PER_GROUP)
    ch = jnp.where(in_grp, logits, -jnp.inf)
    v1 = jnp.max(ch, axis=-1, keepdims=True)
    i1 = jnp.min(jnp.where(ch == v1, lane, big), axis=-1, keepdims=True)
    ch2 = jnp.where(lane == i1, -jnp.inf, ch)
    v2 = jnp.max(ch2, axis=-1, keepdims=True)
    i2 = jnp.min(jnp.where(ch2 == v2, lane, big), axis=-1, keepdims=True)
    e = jnp.exp(v2 - v1)
    w1 = p_group / (1.0 + e)
    w2 = p_group * e / (1.0 + e)
    sel = ((lane == i1) | (lane == i2)).astype(BF16)
    before = jnp.dot(tri_ref[...], sel, preferred_element_type=F32)
    rank1 = jnp.sum(jnp.where(lane == i1, before, 0.0), axis=-1, keepdims=True)
    rank2 = jnp.sum(jnp.where(lane == i2, before, 0.0), axis=-1, keepdims=True)
    cnt_out_ref[0] = jnp.sum(sel.astype(F32), axis=0, keepdims=True)
    vals = (i1.astype(F32), i2.astype(F32), w1, w2, rank1, rank2)
    route = jnp.zeros((tm, LANES), F32)
    for k, v in enumerate(vals):
        route = jnp.where(lane == k, v, route)
    route_ref[...] = route


def _merge(x2, proj, ret, conv, att, b_gate, mod, wr, wc, wa, wo, g_ffn, w_rt, b_rt, S):
    T, D = x2.shape
    tm = ROUTE_TILE
    row = lambda i: (i, 0)
    const = lambda i: (0, 0)
    br = pl.BlockSpec((tm, RET_W), row)
    return pl.pallas_call(
        _merge_kernel,
        grid=(T // tm,),
        in_specs=[pl.BlockSpec((tm, D), row), br, br, br,
                  pl.BlockSpec((tm, D), lambda i: (i, COL_GATE)),
                  pl.BlockSpec((tm, D), lambda i: (i, COL_GATE + 1)),
                  pl.BlockSpec((tm, D), lambda i: (i, COL_GATE + 2)),
                  pl.BlockSpec((1, N_BRANCH * D), const),
                  pl.BlockSpec((1, N_MOD, D), lambda i: (i * tm // S, 0, 0)),
                  pl.BlockSpec((RET_W, D), const), pl.BlockSpec((CONV_W, D), const),
                  pl.BlockSpec((ATT_W, D), const), pl.BlockSpec((D, D), const),
                  pl.BlockSpec((1, D), const),
                  pl.BlockSpec((D, 2 * LANES), const), pl.BlockSpec((1, LANES), const)],
        out_specs=[pl.BlockSpec((tm, D), row), pl.BlockSpec((tm, D), row),
                   pl.BlockSpec((tm, LANES), row), pl.BlockSpec((1, 1, LANES), lambda i: (i, 0, 0))],
        out_shape=[jax.ShapeDtypeStruct((T, D), F32), jax.ShapeDtypeStruct((T, D), BF16),
                   jax.ShapeDtypeStruct((T, LANES), F32), jax.ShapeDtypeStruct((T // tm, 1, LANES), F32)],
        scratch_shapes=[pltpu.VMEM((tm, tm), BF16)],
        compiler_params=_cparams(("arbitrary",)),
        name="merge_router",
    )(x2, ret, conv, att, proj, proj, proj, b_gate.reshape(1, N_BRANCH * D), mod,
      wr, wc, wa, wo, g_ffn.reshape(1, D), w_rt, b_rt)


def _pair_slots(route, srcoff_row):
    tm = route.shape[0]
    lane = lax.broadcasted_iota(jnp.int32, (tm, LANES), 1).astype(F32)
    s0 = jnp.sum(jnp.where(lane == route[:, 0:1], srcoff_row, 0.0), axis=-1, keepdims=True) + route[:, 4:5]
    s1 = jnp.sum(jnp.where(lane == route[:, 1:2], srcoff_row, 0.0), axis=-1, keepdims=True) + route[:, 5:6]
    return s0, s1


def _chunk_copies(tile, so_ref, nch_ref, dst_ref, make_copy):
    base = tile * N_EXPERTS
    for e in range(N_EXPERTS):
        so = so_ref[base + e]
        d = dst_ref[base + e]

        def body(c, carry, so=so, d=d):
            make_copy(pl.multiple_of(so + c * ROW_ALIGN, ROW_ALIGN),
                      pl.multiple_of(d + c * ROW_ALIGN, ROW_ALIGN)).start()
            return carry

        lax.fori_loop(0, nch_ref[base + e], body, 0)


def _wait_chunks(n, make_wait):
    bit = 1
    while bit * ROW_ALIGN <= SORT_SLOTS:
        @pl.when((n & bit) != 0)
        def _(bit=bit):
            make_wait(bit * ROW_ALIGN).wait()

        bit *= 2


def _dispatch_kernel(so_ref, nch_ref, dst_ref, ntot_ref, zpos_ref, h_ref, route_ref, sof_ref,
                     xs_hbm, sorted_ref, zero_ref, sem, zsem, *, nsteps):
    i = pl.program_id(0)

    @pl.when(i == 0)
    def _():
        zero_ref[...] = jnp.zeros_like(zero_ref)
        for e in range(N_EXPERTS):
            pltpu.make_async_copy(zero_ref, xs_hbm.at[pl.ds(pl.multiple_of(zpos_ref[e], ROW_ALIGN), MOE_TILE), :],
                                  zsem).start()
        for e in range(N_EXPERTS):
            pltpu.make_async_copy(zero_ref, xs_hbm.at[pl.ds(0, MOE_TILE), :], zsem).wait()

    tail = zpos_ref[N_EXPERTS]
    n_tail = (xs_hbm.shape[0] - tail) // MOE_TILE
    tail_per_step = -(-(xs_hbm.shape[0] // MOE_TILE) // nsteps)

    def tail_copy(c):
        return pltpu.make_async_copy(
            zero_ref, xs_hbm.at[pl.ds(pl.multiple_of(tail + c * MOE_TILE, MOE_TILE), MOE_TILE), :], zsem)

    for k in range(tail_per_step):
        @pl.when(i * tail_per_step + k < n_tail)
        def _(k=k):
            tail_copy(i * tail_per_step + k).start()

    r = route_ref[...]
    tm = r.shape[0]
    lane = lax.broadcasted_iota(jnp.int32, (tm, LANES), 1)
    s0, s1 = _pair_slots(r, sof_ref[0])
    slots = lax.broadcasted_iota(jnp.int32, (tm, SORT_SLOTS), 1).astype(F32)
    onehot = ((slots == s0) | (slots == s1)).astype(BF16)

    def split(w):
        hi = w.astype(BF16).astype(F32)
        return hi, (w - hi).astype(BF16).astype(F32)

    extra = jnp.zeros((tm, LANES), F32)
    for k, v in enumerate((r[:, 0:1], *split(r[:, 2:3]), *split(r[:, 3:4]))):
        extra = jnp.where(lane == k, v, extra)
    hx = jnp.concatenate([h_ref[...], extra.astype(BF16)], axis=1)
    slot = i % 2
    sorted_ref[slot] = lax.dot_general(onehot, hx, (((0,), (0,)), ((), ())),
                                       preferred_element_type=F32).astype(BF16)

    def copies(s):
        def make_copy(src_row, dst_row):
            return pltpu.make_async_copy(sorted_ref.at[s, pl.ds(src_row, ROW_ALIGN), :],
                                         xs_hbm.at[pl.ds(dst_row, ROW_ALIGN), :], sem.at[s])
        return make_copy

    _chunk_copies(i, so_ref, nch_ref, dst_ref, copies(slot))

    def wait_all(tile, s):
        _wait_chunks(ntot_ref[tile], lambda rows: pltpu.make_async_copy(
            sorted_ref.at[s, pl.ds(0, rows), :], xs_hbm.at[pl.ds(0, rows), :], sem.at[s]))

    @pl.when(i > 0)
    def _():
        wait_all(i - 1, 1 - slot)

    @pl.when(i == nsteps - 1)
    def _():
        wait_all(i, slot)

        def wait_tail(c, carry):
            tail_copy(0).wait()
            return carry

        lax.fori_loop(0, n_tail, wait_tail, 0)


def _dispatch(h2, route, tabs, n_rows):
    T, D = h2.shape
    tm = ROUTE_TILE
    grid_spec = pltpu.PrefetchScalarGridSpec(
        num_scalar_prefetch=5,
        grid=(T // tm,),
        in_specs=[pl.BlockSpec((tm, D), lambda i, *_: (i, 0)),
                  pl.BlockSpec((tm, LANES), lambda i, *_: (i, 0)),
                  pl.BlockSpec((1, 1, LANES), lambda i, *_: (i, 0, 0))],
        out_specs=pl.BlockSpec(memory_space=pl.ANY),
        scratch_shapes=[pltpu.VMEM((2, SORT_SLOTS, XS_W), BF16), pltpu.VMEM((MOE_TILE, XS_W), BF16),
                        pltpu.SemaphoreType.DMA((2,)), pltpu.SemaphoreType.DMA(())],
    )
    return pl.pallas_call(
        functools.partial(_dispatch_kernel, nsteps=T // tm),
        grid_spec=grid_spec,
        out_shape=jax.ShapeDtypeStruct((n_rows + MOE_TILE, XS_W), BF16),
        compiler_params=_cparams(("arbitrary",)),
        name="moe_dispatch",
    )(tabs["srcoff"], tabs["nchunk"], tabs["dst"], tabs["ntot"], tabs["zpos"], h2, route, tabs["srcoff_f"])


def _ffn_kernel(te_ref, nv_ref, xs_ref, wu_ref, wd_ref, o_ref, wub_ref, wdb_ref):
    i = pl.program_id(0)

    @pl.when((i == 0) | (te_ref[i] != te_ref[jnp.maximum(i - 1, 0)]))
    def _():
        wub_ref[...] = wu_ref[0].astype(BF16)
        wdb_ref[...] = wd_ref[0].astype(BF16)

    @pl.when(i < nv_ref[0])
    def _():
        x = xs_ref[:, :D_MODEL]
        wv = xs_ref[:, D_MODEL:].astype(F32)
        mine = wv[:, 0:1] == te_ref[i].astype(F32)
        w = jnp.where(mine, wv[:, 1:2] + wv[:, 2:3], wv[:, 3:4] + wv[:, 4:5])
        hid = jnp.dot(x, wub_ref[...], preferred_element_type=F32)
        a = hid[:, :EXPERT_HIDDEN]
        b = hid[:, EXPERT_HIDDEN:]
        act = a * jax.nn.sigmoid(a) * b * w
        o_ref[...] = jnp.dot(act.astype(BF16), wdb_ref[...], preferred_element_type=F32).astype(BF16)

    @pl.when(i >= nv_ref[0])
    def _():
        o_ref[...] = jnp.zeros_like(o_ref)


def _ffn(xs, tile_expert, n_valid, w_up, w_down, layer, n_rows):
    tr = MOE_TILE
    D = D_MODEL
    F2 = w_up.shape[-1]
    first = layer * N_EXPERTS
    grid_spec = pltpu.PrefetchScalarGridSpec(
        num_scalar_prefetch=2,
        grid=(n_rows // tr,),
        in_specs=[pl.BlockSpec((tr, XS_W), lambda i, te, nv: (jnp.minimum(i, nv[0] - 1), 0)),
                  pl.BlockSpec((1, D, F2), lambda i, te, nv: (first + te[i], 0, 0)),
                  pl.BlockSpec((1, EXPERT_HIDDEN, D), lambda i, te, nv: (first + te[i], 0, 0))],
        out_specs=pl.BlockSpec((tr, D), lambda i, te, nv: (i, 0)),
        scratch_shapes=[pltpu.VMEM((D, F2), BF16), pltpu.VMEM((EXPERT_HIDDEN, D), BF16)],
    )
    return pl.pallas_call(
        _ffn_kernel,
        grid_spec=grid_spec,
        out_shape=jax.ShapeDtypeStruct((n_rows, D), BF16),
        compiler_params=_cparams(("arbitrary",)),
        name="moe_experts",
    )(tile_expert, n_valid, xs, w_up, w_down)


def _combine_kernel(so_ref, nch_ref, dst_ref, ntot_ref, x_ref, route_ref, sof_ref, mod_ref, ye_hbm, *rest,
                    nsteps, emit_next):
    if emit_next:
        modn_ref, gn_ref, o_ref, hn_ref, ys_ref, sem = rest
    else:
        o_ref, ys_ref, sem = rest
    i = pl.program_id(0)

    def copies(slot):
        def make_copy(src_row, dst_row):
            return pltpu.make_async_copy(ye_hbm.at[pl.ds(dst_row, ROW_ALIGN), :],
                                         ys_ref.at[slot, pl.ds(src_row, ROW_ALIGN), :], sem.at[slot])
        return make_copy

    @pl.when(i == 0)
    def _():
        ys_ref[...] = jnp.zeros_like(ys_ref)
        _chunk_copies(0, so_ref, nch_ref, dst_ref, copies(0))

    @pl.when(i + 1 < nsteps)
    def _():
        _chunk_copies(i + 1, so_ref, nch_ref, dst_ref, copies((i + 1) % 2))

    slot = i % 2
    _wait_chunks(ntot_ref[i], lambda rows: pltpu.make_async_copy(
        ye_hbm.at[pl.ds(0, rows), :], ys_ref.at[slot, pl.ds(0, rows), :], sem.at[slot]))
    s0, s1 = _pair_slots(route_ref[...], sof_ref[0])
    slots = lax.broadcasted_iota(jnp.int32, (s0.shape[0], SORT_SLOTS), 1).astype(F32)
    onehot = ((slots == s0) | (slots == s1)).astype(BF16)
    y = jnp.dot(onehot, ys_ref[slot], preferred_element_type=F32)
    x = x_ref[...] + mod_ref[0, 5:6, :] * y
    o_ref[...] = x
    if emit_next:
        hn_ref[...] = _modulated_rmsnorm(x, gn_ref[...], modn_ref, 0, 1).astype(BF16)


def _combine(x2, route, mod, ye, tabs, S, next_mod=None, next_gain=None):
    T, D = x2.shape
    tm = ROUTE_TILE
    nsteps = T // tm
    emit_next = next_mod is not None
    row = pl.BlockSpec((tm, D), lambda i, *_: (i, 0))
    modspec = pl.BlockSpec((1, N_MOD, D), lambda i, *_: (i * tm // S, 0, 0))
    in_specs = [row,
                pl.BlockSpec((tm, LANES), lambda i, *_: (i, 0)),
                pl.BlockSpec((1, 1, LANES), lambda i, *_: (i, 0, 0)),
                modspec,
                pl.BlockSpec(memory_space=pl.ANY)]
    args = [x2, route, tabs["srcoff_f"], mod, ye]
    out_specs, out_shape = row, jax.ShapeDtypeStruct((T, D), F32)
    if emit_next:
        in_specs += [modspec, pl.BlockSpec((1, D), lambda i, *_: (0, 0))]
        args += [next_mod, next_gain.reshape(1, D)]
        out_specs, out_shape = [row, row], [out_shape, jax.ShapeDtypeStruct((T, D), BF16)]
    grid_spec = pltpu.PrefetchScalarGridSpec(
        num_scalar_prefetch=4,
        grid=(nsteps,),
        in_specs=in_specs,
        out_specs=out_specs,
        scratch_shapes=[pltpu.VMEM((2, SORT_SLOTS, D), BF16), pltpu.SemaphoreType.DMA((2,))],
    )
    return pl.pallas_call(
        functools.partial(_combine_kernel, nsteps=nsteps, emit_next=emit_next),
        grid_spec=grid_spec,
        out_shape=out_shape,
        compiler_params=_cparams(("arbitrary",)),
        name="moe_combine",
    )(tabs["srcoff"], tabs["nchunk"], tabs["dst"], tabs["ntot"], *args)


def _moe_rows_bound(T):
    nt = T // ROUTE_TILE
    rows = 2 * T + nt * N_EXPERTS * (ROW_ALIGN - 1) + N_EXPERTS * (MOE_TILE - 1)
    return -(-rows // MOE_TILE) * MOE_TILE


def _routing_tables(counts, n_rows):
    tr = MOE_TILE
    c = counts[:, 0, :N_EXPERTS].astype(jnp.int32)
    c16 = (c + ROW_ALIGN - 1) // ROW_ALIGN * ROW_ALIGN
    srcoff = jnp.cumsum(c16, axis=1) - c16
    rows = jnp.sum(c16, axis=0)
    ends = jnp.cumsum((rows + tr - 1) // tr * tr)
    offs = ends - (rows + tr - 1) // tr * tr
    dst = offs[None, :] + jnp.cumsum(c16, axis=0) - c16
    tile_start = jnp.arange(n_rows // tr, dtype=jnp.int32) * tr
    tile_expert = jnp.minimum(jnp.sum((tile_start[:, None] >= ends[None, :]).astype(jnp.int32), axis=-1),
                              N_EXPERTS - 1)
    srcoff_f = jnp.zeros((c.shape[0], 1, LANES), F32).at[:, 0, :N_EXPERTS].set(srcoff.astype(F32))
    return {
        "srcoff": srcoff.reshape(-1), "nchunk": (c16 // ROW_ALIGN).reshape(-1), "dst": dst.reshape(-1),
        "ntot": jnp.sum(c16 // ROW_ALIGN, axis=1), "zpos": jnp.concatenate([offs + rows, ends[-1:]]),
        "srcoff_f": srcoff_f,
        "tile_expert": tile_expert, "n_valid": (ends[-1] // tr).astype(jnp.int32).reshape(1),
    }


def _reorder_in_cols(w_in):
    cuts = 4 * RET_W + 2 * CONV_W + 3 * ATT_W
    return jnp.concatenate([w_in[..., cuts:], w_in[..., :cuts]], axis=-1)


def kernel(x, c, positions, w_ada, b_ada, g_mix, g_ffn, w_in, b_gate, ret_gn, conv_w, conv_b, conv_ln_g, conv_ln_b, att_q_gain, att_k_gain, att_rel_bias, w_ret_out, w_conv_out, w_att_out, w_out, w_group, b_group, w_inner, b_inner, w_up, w_down):
    B, S, D = x.shape
    L = w_ada.shape[0]
    T = B * S
    assert D == D_MODEL and S % INPROJ_TM == 0 and T % 2048 == 0
    n_rows = _moe_rows_bound(T)

    mod_all = _adaln(c, w_ada, b_ada).reshape(L, B, N_MOD, D)
    cs_tab = _rope_table(positions)
    x2 = x.reshape(T, D)
    h = None
    for l in range(L):
        mod = mod_all[l]
        w_in_b = _reorder_in_cols(w_in[l]).astype(BF16)
        proj = _inproj_norm(x2, mod, g_mix[l], w_in_b, S) if h is None else _inproj(h, w_in_b)
        ret = _retention(proj, cs_tab, ret_gn[l], B, S)
        conv = _conv(proj, conv_w[l], conv_b[l], conv_ln_g[l], conv_ln_b[l], B, S)
        att = _attention(proj, att_rel_bias[l], att_q_gain[l], att_k_gain[l], B, S)
        pad = LANES - N_EXPERTS - N_GROUPS
        w_rt = jnp.concatenate([w_inner[l], w_group[l], jnp.zeros((D, pad), F32)], axis=1)
        b_rt = jnp.concatenate([b_inner[l], b_group[l], jnp.zeros((pad,), F32)]).reshape(1, LANES)
        w_rt_hi = w_rt.astype(BF16)
        w_rt2 = jnp.concatenate([w_rt_hi, (w_rt - w_rt_hi.astype(F32)).astype(BF16)], axis=1)
        x2, h2, route, counts = _merge(x2, proj, ret, conv, att, b_gate[l], mod,
                                       w_ret_out[l].astype(BF16), w_conv_out[l].astype(BF16),
                                       w_att_out[l].astype(BF16), w_out[l].astype(BF16),
                                       g_ffn[l], w_rt2, b_rt, S)
        tabs = _routing_tables(counts, n_rows)
        xs = _dispatch(h2, route, tabs, n_rows)
        ye = _ffn(xs, tabs["tile_expert"], tabs["n_valid"],
                  w_up.reshape(L * N_EXPERTS, D, 2 * EXPERT_HIDDEN),
                  w_down.reshape(L * N_EXPERTS, EXPERT_HIDDEN, D), l, n_rows)
        if l + 1 < L:
            x2, h = _combine(x2, route, mod, ye, tabs, S, mod_all[l + 1], g_mix[l + 1])
        else:
            x2 = _combine(x2, route, mod, ye, tabs, S)
    return x2.reshape(B, S, D)
```

```python
import functools
import math

import jax
import jax.numpy as jnp
from jax import lax
from jax.experimental import pallas as pl
from jax.experimental.pallas import tpu as pltpu

F32 = jnp.float32
BF16 = jnp.bfloat16

D_MODEL = 1024
CHUNK = 64
EPS = 1e-6
NEG_INF = -1e30
RET_HEADS = 4
RET_DK = 128
RET_W = 512
ROPE_BASE = 10000.0
CONV_W = 512
CONV_K = 31
ATT_HEADS = 8
ATT_DH = 64
ATT_W = 512
ATT_LEFT_CHUNKS = 8
MAX_REL = 256
N_BRANCH = 3
IN_COLS = 4 * RET_W + 2 * CONV_W + 3 * ATT_W + N_BRANCH * D_MODEL
N_GROUPS = 4
EXPERTS_PER_GROUP = 8
N_EXPERTS = N_GROUPS * EXPERTS_PER_GROUP
EXPERT_HIDDEN = 256
N_MOD = 6

LANES = 128
V7X_VMEM_LIMIT_BYTES = 56 * 1024 * 1024

COL_GATE = 0
COL_RQ, COL_RK, COL_RV, COL_RG = 6, 7, 8, 9
COL_CA, COL_CB = 10, 11
COL_AQ, COL_AK, COL_AV = 12, 13, 14

RET_BLOCK = 256
ATT_QBLOCK = 256
ATT_BAND = ATT_QBLOCK + ATT_LEFT_CHUNKS * CHUNK
ATT_PAD = ATT_LEFT_CHUNKS * CHUNK
CONV_TILE = 512
CONV_HIST = 32
CONV_ROWS = 64
MOE_TILE = 512
EXPERT_TILES_PER_STEP = 2
ROUTE_TILE = 512
ROW_ALIGN = 16
SORT_SLOTS = 2 * ROUTE_TILE + N_EXPERTS * ROW_ALIGN
XS_W = D_MODEL + LANES


def _cparams(sem):
    return pltpu.CompilerParams(dimension_semantics=sem, vmem_limit_bytes=V7X_VMEM_LIMIT_BYTES)


def _adaln_kernel(c_ref, w_ref, b_ref, o_ref):
    c = c_ref[...]
    cs = c * jax.nn.sigmoid(c)
    w = w_ref[0]

    def split(v):
        hi = v.astype(BF16)
        return hi, (v - hi.astype(F32)).astype(BF16)

    (c_hi, c_lo), (w_hi, w_lo) = split(cs), split(w)
    o_ref[0] = (jnp.dot(c_hi, w_hi, preferred_element_type=F32) + jnp.dot(c_lo, w_hi, preferred_element_type=F32)
                + jnp.dot(c_hi, w_lo, preferred_element_type=F32) + b_ref[0])


def _adaln(c, w_ada, b_ada):
    L, D, N = w_ada.shape
    B = c.shape[0]
    tn = 1536
    return pl.pallas_call(
        _adaln_kernel,
        grid=(L, N // tn),
        in_specs=[pl.BlockSpec((B, D), lambda l, j: (0, 0)),
                  pl.BlockSpec((1, D, tn), lambda l, j: (l, 0, j)),
                  pl.BlockSpec((1, 1, tn), lambda l, j: (l, 0, j))],
        out_specs=pl.BlockSpec((1, B, tn), lambda l, j: (l, 0, j)),
        out_shape=jax.ShapeDtypeStruct((L, B, N), F32),
        compiler_params=_cparams(("arbitrary", "arbitrary")),
        name="adaln",
    )(c, w_ada, b_ada.reshape(L, 1, N))


def _rope_kernel(pos_ref, inv_ref, ph_ref, o_ref):
    o_ref[...] = jnp.cos(pos_ref[...] * inv_ref[...] - ph_ref[...])


def _rope_table(positions):
    T = positions.size
    half = RET_DK // 2
    inv = ROPE_BASE ** (-jnp.arange(half, dtype=F32) / half)
    inv2 = jnp.concatenate([inv, inv]).reshape(1, LANES)
    phase = jnp.concatenate([jnp.zeros((half,), F32), jnp.full((half,), math.pi / 2, F32)]).reshape(1, LANES)
    posb = jnp.broadcast_to(positions.reshape(T, 1).astype(F32), (T, LANES))
    tm = 2048
    return pl.pallas_call(
        _rope_kernel,
        grid=(T // tm,),
        in_specs=[pl.BlockSpec((tm, LANES), lambda i: (i, 0)),
                  pl.BlockSpec((1, LANES), lambda i: (0, 0)),
                  pl.BlockSpec((1, LANES), lambda i: (0, 0))],
        out_specs=pl.BlockSpec((tm, LANES), lambda i: (i, 0)),
        out_shape=jax.ShapeDtypeStruct((T, LANES), F32),
        compiler_params=_cparams(("arbitrary",)),
        name="rope_table",
    )(posb, inv2, phase)


def _modulated_rmsnorm(x, g, mod_ref, shift_row, scale_row):
    y = x * lax.rsqrt(jnp.mean(x * x, axis=-1, keepdims=True) + EPS) * g
    return y * (1.0 + mod_ref[0, scale_row:scale_row + 1, :]) + mod_ref[0, shift_row:shift_row + 1, :]


INPROJ_TM, INPROJ_TN = 2048, 2560


def _inproj_kernel(h_ref, w_ref, o_ref):
    o_ref[...] = jnp.dot(h_ref[...], w_ref[...], preferred_element_type=F32).astype(BF16)


def _inproj(h, w_in_b):
    T, D = h.shape
    N = w_in_b.shape[1]
    tm, tn = INPROJ_TM, INPROJ_TN
    return pl.pallas_call(
        _inproj_kernel,
        grid=(T // tm, N // tn),
        in_specs=[pl.BlockSpec((tm, D), lambda i, j: (i, 0)),
                  pl.BlockSpec((D, tn), lambda i, j: (0, j))],
        out_specs=pl.BlockSpec((tm, tn), lambda i, j: (i, j)),
        out_shape=jax.ShapeDtypeStruct((T, N), BF16),
        compiler_params=_cparams(("arbitrary", "arbitrary")),
        name="inproj",
    )(h, w_in_b)


def _inproj_norm_kernel(x_ref, mod_ref, g_ref, w_ref, o_ref, h_ref):
    @pl.when(pl.program_id(1) == 0)
    def _():
        h_ref[...] = _modulated_rmsnorm(x_ref[...], g_ref[...], mod_ref, 0, 1).astype(BF16)

    o_ref[...] = jnp.dot(h_ref[...], w_ref[...], preferred_element_type=F32).astype(BF16)


def _inproj_norm(x2, mod, g_mix, w_in_b, S):
    T, D = x2.shape
    N = w_in_b.shape[1]
    tm, tn = INPROJ_TM // 2, INPROJ_TN
    return pl.pallas_call(
        _inproj_norm_kernel,
        grid=(T // tm, N // tn),
        in_specs=[pl.BlockSpec((tm, D), lambda i, j: (i, 0)),
                  pl.BlockSpec((1, N_MOD, D), lambda i, j: (i * tm // S, 0, 0)),
                  pl.BlockSpec((1, D), lambda i, j: (0, 0)),
                  pl.BlockSpec((D, tn), lambda i, j: (0, j))],
        out_specs=pl.BlockSpec((tm, tn), lambda i, j: (i, j)),
        out_shape=jax.ShapeDtypeStruct((T, N), BF16),
        scratch_shapes=[pltpu.VMEM((tm, D), BF16)],
        compiler_params=_cparams(("arbitrary", "arbitrary")),
        name="inproj_norm",
    )(x2, mod, g_mix.reshape(1, D), w_in_b)


def _retention_kernel(q_ref, k_ref, v_ref, g_ref, cs_ref, gn_ref, o_ref, state_ref, decay_ref):
    C = RET_BLOCK

    @pl.when(pl.program_id(1) == 0)
    def _():
        state_ref[...] = jnp.zeros_like(state_ref)

    @pl.when((pl.program_id(0) == 0) & (pl.program_id(1) == 0))
    def _():
        d = (lax.broadcasted_iota(jnp.int32, (C, C), 0) - lax.broadcasted_iota(jnp.int32, (C, C), 1)).astype(F32)
        for h in range(RET_HEADS):
            lg = math.log1p(-(2.0 ** (-5.0 - h)))
            decay_ref[h] = jnp.where(d >= 0.0, jnp.exp(lg * jnp.maximum(d, 0.0)), 0.0)

    cs = cs_ref[...]
    csr = pltpu.roll(cs, RET_DK // 2, 1)
    first = lax.broadcasted_iota(jnp.int32, (C, RET_DK), 1) < RET_DK // 2
    cos2 = jnp.where(first, cs, csr)
    sin2 = jnp.where(first, -csr, cs)
    row = lax.broadcasted_iota(jnp.int32, (C, 1), 0).astype(F32)
    for h in range(RET_HEADS):
        lg = math.log1p(-(2.0 ** (-5.0 - h)))
        sl = slice(h * RET_DK, (h + 1) * RET_DK)
        qh = q_ref[:, sl].astype(F32)
        kh = k_ref[:, sl].astype(F32)
        vh = v_ref[:, sl]
        qr = (qh * cos2 + pltpu.roll(qh, RET_DK // 2, 1) * sin2).astype(BF16)
        kr = (kh * cos2 + pltpu.roll(kh, RET_DK // 2, 1) * sin2) * (RET_DK ** -0.5)
        s = lax.dot_general(qr, kr.astype(BF16), (((1,), (1,)), ((), ())),
                            preferred_element_type=F32) * decay_ref[h]
        o = jnp.dot(s.astype(BF16), vh, preferred_element_type=F32)
        st = state_ref[h]
        o = o + jnp.dot(qr, st.astype(BF16), preferred_element_type=F32) * jnp.exp(lg * (row + 1.0))
        kd = (kr * jnp.exp(lg * (C - 1.0 - row))).astype(BF16)
        kv = lax.dot_general(kd, vh, (((0,), (0,)), ((), ())), preferred_element_type=F32)
        state_ref[h] = math.exp(lg * C) * st + kv
        mu = jnp.mean(o, axis=-1, keepdims=True)
        oc = o - mu
        var = jnp.mean(oc * oc, axis=-1, keepdims=True)
        on = oc * lax.rsqrt(var + EPS) * gn_ref[:, sl]
        gh = g_ref[:, sl].astype(F32)
        o_ref[:, sl] = (on * (gh * jax.nn.sigmoid(gh))).astype(BF16)


def _retention(proj, cs_tab, ret_gn, B, S):
    T = B * S
    C = RET_BLOCK
    nblk = S // C

    def col(c):
        return pl.BlockSpec((C, RET_W), lambda b, i, c=c: (b * nblk + i, c))

    return pl.pallas_call(
        _retention_kernel,
        grid=(B, nblk),
        in_specs=[col(COL_RQ), col(COL_RK), col(COL_RV), col(COL_RG),
                  pl.BlockSpec((C, LANES), lambda b, i: (b * nblk + i, 0)),
                  pl.BlockSpec((1, RET_W), lambda b, i: (0, 0))],
        out_specs=pl.BlockSpec((C, RET_W), lambda b, i: (b * nblk + i, 0)),
        out_shape=jax.ShapeDtypeStruct((T, RET_W), BF16),
        scratch_shapes=[pltpu.VMEM((RET_HEADS, RET_DK, RET_DK), F32),
                        pltpu.VMEM((RET_HEADS, C, C), F32)],
        compiler_params=_cparams(("arbitrary", "arbitrary")),
        name="retention",
    )(proj, proj, proj, proj, cs_tab, ret_gn.reshape(1, RET_W))


def _conv_kernel(a_ref, b_ref, w_ref, bias_ref, lg_ref, lb_ref, o_ref, z_ref, zs_ref):
    tm = CONV_TILE
    H = CONV_HIST
    SUB = 8

    @pl.when(pl.program_id(1) == 0)
    def _():
        z_ref[0:H, :] = jnp.zeros((H, CONV_W), F32)

    a = a_ref[...].astype(F32)
    b = b_ref[...].astype(F32)
    z_ref[H:H + tm, :] = a * jax.nn.sigmoid(b)
    base = H - (CONV_K - 1)
    n_shift = tm + (base + CONV_K - 2) // SUB * SUB
    for r in range(1, SUB):
        for c0 in range(0, n_shift, 128):
            cs = min(128, n_shift - c0)
            zs_ref[r - 1, c0:c0 + cs, :] = z_ref[c0 + r:c0 + r + cs, :]
    for r0 in range(0, tm, CONV_ROWS):
        acc = jnp.zeros((CONV_ROWS, CONV_W), F32)
        for j in range(CONV_K):
            al, r = (base + j) // SUB * SUB, (base + j) % SUB
            if r == 0:
                tap = z_ref[r0 + al:r0 + al + CONV_ROWS, :]
            else:
                tap = zs_ref[r - 1, r0 + al:r0 + al + CONV_ROWS, :]
            acc = acc + tap * w_ref[j:j + 1, :]
        acc = acc + bias_ref[...]
        mu = jnp.mean(acc, axis=-1, keepdims=True)
        ac = acc - mu
        var = jnp.mean(ac * ac, axis=-1, keepdims=True)
        y = ac * lax.rsqrt(var + EPS) * lg_ref[...] + lb_ref[...]
        o_ref[r0:r0 + CONV_ROWS, :] = (y * jax.nn.sigmoid(y)).astype(BF16)
    z_ref[0:H, :] = z_ref[tm:tm + H, :]


def _conv(proj, conv_w, conv_b, ln_g, ln_b, B, S):
    T = B * S
    tm = CONV_TILE
    nblk = S // tm
    vec = pl.BlockSpec((1, CONV_W), lambda b, i: (0, 0))
    return pl.pallas_call(
        _conv_kernel,
        grid=(B, nblk),
        in_specs=[pl.BlockSpec((tm, CONV_W), lambda b, i: (b * nblk + i, COL_CA)),
                  pl.BlockSpec((tm, CONV_W), lambda b, i: (b * nblk + i, COL_CB)),
                  pl.BlockSpec((CONV_K, CONV_W), lambda b, i: (0, 0)),
                  vec, vec, vec],
        out_specs=pl.BlockSpec((tm, CONV_W), lambda b, i: (b * nblk + i, 0)),
        out_shape=jax.ShapeDtypeStruct((T, CONV_W), BF16),
        scratch_shapes=[pltpu.VMEM((CONV_HIST + tm, CONV_W), F32),
                        pltpu.VMEM((7, CONV_HIST + tm, CONV_W), F32)],
        compiler_params=_cparams(("arbitrary", "arbitrary")),
        name="conv",
    )(proj, proj, conv_w.reshape(CONV_K, CONV_W), conv_b.reshape(1, CONV_W),
      ln_g.reshape(1, CONV_W), ln_b.reshape(1, CONV_W))


def _head_mean_sq(x, hm_ref):
    sq = x * x
    hi = sq.astype(BF16)
    lo = (sq - hi.astype(F32)).astype(BF16)
    tot = jnp.dot(hi, hm_ref[...], preferred_element_type=F32) + jnp.dot(lo, hm_ref[...], preferred_element_type=F32)
    return tot * (1.0 / ATT_DH)


def _attn_kernel(q_ref, k_ref, v_ref, rrow_ref, qg_ref, kg_ref, hm_ref, o_ref, kn_ref, vt_ref, bmt_ref, *, S):
    tq = ATT_QBLOCK
    qi = pl.program_id(1)
    heads = [slice(h * ATT_DH, (h + 1) * ATT_DH) for h in range(ATT_HEADS)]
    nkb = ATT_BAND // tq

    @pl.when((qi == 0) & (pl.program_id(0) == 0))
    def _():
        r = lax.broadcasted_iota(jnp.int32, (tq, ATT_BAND), 0)
        u = lax.broadcasted_iota(jnp.int32, (tq, ATT_BAND), 1)
        off = u - (r // CHUNK) * CHUNK
        inband = (off >= 0) & (off < (ATT_LEFT_CHUNKS + 1) * CHUNK)
        for h in range(ATT_HEADS):
            gen = jnp.broadcast_to(rrow_ref[h], (tq, rrow_ref.shape[-1]))
            t = pltpu.roll(gen, 0, 1, stride=1, stride_axis=0)
            bmt_ref[h] = jnp.where(inband, t[:, :ATT_BAND], NEG_INF).T

    @pl.when(qi == 0)
    def _():
        for h in range(ATT_HEADS):
            kn_ref[h, 0:ATT_PAD, :] = jnp.zeros((ATT_PAD, ATT_DH), BF16)
        for j in range(ATT_PAD // tq):
            vt_ref[j] = jnp.zeros((ATT_W, tq), BF16)
        rows = 512
        for r0 in range(0, S, rows):
            k = k_ref[r0:r0 + rows, :].astype(F32)
            kn = (k * lax.rsqrt(_head_mean_sq(k, hm_ref) + EPS) * kg_ref[...]).astype(BF16)
            for h in range(ATT_HEADS):
                kn_ref[h, ATT_PAD + r0:ATT_PAD + r0 + rows, :] = kn[:, heads[h]]
        for r0 in range(0, S, tq):
            vt_ref[(ATT_PAD + r0) // tq] = v_ref[r0:r0 + tq, :].astype(F32).T.astype(BF16)

    start = pl.multiple_of(qi * tq, tq)
    q = q_ref[...].astype(F32)
    qn = (q * lax.rsqrt(_head_mean_sq(q, hm_ref) + EPS) * qg_ref[...]).astype(BF16)

    def attend(mask_pad):
        if mask_pad:
            key_pos = lax.broadcasted_iota(jnp.int32, (ATT_BAND, LANES), 0) + start
            valid = key_pos >= ATT_PAD
        def scores(h):
            kb = kn_ref[h, pl.ds(start, ATT_BAND), :]
            return lax.dot_general(kb, qn[:, heads[h]], (((1,), (1,)), ((), ())), preferred_element_type=F32)

        def softmax(h, st):
            ps, ls = [], []
            for c0 in range(0, tq, LANES):
                ss = st[:, c0:c0 + LANES] + bmt_ref[h, :, c0:c0 + LANES]
                if mask_pad:
                    ss = jnp.where(valid, ss, NEG_INF)
                m = jnp.max(ss, axis=0, keepdims=True)
                p = jnp.exp2(ss - m)
                ls.append(jnp.sum(p, axis=0, keepdims=True))
                ps.append(p.astype(BF16))
            return jnp.concatenate(ps, axis=1), jnp.concatenate(ls, axis=1)

        def values(h, pt, l):
            ot = jnp.zeros((ATT_DH, tq), F32)
            for t in range(nkb):
                ot = ot + jnp.dot(vt_ref[qi + t, heads[h], :], pt[t * tq:(t + 1) * tq, :],
                                  preferred_element_type=F32)
            return ot / l

        sts, pls, outs = {}, {}, []
        for step in range(ATT_HEADS + 2):
            if step < ATT_HEADS:
                sts[step] = scores(step)
            if 0 <= step - 1 < ATT_HEADS:
                pls[step - 1] = softmax(step - 1, sts.pop(step - 1))
            if 0 <= step - 2 < ATT_HEADS:
                outs.append(values(step - 2, *pls.pop(step - 2)))
        o_ref[...] = jnp.concatenate(outs, axis=0).T.astype(BF16)

    first = ATT_PAD // tq

    @pl.when(qi < first)
    def _():
        attend(True)

    @pl.when(qi >= first)
    def _():
        attend(False)


def _attn_bias_row(rel_table):
    t = rel_table.astype(F32) * math.log2(math.e)
    edge = jnp.broadcast_to(t[:, 2 * MAX_REL:], (ATT_HEADS, ATT_QBLOCK))
    row = jnp.concatenate([edge, jnp.flip(t[:, 1:], axis=1), edge], axis=1)
    return row.reshape(ATT_HEADS, 1, 2 * ATT_QBLOCK + 2 * MAX_REL)


def _attention(proj, rel_table, q_gain, k_gain, B, S):
    T = B * S
    tq = ATT_QBLOCK
    nblk = S // tq
    gen = _attn_bias_row(rel_table)
    qg = jnp.tile(q_gain.astype(F32) * (ATT_DH ** -0.5 * math.log2(math.e)), ATT_HEADS).reshape(1, ATT_W)
    kg = jnp.tile(k_gain.astype(F32), ATT_HEADS).reshape(1, ATT_W)
    lane_head = jnp.arange(ATT_W) // ATT_DH
    hm = (lane_head[:, None] == lane_head[None, :]).astype(BF16)
    vec = pl.BlockSpec((1, ATT_W), lambda b, i: (0, 0))
    return pl.pallas_call(
        functools.partial(_attn_kernel, S=S),
        grid=(B, nblk),
        in_specs=[pl.BlockSpec((tq, ATT_W), lambda b, i: (b * nblk + i, COL_AQ)),
                  pl.BlockSpec((S, ATT_W), lambda b, i: (b, COL_AK)),
                  pl.BlockSpec((S, ATT_W), lambda b, i: (b, COL_AV)),
                  pl.BlockSpec(gen.shape, lambda b, i: (0, 0, 0)),
                  vec, vec,
                  pl.BlockSpec((ATT_W, ATT_W), lambda b, i: (0, 0))],
        out_specs=pl.BlockSpec((tq, ATT_W), lambda b, i: (b * nblk + i, 0)),
        out_shape=jax.ShapeDtypeStruct((T, ATT_W), BF16),
        scratch_shapes=[pltpu.VMEM((ATT_HEADS, ATT_PAD + S, ATT_DH), BF16),
                        pltpu.VMEM(((ATT_PAD + S) // tq, ATT_W, tq), BF16),
                        pltpu.VMEM((ATT_HEADS, ATT_BAND, tq), F32)],
        compiler_params=_cparams(("arbitrary", "arbitrary")),
        name="chunk_attention",
    )(proj, proj, proj, gen, qg, kg, hm)


def _merge_kernel(x_ref, ret_ref, conv_ref, att_ref, g0_ref, g1_ref, g2_ref, bg_ref, mod_ref,
                  wr_ref, wc_ref, wa_ref, wo_ref, gf_ref, wrt_ref, brt_ref,
                  xo_ref, h2_ref, route_ref, cnt_out_ref, tri_ref):
    D = D_MODEL

    @pl.when(pl.program_id(0) == 0)
    def _():
        n = tri_ref.shape[0]
        tri_ref[...] = (lax.broadcasted_iota(jnp.int32, (n, n), 1)
                        < lax.broadcasted_iota(jnp.int32, (n, n), 0)).astype(BF16)

    def gated(g_ref, k, b_ref, w_ref):
        gate = jax.nn.sigmoid(g_ref[...].astype(F32) + bg_ref[:, k * D:(k + 1) * D])
        return gate * jnp.dot(b_ref[...], w_ref[...], preferred_element_type=F32)

    merged = gated(g0_ref, 0, ret_ref, wr_ref) + gated(g1_ref, 1, conv_ref, wc_ref) + gated(g2_ref, 2, att_ref, wa_ref)
    y = jnp.dot(merged.astype(BF16), wo_ref[...], preferred_element_type=F32)
    x = x_ref[...] + mod_ref[0, 2:3, :] * y
    xo_ref[...] = x

    h2 = _modulated_rmsnorm(x, gf_ref[...], mod_ref, 3, 4)
    h_hi = h2.astype(BF16)
    h2_ref[...] = h_hi
    h_lo = (h2 - h_hi.astype(F32)).astype(BF16)
    both = jnp.dot(h_hi, wrt_ref[...], preferred_element_type=F32)
    logits = (both[:, :LANES] + both[:, LANES:]
              + jnp.dot(h_lo, wrt_ref[:, :LANES], preferred_element_type=F32) + brt_ref[...])
    tm = logits.shape[0]
    lane = lax.broadcasted_iota(jnp.int32, (tm, LANES), 1)
    big = jnp.int32(LANES)
    is_g = (lane >= N_EXPERTS) & (lane < N_EXPERTS + N_GROUPS)
    gl = jnp.where(is_g, logits, -jnp.inf)
    gmax = jnp.max(gl, axis=-1, keepdims=True)
    gsel = jnp.min(jnp.where(gl == gmax, lane - N_EXPERTS, big), axis=-1, keepdims=True)
    p_group = 1.0 / jnp.sum(jnp.where(is_g, jnp.exp(gl - gmax), 0.0), axis=-1, keepdims=True)
    in_grp = (lane >= gsel * EXPERTS_PER_GROUP) & (lane < (gsel + 1) * EXPERTS_PER_GROUP)
    ch = jnp.where(in_grp, logits, -jnp.inf)
    v1 = jnp.max(ch, axis=-1, keepdims=True)
    i1 = jnp.min(jnp.where(ch == v1, lane, big), axis=-1, keepdims=True)
    ch2 = jnp.where(lane == i1, -jnp.inf, ch)
    v2 = jnp.max(ch2, axis=-1, keepdims=True)
    i2 = jnp.min(jnp.where(ch2 == v2, lane, big), axis=-1, keepdims=True)
    e = jnp.exp(v2 - v1)
    w1 = p_group / (1.0 + e)
    w2 = p_group * e / (1.0 + e)
    sel = ((lane == i1) | (lane == i2)).astype(BF16)
    before = jnp.dot(tri_ref[...], sel, preferred_element_type=F32)
    rank1 = jnp.sum(jnp.where(lane == i1, before, 0.0), axis=-1, keepdims=True)
    rank2 = jnp.sum(jnp.where(lane == i2, before, 0.0), axis=-1, keepdims=True)
    cnt_out_ref[0] = jnp.sum(sel.astype(F32), axis=0, keepdims=True)
    vals = (i1.astype(F32), i2.astype(F32), w1, w2, rank1, rank2)
    route = jnp.zeros((tm, LANES), F32)
    for k, v in enumerate(vals):
        route = jnp.where(lane == k, v, route)
    route_ref[...] = route


def _merge(x2, proj, ret, conv, att, b_gate, mod, wr, wc, wa, wo, g_ffn, w_rt, b_rt, S):
    T, D = x2.shape
    tm = ROUTE_TILE
    row = lambda i: (i, 0)
    const = lambda i: (0, 0)
    br = pl.BlockSpec((tm, RET_W), row)
    return pl.pallas_call(
        _merge_kernel,
        grid=(T // tm,),
        in_specs=[pl.BlockSpec((tm, D), row), br, br, br,
                  pl.BlockSpec((tm, D), lambda i: (i, COL_GATE)),
                  pl.BlockSpec((tm, D), lambda i: (i, COL_GATE + 1)),
                  pl.BlockSpec((tm, D), lambda i: (i, COL_GATE + 2)),
                  pl.BlockSpec((1, N_BRANCH * D), const),
                  pl.BlockSpec((1, N_MOD, D), lambda i: (i * tm // S, 0, 0)),
                  pl.BlockSpec((RET_W, D), const), pl.BlockSpec((CONV_W, D), const),
                  pl.BlockSpec((ATT_W, D), const), pl.BlockSpec((D, D), const),
                  pl.BlockSpec((1, D), const),
                  pl.BlockSpec((D, 2 * LANES), const), pl.BlockSpec((1, LANES), const)],
        out_specs=[pl.BlockSpec((tm, D), row), pl.BlockSpec((tm, D), row),
                   pl.BlockSpec((tm, LANES), row), pl.BlockSpec((1, 1, LANES), lambda i: (i, 0, 0))],
        out_shape=[jax.ShapeDtypeStruct((T, D), F32), jax.ShapeDtypeStruct((T, D), BF16),
                   jax.ShapeDtypeStruct((T, LANES), F32), jax.ShapeDtypeStruct((T // tm, 1, LANES), F32)],
        scratch_shapes=[pltpu.VMEM((tm, tm), BF16)],
        compiler_params=_cparams(("arbitrary",)),
        name="merge_router",
    )(x2, ret, conv, att, proj, proj, proj, b_gate.reshape(1, N_BRANCH * D), mod,
      wr, wc, wa, wo, g_ffn.reshape(1, D), w_rt, b_rt)


def _pair_slots(route, srcoff_row):
    tm = route.shape[0]
    lane = lax.broadcasted_iota(jnp.int32, (tm, LANES), 1).astype(F32)
    s0 = jnp.sum(jnp.where(lane == route[:, 0:1], srcoff_row, 0.0), axis=-1, keepdims=True) + route[:, 4:5]
    s1 = jnp.sum(jnp.where(lane == route[:, 1:2], srcoff_row, 0.0), axis=-1, keepdims=True) + route[:, 5:6]
    return s0, s1


def _chunk_copies(tile, so_ref, nch_ref, dst_ref, make_copy):
    base = tile * N_EXPERTS
    for e in range(N_EXPERTS):
        so = so_ref[base + e]
        d = dst_ref[base + e]

        def body(c, carry, so=so, d=d):
            make_copy(pl.multiple_of(so + c * ROW_ALIGN, ROW_ALIGN),
                      pl.multiple_of(d + c * ROW_ALIGN, ROW_ALIGN)).start()
            return carry

        lax.fori_loop(0, nch_ref[base + e], body, 0)


def _wait_chunks(n, make_wait):
    bit = 1
    while bit * ROW_ALIGN <= SORT_SLOTS:
        @pl.when((n & bit) != 0)
        def _(bit=bit):
            make_wait(bit * ROW_ALIGN).wait()

        bit *= 2


def _dispatch_kernel(so_ref, nch_ref, dst_ref, ntot_ref, zpos_ref, h_ref, route_ref, sof_ref,
                     xs_hbm, sorted_ref, zero_ref, sem, zsem, *, nsteps):
    i = pl.program_id(0)

    @pl.when(i == 0)
    def _():
        zero_ref[...] = jnp.zeros_like(zero_ref)
        for e in range(N_EXPERTS):
            pltpu.make_async_copy(zero_ref, xs_hbm.at[pl.ds(pl.multiple_of(zpos_ref[e], ROW_ALIGN), MOE_TILE), :],
                                  zsem).start()
        for e in range(N_EXPERTS):
            pltpu.make_async_copy(zero_ref, xs_hbm.at[pl.ds(0, MOE_TILE), :], zsem).wait()

    tail = zpos_ref[N_EXPERTS]
    n_tail = (xs_hbm.shape[0] - tail) // MOE_TILE
    tail_per_step = -(-(xs_hbm.shape[0] // MOE_TILE) // nsteps)

    def tail_copy(c):
        return pltpu.make_async_copy(
            zero_ref, xs_hbm.at[pl.ds(pl.multiple_of(tail + c * MOE_TILE, MOE_TILE), MOE_TILE), :], zsem)

    for k in range(tail_per_step):
        @pl.when(i * tail_per_step + k < n_tail)
        def _(k=k):
            tail_copy(i * tail_per_step + k).start()

    r = route_ref[...]
    tm = r.shape[0]
    lane = lax.broadcasted_iota(jnp.int32, (tm, LANES), 1)
    s0, s1 = _pair_slots(r, sof_ref[0])
    slots = lax.broadcasted_iota(jnp.int32, (tm, SORT_SLOTS), 1).astype(F32)
    onehot = ((slots == s0) | (slots == s1)).astype(BF16)

    def split(w):
        hi = w.astype(BF16).astype(F32)
        return hi, (w - hi).astype(BF16).astype(F32)

    extra = jnp.zeros((tm, LANES), F32)
    for k, v in enumerate((r[:, 0:1], *split(r[:, 2:3]), *split(r[:, 3:4]))):
        extra = jnp.where(lane == k, v, extra)
    hx = jnp.concatenate([h_ref[...], extra.astype(BF16)], axis=1)
    slot = i % 2
    sorted_ref[slot] = lax.dot_general(onehot, hx, (((0,), (0,)), ((), ())),
                                       preferred_element_type=F32).astype(BF16)

    def copies(s):
        def make_copy(src_row, dst_row):
            return pltpu.make_async_copy(sorted_ref.at[s, pl.ds(src_row, ROW_ALIGN), :],
                                         xs_hbm.at[pl.ds(dst_row, ROW_ALIGN), :], sem.at[s])
        return make_copy

    _chunk_copies(i, so_ref, nch_ref, dst_ref, copies(slot))

    def wait_all(tile, s):
        _wait_chunks(ntot_ref[tile], lambda rows: pltpu.make_async_copy(
            sorted_ref.at[s, pl.ds(0, rows), :], xs_hbm.at[pl.ds(0, rows), :], sem.at[s]))

    @pl.when(i > 0)
    def _():
        wait_all(i - 1, 1 - slot)

    @pl.when(i == nsteps - 1)
    def _():
        wait_all(i, slot)

        def wait_tail(c, carry):
            tail_copy(0).wait()
            return carry

        lax.fori_loop(0, n_tail, wait_tail, 0)


def _dispatch(h2, route, tabs, n_rows):
    T, D = h2.shape
    tm = ROUTE_TILE
    grid_spec = pltpu.PrefetchScalarGridSpec(
        num_scalar_prefetch=5,
        grid=(T // tm,),
        in_specs=[pl.BlockSpec((tm, D), lambda i, *_: (i, 0)),
                  pl.BlockSpec((tm, LANES), lambda i, *_: (i, 0)),
                  pl.BlockSpec((1, 1, LANES), lambda i, *_: (i, 0, 0))],
        out_specs=pl.BlockSpec(memory_space=pl.ANY),
        scratch_shapes=[pltpu.VMEM((2, SORT_SLOTS, XS_W), BF16), pltpu.VMEM((MOE_TILE, XS_W), BF16),
                        pltpu.SemaphoreType.DMA((2,)), pltpu.SemaphoreType.DMA(())],
    )
    return pl.pallas_call(
        functools.partial(_dispatch_kernel, nsteps=T // tm),
        grid_spec=grid_spec,
        out_shape=jax.ShapeDtypeStruct((n_rows + MOE_TILE, XS_W), BF16),
        compiler_params=_cparams(("arbitrary",)),
        name="moe_dispatch",
    )(tabs["srcoff"], tabs["nchunk"], tabs["dst"], tabs["ntot"], tabs["zpos"], h2, route, tabs["srcoff_f"])


def _ffn_kernel(te_ref, nv_ref, xs_ref, wu0_ref, wd0_ref, wu1_ref, wd1_ref, o_ref, wub_ref, wdb_ref):
    i = pl.program_id(0)
    weights = ((wu0_ref, wd0_ref), (wu1_ref, wd1_ref))
    for half in range(EXPERT_TILES_PER_STEP):
        wu_ref, wd_ref = weights[half]
        t = i * EXPERT_TILES_PER_STEP + half
        rows = slice(half * MOE_TILE, (half + 1) * MOE_TILE)

        @pl.when((i == 0) | (te_ref[t] != te_ref[jnp.maximum(t - EXPERT_TILES_PER_STEP, 0)]))
        def _(wu_ref=wu_ref, wd_ref=wd_ref, half=half):
            wub_ref[half] = wu_ref[0].astype(BF16)
            wdb_ref[half] = wd_ref[0].astype(BF16)

        @pl.when(t < nv_ref[0])
        def _(half=half, t=t, rows=rows):
            x = xs_ref[rows, :D_MODEL]
            wv = xs_ref[rows, D_MODEL:].astype(F32)
            mine = wv[:, 0:1] == te_ref[t].astype(F32)
            w = jnp.where(mine, wv[:, 1:2] + wv[:, 2:3], wv[:, 3:4] + wv[:, 4:5])
            hid = jnp.dot(x, wub_ref[half], preferred_element_type=F32)
            a = hid[:, :EXPERT_HIDDEN]
            b = hid[:, EXPERT_HIDDEN:]
            act = a * jax.nn.sigmoid(a) * b * w
            o_ref[rows, :] = jnp.dot(act.astype(BF16), wdb_ref[half], preferred_element_type=F32).astype(BF16)

        @pl.when(t >= nv_ref[0])
        def _(rows=rows):
            o_ref[rows, :] = jnp.zeros((MOE_TILE, D_MODEL), BF16)


def _ffn(xs, tile_expert, n_valid, w_up, w_down, layer, n_rows):
    per = EXPERT_TILES_PER_STEP
    tr = MOE_TILE * per
    D = D_MODEL
    F2 = w_up.shape[-1]
    first = layer * N_EXPERTS

    def wspec(shape, half):
        return pl.BlockSpec(shape, lambda i, te, nv: (first + te[i * per + half], 0, 0))

    grid_spec = pltpu.PrefetchScalarGridSpec(
        num_scalar_prefetch=2,
        grid=(n_rows // tr,),
        in_specs=[pl.BlockSpec((tr, XS_W), lambda i, te, nv: (jnp.minimum(i, (nv[0] - 1) // per), 0)),
                  wspec((1, D, F2), 0), wspec((1, EXPERT_HIDDEN, D), 0),
                  wspec((1, D, F2), 1), wspec((1, EXPERT_HIDDEN, D), 1)],
        out_specs=pl.BlockSpec((tr, D), lambda i, te, nv: (i, 0)),
        scratch_shapes=[pltpu.VMEM((per, D, F2), BF16), pltpu.VMEM((per, EXPERT_HIDDEN, D), BF16)],
    )
    return pl.pallas_call(
        _ffn_kernel,
        grid_spec=grid_spec,
        out_shape=jax.ShapeDtypeStruct((n_rows, D), BF16),
        compiler_params=_cparams(("arbitrary",)),
        name="moe_experts",
    )(tile_expert, n_valid, xs, w_up, w_down, w_up, w_down)


def _combine_kernel(so_ref, nch_ref, dst_ref, ntot_ref, x_ref, route_ref, sof_ref, mod_ref, ye_hbm, *rest,
                    nsteps, emit_next):
    if emit_next:
        modn_ref, gn_ref, o_ref, hn_ref, ys_ref, sem = rest
    else:
        o_ref, ys_ref, sem = rest
    i = pl.program_id(0)

    def copies(slot):
        def make_copy(src_row, dst_row):
            return pltpu.make_async_copy(ye_hbm.at[pl.ds(dst_row, ROW_ALIGN), :],
                                         ys_ref.at[slot, pl.ds(src_row, ROW_ALIGN), :], sem.at[slot])
        return make_copy

    @pl.when(i == 0)
    def _():
        ys_ref[...] = jnp.zeros_like(ys_ref)
        _chunk_copies(0, so_ref, nch_ref, dst_ref, copies(0))

    @pl.when(i + 1 < nsteps)
    def _():
        _chunk_copies(i + 1, so_ref, nch_ref, dst_ref, copies((i + 1) % 2))

    slot = i % 2
    _wait_chunks(ntot_ref[i], lambda rows: pltpu.make_async_copy(
        ye_hbm.at[pl.ds(0, rows), :], ys_ref.at[slot, pl.ds(0, rows), :], sem.at[slot]))
    s0, s1 = _pair_slots(route_ref[...], sof_ref[0])
    slots = lax.broadcasted_iota(jnp.int32, (s0.shape[0], SORT_SLOTS), 1).astype(F32)
    onehot = ((slots == s0) | (slots == s1)).astype(BF16)
    y = jnp.dot(onehot, ys_ref[slot], preferred_element_type=F32)
    x = x_ref[...] + mod_ref[0, 5:6, :] * y
    o_ref[...] = x
    if emit_next:
        hn_ref[...] = _modulated_rmsnorm(x, gn_ref[...], modn_ref, 0, 1).astype(BF16)


def _combine(x2, route, mod, ye, tabs, S, next_mod=None, next_gain=None):
    T, D = x2.shape
    tm = ROUTE_TILE
    nsteps = T // tm
    emit_next = next_mod is not None
    row = pl.BlockSpec((tm, D), lambda i, *_: (i, 0))
    modspec = pl.BlockSpec((1, N_MOD, D), lambda i, *_: (i * tm // S, 0, 0))
    in_specs = [row,
                pl.BlockSpec((tm, LANES), lambda i, *_: (i, 0)),
                pl.BlockSpec((1, 1, LANES), lambda i, *_: (i, 0, 0)),
                modspec,
                pl.BlockSpec(memory_space=pl.ANY)]
    args = [x2, route, tabs["srcoff_f"], mod, ye]
    out_specs, out_shape = row, jax.ShapeDtypeStruct((T, D), F32)
    if emit_next:
        in_specs += [modspec, pl.BlockSpec((1, D), lambda i, *_: (0, 0))]
        args += [next_mod, next_gain.reshape(1, D)]
        out_specs, out_shape = [row, row], [out_shape, jax.ShapeDtypeStruct((T, D), BF16)]
    grid_spec = pltpu.PrefetchScalarGridSpec(
        num_scalar_prefetch=4,
        grid=(nsteps,),
        in_specs=in_specs,
        out_specs=out_specs,
        scratch_shapes=[pltpu.VMEM((2, SORT_SLOTS, D), BF16), pltpu.SemaphoreType.DMA((2,))],
    )
    return pl.pallas_call(
        functools.partial(_combine_kernel, nsteps=nsteps, emit_next=emit_next),
        grid_spec=grid_spec,
        out_shape=out_shape,
        compiler_params=_cparams(("arbitrary",)),
        name="moe_combine",
    )(tabs["srcoff"], tabs["nchunk"], tabs["dst"], tabs["ntot"], *args)


def _moe_rows_bound(T):
    nt = T // ROUTE_TILE
    rows = 2 * T + nt * N_EXPERTS * (ROW_ALIGN - 1) + N_EXPERTS * (MOE_TILE - 1)
    step_rows = MOE_TILE * EXPERT_TILES_PER_STEP
    return -(-rows // step_rows) * step_rows


def _routing_tables(counts, n_rows):
    tr = MOE_TILE
    c = counts[:, 0, :N_EXPERTS].astype(jnp.int32)
    c16 = (c + ROW_ALIGN - 1) // ROW_ALIGN * ROW_ALIGN
    srcoff = jnp.cumsum(c16, axis=1) - c16
    rows = jnp.sum(c16, axis=0)
    ends = jnp.cumsum((rows + tr - 1) // tr * tr)
    offs = ends - (rows + tr - 1) // tr * tr
    dst = offs[None, :] + jnp.cumsum(c16, axis=0) - c16
    tile_start = jnp.arange(n_rows // tr, dtype=jnp.int32) * tr
    tile_expert = jnp.minimum(jnp.sum((tile_start[:, None] >= ends[None, :]).astype(jnp.int32), axis=-1),
                              N_EXPERTS - 1)
    srcoff_f = jnp.zeros((c.shape[0], 1, LANES), F32).at[:, 0, :N_EXPERTS].set(srcoff.astype(F32))
    return {
        "srcoff": srcoff.reshape(-1), "nchunk": (c16 // ROW_ALIGN).reshape(-1), "dst": dst.reshape(-1),
        "ntot": jnp.sum(c16 // ROW_ALIGN, axis=1), "zpos": jnp.concatenate([offs + rows, ends[-1:]]),
        "srcoff_f": srcoff_f,
        "tile_expert": tile_expert, "n_valid": (ends[-1] // tr).astype(jnp.int32).reshape(1),
    }


def _reorder_in_cols(w_in):
    cuts = 4 * RET_W + 2 * CONV_W + 3 * ATT_W
    return jnp.concatenate([w_in[..., cuts:], w_in[..., :cuts]], axis=-1)


def kernel(x, c, positions, w_ada, b_ada, g_mix, g_ffn, w_in, b_gate, ret_gn, conv_w, conv_b, conv_ln_g, conv_ln_b, att_q_gain, att_k_gain, att_rel_bias, w_ret_out, w_conv_out, w_att_out, w_out, w_group, b_group, w_inner, b_inner, w_up, w_down):
    B, S, D = x.shape
    L = w_ada.shape[0]
    T = B * S
    assert D == D_MODEL and S % INPROJ_TM == 0 and T % 2048 == 0
    n_rows = _moe_rows_bound(T)

    mod_all = _adaln(c, w_ada, b_ada).reshape(L, B, N_MOD, D)
    cs_tab = _rope_table(positions)
    x2 = x.reshape(T, D)
    h = None
    for l in range(L):
        mod = mod_all[l]
        w_in_b = _reorder_in_cols(w_in[l]).astype(BF16)
        proj = _inproj_norm(x2, mod, g_mix[l], w_in_b, S) if h is None else _inproj(h, w_in_b)
        ret = _retention(proj, cs_tab, ret_gn[l], B, S)
        conv = _conv(proj, conv_w[l], conv_b[l], conv_ln_g[l], conv_ln_b[l], B, S)
        att = _attention(proj, att_rel_bias[l], att_q_gain[l], att_k_gain[l], B, S)
        pad = LANES - N_EXPERTS - N_GROUPS
        w_rt = jnp.concatenate([w_inner[l], w_group[l], jnp.zeros((D, pad), F32)], axis=1)
        b_rt = jnp.concatenate([b_inner[l], b_group[l], jnp.zeros((pad,), F32)]).reshape(1, LANES)
        w_rt_hi = w_rt.astype(BF16)
        w_rt2 = jnp.concatenate([w_rt_hi, (w_rt - w_rt_hi.astype(F32)).astype(BF16)], axis=1)
        x2, h2, route, counts = _merge(x2, proj, ret, conv, att, b_gate[l], mod,
                                       w_ret_out[l].astype(BF16), w_conv_out[l].astype(BF16),
                                       w_att_out[l].astype(BF16), w_out[l].astype(BF16),
                                       g_ffn[l], w_rt2, b_rt, S)
        tabs = _routing_tables(counts, n_rows)
        xs = _dispatch(h2, route, tabs, n_rows)
        ye = _ffn(xs, tabs["tile_expert"], tabs["n_valid"],
                  w_up.reshape(L * N_EXPERTS, D, 2 * EXPERT_HIDDEN),
                  w_down.reshape(L * N_EXPERTS, EXPERT_HIDDEN, D), l, n_rows)
        if l + 1 < L:
            x2, h = _combine(x2, route, mod, ye, tabs, S, mod_all[l + 1], g_mix[l + 1])
        else:
            x2 = _combine(x2, route, mod, ye, tabs, S)
    return x2.reshape(B, S, D)
```

```python
import functools
import math

import jax
import jax.numpy as jnp
from jax import lax
from jax.experimental import pallas as pl
from jax.experimental.pallas import tpu as pltpu

F32 = jnp.float32
BF16 = jnp.bfloat16

D_MODEL = 1024
CHUNK = 64
EPS = 1e-6
NEG_INF = -1e30
RET_HEADS = 4
RET_DK = 128
RET_W = 512
ROPE_BASE = 10000.0
CONV_W = 512
CONV_K = 31
ATT_HEADS = 8
ATT_DH = 64
ATT_W = 512
ATT_LEFT_CHUNKS = 8
MAX_REL = 256
N_BRANCH = 3
IN_COLS = 4 * RET_W + 2 * CONV_W + 3 * ATT_W + N_BRANCH * D_MODEL
N_GROUPS = 4
EXPERTS_PER_GROUP = 8
N_EXPERTS = N_GROUPS * EXPERTS_PER_GROUP
EXPERT_HIDDEN = 256
N_MOD = 6

LANES = 128
V7X_VMEM_LIMIT_BYTES = 56 * 1024 * 1024

COL_GATE = 0
COL_RQ, COL_RK, COL_RV, COL_RG = 6, 7, 8, 9
COL_CA, COL_CB = 10, 11
COL_AQ, COL_AK, COL_AV = 12, 13, 14

RET_BLOCK = 256
ATT_QBLOCK = 256
ATT_QBLOCKS_PER_STEP = 2
ATT_BAND = ATT_QBLOCK + ATT_LEFT_CHUNKS * CHUNK
ATT_PAD = ATT_LEFT_CHUNKS * CHUNK
CONV_TILE = 1024
CONV_HIST = 32
CONV_ROWS = 64
MOE_TILE = 512
EXPERT_TILES_PER_STEP = 2
ROUTE_TILE = 512
ROW_ALIGN = 16
SORT_SLOTS = 2 * ROUTE_TILE + N_EXPERTS * ROW_ALIGN
XS_W = D_MODEL + LANES


def _cparams(sem):
    return pltpu.CompilerParams(dimension_semantics=sem, vmem_limit_bytes=V7X_VMEM_LIMIT_BYTES)


def _adaln_kernel(c_ref, w_ref, b_ref, o_ref):
    c = c_ref[...]
    cs = c * jax.nn.sigmoid(c)
    w = w_ref[0]

    def split(v):
        hi = v.astype(BF16)
        return hi, (v - hi.astype(F32)).astype(BF16)

    (c_hi, c_lo), (w_hi, w_lo) = split(cs), split(w)
    o_ref[0] = (jnp.dot(c_hi, w_hi, preferred_element_type=F32) + jnp.dot(c_lo, w_hi, preferred_element_type=F32)
                + jnp.dot(c_hi, w_lo, preferred_element_type=F32) + b_ref[0])


def _adaln(c, w_ada, b_ada):
    L, D, N = w_ada.shape
    B = c.shape[0]
    tn = 1536
    return pl.pallas_call(
        _adaln_kernel,
        grid=(L, N // tn),
        in_specs=[pl.BlockSpec((B, D), lambda l, j: (0, 0)),
                  pl.BlockSpec((1, D, tn), lambda l, j: (l, 0, j)),
                  pl.BlockSpec((1, 1, tn), lambda l, j: (l, 0, j))],
        out_specs=pl.BlockSpec((1, B, tn), lambda l, j: (l, 0, j)),
        out_shape=jax.ShapeDtypeStruct((L, B, N), F32),
        compiler_params=_cparams(("arbitrary", "arbitrary")),
        name="adaln",
    )(c, w_ada, b_ada.reshape(L, 1, N))


def _rope_kernel(pos_ref, inv_ref, ph_ref, o_ref):
    o_ref[...] = jnp.cos(pos_ref[...] * inv_ref[...] - ph_ref[...])


def _rope_table(positions):
    T = positions.size
    half = RET_DK // 2
    inv = ROPE_BASE ** (-jnp.arange(half, dtype=F32) / half)
    inv2 = jnp.concatenate([inv, inv]).reshape(1, LANES)
    phase = jnp.concatenate([jnp.zeros((half,), F32), jnp.full((half,), math.pi / 2, F32)]).reshape(1, LANES)
    posb = jnp.broadcast_to(positions.reshape(T, 1).astype(F32), (T, LANES))
    tm = 2048
    return pl.pallas_call(
        _rope_kernel,
        grid=(T // tm,),
        in_specs=[pl.BlockSpec((tm, LANES), lambda i: (i, 0)),
                  pl.BlockSpec((1, LANES), lambda i: (0, 0)),
                  pl.BlockSpec((1, LANES), lambda i: (0, 0))],
        out_specs=pl.BlockSpec((tm, LANES), lambda i: (i, 0)),
        out_shape=jax.ShapeDtypeStruct((T, LANES), F32),
        compiler_params=_cparams(("arbitrary",)),
        name="rope_table",
    )(posb, inv2, phase)


def _modulated_rmsnorm(x, g, mod_ref, shift_row, scale_row):
    y = x * lax.rsqrt(jnp.mean(x * x, axis=-1, keepdims=True) + EPS) * g
    return y * (1.0 + mod_ref[0, scale_row:scale_row + 1, :]) + mod_ref[0, shift_row:shift_row + 1, :]


INPROJ_TM, INPROJ_TN = 2048, 2560


def _inproj_kernel(h_ref, w_ref, o_ref):
    o_ref[...] = jnp.dot(h_ref[...], w_ref[...], preferred_element_type=F32).astype(BF16)


def _inproj(h, w_in_b):
    T, D = h.shape
    N = w_in_b.shape[1]
    tm, tn = INPROJ_TM, INPROJ_TN
    return pl.pallas_call(
        _inproj_kernel,
        grid=(T // tm, N // tn),
        in_specs=[pl.BlockSpec((tm, D), lambda i, j: (i, 0)),
                  pl.BlockSpec((D, tn), lambda i, j: (0, j))],
        out_specs=pl.BlockSpec((tm, tn), lambda i, j: (i, j)),
        out_shape=jax.ShapeDtypeStruct((T, N), BF16),
        compiler_params=_cparams(("arbitrary", "arbitrary")),
        name="inproj",
    )(h, w_in_b)


def _inproj_norm_kernel(x_ref, mod_ref, g_ref, w_ref, o_ref, h_ref):
    @pl.when(pl.program_id(1) == 0)
    def _():
        h_ref[...] = _modulated_rmsnorm(x_ref[...], g_ref[...], mod_ref, 0, 1).astype(BF16)

    o_ref[...] = jnp.dot(h_ref[...], w_ref[...], preferred_element_type=F32).astype(BF16)


def _inproj_norm(x2, mod, g_mix, w_in_b, S):
    T, D = x2.shape
    N = w_in_b.shape[1]
    tm, tn = INPROJ_TM // 2, INPROJ_TN
    return pl.pallas_call(
        _inproj_norm_kernel,
        grid=(T // tm, N // tn),
        in_specs=[pl.BlockSpec((tm, D), lambda i, j: (i, 0)),
                  pl.BlockSpec((1, N_MOD, D), lambda i, j: (i * tm // S, 0, 0)),
                  pl.BlockSpec((1, D), lambda i, j: (0, 0)),
                  pl.BlockSpec((D, tn), lambda i, j: (0, j))],
        out_specs=pl.BlockSpec((tm, tn), lambda i, j: (i, j)),
        out_shape=jax.ShapeDtypeStruct((T, N), BF16),
        scratch_shapes=[pltpu.VMEM((tm, D), BF16)],
        compiler_params=_cparams(("arbitrary", "arbitrary")),
        name="inproj_norm",
    )(x2, mod, g_mix.reshape(1, D), w_in_b)


def _retention_kernel(q_ref, k_ref, v_ref, g_ref, cs_ref, gn_ref, o_ref, state_ref, decay_ref):
    C = RET_BLOCK

    @pl.when(pl.program_id(1) == 0)
    def _():
        state_ref[...] = jnp.zeros_like(state_ref)

    @pl.when((pl.program_id(0) == 0) & (pl.program_id(1) == 0))
    def _():
        d = (lax.broadcasted_iota(jnp.int32, (C, C), 0) - lax.broadcasted_iota(jnp.int32, (C, C), 1)).astype(F32)
        for h in range(RET_HEADS):
            lg = math.log1p(-(2.0 ** (-5.0 - h)))
            decay_ref[h] = jnp.where(d >= 0.0, jnp.exp(lg * jnp.maximum(d, 0.0)), 0.0)

    cs = cs_ref[...]
    csr = pltpu.roll(cs, RET_DK // 2, 1)
    first = lax.broadcasted_iota(jnp.int32, (C, RET_DK), 1) < RET_DK // 2
    cos2 = jnp.where(first, cs, csr)
    sin2 = jnp.where(first, -csr, cs)
    row = lax.broadcasted_iota(jnp.int32, (C, 1), 0).astype(F32)
    for h in range(RET_HEADS):
        lg = math.log1p(-(2.0 ** (-5.0 - h)))
        sl = slice(h * RET_DK, (h + 1) * RET_DK)
        qh = q_ref[:, sl].astype(F32)
        kh = k_ref[:, sl].astype(F32)
        vh = v_ref[:, sl]
        qr = (qh * cos2 + pltpu.roll(qh, RET_DK // 2, 1) * sin2).astype(BF16)
        kr = (kh * cos2 + pltpu.roll(kh, RET_DK // 2, 1) * sin2) * (RET_DK ** -0.5)
        s = lax.dot_general(qr, kr.astype(BF16), (((1,), (1,)), ((), ())),
                            preferred_element_type=F32) * decay_ref[h]
        o = jnp.dot(s.astype(BF16), vh, preferred_element_type=F32)
        st = state_ref[h]
        o = o + jnp.dot(qr, st.astype(BF16), preferred_element_type=F32) * jnp.exp(lg * (row + 1.0))
        kd = (kr * jnp.exp(lg * (C - 1.0 - row))).astype(BF16)
        kv = lax.dot_general(kd, vh, (((0,), (0,)), ((), ())), preferred_element_type=F32)
        state_ref[h] = math.exp(lg * C) * st + kv
        mu = jnp.mean(o, axis=-1, keepdims=True)
        oc = o - mu
        var = jnp.mean(oc * oc, axis=-1, keepdims=True)
        on = oc * lax.rsqrt(var + EPS) * gn_ref[:, sl]
        gh = g_ref[:, sl].astype(F32)
        o_ref[:, sl] = (on * (gh * jax.nn.sigmoid(gh))).astype(BF16)


def _retention(proj, cs_tab, ret_gn, B, S):
    T = B * S
    C = RET_BLOCK
    nblk = S // C

    def col(c):
        return pl.BlockSpec((C, RET_W), lambda b, i, c=c: (b * nblk + i, c))

    return pl.pallas_call(
        _retention_kernel,
        grid=(B, nblk),
        in_specs=[col(COL_RQ), col(COL_RK), col(COL_RV), col(COL_RG),
                  pl.BlockSpec((C, LANES), lambda b, i: (b * nblk + i, 0)),
                  pl.BlockSpec((1, RET_W), lambda b, i: (0, 0))],
        out_specs=pl.BlockSpec((C, RET_W), lambda b, i: (b * nblk + i, 0)),
        out_shape=jax.ShapeDtypeStruct((T, RET_W), BF16),
        scratch_shapes=[pltpu.VMEM((RET_HEADS, RET_DK, RET_DK), F32),
                        pltpu.VMEM((RET_HEADS, C, C), F32)],
        compiler_params=_cparams(("arbitrary", "arbitrary")),
        name="retention",
    )(proj, proj, proj, proj, cs_tab, ret_gn.reshape(1, RET_W))


def _conv_kernel(a_ref, b_ref, w_ref, bias_ref, lg_ref, lb_ref, o_ref, z_ref, zs_ref):
    tm = CONV_TILE
    H = CONV_HIST
    SUB = 8

    @pl.when(pl.program_id(1) == 0)
    def _():
        z_ref[0:H, :] = jnp.zeros((H, CONV_W), F32)

    a = a_ref[...].astype(F32)
    b = b_ref[...].astype(F32)
    z_ref[H:H + tm, :] = a * jax.nn.sigmoid(b)
    base = H - (CONV_K - 1)
    n_shift = tm + (base + CONV_K - 2) // SUB * SUB
    for r in range(1, SUB):
        for c0 in range(0, n_shift, 128):
            cs = min(128, n_shift - c0)
            zs_ref[r - 1, c0:c0 + cs, :] = z_ref[c0 + r:c0 + r + cs, :]
    for r0 in range(0, tm, CONV_ROWS):
        acc = jnp.zeros((CONV_ROWS, CONV_W), F32)
        for j in range(CONV_K):
            al, r = (base + j) // SUB * SUB, (base + j) % SUB
            if r == 0:
                tap = z_ref[r0 + al:r0 + al + CONV_ROWS, :]
            else:
                tap = zs_ref[r - 1, r0 + al:r0 + al + CONV_ROWS, :]
            acc = acc + tap * w_ref[j:j + 1, :]
        acc = acc + bias_ref[...]
        mu = jnp.mean(acc, axis=-1, keepdims=True)
        ac = acc - mu
        var = jnp.mean(ac * ac, axis=-1, keepdims=True)
        y = ac * lax.rsqrt(var + EPS) * lg_ref[...] + lb_ref[...]
        o_ref[r0:r0 + CONV_ROWS, :] = (y * jax.nn.sigmoid(y)).astype(BF16)
    z_ref[0:H, :] = z_ref[tm:tm + H, :]


def _conv(proj, conv_w, conv_b, ln_g, ln_b, B, S):
    T = B * S
    tm = CONV_TILE
    nblk = S // tm
    vec = pl.BlockSpec((1, CONV_W), lambda b, i: (0, 0))
    return pl.pallas_call(
        _conv_kernel,
        grid=(B, nblk),
        in_specs=[pl.BlockSpec((tm, CONV_W), lambda b, i: (b * nblk + i, COL_CA)),
                  pl.BlockSpec((tm, CONV_W), lambda b, i: (b * nblk + i, COL_CB)),
                  pl.BlockSpec((CONV_K, CONV_W), lambda b, i: (0, 0)),
                  vec, vec, vec],
        out_specs=pl.BlockSpec((tm, CONV_W), lambda b, i: (b * nblk + i, 0)),
        out_shape=jax.ShapeDtypeStruct((T, CONV_W), BF16),
        scratch_shapes=[pltpu.VMEM((CONV_HIST + tm, CONV_W), F32),
                        pltpu.VMEM((7, CONV_HIST + tm, CONV_W), F32)],
        compiler_params=_cparams(("arbitrary", "arbitrary")),
        name="conv",
    )(proj, proj, conv_w.reshape(CONV_K, CONV_W), conv_b.reshape(1, CONV_W),
      ln_g.reshape(1, CONV_W), ln_b.reshape(1, CONV_W))


def _head_mean_sq(x, hm_ref):
    sq = x * x
    hi = sq.astype(BF16)
    lo = (sq - hi.astype(F32)).astype(BF16)
    tot = jnp.dot(hi, hm_ref[...], preferred_element_type=F32) + jnp.dot(lo, hm_ref[...], preferred_element_type=F32)
    return tot * (1.0 / ATT_DH)


def _attn_kernel(q_ref, k_ref, v_ref, rrow_ref, qg_ref, kg_ref, hm_ref, o_ref, kn_ref, vt_ref, bmt_ref, *, S):
    tq = ATT_QBLOCK
    step = pl.program_id(1)
    heads = [slice(h * ATT_DH, (h + 1) * ATT_DH) for h in range(ATT_HEADS)]
    nkb = ATT_BAND // tq

    @pl.when((step == 0) & (pl.program_id(0) == 0))
    def _():
        r = lax.broadcasted_iota(jnp.int32, (tq, ATT_BAND), 0)
        u = lax.broadcasted_iota(jnp.int32, (tq, ATT_BAND), 1)
        off = u - (r // CHUNK) * CHUNK
        inband = (off >= 0) & (off < (ATT_LEFT_CHUNKS + 1) * CHUNK)
        for h in range(ATT_HEADS):
            gen = jnp.broadcast_to(rrow_ref[h], (tq, rrow_ref.shape[-1]))
            t = pltpu.roll(gen, 0, 1, stride=1, stride_axis=0)
            bmt_ref[h] = jnp.where(inband, t[:, :ATT_BAND], NEG_INF).T

    @pl.when(step == 0)
    def _():
        for h in range(ATT_HEADS):
            kn_ref[h, 0:ATT_PAD, :] = jnp.zeros((ATT_PAD, ATT_DH), BF16)
        for j in range(ATT_PAD // tq):
            vt_ref[j] = jnp.zeros((ATT_W, tq), BF16)
        rows = 512
        for r0 in range(0, S, rows):
            k = k_ref[r0:r0 + rows, :].astype(F32)
            kn = (k * lax.rsqrt(_head_mean_sq(k, hm_ref) + EPS) * kg_ref[...]).astype(BF16)
            for h in range(ATT_HEADS):
                kn_ref[h, ATT_PAD + r0:ATT_PAD + r0 + rows, :] = kn[:, heads[h]]
        for r0 in range(0, S, tq):
            vt_ref[(ATT_PAD + r0) // tq] = v_ref[r0:r0 + tq, :].astype(F32).T.astype(BF16)

    def block(qi, rows, mask_pad):
        start = pl.multiple_of(qi * tq, tq)
        q = q_ref[rows, :].astype(F32)
        qn = (q * lax.rsqrt(_head_mean_sq(q, hm_ref) + EPS) * qg_ref[...]).astype(BF16)
        if mask_pad:
            key_pos = lax.broadcasted_iota(jnp.int32, (ATT_BAND, LANES), 0) + start
            valid = key_pos >= ATT_PAD

        def scores(h):
            kb = kn_ref[h, pl.ds(start, ATT_BAND), :]
            return lax.dot_general(kb, qn[:, heads[h]], (((1,), (1,)), ((), ())), preferred_element_type=F32)

        def softmax(h, st):
            ps, ls = [], []
            for c0 in range(0, tq, LANES):
                ss = st[:, c0:c0 + LANES] + bmt_ref[h, :, c0:c0 + LANES]
                if mask_pad:
                    ss = jnp.where(valid, ss, NEG_INF)
                m = jnp.max(ss, axis=0, keepdims=True)
                p = jnp.exp2(ss - m)
                ls.append(jnp.sum(p, axis=0, keepdims=True))
                ps.append(p.astype(BF16))
            return jnp.concatenate(ps, axis=1), jnp.concatenate(ls, axis=1)

        def values(h, pt, l):
            ot = jnp.zeros((ATT_DH, tq), F32)
            for t in range(nkb):
                ot = ot + jnp.dot(vt_ref[qi + t, heads[h], :], pt[t * tq:(t + 1) * tq, :],
                                  preferred_element_type=F32)
            return ot / l

        sts, pls, outs = {}, {}, []
        for stage in range(ATT_HEADS + 2):
            if stage < ATT_HEADS:
                sts[stage] = scores(stage)
            if 0 <= stage - 1 < ATT_HEADS:
                pls[stage - 1] = softmax(stage - 1, sts.pop(stage - 1))
            if 0 <= stage - 2 < ATT_HEADS:
                outs.append(values(stage - 2, *pls.pop(stage - 2)))
        o_ref[rows, :] = jnp.concatenate(outs, axis=0).T.astype(BF16)

    def blocks(mask_pad):
        for sub in range(ATT_QBLOCKS_PER_STEP):
            block(step * ATT_QBLOCKS_PER_STEP + sub, slice(sub * tq, (sub + 1) * tq), mask_pad)

    first = ATT_PAD // (tq * ATT_QBLOCKS_PER_STEP)

    @pl.when(step < first)
    def _():
        blocks(True)

    @pl.when(step >= first)
    def _():
        blocks(False)


def _attn_bias_row(rel_table):
    t = rel_table.astype(F32) * math.log2(math.e)
    edge = jnp.broadcast_to(t[:, 2 * MAX_REL:], (ATT_HEADS, ATT_QBLOCK))
    row = jnp.concatenate([edge, jnp.flip(t[:, 1:], axis=1), edge], axis=1)
    return row.reshape(ATT_HEADS, 1, 2 * ATT_QBLOCK + 2 * MAX_REL)


def _attention(proj, rel_table, q_gain, k_gain, B, S):
    T = B * S
    tq = ATT_QBLOCK
    rows = tq * ATT_QBLOCKS_PER_STEP
    nblk = S // rows
    gen = _attn_bias_row(rel_table)
    qg = jnp.tile(q_gain.astype(F32) * (ATT_DH ** -0.5 * math.log2(math.e)), ATT_HEADS).reshape(1, ATT_W)
    kg = jnp.tile(k_gain.astype(F32), ATT_HEADS).reshape(1, ATT_W)
    lane_head = jnp.arange(ATT_W) // ATT_DH
    hm = (lane_head[:, None] == lane_head[None, :]).astype(BF16)
    vec = pl.BlockSpec((1, ATT_W), lambda b, i: (0, 0))
    return pl.pallas_call(
        functools.partial(_attn_kernel, S=S),
        grid=(B, nblk),
        in_specs=[pl.BlockSpec((rows, ATT_W), lambda b, i: (b * nblk + i, COL_AQ)),
                  pl.BlockSpec((S, ATT_W), lambda b, i: (b, COL_AK)),
                  pl.BlockSpec((S, ATT_W), lambda b, i: (b, COL_AV)),
                  pl.BlockSpec(gen.shape, lambda b, i: (0, 0, 0)),
                  vec, vec,
                  pl.BlockSpec((ATT_W, ATT_W), lambda b, i: (0, 0))],
        out_specs=pl.BlockSpec((rows, ATT_W), lambda b, i: (b * nblk + i, 0)),
        out_shape=jax.ShapeDtypeStruct((T, ATT_W), BF16),
        scratch_shapes=[pltpu.VMEM((ATT_HEADS, ATT_PAD + S, ATT_DH), BF16),
                        pltpu.VMEM(((ATT_PAD + S) // tq, ATT_W, tq), BF16),
                        pltpu.VMEM((ATT_HEADS, ATT_BAND, tq), F32)],
        compiler_params=_cparams(("arbitrary", "arbitrary")),
        name="chunk_attention",
    )(proj, proj, proj, gen, qg, kg, hm)


def _merge_kernel(x_ref, ret_ref, conv_ref, att_ref, g0_ref, g1_ref, g2_ref, bg_ref, mod_ref,
                  wr_ref, wc_ref, wa_ref, wo_ref, gf_ref, wrt_ref, brt_ref,
                  xo_ref, h2_ref, route_ref, cnt_out_ref, tri_ref):
    D = D_MODEL

    @pl.when(pl.program_id(0) == 0)
    def _():
        n = tri_ref.shape[0]
        tri_ref[...] = (lax.broadcasted_iota(jnp.int32, (n, n), 1)
                        < lax.broadcasted_iota(jnp.int32, (n, n), 0)).astype(BF16)

    def gated(g_ref, k, b_ref, w_ref):
        gate = jax.nn.sigmoid(g_ref[...].astype(F32) + bg_ref[:, k * D:(k + 1) * D])
        return gate * jnp.dot(b_ref[...], w_ref[...], preferred_element_type=F32)

    merged = gated(g0_ref, 0, ret_ref, wr_ref) + gated(g1_ref, 1, conv_ref, wc_ref) + gated(g2_ref, 2, att_ref, wa_ref)
    y = jnp.dot(merged.astype(BF16), wo_ref[...], preferred_element_type=F32)
    x = x_ref[...] + mod_ref[0, 2:3, :] * y
    xo_ref[...] = x

    h2 = _modulated_rmsnorm(x, gf_ref[...], mod_ref, 3, 4)
    h_hi = h2.astype(BF16)
    h2_ref[...] = h_hi
    h_lo = (h2 - h_hi.astype(F32)).astype(BF16)
    both = jnp.dot(h_hi, wrt_ref[...], preferred_element_type=F32)
    logits = (both[:, :LANES] + both[:, LANES:]
              + jnp.dot(h_lo, wrt_ref[:, :LANES], preferred_element_type=F32) + brt_ref[...])
    tm = logits.shape[0]
    lane = lax.broadcasted_iota(jnp.int32, (tm, LANES), 1)
    big = jnp.int32(LANES)
    is_g = (lane >= N_EXPERTS) & (lane < N_EXPERTS + N_GROUPS)
    gl = jnp.where(is_g, logits, -jnp.inf)
    gmax = jnp.max(gl, axis=-1, keepdims=True)
    gsel = jnp.min(jnp.where(gl == gmax, lane - N_EXPERTS, big), axis=-1, keepdims=True)
    p_group = 1.0 / jnp.sum(jnp.where(is_g, jnp.exp(gl - gmax), 0.0), axis=-1, keepdims=True)
    in_grp = (lane >= gsel * EXPERTS_PER_GROUP) & (lane < (gsel + 1) * EXPERTS_PER_GROUP)
    ch = jnp.where(in_grp, logits, -jnp.inf)
    v1 = jnp.max(ch, axis=-1, keepdims=True)
    i1 = jnp.min(jnp.where(ch == v1, lane, big), axis=-1, keepdims=True)
    ch2 = jnp.where(lane == i1, -jnp.inf, ch)
    v2 = jnp.max(ch2, axis=-1, keepdims=True)
    i2 = jnp.min(jnp.where(ch2 == v2, lane, big), axis=-1, keepdims=True)
    e = jnp.exp(v2 - v1)
    w1 = p_group / (1.0 + e)
    w2 = p_group * e / (1.0 + e)
    sel = ((lane == i1) | (lane == i2)).astype(BF16)
    before = jnp.dot(tri_ref[...], sel, preferred_element_type=F32)
    rank1 = jnp.sum(jnp.where(lane == i1, before, 0.0), axis=-1, keepdims=True)
    rank2 = jnp.sum(jnp.where(lane == i2, before, 0.0), axis=-1, keepdims=True)
    cnt_out_ref[0] = jnp.sum(sel.astype(F32), axis=0, keepdims=True)
    vals = (i1.astype(F32), i2.astype(F32), w1, w2, rank1, rank2)
    route = jnp.zeros((tm, LANES), F32)
    for k, v in enumerate(vals):
        route = jnp.where(lane == k, v, route)
    route_ref[...] = route


def _merge(x2, proj, ret, conv, att, b_gate, mod, wr, wc, wa, wo, g_ffn, w_rt, b_rt, S):
    T, D = x2.shape
    tm = ROUTE_TILE
    row = lambda i: (i, 0)
    const = lambda i: (0, 0)
    br = pl.BlockSpec((tm, RET_W), row)
    return pl.pallas_call(
        _merge_kernel,
        grid=(T // tm,),
        in_specs=[pl.BlockSpec((tm, D), row), br, br, br,
                  pl.BlockSpec((tm, D), lambda i: (i, COL_GATE)),
                  pl.BlockSpec((tm, D), lambda i: (i, COL_GATE + 1)),
                  pl.BlockSpec((tm, D), lambda i: (i, COL_GATE + 2)),
                  pl.BlockSpec((1, N_BRANCH * D), const),
                  pl.BlockSpec((1, N_MOD, D), lambda i: (i * tm // S, 0, 0)),
                  pl.BlockSpec((RET_W, D), const), pl.BlockSpec((CONV_W, D), const),
                  pl.BlockSpec((ATT_W, D), const), pl.BlockSpec((D, D), const),
                  pl.BlockSpec((1, D), const),
                  pl.BlockSpec((D, 2 * LANES), const), pl.BlockSpec((1, LANES), const)],
        out_specs=[pl.BlockSpec((tm, D), row), pl.BlockSpec((tm, D), row),
                   pl.BlockSpec((tm, LANES), row), pl.BlockSpec((1, 1, LANES), lambda i: (i, 0, 0))],
        out_shape=[jax.ShapeDtypeStruct((T, D), F32), jax.ShapeDtypeStruct((T, D), BF16),
                   jax.ShapeDtypeStruct((T, LANES), F32), jax.ShapeDtypeStruct((T // tm, 1, LANES), F32)],
        scratch_shapes=[pltpu.VMEM((tm, tm), BF16)],
        compiler_params=_cparams(("arbitrary",)),
        name="merge_router",
    )(x2, ret, conv, att, proj, proj, proj, b_gate.reshape(1, N_BRANCH * D), mod,
      wr, wc, wa, wo, g_ffn.reshape(1, D), w_rt, b_rt)


def _pair_slots(route, srcoff_row):
    tm = route.shape[0]
    lane = lax.broadcasted_iota(jnp.int32, (tm, LANES), 1).astype(F32)
    s0 = jnp.sum(jnp.where(lane == route[:, 0:1], srcoff_row, 0.0), axis=-1, keepdims=True) + route[:, 4:5]
    s1 = jnp.sum(jnp.where(lane == route[:, 1:2], srcoff_row, 0.0), axis=-1, keepdims=True) + route[:, 5:6]
    return s0, s1


def _chunk_copies(tile, so_ref, nch_ref, dst_ref, make_copy):
    base = tile * N_EXPERTS
    for e in range(N_EXPERTS):
        so = so_ref[base + e]
        d = dst_ref[base + e]

        def body(c, carry, so=so, d=d):
            make_copy(pl.multiple_of(so + c * ROW_ALIGN, ROW_ALIGN),
                      pl.multiple_of(d + c * ROW_ALIGN, ROW_ALIGN)).start()
            return carry

        lax.fori_loop(0, nch_ref[base + e], body, 0)


def _wait_chunks(n, make_wait):
    bit = 1
    while bit * ROW_ALIGN <= SORT_SLOTS:
        @pl.when((n & bit) != 0)
        def _(bit=bit):
            make_wait(bit * ROW_ALIGN).wait()

        bit *= 2


def _dispatch_kernel(so_ref, nch_ref, dst_ref, ntot_ref, zpos_ref, h_ref, route_ref, sof_ref,
                     xs_hbm, sorted_ref, zero_ref, sem, zsem, *, nsteps):
    i = pl.program_id(0)

    @pl.when(i == 0)
    def _():
        zero_ref[...] = jnp.zeros_like(zero_ref)
        for e in range(N_EXPERTS):
            pltpu.make_async_copy(zero_ref, xs_hbm.at[pl.ds(pl.multiple_of(zpos_ref[e], ROW_ALIGN), MOE_TILE), :],
                                  zsem).start()
        for e in range(N_EXPERTS):
            pltpu.make_async_copy(zero_ref, xs_hbm.at[pl.ds(0, MOE_TILE), :], zsem).wait()

    tail = zpos_ref[N_EXPERTS]
    n_tail = (xs_hbm.shape[0] - tail) // MOE_TILE
    tail_per_step = -(-(xs_hbm.shape[0] // MOE_TILE) // nsteps)

    def tail_copy(c):
        return pltpu.make_async_copy(
            zero_ref, xs_hbm.at[pl.ds(pl.multiple_of(tail + c * MOE_TILE, MOE_TILE), MOE_TILE), :], zsem)

    for k in range(tail_per_step):
        @pl.when(i * tail_per_step + k < n_tail)
        def _(k=k):
            tail_copy(i * tail_per_step + k).start()

    r = route_ref[...]
    tm = r.shape[0]
    lane = lax.broadcasted_iota(jnp.int32, (tm, LANES), 1)
    s0, s1 = _pair_slots(r, sof_ref[0])
    slots = lax.broadcasted_iota(jnp.int32, (tm, SORT_SLOTS), 1).astype(F32)
    onehot = ((slots == s0) | (slots == s1)).astype(BF16)

    def split(w):
        hi = w.astype(BF16).astype(F32)
        return hi, (w - hi).astype(BF16).astype(F32)

    extra = jnp.zeros((tm, LANES), F32)
    for k, v in enumerate((r[:, 0:1], *split(r[:, 2:3]), *split(r[:, 3:4]))):
        extra = jnp.where(lane == k, v, extra)
    hx = jnp.concatenate([h_ref[...], extra.astype(BF16)], axis=1)
    slot = i % 2
    sorted_ref[slot] = lax.dot_general(onehot, hx, (((0,), (0,)), ((), ())),
                                       preferred_element_type=F32).astype(BF16)

    def copies(s):
        def make_copy(src_row, dst_row):
            return pltpu.make_async_copy(sorted_ref.at[s, pl.ds(src_row, ROW_ALIGN), :],
                                         xs_hbm.at[pl.ds(dst_row, ROW_ALIGN), :], sem.at[s])
        return make_copy

    _chunk_copies(i, so_ref, nch_ref, dst_ref, copies(slot))

    def wait_all(tile, s):
        _wait_chunks(ntot_ref[tile], lambda rows: pltpu.make_async_copy(
            sorted_ref.at[s, pl.ds(0, rows), :], xs_hbm.at[pl.ds(0, rows), :], sem.at[s]))

    @pl.when(i > 0)
    def _():
        wait_all(i - 1, 1 - slot)

    @pl.when(i == nsteps - 1)
    def _():
        wait_all(i, slot)

        def wait_tail(c, carry):
            tail_copy(0).wait()
            return carry

        lax.fori_loop(0, n_tail, wait_tail, 0)


def _dispatch(h2, route, tabs, n_rows):
    T, D = h2.shape
    tm = ROUTE_TILE
    grid_spec = pltpu.PrefetchScalarGridSpec(
        num_scalar_prefetch=5,
        grid=(T // tm,),
        in_specs=[pl.BlockSpec((tm, D), lambda i, *_: (i, 0)),
                  pl.BlockSpec((tm, LANES), lambda i, *_: (i, 0)),
                  pl.BlockSpec((1, 1, LANES), lambda i, *_: (i, 0, 0))],
        out_specs=pl.BlockSpec(memory_space=pl.ANY),
        scratch_shapes=[pltpu.VMEM((2, SORT_SLOTS, XS_W), BF16), pltpu.VMEM((MOE_TILE, XS_W), BF16),
                        pltpu.SemaphoreType.DMA((2,)), pltpu.SemaphoreType.DMA(())],
    )
    return pl.pallas_call(
        functools.partial(_dispatch_kernel, nsteps=T // tm),
        grid_spec=grid_spec,
        out_shape=jax.ShapeDtypeStruct((n_rows + MOE_TILE, XS_W), BF16),
        compiler_params=_cparams(("arbitrary",)),
        name="moe_dispatch",
    )(tabs["srcoff"], tabs["nchunk"], tabs["dst"], tabs["ntot"], tabs["zpos"], h2, route, tabs["srcoff_f"])


def _ffn_kernel(te_ref, nv_ref, xs_ref, wu0_ref, wd0_ref, wu1_ref, wd1_ref, o_ref, wub_ref, wdb_ref):
    i = pl.program_id(0)
    weights = ((wu0_ref, wd0_ref), (wu1_ref, wd1_ref))
    for half in range(EXPERT_TILES_PER_STEP):
        wu_ref, wd_ref = weights[half]
        t = i * EXPERT_TILES_PER_STEP + half
        rows = slice(half * MOE_TILE, (half + 1) * MOE_TILE)

        @pl.when((i == 0) | (te_ref[t] != te_ref[jnp.maximum(t - EXPERT_TILES_PER_STEP, 0)]))
        def _(wu_ref=wu_ref, wd_ref=wd_ref, half=half):
            wub_ref[half] = wu_ref[0].astype(BF16)
            wdb_ref[half] = wd_ref[0].astype(BF16)

        @pl.when(t < nv_ref[0])
        def _(half=half, t=t, rows=rows):
            x = xs_ref[rows, :D_MODEL]
            wv = xs_ref[rows, D_MODEL:].astype(F32)
            mine = wv[:, 0:1] == te_ref[t].astype(F32)
            w = jnp.where(mine, wv[:, 1:2] + wv[:, 2:3], wv[:, 3:4] + wv[:, 4:5])
            hid = jnp.dot(x, wub_ref[half], preferred_element_type=F32)
            a = hid[:, :EXPERT_HIDDEN]
            b = hid[:, EXPERT_HIDDEN:]
            act = a * jax.nn.sigmoid(a) * b * w
            o_ref[rows, :] = jnp.dot(act.astype(BF16), wdb_ref[half], preferred_element_type=F32).astype(BF16)

        @pl.when(t >= nv_ref[0])
        def _(rows=rows):
            o_ref[rows, :] = jnp.zeros((MOE_TILE, D_MODEL), BF16)


def _ffn(xs, tile_expert, n_valid, w_up, w_down, layer, n_rows):
    per = EXPERT_TILES_PER_STEP
    tr = MOE_TILE * per
    D = D_MODEL
    F2 = w_up.shape[-1]
    first = layer * N_EXPERTS

    def wspec(shape, half):
        return pl.BlockSpec(shape, lambda i, te, nv: (first + te[i * per + half], 0, 0))

    grid_spec = pltpu.PrefetchScalarGridSpec(
        num_scalar_prefetch=2,
        grid=(n_rows // tr,),
        in_specs=[pl.BlockSpec((tr, XS_W), lambda i, te, nv: (jnp.minimum(i, (nv[0] - 1) // per), 0)),
                  wspec((1, D, F2), 0), wspec((1, EXPERT_HIDDEN, D), 0),
                  wspec((1, D, F2), 1), wspec((1, EXPERT_HIDDEN, D), 1)],
        out_specs=pl.BlockSpec((tr, D), lambda i, te, nv: (i, 0)),
        scratch_shapes=[pltpu.VMEM((per, D, F2), BF16), pltpu.VMEM((per, EXPERT_HIDDEN, D), BF16)],
    )
    return pl.pallas_call(
        _ffn_kernel,
        grid_spec=grid_spec,
        out_shape=jax.ShapeDtypeStruct((n_rows, D), BF16),
        compiler_params=_cparams(("arbitrary",)),
        name="moe_experts",
    )(tile_expert, n_valid, xs, w_up, w_down, w_up, w_down)


def _combine_kernel(so_ref, nch_ref, dst_ref, ntot_ref, x_ref, route_ref, sof_ref, mod_ref, ye_hbm, *rest,
                    nsteps, emit_next):
    if emit_next:
        modn_ref, gn_ref, o_ref, hn_ref, ys_ref, sem = rest
    else:
        o_ref, ys_ref, sem = rest
    i = pl.program_id(0)

    def copies(slot):
        def make_copy(src_row, dst_row):
            return pltpu.make_async_copy(ye_hbm.at[pl.ds(dst_row, ROW_ALIGN), :],
                                         ys_ref.at[slot, pl.ds(src_row, ROW_ALIGN), :], sem.at[slot])
        return make_copy

    @pl.when(i == 0)
    def _():
        ys_ref[...] = jnp.zeros_like(ys_ref)
        _chunk_copies(0, so_ref, nch_ref, dst_ref, copies(0))

    @pl.when(i + 1 < nsteps)
    def _():
        _chunk_copies(i + 1, so_ref, nch_ref, dst_ref, copies((i + 1) % 2))

    slot = i % 2
    _wait_chunks(ntot_ref[i], lambda rows: pltpu.make_async_copy(
        ye_hbm.at[pl.ds(0, rows), :], ys_ref.at[slot, pl.ds(0, rows), :], sem.at[slot]))
    s0, s1 = _pair_slots(route_ref[...], sof_ref[0])
    slots = lax.broadcasted_iota(jnp.int32, (s0.shape[0], SORT_SLOTS), 1).astype(F32)
    onehot = ((slots == s0) | (slots == s1)).astype(BF16)
    y = jnp.dot(onehot, ys_ref[slot], preferred_element_type=F32)
    x = x_ref[...] + mod_ref[0, 5:6, :] * y
    o_ref[...] = x
    if emit_next:
        hn_ref[...] = _modulated_rmsnorm(x, gn_ref[...], modn_ref, 0, 1).astype(BF16)


def _combine(x2, route, mod, ye, tabs, S, next_mod=None, next_gain=None):
    T, D = x2.shape
    tm = ROUTE_TILE
    nsteps = T // tm
    emit_next = next_mod is not None
    row = pl.BlockSpec((tm, D), lambda i, *_: (i, 0))
    modspec = pl.BlockSpec((1, N_MOD, D), lambda i, *_: (i * tm // S, 0, 0))
    in_specs = [row,
                pl.BlockSpec((tm, LANES), lambda i, *_: (i, 0)),
                pl.BlockSpec((1, 1, LANES), lambda i, *_: (i, 0, 0)),
                modspec,
                pl.BlockSpec(memory_space=pl.ANY)]
    args = [x2, route, tabs["srcoff_f"], mod, ye]
    out_specs, out_shape = row, jax.ShapeDtypeStruct((T, D), F32)
    if emit_next:
        in_specs += [modspec, pl.BlockSpec((1, D), lambda i, *_: (0, 0))]
        args += [next_mod, next_gain.reshape(1, D)]
        out_specs, out_shape = [row, row], [out_shape, jax.ShapeDtypeStruct((T, D), BF16)]
    grid_spec = pltpu.PrefetchScalarGridSpec(
        num_scalar_prefetch=4,
        grid=(nsteps,),
        in_specs=in_specs,
        out_specs=out_specs,
        scratch_shapes=[pltpu.VMEM((2, SORT_SLOTS, D), BF16), pltpu.SemaphoreType.DMA((2,))],
    )
    return pl.pallas_call(
        functools.partial(_combine_kernel, nsteps=nsteps, emit_next=emit_next),
        grid_spec=grid_spec,
        out_shape=out_shape,
        compiler_params=_cparams(("arbitrary",)),
        name="moe_combine",
    )(tabs["srcoff"], tabs["nchunk"], tabs["dst"], tabs["ntot"], *args)


def _moe_rows_bound(T):
    nt = T // ROUTE_TILE
    rows = 2 * T + nt * N_EXPERTS * (ROW_ALIGN - 1) + N_EXPERTS * (MOE_TILE - 1)
    step_rows = MOE_TILE * EXPERT_TILES_PER_STEP
    return -(-rows // step_rows) * step_rows


def _routing_tables(counts, n_rows):
    tr = MOE_TILE
    c = counts[:, 0, :N_EXPERTS].astype(jnp.int32)
    c16 = (c + ROW_ALIGN - 1) // ROW_ALIGN * ROW_ALIGN
    srcoff = jnp.cumsum(c16, axis=1) - c16
    rows = jnp.sum(c16, axis=0)
    ends = jnp.cumsum((rows + tr - 1) // tr * tr)
    offs = ends - (rows + tr - 1) // tr * tr
    dst = offs[None, :] + jnp.cumsum(c16, axis=0) - c16
    tile_start = jnp.arange(n_rows // tr, dtype=jnp.int32) * tr
    tile_expert = jnp.minimum(jnp.sum((tile_start[:, None] >= ends[None, :]).astype(jnp.int32), axis=-1),
                              N_EXPERTS - 1)
    srcoff_f = jnp.zeros((c.shape[0], 1, LANES), F32).at[:, 0, :N_EXPERTS].set(srcoff.astype(F32))
    return {
        "srcoff": srcoff.reshape(-1), "nchunk": (c16 // ROW_ALIGN).reshape(-1), "dst": dst.reshape(-1),
        "ntot": jnp.sum(c16 // ROW_ALIGN, axis=1), "zpos": jnp.concatenate([offs + rows, ends[-1:]]),
        "srcoff_f": srcoff_f,
        "tile_expert": tile_expert, "n_valid": (ends[-1] // tr).astype(jnp.int32).reshape(1),
    }


def _reorder_in_cols(w_in):
    cuts = 4 * RET_W + 2 * CONV_W + 3 * ATT_W
    return jnp.concatenate([w_in[..., cuts:], w_in[..., :cuts]], axis=-1)


def kernel(x, c, positions, w_ada, b_ada, g_mix, g_ffn, w_in, b_gate, ret_gn, conv_w, conv_b, conv_ln_g, conv_ln_b, att_q_gain, att_k_gain, att_rel_bias, w_ret_out, w_conv_out, w_att_out, w_out, w_group, b_group, w_inner, b_inner, w_up, w_down):
    B, S, D = x.shape
    L = w_ada.shape[0]
    T = B * S
    assert D == D_MODEL and S % INPROJ_TM == 0 and T % 2048 == 0
    n_rows = _moe_rows_bound(T)

    mod_all = _adaln(c, w_ada, b_ada).reshape(L, B, N_MOD, D)
    cs_tab = _rope_table(positions)
    x2 = x.reshape(T, D)
    h = None
    for l in range(L):
        mod = mod_all[l]
        w_in_b = _reorder_in_cols(w_in[l]).astype(BF16)
        proj = _inproj_norm(x2, mod, g_mix[l], w_in_b, S) if h is None else _inproj(h, w_in_b)
        ret = _retention(proj, cs_tab, ret_gn[l], B, S)
        conv = _conv(proj, conv_w[l], conv_b[l], conv_ln_g[l], conv_ln_b[l], B, S)
        att = _attention(proj, att_rel_bias[l], att_q_gain[l], att_k_gain[l], B, S)
        pad = LANES - N_EXPERTS - N_GROUPS
        w_rt = jnp.concatenate([w_inner[l], w_group[l], jnp.zeros((D, pad), F32)], axis=1)
        b_rt = jnp.concatenate([b_inner[l], b_group[l], jnp.zeros((pad,), F32)]).reshape(1, LANES)
        w_rt_hi = w_rt.astype(BF16)
        w_rt2 = jnp.concatenate([w_rt_hi, (w_rt - w_rt_hi.astype(F32)).astype(BF16)], axis=1)
        x2, h2, route, counts = _merge(x2, proj, ret, conv, att, b_gate[l], mod,
                                       w_ret_out[l].astype(BF16), w_conv_out[l].astype(BF16),
                                       w_att_out[l].astype(BF16), w_out[l].astype(BF16),
                                       g_ffn[l], w_rt2, b_rt, S)
        tabs = _routing_tables(counts, n_rows)
        xs = _dispatch(h2, route, tabs, n_rows)
        ye = _ffn(xs, tabs["tile_expert"], tabs["n_valid"],
                  w_up.reshape(L * N_EXPERTS, D, 2 * EXPERT_HIDDEN),
                  w_down.reshape(L * N_EXPERTS, EXPERT_HIDDEN, D), l, n_rows)
        if l + 1 < L:
            x2, h = _combine(x2, route, mod, ye, tabs, S, mod_all[l + 1], g_mix[l + 1])
        else:
            x2 = _combine(x2, route, mod, ye, tabs, S)
    return x2.reshape(B, S, D)
```
